```python
import math
import jax, jax.numpy as jnp
from jax import lax
import numpy as np

D_MODEL = 1024
BATCH = 1
SEQ = 16384
DEPTH = 2
DEC_BATCH = 32
DEC_SEQ = 32
PAST_LEN = 1024

CHUNK = 64
Q_BLOCK = 128
EPS = 1e-6
NEG = -1e30

DA_WIDTH = 512
DA_HEADS = 4
DA_HEAD_DIM = 64
DA_V_DIM = 2 * DA_HEAD_DIM
RW_WIDTH = 512
RW_HEADS = 8
RW_HEAD = 64
RW_DECAY_LORA = 64
RW_A_LORA = 64
RW_GN_EPS = 64e-5
RW_SHIFT_COLS = 3 * RW_WIDTH + RW_DECAY_LORA + RW_A_LORA
DA_COLS = 4 * DA_WIDTH
RW_COLS = RW_SHIFT_COLS + RW_WIDTH
EVEN_IN_COLS = DA_COLS + RW_COLS
EVEN_SPLITS = [DA_WIDTH, 2 * DA_WIDTH, 3 * DA_WIDTH, DA_COLS, DA_COLS + RW_SHIFT_COLS]
RW_SPLITS = [RW_WIDTH, 2 * RW_WIDTH, 3 * RW_WIDTH, 3 * RW_WIDTH + RW_DECAY_LORA]
SB_WIDTH = D_MODEL
SB_HEADS = 16
SB_HEAD_DIM = 64
ODD_IN_COLS = 4 * SB_WIDTH
N_BUCKETS = 32
MAX_DISTANCE = 128

kernel_name = "hybrid_diffattn_rwkv7_stickbreaking_step"


def rms_norm(x, g, eps=EPS):
    xf = x.astype(jnp.float32)
    y = xf * lax.rsqrt(jnp.mean(xf * xf, axis=-1, keepdims=True) + eps)
    return (y * g.astype(jnp.float32)).astype(x.dtype)


def lambda_init_fn(layer_idx):
    return 0.8 - 0.6 * math.exp(-0.3 * layer_idx)


def t5_bucket(rel):
    nb = N_BUCKETS // 2
    max_exact = nb // 2
    n = jnp.abs(rel)
    nf = jnp.maximum(n, 1).astype(jnp.float32)
    large = max_exact + (jnp.log(nf / max_exact) / math.log(MAX_DISTANCE / max_exact) * (nb - max_exact)).astype(jnp.int32)
    large = jnp.minimum(large, nb - 1)
    return jnp.where(rel > 0, nb, 0) + jnp.where(n < max_exact, n, large)


def blocked_queries(fn, q):
    B, T = q.shape[0], q.shape[1]
    nblk = T // Q_BLOCK
    qb = jnp.moveaxis(q.reshape((B, nblk, Q_BLOCK) + q.shape[2:]), 1, 0)
    qpos = jnp.arange(T, dtype=jnp.int32).reshape(nblk, Q_BLOCK)
    out = lax.map(lambda args: fn(args[0], args[1]), (qb, qpos))
    out = jnp.moveaxis(out, 0, 1)
    return out.reshape((B, T) + out.shape[3:])


def diff_attention(q, k, v, qpos, kpos, lam, rel_bias):
    d = DA_HEAD_DIM
    scale = d ** -0.5
    s1 = jnp.einsum('bqhd,bkhd->bhqk', q[..., :d], k[..., :d]).astype(jnp.float32) * scale
    s2 = jnp.einsum('bqhd,bkhd->bhqk', q[..., d:], k[..., d:]).astype(jnp.float32) * scale
    bias = jnp.transpose(rel_bias[t5_bucket(kpos[None, :] - qpos[:, None])], (2, 0, 1)).astype(jnp.float32)[None]
    mask = (kpos[None, :] // CHUNK) <= (qpos[:, None] // CHUNK)
    p1 = jax.nn.softmax(jnp.where(mask, s1 + bias, NEG), axis=-1)
    p2 = jax.nn.softmax(jnp.where(mask, s2 + bias, NEG), axis=-1)
    w = p1 - lam * p2
    return jnp.einsum('bhqk,bkhd->bqhd', w.astype(v.dtype), v)


def stick_breaking(q, k, v, qpos, kpos):
    z = jnp.einsum('bqhd,bkhd->bhqk', q, k).astype(jnp.float32) * (SB_HEAD_DIM ** -0.5)
    causal = kpos[None, :] < qpos[:, None]
    log1m = jnp.where(causal, -jax.nn.softplus(z), 0.0)
    suffix = lax.cumsum(log1m, axis=3, reverse=True)
    excl = jnp.concatenate([suffix[..., 1:], jnp.zeros_like(suffix[..., :1])], axis=3)
    a = jnp.where(causal, jnp.exp(jax.nn.log_sigmoid(z) + excl), 0.0)
    return jnp.einsum('bhqk,bkhd->bqhd', a.astype(v.dtype), v)


def rwkv7_scan(state, r, decay, kk, kka, k, v):
    def step(S, xs):
        r_t, w_t, kk_t, kka_t, k_t, v_t = xs
        sa = jnp.einsum('bhvk,bhk->bhv', S, kk_t)
        S = S * w_t[:, :, None, :] - sa[..., None] * kka_t[:, :, None, :] + v_t[..., None] * k_t[:, :, None, :]
        return S, jnp.einsum('bhvk,bhk->bhv', S, r_t)
    xs = tuple(jnp.moveaxis(t.astype(jnp.float32), 1, 0) for t in (r, decay, kk, kka, k, v))
    S, o = lax.scan(step, state.astype(jnp.float32), xs)
    return S.astype(state.dtype), jnp.moveaxis(o, 0, 1)


def head_group_norm(o, g, b):
    mu = jnp.mean(o, axis=-1, keepdims=True)
    var = jnp.mean(jnp.square(o - mu), axis=-1, keepdims=True)
    y = (o - mu) * lax.rsqrt(var + RW_GN_EPS)
    B, T = o.shape[0], o.shape[1]
    return y.reshape(B, T, RW_WIDTH) * g.astype(jnp.float32) + b.astype(jnp.float32)


def even_layer(x, shift_prev, wkv_prev, past_k, past_v, p, layer_idx):
    (norm_g, w_in, w_out, rel_bias, q_g, k_g, lq1, lk1, lq2, lk2, subln_g,
     mu, w0, w_up, a0, a_up, k_k, k_a, r_k, lnx_g, lnx_b) = p
    B, T, _ = x.shape
    lambda_init = lambda_init_fn(layer_idx)
    xn = rms_norm(x, norm_g)
    proj = xn @ w_in
    qa, ka, va, za, prw, zb = jnp.split(proj, EVEN_SPLITS, axis=-1)

    def qk_norm(t, g):
        return rms_norm(t.reshape(B, T, DA_HEADS, 2, DA_HEAD_DIM), g).reshape(B, T, DA_HEADS, 2 * DA_HEAD_DIM)
    q = qk_norm(qa, q_g)
    k = qk_norm(ka, k_g)
    v = va.reshape(B, T, DA_HEADS, DA_V_DIM)
    lam = (jnp.exp(jnp.sum(lq1.astype(jnp.float32) * lk1.astype(jnp.float32)))
           - jnp.exp(jnp.sum(lq2.astype(jnp.float32) * lk2.astype(jnp.float32))) + lambda_init)
    if past_k is None:
        kpos = jnp.arange(T, dtype=jnp.int32)
        attn = blocked_queries(lambda qb, qp: diff_attention(qb, k, v, qp, kpos, lam, rel_bias), q)
    else:
        P = past_k.shape[1]
        k_all = jnp.concatenate([past_k, k], axis=1)
        v_all = jnp.concatenate([past_v, v], axis=1)
        qpos = P + jnp.arange(T, dtype=jnp.int32)
        kpos = jnp.arange(P + T, dtype=jnp.int32)
        attn = diff_attention(q, k_all, v_all, qpos, kpos, lam, rel_bias)
    oa = rms_norm(attn, subln_g) * (1.0 - lambda_init)
    oa = oa.reshape(B, T, DA_WIDTH) * jax.nn.silu(za)

    prev = jnp.concatenate([shift_prev.astype(prw.dtype), prw[:, :-1]], axis=1)
    mix = prw + (prev - prw) * mu
    r, kr, vr, wl, al = jnp.split(mix, RW_SPLITS, axis=-1)
    w = -jax.nn.softplus(-(w0 + jnp.tanh(wl) @ w_up).astype(jnp.float32)) - 0.5
    decay = jnp.exp(-jnp.exp(w))
    a = jax.nn.sigmoid(a0 + al @ a_up)
    heads = lambda t: t.reshape(B, T, RW_HEADS, RW_HEAD)
    kk = heads((kr * k_k).astype(jnp.float32))
    kk = kk / jnp.maximum(jnp.sqrt(jnp.sum(kk * kk, axis=-1, keepdims=True)), 1e-12)
    kr = kr * (1.0 + (a - 1.0) * k_a)
    r_h, k_h, v_h, a_h = heads(r), heads(kr), heads(vr), heads(a)
    wkv_new, o = rwkv7_scan(wkv_prev, r_h, heads(decay), kk, kk * a_h.astype(jnp.float32), k_h, v_h)
    bonus = (jnp.sum((r_h * k_h * r_k).astype(jnp.float32), axis=-1, keepdims=True) * v_h.astype(jnp.float32)).reshape(B, T, RW_WIDTH)
    ob = (head_group_norm(o, lnx_g, lnx_b) + bonus).astype(x.dtype) * jax.nn.silu(zb)

    y = x + jnp.concatenate([oa, ob], axis=-1) @ w_out
    return y, k, v, prw[:, -1:], wkv_new


def odd_layer(x, past_k, past_v, p):
    norm_g, w_in, w_out = p
    B, T, _ = x.shape
    xn = rms_norm(x, norm_g)
    q, k, v, z = jnp.split(xn @ w_in, 4, axis=-1)
    q = q.reshape(B, T, SB_HEADS, SB_HEAD_DIM)
    k = k.reshape(B, T, SB_HEADS, SB_HEAD_DIM)
    v = v.reshape(B, T, SB_HEADS, SB_HEAD_DIM)
    if past_k is None:
        kpos = jnp.arange(T, dtype=jnp.int32)
        o = blocked_queries(lambda qb, qp: stick_breaking(qb, k, v, qp, kpos), q)
    else:
        P = past_k.shape[1]
        k_all = jnp.concatenate([past_k, k], axis=1)
        v_all = jnp.concatenate([past_v, v], axis=1)
        qpos = P + jnp.arange(T, dtype=jnp.int32)
        kpos = jnp.arange(P + T, dtype=jnp.int32)
        o = stick_breaking(q, k_all, v_all, qpos, kpos)
    o = o.reshape(B, T, SB_WIDTH) * jax.nn.silu(z)
    return x + o @ w_out, k, v


def setup_inputs(seed: int = 0) -> dict:
    key = jax.random.key(seed)
    ks = jax.random.split(key, 32)
    nrm = lambda k, shape, s: jax.random.normal(k, shape, jnp.float32) * s
    D = D_MODEL
    return {
        "x_prompt": nrm(ks[0], (BATCH, SEQ, D), 1.0),
        "x_sample": nrm(ks[1], (DEC_BATCH, DEC_SEQ, D), 1.0),
        "cache_l0_k": nrm(ks[2], (DEC_BATCH, PAST_LEN, DA_HEADS, 2 * DA_HEAD_DIM), 1.0),
        "cache_l0_v": nrm(ks[3], (DEC_BATCH, PAST_LEN, DA_HEADS, DA_V_DIM), 1.0),
        "state_l0_shift": nrm(ks[4], (DEC_BATCH, 1, RW_SHIFT_COLS), 1.0),
        "state_l0_wkv": nrm(ks[5], (DEC_BATCH, RW_HEADS, RW_HEAD, RW_HEAD), 0.5),
        "cache_l1_k": nrm(ks[6], (DEC_BATCH, PAST_LEN, SB_HEADS, SB_HEAD_DIM), 1.0),
        "cache_l1_v": nrm(ks[7], (DEC_BATCH, PAST_LEN, SB_HEADS, SB_HEAD_DIM), 1.0),
        "rel_bias": nrm(ks[8], (N_BUCKETS, DA_HEADS), 0.5),
        "norm_l0": 1.0 + nrm(ks[9], (D,), 0.05),
        "w_in_l0": nrm(ks[10], (D, EVEN_IN_COLS), D ** -0.5),
        "w_out_l0": nrm(ks[11], (DA_WIDTH + RW_WIDTH, D), (DA_WIDTH + RW_WIDTH) ** -0.5),
        "da_q_norm": 1.0 + nrm(ks[12], (DA_HEAD_DIM,), 0.05),
        "da_k_norm": 1.0 + nrm(ks[13], (DA_HEAD_DIM,), 0.05),
        "da_lambda_q1": nrm(ks[14], (DA_HEAD_DIM,), 0.1),
        "da_lambda_k1": nrm(ks[15], (DA_HEAD_DIM,), 0.1),
        "da_lambda_q2": nrm(ks[16], (DA_HEAD_DIM,), 0.1),
        "da_lambda_k2": nrm(ks[17], (DA_HEAD_DIM,), 0.1),
        "da_subln": 1.0 + nrm(ks[18], (DA_V_DIM,), 0.05),
        "rw_mu": jax.random.uniform(ks[19], (RW_SHIFT_COLS,), jnp.float32),
        "rw_w0": jax.random.uniform(ks[20], (RW_WIDTH,), jnp.float32, -6.0, 0.0),
        "rw_w_up": nrm(ks[21], (RW_DECAY_LORA, RW_WIDTH), 0.5 * RW_DECAY_LORA ** -0.5),
        "rw_a0": nrm(ks[22], (RW_WIDTH,), 0.5),
        "rw_a_up": nrm(ks[23], (RW_A_LORA, RW_WIDTH), 0.5 * RW_A_LORA ** -0.5),
        "rw_k_k": 0.85 + nrm(ks[24], (RW_WIDTH,), 0.05),
        "rw_k_a": 1.0 + nrm(ks[25], (RW_WIDTH,), 0.05),
        "rw_r_k": nrm(ks[26], (RW_HEADS, RW_HEAD), 0.1),
        "rw_lnx_g": 1.0 + nrm(ks[27], (RW_WIDTH,), 0.05),
        "rw_lnx_b": nrm(ks[28], (RW_WIDTH,), 0.02),
        "norm_l1": 1.0 + nrm(ks[29], (D,), 0.05),
        "w_in_l1": nrm(ks[30], (D, ODD_IN_COLS), D ** -0.5),
        "w_out_l1": nrm(ks[31], (SB_WIDTH, D), SB_WIDTH ** -0.5),
    }


def reference(x_prompt, x_sample, cache_l0_k, cache_l0_v, state_l0_shift, state_l0_wkv, cache_l1_k, cache_l1_v,
              rel_bias, norm_l0, w_in_l0, w_out_l0, da_q_norm, da_k_norm, da_lambda_q1, da_lambda_k1,
              da_lambda_q2, da_lambda_k2, da_subln, rw_mu, rw_w0, rw_w_up, rw_a0, rw_a_up, rw_k_k, rw_k_a,
              rw_r_k, rw_lnx_g, rw_lnx_b, norm_l1, w_in_l1, w_out_l1):
    even_params = [(norm_l0, w_in_l0, w_out_l0, rel_bias, da_q_norm, da_k_norm, da_lambda_q1, da_lambda_k1,
                    da_lambda_q2, da_lambda_k2, da_subln, rw_mu, rw_w0, rw_w_up, rw_a0, rw_a_up, rw_k_k, rw_k_a,
                    rw_r_k, rw_lnx_g, rw_lnx_b)]
    odd_params = [(norm_l1, w_in_l1, w_out_l1)]
    even_caches = [(cache_l0_k, cache_l0_v, state_l0_shift, state_l0_wkv)]
    odd_caches = [(cache_l1_k, cache_l1_v)]
    yp, ys = x_prompt, x_sample
    Bp = x_prompt.shape[0]
    for layer in range(DEPTH):
        if layer % 2 == 0:
            p = even_params[layer // 2]
            ck, cv, cs, cw = even_caches[layer // 2]
            shift0 = jnp.zeros((Bp, 1, RW_SHIFT_COLS), x_prompt.dtype)
            wkv0 = jnp.zeros((Bp, RW_HEADS, RW_HEAD, RW_HEAD), state_l0_wkv.dtype)
            yp, l0_k_p, l0_v_p, l0_shift_p, l0_wkv_p = even_layer(yp, shift0, wkv0, None, None, p, layer)
            ys, l0_k_s, l0_v_s, l0_shift_s, l0_wkv_s = even_layer(ys, cs, cw, ck, cv, p, layer)
        else:
            p = odd_params[layer // 2]
            ck, cv = odd_caches[layer // 2]
            yp, l1_k_p, l1_v_p = odd_layer(yp, None, None, p)
            ys, l1_k_s, l1_v_s = odd_layer(ys, ck, cv, p)
    return (yp, ys, l0_k_p, l0_v_p, l0_shift_p, l0_wkv_p, l1_k_p, l1_v_p,
            l0_k_s, l0_v_s, l0_shift_s, l0_wkv_s, l1_k_s, l1_v_s)
```

```python
import functools
import math

import numpy as np
import jax
import jax.numpy as jnp
from jax import lax
from jax.experimental import pallas as pl
from jax.experimental.pallas import tpu as pltpu

F32 = jnp.float32
BF16 = jnp.bfloat16

EPS = 1e-6
NEG = -1e30
CHUNK = 64
LANES = 128
SUBLANES = 8
HEAD = 64
DA_HEADS = 4
DA_WIDTH = 512
RW_WIDTH = 512
RW_HEADS = 8
RW_SHIFT_COLS = 3 * RW_WIDTH + 128
RW_GN_EPS = 64e-5
SB_HEADS = 16
SB_WIDTH = 1024
N_BUCKETS = 32
MAX_DISTANCE = 128
LAMBDA_INIT = 0.8 - 0.6 * math.exp(-0.3 * 0)
VMEM_LIMIT = 56 * 1024 * 1024


def _params(sem):
    return pltpu.CompilerParams(dimension_semantics=sem, vmem_limit_bytes=VMEM_LIMIT)


def _dot(a, b):
    return jnp.dot(a, b, preferred_element_type=F32)


def _dot_nt(a, b):
    return lax.dot_general(a, b, (((1,), (1,)), ((), ())), preferred_element_type=F32)


def _dot2(x, w):
    hi = x.astype(BF16)
    lo = (x - hi.astype(F32)).astype(BF16)
    return _dot(hi, w) + _dot(lo, w)


def _silu(z):
    return z / (1.0 + jnp.exp(-z))


def _rms(x, g):
    return x * lax.rsqrt(jnp.mean(x * x, axis=-1, keepdims=True) + EPS) * g


def _group_matrix(n, scale):
    idx = np.arange(n) // HEAD
    return jnp.asarray((idx[:, None] == idx[None, :]).astype(np.float32) * scale, dtype=BF16)


def _inproj0_kernel(x_ref, g_ref, w_ref, qg_ref, kg_ref, avg_ref,
                    qn_ref, kn_ref, knb_ref, v_ref, vb_ref, za_ref, prw_ref, zb_ref):
    y = _dot(_rms(x_ref[...], g_ref[...]).astype(BF16), w_ref[...])

    def head_norm(t, g):
        return t * lax.rsqrt(_dot2(t * t, avg_ref[...]) + EPS) * g

    qn_ref[...] = (head_norm(y[:, 0:512], qg_ref[...]) * (HEAD ** -0.5)).astype(BF16)
    kn = head_norm(y[:, 512:1024], kg_ref[...])
    kn_ref[...] = kn
    knb_ref[...] = kn.astype(BF16)
    v = y[:, 1024:1536]
    v_ref[...] = v
    vb_ref[...] = v.astype(BF16)
    za_ref[...] = y[:, 1536:2048]
    prw_ref[...] = y[:, 2048:2048 + RW_SHIFT_COLS]
    zb_ref[...] = y[:, 2048 + RW_SHIFT_COLS:]


def _inproj0(x, g, w, qg, kg, tm):
    m, d = x.shape
    n = w.shape[1]
    row = lambda c: pl.BlockSpec((tm, c), lambda i: (i, 0))
    full = lambda a: pl.BlockSpec(a.shape, lambda i: (0,) * a.ndim)
    avg = _group_matrix(512, 1.0 / HEAD)
    widths = [(512, BF16), (512, F32), (512, BF16), (512, F32), (512, BF16), (512, F32),
              (RW_SHIFT_COLS, F32), (512, F32)]
    return pl.pallas_call(
        _inproj0_kernel,
        grid=(m // tm,),
        in_specs=[row(d), full(g), full(w), full(qg), full(kg), full(avg)],
        out_specs=[row(c) for c, _ in widths],
        out_shape=[jax.ShapeDtypeStruct((m, c), dt) for c, dt in widths],
        compiler_params=_params(("arbitrary",)),
        name="inproj0",
    )(x, g, w, qg, kg, avg)


def _inproj1_kernel(x_ref, g_ref, w_ref, qb_ref, k_ref, kb_ref, v_ref, vb_ref, z_ref):
    y = _dot(_rms(x_ref[...], g_ref[...]).astype(BF16), w_ref[...])
    qb_ref[...] = (y[:, 0:1024] * (HEAD ** -0.5)).astype(BF16)
    k = y[:, 1024:2048]
    k_ref[...] = k
    kb_ref[...] = k.astype(BF16)
    v = y[:, 2048:3072]
    v_ref[...] = v
    vb_ref[...] = v.astype(BF16)
    z_ref[...] = y[:, 3072:4096]


def _inproj1(x, g, w, tm):
    m, d = x.shape
    row = lambda c: pl.BlockSpec((tm, c), lambda i: (i, 0))
    full = lambda a: pl.BlockSpec(a.shape, lambda i: (0,) * a.ndim)
    dts = [BF16, F32, BF16, F32, BF16, F32]
    return pl.pallas_call(
        _inproj1_kernel,
        grid=(m // tm,),
        in_specs=[row(d), full(g), full(w)],
        out_specs=[row(1024) for _ in dts],
        out_shape=[jax.ShapeDtypeStruct((m, 1024), dt) for dt in dts],
        compiler_params=_params(("arbitrary",)),
        name="inproj1",
    )(x, g, w)


def _outproj_kernel(*refs):
    n = (len(refs) - 2) // 2
    x_ref, o_ref = refs[0], refs[-1]
    acc = x_ref[...]
    for a_ref, w_ref in zip(refs[1:1 + n], refs[1 + n:1 + 2 * n]):
        acc = acc + _dot(a_ref[...], w_ref[...])
    o_ref[...] = acc


def _outproj(x, acts, ws, tm):
    m, d = x.shape
    row = lambda c: pl.BlockSpec((tm, c), lambda i: (i, 0))
    full = lambda a: pl.BlockSpec(a.shape, lambda i: (0,) * a.ndim)
    return pl.pallas_call(
        _outproj_kernel,
        grid=(m // tm,),
        in_specs=[row(d)] + [row(a.shape[1]) for a in acts] + [full(w) for w in ws],
        out_specs=row(d),
        out_shape=jax.ShapeDtypeStruct((m, d), F32),
        compiler_params=_params(("arbitrary",)),
        name="outproj",
    )(x, *acts, *ws)


def _t5_bucket_np(rel):
    nb = N_BUCKETS // 2
    max_exact = nb // 2
    n = np.abs(rel)
    nf = np.maximum(n, 1).astype(np.float32)
    large = max_exact + (np.log(nf / np.float32(max_exact)) / np.float32(math.log(MAX_DISTANCE / max_exact))
                         * np.float32(nb - max_exact)).astype(np.int32)
    large = np.minimum(large, nb - 1)
    return np.where(rel > 0, nb, 0) + np.where(n < max_exact, n, large)


FAR_BUCKET = N_BUCKETS // 2 - 1


def _stack_halves(q):
    lane = lax.broadcasted_iota(jnp.int32, q.shape, 1)
    zero = jnp.zeros_like(q)
    return jnp.concatenate([jnp.where(lane < HEAD, q, zero), jnp.where(lane >= HEAD, q, zero)], axis=0)


def _softmax_step(qs, k, v, bias, m_sc, l_sc, acc_sc):
    s = _dot_nt(qs, k)
    if bias is not None:
        s = s + bias
    reps = s.shape[1] // LANES
    m_old = m_sc[...]
    m_new = jnp.maximum(m_old, jnp.max(s, axis=-1, keepdims=True))
    alpha = jnp.exp(m_old - m_new)
    p = jnp.exp(s - jnp.concatenate([m_new] * reps, axis=1))
    l_sc[...] = alpha * l_sc[...] + jnp.sum(p, axis=-1, keepdims=True)
    acc_sc[...] = alpha * acc_sc[...] + _dot(p.astype(BF16), v)
    m_sc[...] = m_new


def _diff_finish(lam_ref, za, sg, l_sc, acc_sc, tq):
    lv = lam_ref[...]
    lam = (jnp.exp(jnp.sum(lv[0:1] * lv[1:2], axis=-1, keepdims=True))
           - jnp.exp(jnp.sum(lv[2:3] * lv[3:4], axis=-1, keepdims=True)) + LAMBDA_INIT)
    o = acc_sc[...] / l_sc[...]
    attn = o[:tq] - lam * o[tq:]
    return (_rms(attn, sg) * (1.0 - LAMBDA_INIT)) * _silu(za)


def _da_prompt_kernel(lam_ref, q_ref, k_ref, v_ref, za_ref, tab_ref, sg_ref, o_ref,
                      m_sc, l_sc, acc_sc, *, tq, tk):
    i = pl.program_id(1)
    qs = _stack_halves(q_ref[...])
    m_sc[...] = jnp.full(m_sc.shape, NEG, F32)
    l_sc[...] = jnp.zeros(l_sc.shape, F32)
    acc_sc[...] = jnp.zeros(acc_sc.shape, F32)

    def tile(j, bias):
        off = pl.multiple_of(j * tk, tk)
        _softmax_step(qs, k_ref[pl.ds(off, tk), :], v_ref[pl.ds(off, tk), :], bias, m_sc, l_sc, acc_sc)

    def far(j, c):
        tile(j, None)
        return c

    lax.fori_loop(0, jnp.maximum(i - 1, 0), far, 0)

    @pl.when(i >= 1)
    def _():
        b = tab_ref[0, 1]
        tile(i - 1, jnp.concatenate([b, b], axis=0))

    b = tab_ref[0, 0]
    tile(i, jnp.concatenate([b, b], axis=0))
    o_ref[...] = _diff_finish(lam_ref, za_ref[...], sg_ref[...], l_sc, acc_sc, tq).astype(BF16)


def _da_prompt(lamv, qn, knb, vb, za, tab, sg, tq):
    t = qn.shape[0]
    tk = tq
    kern = functools.partial(_da_prompt_kernel, tq=tq, tk=tk)
    return pl.pallas_call(
        kern,
        grid=(DA_HEADS, t // tq),
        in_specs=[
            pl.BlockSpec(lamv.shape, lambda h, i: (0, 0)),
            pl.BlockSpec((tq, LANES), lambda h, i: (i, h)),
            pl.BlockSpec((t, LANES), lambda h, i: (0, h)),
            pl.BlockSpec((t, LANES), lambda h, i: (0, h)),
            pl.BlockSpec((tq, LANES), lambda h, i: (i, h)),
            pl.BlockSpec((1, 2, tq, tk), lambda h, i: (h, 0, 0, 0)),
            pl.BlockSpec(sg.shape, lambda h, i: (0, 0)),
        ],
        out_specs=pl.BlockSpec((tq, LANES), lambda h, i: (i, h)),
        out_shape=jax.ShapeDtypeStruct((t, DA_WIDTH), BF16),
        scratch_shapes=[pltpu.VMEM((2 * tq, LANES), F32)] * 3,
        compiler_params=_params(("arbitrary", "arbitrary")),
        name="diff_attn_prompt",
    )(lamv, qn, knb, vb, za, tab, sg)


def _da_sample_kernel(lam_ref, q_ref, ck_ref, cv_ref, kn_ref, vn_ref, za_ref, tabc_ref, tabn_ref, sg_ref,
                      o_ref, m_sc, l_sc, acc_sc, *, ts, tk):
    qs = _stack_halves(q_ref[...])
    m_sc[...] = jnp.full(m_sc.shape, NEG, F32)
    l_sc[...] = jnp.zeros(l_sc.shape, F32)
    acc_sc[...] = jnp.zeros(acc_sc.shape, F32)
    past = ck_ref.shape[1]
    for j in range(past // tk):
        b = tabc_ref[0, :, j * tk:(j + 1) * tk]
        _softmax_step(qs, ck_ref[0, j * tk:(j + 1) * tk, :].astype(BF16),
                      cv_ref[0, j * tk:(j + 1) * tk, :].astype(BF16),
                      jnp.concatenate([b, b], axis=0), m_sc, l_sc, acc_sc)
    pad = jnp.zeros((LANES - ts, LANES), BF16)
    kn = jnp.concatenate([kn_ref[...].astype(BF16), pad], axis=0)
    vn = jnp.concatenate([vn_ref[...].astype(BF16), pad], axis=0)
    b = tabn_ref[0]
    _softmax_step(qs, kn, vn, jnp.concatenate([b, b], axis=0), m_sc, l_sc, acc_sc)
    o_ref[...] = _diff_finish(lam_ref, za_ref[...], sg_ref[...], l_sc, acc_sc, ts).astype(BF16)


def _da_sample(lamv, qn, ck, cv, kn, vn, za, tabc, tabn, sg, ts):
    nb, past, _ = ck.shape
    kern = functools.partial(_da_sample_kernel, ts=ts, tk=256)
    rows = pl.BlockSpec((ts, LANES), lambda b, h: (b, h))
    return pl.pallas_call(
        kern,
        grid=(nb, DA_HEADS),
        in_specs=[
            pl.BlockSpec(lamv.shape, lambda b, h: (0, 0)),
            rows,
            pl.BlockSpec((1, past, LANES), lambda b, h: (b, 0, h)),
            pl.BlockSpec((1, past, LANES), lambda b, h: (b, 0, h)),
            rows, rows, rows,
            pl.BlockSpec((1, ts, past), lambda b, h: (h, 0, 0)),
            pl.BlockSpec((1, ts, LANES), lambda b, h: (h, 0, 0)),
            pl.BlockSpec(sg.shape, lambda b, h: (0, 0)),
        ],
        out_specs=rows,
        out_shape=jax.ShapeDtypeStruct((nb * ts, DA_WIDTH), BF16),
        scratch_shapes=[pltpu.VMEM((2 * ts, LANES), F32)] * 3,
        compiler_params=_params(("arbitrary", "arbitrary")),
        name="diff_attn_sample",
    )(lamv, qn, ck, cv, kn, vn, za, tabc, tabn, sg)


def _bias_tables(rel_bias, tq, ts, past):
    def lookup(rel, mask):
        b = jnp.take(rel_bias, jnp.asarray(_t5_bucket_np(rel)), axis=0) - rel_bias[FAR_BUCKET]
        b = jnp.where(jnp.asarray(mask)[..., None], b, NEG)
        return jnp.moveaxis(b, -1, 0).astype(F32)

    r = np.arange(tq)[:, None]
    c = np.arange(tq)[None, :]
    near = lookup(c - r - tq, np.ones((tq, tq), bool))
    diag = lookup(c - r, (c // CHUNK) <= (r // CHUNK))
    prompt_tab = jnp.stack([diag, near], axis=1)

    qpos = past + np.arange(ts)[:, None]
    kc = np.arange(past)[None, :]
    kn = past + np.arange(LANES)[None, :]
    tabc = lookup(kc - qpos, (kc // CHUNK) <= (qpos // CHUNK))
    tabn = lookup(kn - qpos, ((kn // CHUNK) <= (qpos // CHUNK)) & (kn < past + ts))
    return prompt_tab, tabc, tabn


def _rwkv_kernel(prw_ref, look_ref, first_ref, zb_ref, s0_ref, mu_ref, w0_ref, a0_ref, wwa_ref,
                 kkw_ref, kaw_ref, rk_ref, lng_ref, lnb_ref, g2_ref, avg_ref, ones_ref, e_ref,
                 ob_ref, sfin_ref,
                 s_sc, r_sc, w_sc, kk_sc, kka_sc, k_sc, v_sc, o_sc, *, tm):
    j = pl.program_id(1)

    @pl.when(j == 0)
    def _():
        s_sc[...] = s0_ref[0]

    x = prw_ref[...]
    row0 = jnp.where(j == 0, first_ref[0], look_ref[7:8, :])
    rows = lax.broadcasted_iota(jnp.int32, x.shape, 0)
    prev = jnp.where(rows == 0, row0, pltpu.roll(x, 1, axis=0))
    mix = x + (prev - x) * mu_ref[...]
    r = mix[:, 0:512]
    kr = mix[:, 512:1024]
    vr = mix[:, 1024:1536]
    la = mix[:, 1536:1664]
    lane = lax.broadcasted_iota(jnp.int32, la.shape, 1)
    la = jnp.where(lane < HEAD, jnp.tanh(la), la)
    hi = la.astype(BF16)
    lo = (la - hi.astype(F32)).astype(BF16)
    wwa = wwa_ref[...]
    lora = _dot(hi, wwa[0]) + _dot(lo, wwa[0]) + _dot(hi, wwa[1])
    wpre = w0_ref[...] + lora[:, 0:512]
    nw = -wpre
    w = -(jnp.maximum(nw, 0.0) + jnp.log(1.0 + jnp.exp(-jnp.abs(nw)))) - 0.5
    w_sc[...] = jnp.exp(-jnp.exp(w))
    a = 1.0 / (1.0 + jnp.exp(-(a0_ref[...] + lora[:, 512:1024])))
    kk = kr * kkw_ref[...]
    nrm = jnp.sqrt(_dot2(kk * kk, ones_ref[...]))
    kk = kk / jnp.maximum(nrm, 1e-12)
    kk_sc[...] = kk
    kka_sc[...] = kk * a
    kh = kr * (1.0 + (a - 1.0) * kaw_ref[...])
    k_sc[...] = kh
    r_sc[...] = r
    v_sc[...] = vr

    g2 = g2_ref[...]
    e4 = e_ref[...]

    def rowvec(blk, i):
        return jnp.concatenate(
            [jnp.broadcast_to(blk[i:i + 1, c * LANES:(c + 1) * LANES], (HEAD, LANES)) for c in range(4)],
            axis=0)

    def steps(g, s):
        base = pl.multiple_of(g * SUBLANES, SUBLANES)
        kk_b, kka_b, k_b, v_b, w_b, r_b = (ref[pl.ds(base, SUBLANES), :]
                                           for ref in (kk_sc, kka_sc, k_sc, v_sc, w_sc, r_sc))
        out_rows = []
        for i in range(SUBLANES):
            sa = _dot2(s * rowvec(kk_b, i), g2)
            vcol = _dot2(rowvec(v_b, i) * e4, g2)
            s = s * rowvec(w_b, i) - sa * rowvec(kka_b, i) + vcol * rowvec(k_b, i)
            oe = _dot2(s * rowvec(r_b, i), g2) * e4
            out_rows.append(jnp.concatenate(
                [jnp.sum(oe[c * HEAD:(c + 1) * HEAD], axis=0, keepdims=True) for c in range(4)], axis=1))
        o_sc[pl.ds(base, SUBLANES), :] = jnp.concatenate(out_rows, axis=0)
        return s

    s_sc[...] = lax.fori_loop(0, tm // SUBLANES, steps, s_sc[...])

    o = o_sc[...]
    mean = _dot2(o, avg_ref[...])
    d = o - mean
    var = _dot2(d * d, avg_ref[...])
    y = d * lax.rsqrt(var + RW_GN_EPS) * lng_ref[...] + lnb_ref[...]
    bonus = _dot2(r * kh * rk_ref[...], ones_ref[...]) * vr
    ob_ref[...] = ((y + bonus) * _silu(zb_ref[...])).astype(BF16)

    @pl.when(j == pl.num_programs(1) - 1)
    def _():
        sfin_ref[0] = s_sc[...]


def _rwkv(prw, first, zb, s0, p, seq_len, tm):
    m = prw.shape[0]
    nseq = m // seq_len
    ntile = seq_len // tm
    kern = functools.partial(_rwkv_kernel, tm=tm)
    full = lambda a: pl.BlockSpec(a.shape, lambda b, j: (0,) * a.ndim)
    rows = lambda c: pl.BlockSpec((tm, c), lambda b, j: (b * ntile + j, 0))
    look = pl.BlockSpec((8, RW_SHIFT_COLS), lambda b, j: (jnp.maximum((b * ntile + j) * (tm // 8) - 1, 0), 0))
    consts = [p["mu"], p["w0"], p["a0"], p["wwa"], p["kkw"], p["kaw"], p["rk"], p["lng"], p["lnb"],
              p["g2"], p["avg"], p["ones"], p["e4"]]
    return pl.pallas_call(
        kern,
        grid=(nseq, ntile),
        in_specs=[rows(RW_SHIFT_COLS), look,
                  pl.BlockSpec((1, 1, RW_SHIFT_COLS), lambda b, j: (b, 0, 0)),
                  rows(RW_WIDTH),
                  pl.BlockSpec((1, 256, LANES), lambda b, j: (b, 0, 0))] + [full(c) for c in consts],
        out_specs=[rows(RW_WIDTH), pl.BlockSpec((1, 256, LANES), lambda b, j: (b, 0, 0))],
        out_shape=[jax.ShapeDtypeStruct((m, RW_WIDTH), BF16), jax.ShapeDtypeStruct((nseq, 256, LANES), F32)],
        scratch_shapes=[pltpu.VMEM((256, LANES), F32)] + [pltpu.VMEM((tm, RW_WIDTH), F32)] * 7,
        compiler_params=_params(("arbitrary", "arbitrary")),
        name="rwkv7",
    )(prw, prw, first, zb, s0, *consts)


def _state_to_rows(s):
    b = s.shape[0]
    return s.reshape(b, 4, 2, HEAD, HEAD).transpose(0, 1, 3, 2, 4).reshape(b, 256, LANES)


def _rows_to_state(s):
    b = s.shape[0]
    return s.reshape(b, 4, HEAD, 2, HEAD).transpose(0, 1, 3, 2, 4).reshape(b, RW_HEADS, HEAD, HEAD)


def _sb_tile(qs, k, v, u, c, mask):
    tk = k.shape[0]
    z = _dot_nt(qs, k)
    lg = -(jnp.maximum(z, 0.0) + jnp.log(1.0 + jnp.exp(-jnp.abs(z))))
    if mask is not None:
        lg = jnp.where(mask, lg, 0.0)
    it = _dot2(lg, u)
    a = jnp.exp(z + it[:, :tk] + c)
    if mask is not None:
        a = jnp.where(mask, a, 0.0)
    return _dot(a.astype(BF16), v), c + it[:, tk:]


def _sb_finish(acc, z):
    t = z.shape[0]
    lane = lax.broadcasted_iota(jnp.int32, z.shape, 1)
    return jnp.where(lane < HEAD, acc[:t], acc[t:]) * _silu(z)


def _sb_prompt_kernel(q_ref, k_ref, v_ref, z_ref, u_ref, o_ref, acc_sc, c_sc, *, tq):
    i = pl.program_id(1)
    tk = LANES
    r = tq // tk
    qs = _stack_halves(q_ref[...])
    u = u_ref[...]
    acc_sc[...] = jnp.zeros(acc_sc.shape, F32)
    c_sc[...] = jnp.zeros(c_sc.shape, F32)
    qrow = lax.broadcasted_iota(jnp.int32, (2 * tq, tk), 0) % tq
    col = lax.broadcasted_iota(jnp.int32, (2 * tq, tk), 1)

    def tile(j, mask):
        off = pl.multiple_of(j * tk, tk)
        da, c = _sb_tile(qs, k_ref[pl.ds(off, tk), :], v_ref[pl.ds(off, tk), :], u, c_sc[...], mask)
        acc_sc[...] += da
        c_sc[...] = c

    for d in range(r):
        jj = r - 1 - d
        tile(i * r + jj, (col + jj * tk) < qrow)

    def body(n, carry):
        tile(i * r - 1 - n, None)
        return carry

    lax.fori_loop(0, i * r, body, 0)
    o_ref[...] = _sb_finish(acc_sc[...], z_ref[...]).astype(BF16)


def _sb_prompt(qb, kb, vb, z, u, tq):
    t = qb.shape[0]
    kern = functools.partial(_sb_prompt_kernel, tq=tq)
    rows = pl.BlockSpec((tq, LANES), lambda h, i: (i, h))
    whole = pl.BlockSpec((t, LANES), lambda h, i: (0, h))
    return pl.pallas_call(
        kern,
        grid=(SB_HEADS // 2, t // tq),
        in_specs=[rows, whole, whole, rows, pl.BlockSpec(u.shape, lambda h, i: (0, 0))],
        out_specs=rows,
        out_shape=jax.ShapeDtypeStruct((t, SB_WIDTH), BF16),
        scratch_shapes=[pltpu.VMEM((2 * tq, LANES), F32)] * 2,
        compiler_params=_params(("arbitrary", "arbitrary")),
        name="stick_breaking_prompt",
    )(qb, kb, vb, z, u)


def _sb_sample_kernel(q_ref, ck_ref, cv_ref, kn_ref, vn_ref, z_ref, u_ref, o_ref, *, ts):
    tk = LANES
    qs = _stack_halves(q_ref[...])
    u = u_ref[...]
    pad = jnp.zeros((tk - ts, LANES), BF16)
    kn = jnp.concatenate([kn_ref[...], pad], axis=0)
    vn = jnp.concatenate([vn_ref[...], pad], axis=0)
    qrow = lax.broadcasted_iota(jnp.int32, (2 * ts, tk), 0) % ts
    col = lax.broadcasted_iota(jnp.int32, (2 * ts, tk), 1)
    acc, c = _sb_tile(qs, kn, vn, u, jnp.zeros((2 * ts, LANES), F32), col < qrow)
    for j in reversed(range(ck_ref.shape[1] // tk)):
        da, c = _sb_tile(qs, ck_ref[0, j * tk:(j + 1) * tk, :].astype(BF16),
                         cv_ref[0, j * tk:(j + 1) * tk, :].astype(BF16), u, c, None)
        acc = acc + da
    o_ref[...] = _sb_finish(acc, z_ref[...]).astype(BF16)


def _sb_sample(qb, ck, cv, kb, vb, z, u, ts):
    nb, past, _ = ck.shape
    kern = functools.partial(_sb_sample_kernel, ts=ts)
    rows = pl.BlockSpec((ts, LANES), lambda b, h: (b, h))
    cache = pl.BlockSpec((1, past, LANES), lambda b, h: (b, 0, h))
    return pl.pallas_call(
        kern,
        grid=(nb, SB_HEADS // 2),
        in_specs=[rows, cache, cache, rows, rows, rows, pl.BlockSpec(u.shape, lambda b, h: (0, 0))],
        out_specs=rows,
        out_shape=jax.ShapeDtypeStruct((nb * ts, SB_WIDTH), BF16),
        compiler_params=_params(("arbitrary", "arbitrary")),
        name="stick_breaking_sample",
    )(qb, ck, cv, kb, vb, z, u)


def _suffix_matrix():
    j = np.arange(LANES)[:, None]
    s = np.arange(LANES)[None, :]
    u = (j >= s).astype(np.float32)
    return jnp.asarray(np.concatenate([u, np.ones((LANES, LANES), np.float32)], axis=1), dtype=BF16)


def _row_tile(m):
    return 256 if m % 256 == 0 else m


def kernel(x_prompt, x_sample, cache_l0_k, cache_l0_v, state_l0_shift, state_l0_wkv, cache_l1_k, cache_l1_v, rel_bias, norm_l0, w_in_l0, w_out_l0, da_q_norm, da_k_norm, da_lambda_q1, da_lambda_k1, da_lambda_q2, da_lambda_k2, da_subln, rw_mu, rw_w0, rw_w_up, rw_a0, rw_a_up, rw_k_k, rw_k_a, rw_r_k, rw_lnx_g, rw_lnx_b, norm_l1, w_in_l1, w_out_l1):
    bp, t, d = x_prompt.shape
    nb, ts, _ = x_sample.shape
    past = cache_l0_k.shape[1]
    assert bp == 1 and ts <= LANES and t % 256 == 0 and past % 256 == 0

    row = lambda v: v.reshape(1, -1).astype(F32)
    w_in0 = w_in_l0.astype(BF16)
    w_in1 = w_in_l1.astype(BF16)
    w_out0a = w_out_l0[:DA_WIDTH].astype(BF16)
    w_out0b = w_out_l0[DA_WIDTH:].astype(BF16)
    w_out1 = w_out_l1.astype(BF16)
    qg = row(jnp.tile(da_q_norm, 512 // HEAD))
    kg = row(jnp.tile(da_k_norm, 512 // HEAD))
    lamv = jnp.stack([da_lambda_q1, da_lambda_k1, da_lambda_q2, da_lambda_k2]).astype(F32)
    sg = row(da_subln)
    zeros = jnp.zeros((HEAD, RW_WIDTH), F32)
    wwa_f = jnp.concatenate([jnp.concatenate([rw_w_up, zeros], axis=1),
                             jnp.concatenate([zeros, rw_a_up], axis=1)], axis=0)
    wwa_hi = wwa_f.astype(BF16)
    wwa = jnp.stack([wwa_hi, (wwa_f - wwa_hi.astype(F32)).astype(BF16)])
    e4 = np.equal(np.arange(256)[:, None] % HEAD, np.arange(LANES)[None, :] % HEAD).astype(np.float32)
    rw = dict(mu=row(rw_mu), w0=row(rw_w0), a0=row(rw_a0), wwa=wwa, kkw=row(rw_k_k), kaw=row(rw_k_a),
              rk=row(rw_r_k), lng=row(rw_lnx_g), lnb=row(rw_lnx_b), g2=_group_matrix(LANES, 1.0),
              avg=_group_matrix(RW_WIDTH, 1.0 / HEAD), ones=_group_matrix(RW_WIDTH, 1.0), e4=jnp.asarray(e4))
    tq_da = 256
    prompt_tab, tabc, tabn = _bias_tables(rel_bias.astype(F32), tq_da, ts, past)
    u = _suffix_matrix()

    def layer0(x, first, s0, seq_len, attend):
        m = x.shape[0]
        tm = _row_tile(m)
        qn, kn, knb, v, vb, za, prw, zb = _inproj0(x, row(norm_l0), w_in0, qg, kg, tm)
        oa = attend(qn, kn, knb, v, vb, za)
        ob, sfin = _rwkv(prw, first, zb, _state_to_rows(s0), rw, seq_len, min(seq_len, 256))
        y = _outproj(x, [oa, ob], [w_out0a, w_out0b], tm)
        shift = prw.reshape(m // seq_len, seq_len, RW_SHIFT_COLS)[:, -1:]
        return y, kn, v, shift, _rows_to_state(sfin)

    def layer1(x, attend):
        m = x.shape[0]
        tm = _row_tile(m)
        qb, k, kb, v, vb, z = _inproj1(x, row(norm_l1), w_in1, tm)
        o = attend(qb, kb, vb, z)
        return _outproj(x, [o], [w_out1], tm), k, v

    xp = x_prompt.reshape(t, d)
    yp, k0p, v0p, shp, wkvp = layer0(
        xp, jnp.zeros((1, 1, RW_SHIFT_COLS), F32), jnp.zeros((1, RW_HEADS, HEAD, HEAD), F32), t,
        lambda qn, kn, knb, v, vb, za: _da_prompt(lamv, qn, knb, vb, za, prompt_tab, sg, tq_da))
    yp, k1p, v1p = layer1(yp, lambda qb, kb, vb, z: _sb_prompt(qb, kb, vb, z, u, 128))

    xs = x_sample.reshape(nb * ts, d)
    ck0 = cache_l0_k.reshape(nb, past, DA_WIDTH)
    cv0 = cache_l0_v.reshape(nb, past, DA_WIDTH)
    ys, k0s, v0s, shs, wkvs = layer0(
        xs, state_l0_shift, state_l0_wkv, ts,
        lambda qn, kn, knb, v, vb, za: _da_sample(lamv, qn, ck0, cv0, kn, v, za, tabc, tabn, sg, ts))
    ck1 = cache_l1_k.reshape(nb, past, SB_WIDTH)
    cv1 = cache_l1_v.reshape(nb, past, SB_WIDTH)
    ys, k1s, v1s = layer1(ys, lambda qb, kb, vb, z: _sb_sample(qb, ck1, cv1, kb, vb, z, u, ts))

    return (yp.reshape(1, t, d), ys.reshape(nb, ts, d),
            k0p.reshape(1, t, DA_HEADS, LANES), v0p.reshape(1, t, DA_HEADS, LANES), shp, wkvp,
            k1p.reshape(1, t, SB_HEADS, HEAD), v1p.reshape(1, t, SB_HEADS, HEAD),
            k0s.reshape(nb, ts, DA_HEADS, LANES), v0s.reshape(nb, ts, DA_HEADS, LANES), shs, wkvs,
            k1s.reshape(nb, ts, SB_HEADS, HEAD), v1s.reshape(nb, ts, SB_HEADS, HEAD))
```

```python
import functools
import math

import numpy as np
import jax
import jax.numpy as jnp
from jax import lax
from jax.experimental import pallas as pl
from jax.experimental.pallas import tpu as pltpu

F32 = jnp.float32
BF16 = jnp.bfloat16

EPS = 1e-6
NEG = -1e30
CHUNK = 64
LANES = 128
SUBLANES = 8
HEAD = 64
DA_HEADS = 4
DA_WIDTH = 512
RW_WIDTH = 512
RW_HEADS = 8
RW_SHIFT_COLS = 3 * RW_WIDTH + 128
RW_GN_EPS = 64e-5
SB_HEADS = 16
SB_WIDTH = 1024
N_BUCKETS = 32
MAX_DISTANCE = 128
LAMBDA_INIT = 0.8 - 0.6 * math.exp(-0.3 * 0)
VMEM_LIMIT = 56 * 1024 * 1024


def _params(sem):
    return pltpu.CompilerParams(dimension_semantics=sem, vmem_limit_bytes=VMEM_LIMIT)


def _dot(a, b):
    return jnp.dot(a, b, preferred_element_type=F32)


def _dot_nt(a, b):
    return lax.dot_general(a, b, (((1,), (1,)), ((), ())), preferred_element_type=F32)


def _dot2(x, w):
    hi = x.astype(BF16)
    lo = (x - hi.astype(F32)).astype(BF16)
    return _dot(hi, w) + _dot(lo, w)


def _silu(z):
    return z / (1.0 + jnp.exp(-z))


def _rms(x, g):
    return x * lax.rsqrt(jnp.mean(x * x, axis=-1, keepdims=True) + EPS) * g


def _group_matrix(n, scale):
    idx = np.arange(n) // HEAD
    return jnp.asarray((idx[:, None] == idx[None, :]).astype(np.float32) * scale, dtype=BF16)


def _inproj0_kernel(x_ref, g_ref, w_ref, qg_ref, kg_ref, avg_ref,
                    qn_ref, kn_ref, knb_ref, v_ref, vb_ref, za_ref, prw_ref, zb_ref):
    y = _dot(_rms(x_ref[...], g_ref[...]).astype(BF16), w_ref[...])

    def head_norm(t, g):
        return t * lax.rsqrt(_dot2(t * t, avg_ref[...]) + EPS) * g

    qn_ref[...] = (head_norm(y[:, 0:512], qg_ref[...]) * (HEAD ** -0.5)).astype(BF16)
    kn = head_norm(y[:, 512:1024], kg_ref[...])
    kn_ref[...] = kn
    knb_ref[...] = kn.astype(BF16)
    v = y[:, 1024:1536]
    v_ref[...] = v
    vb_ref[...] = v.astype(BF16)
    za_ref[...] = y[:, 1536:2048]
    prw_ref[...] = y[:, 2048:2048 + RW_SHIFT_COLS]
    zb_ref[...] = y[:, 2048 + RW_SHIFT_COLS:]


def _inproj0(x, g, w, qg, kg, tm):
    m, d = x.shape
    n = w.shape[1]
    row = lambda c: pl.BlockSpec((tm, c), lambda i: (i, 0))
    full = lambda a: pl.BlockSpec(a.shape, lambda i: (0,) * a.ndim)
    avg = _group_matrix(512, 1.0 / HEAD)
    widths = [(512, BF16), (512, F32), (512, BF16), (512, F32), (512, BF16), (512, F32),
              (RW_SHIFT_COLS, F32), (512, F32)]
    return pl.pallas_call(
        _inproj0_kernel,
        grid=(m // tm,),
        in_specs=[row(d), full(g), full(w), full(qg), full(kg), full(avg)],
        out_specs=[row(c) for c, _ in widths],
        out_shape=[jax.ShapeDtypeStruct((m, c), dt) for c, dt in widths],
        compiler_params=_params(("arbitrary",)),
        name="inproj0",
    )(x, g, w, qg, kg, avg)


def _inproj1_kernel(x_ref, g_ref, w_ref, qb_ref, k_ref, kb_ref, v_ref, vb_ref, z_ref):
    y = _dot(_rms(x_ref[...], g_ref[...]).astype(BF16), w_ref[...])
    qb_ref[...] = (y[:, 0:1024] * (HEAD ** -0.5)).astype(BF16)
    k = y[:, 1024:2048]
    k_ref[...] = k
    kb_ref[...] = k.astype(BF16)
    v = y[:, 2048:3072]
    v_ref[...] = v
    vb_ref[...] = v.astype(BF16)
    z_ref[...] = y[:, 3072:4096]


def _inproj1(x, g, w, tm):
    m, d = x.shape
    row = lambda c: pl.BlockSpec((tm, c), lambda i: (i, 0))
    full = lambda a: pl.BlockSpec(a.shape, lambda i: (0,) * a.ndim)
    dts = [BF16, F32, BF16, F32, BF16, F32]
    return pl.pallas_call(
        _inproj1_kernel,
        grid=(m // tm,),
        in_specs=[row(d), full(g), full(w)],
        out_specs=[row(1024) for _ in dts],
        out_shape=[jax.ShapeDtypeStruct((m, 1024), dt) for dt in dts],
        compiler_params=_params(("arbitrary",)),
        name="inproj1",
    )(x, g, w)


def _outproj_kernel(*refs):
    n = (len(refs) - 2) // 2
    x_ref, o_ref = refs[0], refs[-1]
    acc = x_ref[...]
    for a_ref, w_ref in zip(refs[1:1 + n], refs[1 + n:1 + 2 * n]):
        acc = acc + _dot(a_ref[...], w_ref[...])
    o_ref[...] = acc


def _outproj(x, acts, ws, tm):
    m, d = x.shape
    row = lambda c: pl.BlockSpec((tm, c), lambda i: (i, 0))
    full = lambda a: pl.BlockSpec(a.shape, lambda i: (0,) * a.ndim)
    return pl.pallas_call(
        _outproj_kernel,
        grid=(m // tm,),
        in_specs=[row(d)] + [row(a.shape[1]) for a in acts] + [full(w) for w in ws],
        out_specs=row(d),
        out_shape=jax.ShapeDtypeStruct((m, d), F32),
        compiler_params=_params(("arbitrary",)),
        name="outproj",
    )(x, *acts, *ws)


def _t5_bucket_np(rel):
    nb = N_BUCKETS // 2
    max_exact = nb // 2
    n = np.abs(rel)
    nf = np.maximum(n, 1).astype(np.float32)
    large = max_exact + (np.log(nf / np.float32(max_exact)) / np.float32(math.log(MAX_DISTANCE / max_exact))
                         * np.float32(nb - max_exact)).astype(np.int32)
    large = np.minimum(large, nb - 1)
    return np.where(rel > 0, nb, 0) + np.where(n < max_exact, n, large)


FAR_BUCKET = N_BUCKETS // 2 - 1


def _stack_halves(q):
    lane = lax.broadcasted_iota(jnp.int32, q.shape, 1)
    zero = jnp.zeros_like(q)
    return jnp.concatenate([jnp.where(lane < HEAD, q, zero), jnp.where(lane >= HEAD, q, zero)], axis=0)


def _softmax_step(qs, k, v, bias, m_sc, l_sc, acc_sc):
    s = _dot_nt(qs, k)
    if bias is not None:
        s = s + bias
    reps = s.shape[1] // LANES
    m_old = m_sc[...]
    m_new = jnp.maximum(m_old, jnp.max(s, axis=-1, keepdims=True))
    alpha = jnp.exp(m_old - m_new)
    p = jnp.exp(s - jnp.concatenate([m_new] * reps, axis=1))
    l_sc[...] = alpha * l_sc[...] + jnp.sum(p, axis=-1, keepdims=True)
    acc_sc[...] = alpha * acc_sc[...] + _dot(p.astype(BF16), v)
    m_sc[...] = m_new


def _diff_finish(lam_ref, za, sg, l_sc, acc_sc, tq):
    lv = lam_ref[...]
    lam = (jnp.exp(jnp.sum(lv[0:1] * lv[1:2], axis=-1, keepdims=True))
           - jnp.exp(jnp.sum(lv[2:3] * lv[3:4], axis=-1, keepdims=True)) + LAMBDA_INIT)
    o = acc_sc[...] / l_sc[...]
    attn = o[:tq] - lam * o[tq:]
    return (_rms(attn, sg) * (1.0 - LAMBDA_INIT)) * _silu(za)


def _da_prompt_kernel(lam_ref, q_ref, k_ref, v_ref, za_ref, tab_ref, sg_ref, o_ref,
                      m_sc, l_sc, acc_sc, *, tq, tk):
    i = pl.program_id(1)
    qs = _stack_halves(q_ref[...])
    m_sc[...] = jnp.full(m_sc.shape, NEG, F32)
    l_sc[...] = jnp.zeros(l_sc.shape, F32)
    acc_sc[...] = jnp.zeros(acc_sc.shape, F32)

    def tile(j, bias):
        off = pl.multiple_of(j * tk, tk)
        _softmax_step(qs, k_ref[pl.ds(off, tk), :], v_ref[pl.ds(off, tk), :], bias, m_sc, l_sc, acc_sc)

    def far(j, c):
        tile(j, None)
        return c

    lax.fori_loop(0, jnp.maximum(i - 1, 0), far, 0)

    @pl.when(i >= 1)
    def _():
        b = tab_ref[0, 1]
        tile(i - 1, jnp.concatenate([b, b], axis=0))

    b = tab_ref[0, 0]
    tile(i, jnp.concatenate([b, b], axis=0))
    o_ref[...] = _diff_finish(lam_ref, za_ref[...], sg_ref[...], l_sc, acc_sc, tq).astype(BF16)


def _da_prompt(lamv, qn, knb, vb, za, tab, sg, tq):
    t = qn.shape[0]
    tk = tq
    kern = functools.partial(_da_prompt_kernel, tq=tq, tk=tk)
    return pl.pallas_call(
        kern,
        grid=(DA_HEADS, t // tq),
        in_specs=[
            pl.BlockSpec(lamv.shape, lambda h, i: (0, 0)),
            pl.BlockSpec((tq, LANES), lambda h, i: (i, h)),
            pl.BlockSpec((t, LANES), lambda h, i: (0, h)),
            pl.BlockSpec((t, LANES), lambda h, i: (0, h)),
            pl.BlockSpec((tq, LANES), lambda h, i: (i, h)),
            pl.BlockSpec((1, 2, tq, tk), lambda h, i: (h, 0, 0, 0)),
            pl.BlockSpec(sg.shape, lambda h, i: (0, 0)),
        ],
        out_specs=pl.BlockSpec((tq, LANES), lambda h, i: (i, h)),
        out_shape=jax.ShapeDtypeStruct((t, DA_WIDTH), BF16),
        scratch_shapes=[pltpu.VMEM((2 * tq, LANES), F32)] * 3,
        compiler_params=_params(("arbitrary", "arbitrary")),
        name="diff_attn_prompt",
    )(lamv, qn, knb, vb, za, tab, sg)


def _da_sample_kernel(lam_ref, q_ref, ck_ref, cv_ref, kn_ref, vn_ref, za_ref, tabc_ref, tabn_ref, sg_ref,
                      o_ref, m_sc, l_sc, acc_sc, *, ts, tk):
    qs = _stack_halves(q_ref[...])
    m_sc[...] = jnp.full(m_sc.shape, NEG, F32)
    l_sc[...] = jnp.zeros(l_sc.shape, F32)
    acc_sc[...] = jnp.zeros(acc_sc.shape, F32)
    past = ck_ref.shape[1]
    for j in range(past // tk):
        b = tabc_ref[0, :, j * tk:(j + 1) * tk]
        _softmax_step(qs, ck_ref[0, j * tk:(j + 1) * tk, :].astype(BF16),
                      cv_ref[0, j * tk:(j + 1) * tk, :].astype(BF16),
                      jnp.concatenate([b, b], axis=0), m_sc, l_sc, acc_sc)
    pad = jnp.zeros((LANES - ts, LANES), BF16)
    kn = jnp.concatenate([kn_ref[...].astype(BF16), pad], axis=0)
    vn = jnp.concatenate([vn_ref[...].astype(BF16), pad], axis=0)
    b = tabn_ref[0]
    _softmax_step(qs, kn, vn, jnp.concatenate([b, b], axis=0), m_sc, l_sc, acc_sc)
    o_ref[...] = _diff_finish(lam_ref, za_ref[...], sg_ref[...], l_sc, acc_sc, ts).astype(BF16)


def _da_sample(lamv, qn, ck, cv, kn, vn, za, tabc, tabn, sg, ts):
    nb, past, _ = ck.shape
    kern = functools.partial(_da_sample_kernel, ts=ts, tk=256)
    rows = pl.BlockSpec((ts, LANES), lambda b, h: (b, h))
    return pl.pallas_call(
        kern,
        grid=(nb, DA_HEADS),
        in_specs=[
            pl.BlockSpec(lamv.shape, lambda b, h: (0, 0)),
            rows,
            pl.BlockSpec((1, past, LANES), lambda b, h: (b, 0, h)),
            pl.BlockSpec((1, past, LANES), lambda b, h: (b, 0, h)),
            rows, rows, rows,
            pl.BlockSpec((1, ts, past), lambda b, h: (h, 0, 0)),
            pl.BlockSpec((1, ts, LANES), lambda b, h: (h, 0, 0)),
            pl.BlockSpec(sg.shape, lambda b, h: (0, 0)),
        ],
        out_specs=rows,
        out_shape=jax.ShapeDtypeStruct((nb * ts, DA_WIDTH), BF16),
        scratch_shapes=[pltpu.VMEM((2 * ts, LANES), F32)] * 3,
        compiler_params=_params(("arbitrary", "arbitrary")),
        name="diff_attn_sample",
    )(lamv, qn, ck, cv, kn, vn, za, tabc, tabn, sg)


def _bias_kernel(idx_ref, bt_ref, o_ref):
    idx = idx_ref[...]
    bt = bt_ref[...]
    bt = bt - bt[:, FAR_BUCKET:FAR_BUCKET + 1]
    onehot = jnp.where(lax.broadcasted_iota(jnp.int32, (N_BUCKETS, idx.shape[1]), 0) == idx, 1.0, 0.0).astype(BF16)
    hi = bt.astype(BF16)
    rest = bt - hi.astype(F32)
    mid = rest.astype(BF16)
    lo = (rest - mid.astype(F32)).astype(BF16)
    val = _dot(hi, onehot) + _dot(mid, onehot) + _dot(lo, onehot)
    o_ref[...] = jnp.where(idx < 0, NEG, val)


def _bias_tables(rel_bias, tq, ts, past):
    def buckets(rel, mask):
        return np.where(mask, _t5_bucket_np(rel), -1).astype(np.int32).reshape(-1)

    r = np.arange(tq)[:, None]
    c = np.arange(tq)[None, :]
    qpos = past + np.arange(ts)[:, None]
    kc = np.arange(past)[None, :]
    kn = past + np.arange(LANES)[None, :]
    parts = [buckets(c - r, (c // CHUNK) <= (r // CHUNK)),
             buckets(c - r - tq, np.ones((tq, tq), bool)),
             buckets(kc - qpos, (kc // CHUNK) <= (qpos // CHUNK)),
             buckets(kn - qpos, ((kn // CHUNK) <= (qpos // CHUNK)) & (kn < past + ts))]
    idx = np.concatenate(parts)
    n = idx.size
    bt = jnp.zeros((SUBLANES, N_BUCKETS), F32).at[:DA_HEADS].set(rel_bias.T)
    blk = 4096
    assert n % blk == 0
    tab = pl.pallas_call(
        _bias_kernel,
        grid=(n // blk,),
        in_specs=[pl.BlockSpec((1, blk), lambda i: (0, i)), pl.BlockSpec(bt.shape, lambda i: (0, 0))],
        out_specs=pl.BlockSpec((SUBLANES, blk), lambda i: (0, i)),
        out_shape=jax.ShapeDtypeStruct((SUBLANES, n), F32),
        compiler_params=_params(("arbitrary",)),
        name="t5_bias_tables",
    )(jnp.asarray(idx).reshape(1, n), bt)[:DA_HEADS]
    o1 = 2 * tq * tq
    o2 = o1 + ts * past
    prompt_tab = tab[:, :o1].reshape(DA_HEADS, 2, tq, tq)
    tabc = tab[:, o1:o2].reshape(DA_HEADS, ts, past)
    tabn = tab[:, o2:].reshape(DA_HEADS, ts, LANES)
    return prompt_tab, tabc, tabn


def _rwkv_kernel(prw_ref, look_ref, first_ref, zb_ref, s0_ref, mu_ref, w0_ref, a0_ref, wwa_ref,
                 kkw_ref, kaw_ref, rk_ref, lng_ref, lnb_ref, g2_ref, avg_ref, ones_ref, e_ref,
                 ob_ref, sfin_ref,
                 s_sc, r_sc, w_sc, kk_sc, kka_sc, k_sc, v_sc, o_sc, *, tm):
    j = pl.program_id(1)

    @pl.when(j == 0)
    def _():
        s_sc[...] = s0_ref[0]

    x = prw_ref[...]
    row0 = jnp.where(j == 0, first_ref[0], look_ref[7:8, :])
    rows = lax.broadcasted_iota(jnp.int32, x.shape, 0)
    prev = jnp.where(rows == 0, row0, pltpu.roll(x, 1, axis=0))
    mix = x + (prev - x) * mu_ref[...]
    r = mix[:, 0:512]
    kr = mix[:, 512:1024]
    vr = mix[:, 1024:1536]
    la = mix[:, 1536:1664]
    lane = lax.broadcasted_iota(jnp.int32, la.shape, 1)
    la = jnp.where(lane < HEAD, jnp.tanh(la), la)
    hi = la.astype(BF16)
    lo = (la - hi.astype(F32)).astype(BF16)
    wwa = wwa_ref[...]
    lora = _dot(hi, wwa[0]) + _dot(lo, wwa[0]) + _dot(hi, wwa[1])
    wpre = w0_ref[...] + lora[:, 0:512]
    nw = -wpre
    w = -(jnp.maximum(nw, 0.0) + jnp.log(1.0 + jnp.exp(-jnp.abs(nw)))) - 0.5
    w_sc[...] = jnp.exp(-jnp.exp(w))
    a = 1.0 / (1.0 + jnp.exp(-(a0_ref[...] + lora[:, 512:1024])))
    kk = kr * kkw_ref[...]
    nrm = jnp.sqrt(_dot2(kk * kk, ones_ref[...]))
    kk = kk / jnp.maximum(nrm, 1e-12)
    kk_sc[...] = kk
    kka_sc[...] = kk * a
    kh = kr * (1.0 + (a - 1.0) * kaw_ref[...])
    k_sc[...] = kh
    r_sc[...] = r
    v_sc[...] = vr

    g2 = g2_ref[...]
    e4 = e_ref[...]

    def rowvec(blk, i):
        return jnp.concatenate(
            [jnp.broadcast_to(blk[i:i + 1, c * LANES:(c + 1) * LANES], (HEAD, LANES)) for c in range(4)],
            axis=0)

    def steps(g, s):
        base = pl.multiple_of(g * SUBLANES, SUBLANES)
        kk_b, kka_b, k_b, v_b, w_b, r_b = (ref[pl.ds(base, SUBLANES), :]
                                           for ref in (kk_sc, kka_sc, k_sc, v_sc, w_sc, r_sc))
        out_rows = []
        for i in range(SUBLANES):
            sa = _dot2(s * rowvec(kk_b, i), g2)
            vcol = _dot2(rowvec(v_b, i) * e4, g2)
            s = s * rowvec(w_b, i) - sa * rowvec(kka_b, i) + vcol * rowvec(k_b, i)
            oe = _dot2(s * rowvec(r_b, i), g2) * e4
            out_rows.append(jnp.concatenate(
                [jnp.sum(oe[c * HEAD:(c + 1) * HEAD], axis=0, keepdims=True) for c in range(4)], axis=1))
        o_sc[pl.ds(base, SUBLANES), :] = jnp.concatenate(out_rows, axis=0)
        return s

    s_sc[...] = lax.fori_loop(0, tm // SUBLANES, steps, s_sc[...])

    o = o_sc[...]
    mean = _dot2(o, avg_ref[...])
    d = o - mean
    var = _dot2(d * d, avg_ref[...])
    y = d * lax.rsqrt(var + RW_GN_EPS) * lng_ref[...] + lnb_ref[...]
    bonus = _dot2(r * kh * rk_ref[...], ones_ref[...]) * vr
    ob_ref[...] = ((y + bonus) * _silu(zb_ref[...])).astype(BF16)

    @pl.when(j == pl.num_programs(1) - 1)
    def _():
        sfin_ref[0] = s_sc[...]


def _rwkv(prw, first, zb, s0, p, seq_len, tm):
    m = prw.shape[0]
    nseq = m // seq_len
    ntile = seq_len // tm
    kern = functools.partial(_rwkv_kernel, tm=tm)
    full = lambda a: pl.BlockSpec(a.shape, lambda b, j: (0,) * a.ndim)
    rows = lambda c: pl.BlockSpec((tm, c), lambda b, j: (b * ntile + j, 0))
    look = pl.BlockSpec((8, RW_SHIFT_COLS), lambda b, j: (jnp.maximum((b * ntile + j) * (tm // 8) - 1, 0), 0))
    consts = [p["mu"], p["w0"], p["a0"], p["wwa"], p["kkw"], p["kaw"], p["rk"], p["lng"], p["lnb"],
              p["g2"], p["avg"], p["ones"], p["e4"]]
    return pl.pallas_call(
        kern,
        grid=(nseq, ntile),
        in_specs=[rows(RW_SHIFT_COLS), look,
                  pl.BlockSpec((1, 1, RW_SHIFT_COLS), lambda b, j: (b, 0, 0)),
                  rows(RW_WIDTH),
                  pl.BlockSpec((1, 256, LANES), lambda b, j: (b, 0, 0))] + [full(c) for c in consts],
        out_specs=[rows(RW_WIDTH), pl.BlockSpec((1, 256, LANES), lambda b, j: (b, 0, 0))],
        out_shape=[jax.ShapeDtypeStruct((m, RW_WIDTH), BF16), jax.ShapeDtypeStruct((nseq, 256, LANES), F32)],
        scratch_shapes=[pltpu.VMEM((256, LANES), F32)] + [pltpu.VMEM((tm, RW_WIDTH), F32)] * 7,
        compiler_params=_params(("arbitrary", "arbitrary")),
        name="rwkv7",
    )(prw, prw, first, zb, s0, *consts)


def _state_to_rows(s):
    b = s.shape[0]
    return s.reshape(b, 4, 2, HEAD, HEAD).transpose(0, 1, 3, 2, 4).reshape(b, 256, LANES)


def _rows_to_state(s):
    b = s.shape[0]
    return s.reshape(b, 4, HEAD, 2, HEAD).transpose(0, 1, 3, 2, 4).reshape(b, RW_HEADS, HEAD, HEAD)


SKIP_LOG = -104.0


def _sb_tile(qs, k, v, u, c, mask):
    tk = k.shape[0]
    z = _dot_nt(qs, k)
    lg = -(jnp.maximum(z, 0.0) + jnp.log(1.0 + jnp.exp(-jnp.abs(z))))
    if mask is not None:
        lg = jnp.where(mask, lg, 0.0)
    it = _dot2(lg, u)
    a = jnp.exp(z + it[:, :tk] + c)
    if mask is not None:
        a = jnp.where(mask, a, 0.0)
    return _dot(a.astype(BF16), v), c + it[:, tk:]


def _sb_finish(acc, z):
    t = z.shape[0]
    lane = lax.broadcasted_iota(jnp.int32, z.shape, 1)
    return jnp.where(lane < HEAD, acc[:t], acc[t:]) * _silu(z)


def _sb_prompt_kernel(q_ref, k_ref, v_ref, z_ref, u_ref, o_ref, acc_sc, c_sc, *, tq):
    i = pl.program_id(1)
    tk = LANES
    r = tq // tk
    qs = _stack_halves(q_ref[...])
    u = u_ref[...]
    acc_sc[...] = jnp.zeros(acc_sc.shape, F32)
    c_sc[...] = jnp.zeros(c_sc.shape, F32)
    qrow = lax.broadcasted_iota(jnp.int32, (2 * tq, tk), 0) % tq
    col = lax.broadcasted_iota(jnp.int32, (2 * tq, tk), 1)

    def tile(j, mask):
        off = pl.multiple_of(j * tk, tk)
        da, c = _sb_tile(qs, k_ref[pl.ds(off, tk), :], v_ref[pl.ds(off, tk), :], u, c_sc[...], mask)
        acc_sc[...] += da
        c_sc[...] = c

    for d in range(r):
        jj = r - 1 - d
        tile(i * r + jj, (col + jj * tk) < qrow)

    def body(st):
        tile(i * r - 1 - st[0], None)
        return st[0] + 1, jnp.max(c_sc[...])

    lax.while_loop(lambda st: jnp.logical_and(st[0] < i * r, st[1] > SKIP_LOG), body,
                   (jnp.int32(0), jnp.max(c_sc[...])))
    o_ref[...] = _sb_finish(acc_sc[...], z_ref[...]).astype(BF16)


def _sb_prompt(qb, kb, vb, z, u, tq):
    t = qb.shape[0]
    kern = functools.partial(_sb_prompt_kernel, tq=tq)
    rows = pl.BlockSpec((tq, LANES), lambda h, i: (i, h))
    whole = pl.BlockSpec((t, LANES), lambda h, i: (0, h))
    return pl.pallas_call(
        kern,
        grid=(SB_HEADS // 2, t // tq),
        in_specs=[rows, whole, whole, rows, pl.BlockSpec(u.shape, lambda h, i: (0, 0))],
        out_specs=rows,
        out_shape=jax.ShapeDtypeStruct((t, SB_WIDTH), BF16),
        scratch_shapes=[pltpu.VMEM((2 * tq, LANES), F32)] * 2,
        compiler_params=_params(("arbitrary", "arbitrary")),
        name="stick_breaking_prompt",
    )(qb, kb, vb, z, u)


def _sb_sample_kernel(q_ref, ck_ref, cv_ref, kn_ref, vn_ref, z_ref, u_ref, o_ref, acc_sc, c_sc, *, ts):
    tk = LANES
    ntile = ck_ref.shape[1] // tk
    qs = _stack_halves(q_ref[...])
    u = u_ref[...]
    pad = jnp.zeros((tk - ts, LANES), BF16)
    kn = jnp.concatenate([kn_ref[...], pad], axis=0)
    vn = jnp.concatenate([vn_ref[...], pad], axis=0)
    qrow = lax.broadcasted_iota(jnp.int32, (2 * ts, tk), 0) % ts
    col = lax.broadcasted_iota(jnp.int32, (2 * ts, tk), 1)
    acc, c = _sb_tile(qs, kn, vn, u, jnp.zeros((2 * ts, LANES), F32), col < qrow)
    acc_sc[...] = acc
    c_sc[...] = c

    def body(st):
        off = pl.multiple_of((ntile - 1 - st[0]) * tk, tk)
        da, c = _sb_tile(qs, ck_ref[0, pl.ds(off, tk), :].astype(BF16),
                         cv_ref[0, pl.ds(off, tk), :].astype(BF16), u, c_sc[...], None)
        acc_sc[...] += da
        c_sc[...] = c
        return st[0] + 1, jnp.max(c)

    lax.while_loop(lambda st: jnp.logical_and(st[0] < ntile, st[1] > SKIP_LOG), body,
                   (jnp.int32(0), jnp.max(c)))
    o_ref[...] = _sb_finish(acc_sc[...], z_ref[...]).astype(BF16)


def _sb_sample(qb, ck, cv, kb, vb, z, u, ts):
    nb, past, _ = ck.shape
    kern = functools.partial(_sb_sample_kernel, ts=ts)
    rows = pl.BlockSpec((ts, LANES), lambda b, h: (b, h))
    cache = pl.BlockSpec((1, past, LANES), lambda b, h: (b, 0, h))
    return pl.pallas_call(
        kern,
        grid=(nb, SB_HEADS // 2),
        in_specs=[rows, cache, cache, rows, rows, rows, pl.BlockSpec(u.shape, lambda b, h: (0, 0))],
        out_specs=rows,
        out_shape=jax.ShapeDtypeStruct((nb * ts, SB_WIDTH), BF16),
        scratch_shapes=[pltpu.VMEM((2 * ts, LANES), F32)] * 2,
        compiler_params=_params(("arbitrary", "arbitrary")),
        name="stick_breaking_sample",
    )(qb, ck, cv, kb, vb, z, u)


def _suffix_matrix():
    j = np.arange(LANES)[:, None]
    s = np.arange(LANES)[None, :]
    u = (j >= s).astype(np.float32)
    return jnp.asarray(np.concatenate([u, np.ones((LANES, LANES), np.float32)], axis=1), dtype=BF16)


def _row_tile(m):
    return 256 if m % 256 == 0 else m


def kernel(x_prompt, x_sample, cache_l0_k, cache_l0_v, state_l0_shift, state_l0_wkv, cache_l1_k, cache_l1_v, rel_bias, norm_l0, w_in_l0, w_out_l0, da_q_norm, da_k_norm, da_lambda_q1, da_lambda_k1, da_lambda_q2, da_lambda_k2, da_subln, rw_mu, rw_w0, rw_w_up, rw_a0, rw_a_up, rw_k_k, rw_k_a, rw_r_k, rw_lnx_g, rw_lnx_b, norm_l1, w_in_l1, w_out_l1):
    bp, t, d = x_prompt.shape
    nb, ts, _ = x_sample.shape
    past = cache_l0_k.shape[1]
    assert bp == 1 and ts <= LANES and t % 256 == 0 and past % 256 == 0

    row = lambda v: v.reshape(1, -1).astype(F32)
    w_in0 = w_in_l0.astype(BF16)
    w_in1 = w_in_l1.astype(BF16)
    w_out0a = w_out_l0[:DA_WIDTH].astype(BF16)
    w_out0b = w_out_l0[DA_WIDTH:].astype(BF16)
    w_out1 = w_out_l1.astype(BF16)
    qg = row(jnp.tile(da_q_norm, 512 // HEAD))
    kg = row(jnp.tile(da_k_norm, 512 // HEAD))
    lamv = jnp.stack([da_lambda_q1, da_lambda_k1, da_lambda_q2, da_lambda_k2]).astype(F32)
    sg = row(da_subln)
    zeros = jnp.zeros((HEAD, RW_WIDTH), F32)
    wwa_f = jnp.concatenate([jnp.concatenate([rw_w_up, zeros], axis=1),
                             jnp.concatenate([zeros, rw_a_up], axis=1)], axis=0)
    wwa_hi = wwa_f.astype(BF16)
    wwa = jnp.stack([wwa_hi, (wwa_f - wwa_hi.astype(F32)).astype(BF16)])
    e4 = np.equal(np.arange(256)[:, None] % HEAD, np.arange(LANES)[None, :] % HEAD).astype(np.float32)
    rw = dict(mu=row(rw_mu), w0=row(rw_w0), a0=row(rw_a0), wwa=wwa, kkw=row(rw_k_k), kaw=row(rw_k_a),
              rk=row(rw_r_k), lng=row(rw_lnx_g), lnb=row(rw_lnx_b), g2=_group_matrix(LANES, 1.0),
              avg=_group_matrix(RW_WIDTH, 1.0 / HEAD), ones=_group_matrix(RW_WIDTH, 1.0), e4=jnp.asarray(e4))
    tq_da = 256
    prompt_tab, tabc, tabn = _bias_tables(rel_bias.astype(F32), tq_da, ts, past)
    u = _suffix_matrix()

    def layer0(x, first, s0, seq_len, attend):
        m = x.shape[0]
        tm = _row_tile(m)
        qn, kn, knb, v, vb, za, prw, zb = _inproj0(x, row(norm_l0), w_in0, qg, kg, tm)
        oa = attend(qn, kn, knb, v, vb, za)
        ob, sfin = _rwkv(prw, first, zb, _state_to_rows(s0), rw, seq_len, min(seq_len, 256))
        y = _outproj(x, [oa, ob], [w_out0a, w_out0b], tm)
        shift = prw.reshape(m // seq_len, seq_len, RW_SHIFT_COLS)[:, -1:]
        return y, kn, v, shift, _rows_to_state(sfin)

    def layer1(x, attend):
        m = x.shape[0]
        tm = _row_tile(m)
        qb, k, kb, v, vb, z = _inproj1(x, row(norm_l1), w_in1, tm)
        o = attend(qb, kb, vb, z)
        return _outproj(x, [o], [w_out1], tm), k, v

    xp = x_prompt.reshape(t, d)
    yp, k0p, v0p, shp, wkvp = layer0(
        xp, jnp.zeros((1, 1, RW_SHIFT_COLS), F32), jnp.zeros((1, RW_HEADS, HEAD, HEAD), F32), t,
        lambda qn, kn, knb, v, vb, za: _da_prompt(lamv, qn, knb, vb, za, prompt_tab, sg, tq_da))
    yp, k1p, v1p = layer1(yp, lambda qb, kb, vb, z: _sb_prompt(qb, kb, vb, z, u, 512))

    xs = x_sample.reshape(nb * ts, d)
    ck0 = cache_l0_k.reshape(nb, past, DA_WIDTH)
    cv0 = cache_l0_v.reshape(nb, past, DA_WIDTH)
    ys, k0s, v0s, shs, wkvs = layer0(
        xs, state_l0_shift, state_l0_wkv, ts,
        lambda qn, kn, knb, v, vb, za: _da_sample(lamv, qn, ck0, cv0, kn, v, za, tabc, tabn, sg, ts))
    ck1 = cache_l1_k.reshape(nb, past, SB_WIDTH)
    cv1 = cache_l1_v.reshape(nb, past, SB_WIDTH)
    ys, k1s, v1s = layer1(ys, lambda qb, kb, vb, z: _sb_sample(qb, ck1, cv1, kb, vb, z, u, ts))

    return (yp.reshape(1, t, d), ys.reshape(nb, ts, d),
            k0p.reshape(1, t, DA_HEADS, LANES), v0p.reshape(1, t, DA_HEADS, LANES), shp, wkvp,
            k1p.reshape(1, t, SB_HEADS, HEAD), v1p.reshape(1, t, SB_HEADS, HEAD),
            k0s.reshape(nb, ts, DA_HEADS, LANES), v0s.reshape(nb, ts, DA_HEADS, LANES), shs, wkvs,
            k1s.reshape(nb, ts, SB_HEADS, HEAD), v1s.reshape(nb, ts, SB_HEADS, HEAD))
```

```python
import functools
import math

import numpy as np
import jax
import jax.numpy as jnp
from jax import lax
from jax.experimental import pallas as pl
from jax.experimental.pallas import tpu as pltpu

F32 = jnp.float32
BF16 = jnp.bfloat16

EPS = 1e-6
NEG = -1e30
CHUNK = 64
LANES = 128
SUBLANES = 8
HEAD = 64
DA_HEADS = 4
DA_WIDTH = 512
RW_WIDTH = 512
RW_HEADS = 8
RW_SHIFT_COLS = 3 * RW_WIDTH + 128
RW_GN_EPS = 64e-5
RW_STATE_LANES = 256
RW_STATE_GROUPS = RW_WIDTH // RW_STATE_LANES
RW_STATE_SHAPE = (RW_STATE_GROUPS * HEAD, RW_STATE_LANES)
SB_HEADS = 16
SB_WIDTH = 1024
N_BUCKETS = 32
MAX_DISTANCE = 128
LAMBDA_INIT = 0.8 - 0.6 * math.exp(-0.3 * 0)
LOG2E = math.log2(math.e)
VMEM_LIMIT = 56 * 1024 * 1024


def _params(sem):
    return pltpu.CompilerParams(dimension_semantics=sem, vmem_limit_bytes=VMEM_LIMIT)


def _dot(a, b):
    return jnp.dot(a, b, preferred_element_type=F32)


def _dot_nt(a, b):
    return lax.dot_general(a, b, (((1,), (1,)), ((), ())), preferred_element_type=F32)


def _dot2(x, w):
    hi = x.astype(BF16)
    lo = (x - hi.astype(F32)).astype(BF16)
    return _dot(hi, w) + _dot(lo, w)


def _silu(z):
    return z / (1.0 + jnp.exp(-z))


def _rms(x, g):
    return x * lax.rsqrt(jnp.mean(x * x, axis=-1, keepdims=True) + EPS) * g


def _group_matrix(n, scale):
    idx = np.arange(n) // HEAD
    return jnp.asarray((idx[:, None] == idx[None, :]).astype(np.float32) * scale, dtype=BF16)


def _inproj0_kernel(x_ref, g_ref, w_ref, qg_ref, kg_ref, avg_ref,
                    qn_ref, kn_ref, knb_ref, v_ref, vb_ref, za_ref, prw_ref, zb_ref):
    y = _dot(_rms(x_ref[...], g_ref[...]).astype(BF16), w_ref[...])

    def head_norm(t, g):
        return t * lax.rsqrt(_dot2(t * t, avg_ref[...]) + EPS) * g

    qn_ref[...] = (head_norm(y[:, 0:512], qg_ref[...]) * (HEAD ** -0.5 * LOG2E)).astype(BF16)
    kn = head_norm(y[:, 512:1024], kg_ref[...])
    kn_ref[...] = kn
    knb_ref[...] = kn.astype(BF16)
    v = y[:, 1024:1536]
    v_ref[...] = v
    vb_ref[...] = v.astype(BF16)
    za_ref[...] = y[:, 1536:2048]
    prw_ref[...] = y[:, 2048:2048 + RW_SHIFT_COLS]
    zb_ref[...] = y[:, 2048 + RW_SHIFT_COLS:]


def _inproj0(x, g, w, qg, kg, tm):
    m, d = x.shape
    n = w.shape[1]
    row = lambda c: pl.BlockSpec((tm, c), lambda i: (i, 0))
    full = lambda a: pl.BlockSpec(a.shape, lambda i: (0,) * a.ndim)
    avg = _group_matrix(512, 1.0 / HEAD)
    widths = [(512, BF16), (512, F32), (512, BF16), (512, F32), (512, BF16), (512, F32),
              (RW_SHIFT_COLS, F32), (512, F32)]
    return pl.pallas_call(
        _inproj0_kernel,
        grid=(m // tm,),
        in_specs=[row(d), full(g), full(w), full(qg), full(kg), full(avg)],
        out_specs=[row(c) for c, _ in widths],
        out_shape=[jax.ShapeDtypeStruct((m, c), dt) for c, dt in widths],
        compiler_params=_params(("arbitrary",)),
        name="inproj0",
    )(x, g, w, qg, kg, avg)


def _inproj1_kernel(x_ref, g_ref, w_ref, qb_ref, k_ref, kb_ref, v_ref, vb_ref, z_ref):
    y = _dot(_rms(x_ref[...], g_ref[...]).astype(BF16), w_ref[...])
    qb_ref[...] = (y[:, 0:1024] * (HEAD ** -0.5)).astype(BF16)
    k = y[:, 1024:2048]
    k_ref[...] = k
    kb_ref[...] = k.astype(BF16)
    v = y[:, 2048:3072]
    v_ref[...] = v
    vb_ref[...] = v.astype(BF16)
    z_ref[...] = y[:, 3072:4096]


def _inproj1(x, g, w, tm):
    m, d = x.shape
    row = lambda c: pl.BlockSpec((tm, c), lambda i: (i, 0))
    full = lambda a: pl.BlockSpec(a.shape, lambda i: (0,) * a.ndim)
    dts = [BF16, F32, BF16, F32, BF16, F32]
    return pl.pallas_call(
        _inproj1_kernel,
        grid=(m // tm,),
        in_specs=[row(d), full(g), full(w)],
        out_specs=[row(1024) for _ in dts],
        out_shape=[jax.ShapeDtypeStruct((m, 1024), dt) for dt in dts],
        compiler_params=_params(("arbitrary",)),
        name="inproj1",
    )(x, g, w)


def _outproj_kernel(*refs):
    n = (len(refs) - 2) // 2
    x_ref, o_ref = refs[0], refs[-1]
    acc = x_ref[...]
    for a_ref, w_ref in zip(refs[1:1 + n], refs[1 + n:1 + 2 * n]):
        acc = acc + _dot(a_ref[...], w_ref[...])
    o_ref[...] = acc


def _outproj(x, acts, ws, tm):
    m, d = x.shape
    row = lambda c: pl.BlockSpec((tm, c), lambda i: (i, 0))
    full = lambda a: pl.BlockSpec(a.shape, lambda i: (0,) * a.ndim)
    return pl.pallas_call(
        _outproj_kernel,
        grid=(m // tm,),
        in_specs=[row(d)] + [row(a.shape[1]) for a in acts] + [full(w) for w in ws],
        out_specs=row(d),
        out_shape=jax.ShapeDtypeStruct((m, d), F32),
        compiler_params=_params(("arbitrary",)),
        name="outproj",
    )(x, *acts, *ws)


def _t5_bucket_np(rel):
    nb = N_BUCKETS // 2
    max_exact = nb // 2
    n = np.abs(rel)
    nf = np.maximum(n, 1).astype(np.float32)
    large = max_exact + (np.log(nf / np.float32(max_exact)) / np.float32(math.log(MAX_DISTANCE / max_exact))
                         * np.float32(nb - max_exact)).astype(np.int32)
    large = np.minimum(large, nb - 1)
    return np.where(rel > 0, nb, 0) + np.where(n < max_exact, n, large)


FAR_BUCKET = N_BUCKETS // 2 - 1


def _stack_halves(q):
    lane = lax.broadcasted_iota(jnp.int32, q.shape, 1)
    zero = jnp.zeros_like(q)
    return jnp.concatenate([jnp.where(lane < HEAD, q, zero), jnp.where(lane >= HEAD, q, zero)], axis=0)


def _softmax_step(qs, k, v, bias, m_sc, l_sc, acc_sc):
    s = _dot_nt(qs, k)
    if bias is not None:
        s = s + bias
    reps = s.shape[1] // LANES
    m_old = m_sc[...]
    m_new = jnp.maximum(m_old, jnp.max(s, axis=-1, keepdims=True))
    alpha = jnp.exp2(m_old - m_new)
    p = jnp.exp2(s - jnp.concatenate([m_new] * reps, axis=1))
    psum = p[:, :LANES]
    for r in range(1, reps):
        psum = psum + p[:, r * LANES:(r + 1) * LANES]
    l_sc[...] = alpha * l_sc[...] + psum
    acc_sc[...] = alpha * acc_sc[...] + _dot(p.astype(BF16), v)
    m_sc[...] = m_new


def _diff_finish(lam_ref, za, sg, l_sc, acc_sc, tq):
    lv = lam_ref[...]
    lam = (jnp.exp(jnp.sum(lv[0:1] * lv[1:2], axis=-1, keepdims=True))
           - jnp.exp(jnp.sum(lv[2:3] * lv[3:4], axis=-1, keepdims=True)) + LAMBDA_INIT)
    o = acc_sc[...] / jnp.sum(l_sc[...], axis=-1, keepdims=True)
    attn = o[:tq] - lam * o[tq:]
    return (_rms(attn, sg) * (1.0 - LAMBDA_INIT)) * _silu(za)


def _da_prompt_kernel(lam_ref, q_ref, k_ref, v_ref, za_ref, tab_ref, sg_ref, o_ref,
                      m_sc, l_sc, acc_sc, *, tq, tk):
    i = pl.program_id(1)
    qs = _stack_halves(q_ref[...])
    m_sc[...] = jnp.full(m_sc.shape, NEG, F32)
    l_sc[...] = jnp.zeros(l_sc.shape, F32)
    acc_sc[...] = jnp.zeros(acc_sc.shape, F32)

    def tile(j, width, bias):
        off = pl.multiple_of(j * tk, tk)
        _softmax_step(qs, k_ref[pl.ds(off, width), :], v_ref[pl.ds(off, width), :], bias, m_sc, l_sc, acc_sc)

    nfar = jnp.maximum(i - 1, 0)

    def far(j, c):
        tile(2 * j, 2 * tk, None)
        return c

    lax.fori_loop(0, nfar // 2, far, 0)

    @pl.when(nfar % 2 == 1)
    def _():
        tile(nfar - 1, tk, None)

    @pl.when(i >= 1)
    def _():
        b = tab_ref[0, 1]
        tile(i - 1, tk, jnp.concatenate([b, b], axis=0))

    b = tab_ref[0, 0]
    tile(i, tk, jnp.concatenate([b, b], axis=0))
    o_ref[...] = _diff_finish(lam_ref, za_ref[...], sg_ref[...], l_sc, acc_sc, tq).astype(BF16)


def _da_prompt(lamv, qn, knb, vb, za, tab, sg, tq):
    t = qn.shape[0]
    tk = tq
    kern = functools.partial(_da_prompt_kernel, tq=tq, tk=tk)
    return pl.pallas_call(
        kern,
        grid=(DA_HEADS, t // tq),
        in_specs=[
            pl.BlockSpec(lamv.shape, lambda h, i: (0, 0)),
            pl.BlockSpec((tq, LANES), lambda h, i: (i, h)),
            pl.BlockSpec((t, LANES), lambda h, i: (0, h)),
            pl.BlockSpec((t, LANES), lambda h, i: (0, h)),
            pl.BlockSpec((tq, LANES), lambda h, i: (i, h)),
            pl.BlockSpec((1, 2, tq, tk), lambda h, i: (h, 0, 0, 0)),
            pl.BlockSpec(sg.shape, lambda h, i: (0, 0)),
        ],
        out_specs=pl.BlockSpec((tq, LANES), lambda h, i: (i, h)),
        out_shape=jax.ShapeDtypeStruct((t, DA_WIDTH), BF16),
        scratch_shapes=[pltpu.VMEM((2 * tq, LANES), F32)] * 3,
        compiler_params=_params(("arbitrary", "arbitrary")),
        name="diff_attn_prompt",
    )(lamv, qn, knb, vb, za, tab, sg)


def _da_sample_kernel(lam_ref, q_ref, ck_ref, cv_ref, kn_ref, vn_ref, za_ref, tabc_ref, tabn_ref, sg_ref,
                      o_ref, m_sc, l_sc, acc_sc, *, ts, tk):
    qs = _stack_halves(q_ref[...])
    m_sc[...] = jnp.full(m_sc.shape, NEG, F32)
    l_sc[...] = jnp.zeros(l_sc.shape, F32)
    acc_sc[...] = jnp.zeros(acc_sc.shape, F32)
    past = ck_ref.shape[1]
    for j in range(past // tk):
        b = tabc_ref[0, :, j * tk:(j + 1) * tk]
        _softmax_step(qs, ck_ref[0, j * tk:(j + 1) * tk, :].astype(BF16),
                      cv_ref[0, j * tk:(j + 1) * tk, :].astype(BF16),
                      jnp.concatenate([b, b], axis=0), m_sc, l_sc, acc_sc)
    pad = jnp.zeros((LANES - ts, LANES), BF16)
    kn = jnp.concatenate([kn_ref[...].astype(BF16), pad], axis=0)
    vn = jnp.concatenate([vn_ref[...].astype(BF16), pad], axis=0)
    b = tabn_ref[0]
    _softmax_step(qs, kn, vn, jnp.concatenate([b, b], axis=0), m_sc, l_sc, acc_sc)
    o_ref[...] = _diff_finish(lam_ref, za_ref[...], sg_ref[...], l_sc, acc_sc, ts).astype(BF16)


def _da_sample(lamv, qn, ck, cv, kn, vn, za, tabc, tabn, sg, ts):
    nb, past, _ = ck.shape
    kern = functools.partial(_da_sample_kernel, ts=ts, tk=256)
    rows = pl.BlockSpec((ts, LANES), lambda b, h: (b, h))
    return pl.pallas_call(
        kern,
        grid=(nb, DA_HEADS),
        in_specs=[
            pl.BlockSpec(lamv.shape, lambda b, h: (0, 0)),
            rows,
            pl.BlockSpec((1, past, LANES), lambda b, h: (b, 0, h)),
            pl.BlockSpec((1, past, LANES), lambda b, h: (b, 0, h)),
            rows, rows, rows,
            pl.BlockSpec((1, ts, past), lambda b, h: (h, 0, 0)),
            pl.BlockSpec((1, ts, LANES), lambda b, h: (h, 0, 0)),
            pl.BlockSpec(sg.shape, lambda b, h: (0, 0)),
        ],
        out_specs=rows,
        out_shape=jax.ShapeDtypeStruct((nb * ts, DA_WIDTH), BF16),
        scratch_shapes=[pltpu.VMEM((2 * ts, LANES), F32)] * 3,
        compiler_params=_params(("arbitrary", "arbitrary")),
        name="diff_attn_sample",
    )(lamv, qn, ck, cv, kn, vn, za, tabc, tabn, sg)


def _bias_kernel(idx_ref, bt_ref, o_ref):
    idx = idx_ref[...]
    bt = bt_ref[...]
    bt = bt - bt[:, FAR_BUCKET:FAR_BUCKET + 1]
    onehot = jnp.where(lax.broadcasted_iota(jnp.int32, (N_BUCKETS, idx.shape[1]), 0) == idx, 1.0, 0.0).astype(BF16)
    hi = bt.astype(BF16)
    rest = bt - hi.astype(F32)
    mid = rest.astype(BF16)
    lo = (rest - mid.astype(F32)).astype(BF16)
    val = _dot(hi, onehot) + _dot(mid, onehot) + _dot(lo, onehot)
    o_ref[...] = jnp.where(idx < 0, NEG, val * LOG2E)


def _bias_tables(rel_bias, tq, ts, past):
    def buckets(rel, mask):
        return np.where(mask, _t5_bucket_np(rel), -1).astype(np.int32).reshape(-1)

    r = np.arange(tq)[:, None]
    c = np.arange(tq)[None, :]
    qpos = past + np.arange(ts)[:, None]
    kc = np.arange(past)[None, :]
    kn = past + np.arange(LANES)[None, :]
    parts = [buckets(c - r, (c // CHUNK) <= (r // CHUNK)),
             buckets(c - r - tq, np.ones((tq, tq), bool)),
             buckets(kc - qpos, (kc // CHUNK) <= (qpos // CHUNK)),
             buckets(kn - qpos, ((kn // CHUNK) <= (qpos // CHUNK)) & (kn < past + ts))]
    idx = np.concatenate(parts)
    n = idx.size
    bt = jnp.zeros((SUBLANES, N_BUCKETS), F32).at[:DA_HEADS].set(rel_bias.T)
    blk = 4096
    assert n % blk == 0
    tab = pl.pallas_call(
        _bias_kernel,
        grid=(n // blk,),
        in_specs=[pl.BlockSpec((1, blk), lambda i: (0, i)), pl.BlockSpec(bt.shape, lambda i: (0, 0))],
        out_specs=pl.BlockSpec((SUBLANES, blk), lambda i: (0, i)),
        out_shape=jax.ShapeDtypeStruct((SUBLANES, n), F32),
        compiler_params=_params(("arbitrary",)),
        name="t5_bias_tables",
    )(jnp.asarray(idx).reshape(1, n), bt)[:DA_HEADS]
    o1 = 2 * tq * tq
    o2 = o1 + ts * past
    prompt_tab = tab[:, :o1].reshape(DA_HEADS, 2, tq, tq)
    tabc = tab[:, o1:o2].reshape(DA_HEADS, ts, past)
    tabn = tab[:, o2:].reshape(DA_HEADS, ts, LANES)
    return prompt_tab, tabc, tabn


def _rwkv_kernel(prw_ref, look_ref, first_ref, zb_ref, s0_ref, mu_ref, w0_ref, a0_ref, wwa_ref,
                 kkw_ref, kaw_ref, rk_ref, lng_ref, lnb_ref, g2_ref, avg_ref, ones_ref, e_ref,
                 ob_ref, sfin_ref,
                 s_sc, r_sc, w_sc, kk_sc, kka_sc, k_sc, v_sc, o_sc, *, tm):
    j = pl.program_id(1)

    @pl.when(j == 0)
    def _():
        s_sc[...] = s0_ref[0]

    x = prw_ref[...]
    row0 = jnp.where(j == 0, first_ref[0], look_ref[7:8, :])
    rows = lax.broadcasted_iota(jnp.int32, x.shape, 0)
    prev = jnp.where(rows == 0, row0, pltpu.roll(x, 1, axis=0))
    mix = x + (prev - x) * mu_ref[...]
    r = mix[:, 0:512]
    kr = mix[:, 512:1024]
    vr = mix[:, 1024:1536]
    la = mix[:, 1536:1664]
    lane = lax.broadcasted_iota(jnp.int32, la.shape, 1)
    la = jnp.where(lane < HEAD, jnp.tanh(la), la)
    hi = la.astype(BF16)
    lo = (la - hi.astype(F32)).astype(BF16)
    wwa = wwa_ref[...]
    lora = _dot(hi, wwa[0]) + _dot(lo, wwa[0]) + _dot(hi, wwa[1])
    wpre = w0_ref[...] + lora[:, 0:512]
    nw = -wpre
    w = -(jnp.maximum(nw, 0.0) + jnp.log(1.0 + jnp.exp(-jnp.abs(nw)))) - 0.5
    w_sc[...] = jnp.exp(-jnp.exp(w))
    a = 1.0 / (1.0 + jnp.exp(-(a0_ref[...] + lora[:, 512:1024])))
    kk = kr * kkw_ref[...]
    nrm = jnp.sqrt(_dot2(kk * kk, ones_ref[...]))
    kk = kk / jnp.maximum(nrm, 1e-12)
    kk_sc[...] = kk
    kka_sc[...] = kk * a
    kh = kr * (1.0 + (a - 1.0) * kaw_ref[...])
    k_sc[...] = kh
    r_sc[...] = r
    v_sc[...] = vr

    g2 = g2_ref[...]
    e4 = e_ref[...]

    def rowvec(blk, i):
        return jnp.concatenate(
            [jnp.broadcast_to(blk[i:i + 1, c * RW_STATE_LANES:(c + 1) * RW_STATE_LANES], (HEAD, RW_STATE_LANES))
             for c in range(RW_STATE_GROUPS)], axis=0)

    def steps(g, s):
        base = pl.multiple_of(g * SUBLANES, SUBLANES)
        kk_b, kka_b, k_b, v_b, w_b, r_b = (ref[pl.ds(base, SUBLANES), :]
                                           for ref in (kk_sc, kka_sc, k_sc, v_sc, w_sc, r_sc))
        out_rows = []
        for i in range(SUBLANES):
            sa = _dot2(s * rowvec(kk_b, i), g2)
            vcol = _dot2(rowvec(v_b, i) * e4, g2)
            s = s * rowvec(w_b, i) - sa * rowvec(kka_b, i) + vcol * rowvec(k_b, i)
            oe = _dot2(s * rowvec(r_b, i), g2) * e4
            out_rows.append(jnp.concatenate(
                [jnp.sum(oe[c * HEAD:(c + 1) * HEAD], axis=0, keepdims=True) for c in range(RW_STATE_GROUPS)],
                axis=1))
        o_sc[pl.ds(base, SUBLANES), :] = jnp.concatenate(out_rows, axis=0)
        return s

    s_sc[...] = lax.fori_loop(0, tm // SUBLANES, steps, s_sc[...])

    o = o_sc[...]
    mean = _dot2(o, avg_ref[...])
    d = o - mean
    var = _dot2(d * d, avg_ref[...])
    y = d * lax.rsqrt(var + RW_GN_EPS) * lng_ref[...] + lnb_ref[...]
    bonus = _dot2(r * kh * rk_ref[...], ones_ref[...]) * vr
    ob_ref[...] = ((y + bonus) * _silu(zb_ref[...])).astype(BF16)

    @pl.when(j == pl.num_programs(1) - 1)
    def _():
        sfin_ref[0] = s_sc[...]


def _rwkv(prw, first, zb, s0, p, seq_len, tm):
    m = prw.shape[0]
    nseq = m // seq_len
    ntile = seq_len // tm
    kern = functools.partial(_rwkv_kernel, tm=tm)
    full = lambda a: pl.BlockSpec(a.shape, lambda b, j: (0,) * a.ndim)
    rows = lambda c: pl.BlockSpec((tm, c), lambda b, j: (b * ntile + j, 0))
    look = pl.BlockSpec((8, RW_SHIFT_COLS), lambda b, j: (jnp.maximum((b * ntile + j) * (tm // 8) - 1, 0), 0))
    consts = [p["mu"], p["w0"], p["a0"], p["wwa"], p["kkw"], p["kaw"], p["rk"], p["lng"], p["lnb"],
              p["g2"], p["avg"], p["ones"], p["e4"]]
    return pl.pallas_call(
        kern,
        grid=(nseq, ntile),
        in_specs=[rows(RW_SHIFT_COLS), look,
                  pl.BlockSpec((1, 1, RW_SHIFT_COLS), lambda b, j: (b, 0, 0)),
                  rows(RW_WIDTH),
                  pl.BlockSpec((1,) + RW_STATE_SHAPE, lambda b, j: (b, 0, 0))] + [full(c) for c in consts],
        out_specs=[rows(RW_WIDTH), pl.BlockSpec((1,) + RW_STATE_SHAPE, lambda b, j: (b, 0, 0))],
        out_shape=[jax.ShapeDtypeStruct((m, RW_WIDTH), BF16), jax.ShapeDtypeStruct((nseq,) + RW_STATE_SHAPE, F32)],
        scratch_shapes=[pltpu.VMEM(RW_STATE_SHAPE, F32)] + [pltpu.VMEM((tm, RW_WIDTH), F32)] * 7,
        compiler_params=_params(("arbitrary", "arbitrary")),
        name="rwkv7",
    )(prw, prw, first, zb, s0, *consts)


def _state_to_rows(s):
    b = s.shape[0]
    per = RW_HEADS // RW_STATE_GROUPS
    return s.reshape(b, RW_STATE_GROUPS, per, HEAD, HEAD).transpose(0, 1, 3, 2, 4).reshape((b,) + RW_STATE_SHAPE)


def _rows_to_state(s):
    b = s.shape[0]
    per = RW_HEADS // RW_STATE_GROUPS
    return s.reshape(b, RW_STATE_GROUPS, HEAD, per, HEAD).transpose(0, 1, 3, 2, 4).reshape(b, RW_HEADS, HEAD, HEAD)


SKIP_LOG = -104.0


def _sb_tile(qs, k, v, u, c, mask):
    tk = k.shape[0]
    z = _dot_nt(qs, k)
    lg = -(jnp.maximum(z, 0.0) + jnp.log(1.0 + jnp.exp(-jnp.abs(z))))
    if mask is not None:
        lg = jnp.where(mask, lg, 0.0)
    it = _dot2(lg, u)
    a = jnp.exp(z + it[:, :tk] + c)
    if mask is not None:
        a = jnp.where(mask, a, 0.0)
    return _dot(a.astype(BF16), v), c + it[:, tk:]


def _sb_finish(acc, z):
    t = z.shape[0]
    lane = lax.broadcasted_iota(jnp.int32, z.shape, 1)
    return jnp.where(lane < HEAD, acc[:t], acc[t:]) * _silu(z)


def _sb_prompt_kernel(q_ref, k_ref, v_ref, z_ref, u_ref, o_ref, acc_sc, c_sc, *, tq):
    i = pl.program_id(1)
    tk = LANES
    r = tq // tk
    qs = _stack_halves(q_ref[...])
    u = u_ref[...]
    acc_sc[...] = jnp.zeros(acc_sc.shape, F32)
    c_sc[...] = jnp.zeros(c_sc.shape, F32)
    qrow = lax.broadcasted_iota(jnp.int32, (2 * tq, tk), 0) % tq
    col = lax.broadcasted_iota(jnp.int32, (2 * tq, tk), 1)

    def tile(j, mask):
        off = pl.multiple_of(j * tk, tk)
        da, c = _sb_tile(qs, k_ref[pl.ds(off, tk), :], v_ref[pl.ds(off, tk), :], u, c_sc[...], mask)
        acc_sc[...] += da
        c_sc[...] = c

    for d in range(r):
        jj = r - 1 - d
        tile(i * r + jj, (col + jj * tk) < qrow)

    def body(st):
        tile(i * r - 1 - st[0], None)
        return st[0] + 1, jnp.max(c_sc[...])

    lax.while_loop(lambda st: jnp.logical_and(st[0] < i * r, st[1] > SKIP_LOG), body,
                   (jnp.int32(0), jnp.max(c_sc[...])))
    o_ref[...] = _sb_finish(acc_sc[...], z_ref[...]).astype(BF16)


def _sb_prompt(qb, kb, vb, z, u, tq):
    t = qb.shape[0]
    kern = functools.partial(_sb_prompt_kernel, tq=tq)
    rows = pl.BlockSpec((tq, LANES), lambda h, i: (i, h))
    whole = pl.BlockSpec((t, LANES), lambda h, i: (0, h))
    return pl.pallas_call(
        kern,
        grid=(SB_HEADS // 2, t // tq),
        in_specs=[rows, whole, whole, rows, pl.BlockSpec(u.shape, lambda h, i: (0, 0))],
        out_specs=rows,
        out_shape=jax.ShapeDtypeStruct((t, SB_WIDTH), BF16),
        scratch_shapes=[pltpu.VMEM((2 * tq, LANES), F32)] * 2,
        compiler_params=_params(("arbitrary", "arbitrary")),
        name="stick_breaking_prompt",
    )(qb, kb, vb, z, u)


def _sb_sample_kernel(q_ref, ck_ref, cv_ref, kn_ref, vn_ref, z_ref, u_ref, o_ref, acc_sc, c_sc, *, ts):
    tk = LANES
    ntile = ck_ref.shape[1] // tk
    qs = _stack_halves(q_ref[...])
    u = u_ref[...]
    pad = jnp.zeros((tk - ts, LANES), BF16)
    kn = jnp.concatenate([kn_ref[...], pad], axis=0)
    vn = jnp.concatenate([vn_ref[...], pad], axis=0)
    qrow = lax.broadcasted_iota(jnp.int32, (2 * ts, tk), 0) % ts
    col = lax.broadcasted_iota(jnp.int32, (2 * ts, tk), 1)
    acc, c = _sb_tile(qs, kn, vn, u, jnp.zeros((2 * ts, LANES), F32), col < qrow)
    acc_sc[...] = acc
    c_sc[...] = c

    def body(st):
        off = pl.multiple_of((ntile - 1 - st[0]) * tk, tk)
        da, c = _sb_tile(qs, ck_ref[0, pl.ds(off, tk), :].astype(BF16),
                         cv_ref[0, pl.ds(off, tk), :].astype(BF16), u, c_sc[...], None)
        acc_sc[...] += da
        c_sc[...] = c
        return st[0] + 1, jnp.max(c)

    lax.while_loop(lambda st: jnp.logical_and(st[0] < ntile, st[1] > SKIP_LOG), body,
                   (jnp.int32(0), jnp.max(c)))
    o_ref[...] = _sb_finish(acc_sc[...], z_ref[...]).astype(BF16)


def _sb_sample(qb, ck, cv, kb, vb, z, u, ts):
    nb, past, _ = ck.shape
    kern = functools.partial(_sb_sample_kernel, ts=ts)
    rows = pl.BlockSpec((ts, LANES), lambda b, h: (b, h))
    cache = pl.BlockSpec((1, past, LANES), lambda b, h: (b, 0, h))
    return pl.pallas_call(
        kern,
        grid=(nb, SB_HEADS // 2),
        in_specs=[rows, cache, cache, rows, rows, rows, pl.BlockSpec(u.shape, lambda b, h: (0, 0))],
        out_specs=rows,
        out_shape=jax.ShapeDtypeStruct((nb * ts, SB_WIDTH), BF16),
        scratch_shapes=[pltpu.VMEM((2 * ts, LANES), F32)] * 2,
        compiler_params=_params(("arbitrary", "arbitrary")),
        name="stick_breaking_sample",
    )(qb, ck, cv, kb, vb, z, u)


def _suffix_matrix():
    j = np.arange(LANES)[:, None]
    s = np.arange(LANES)[None, :]
    u = (j >= s).astype(np.float32)
    return jnp.asarray(np.concatenate([u, np.ones((LANES, LANES), np.float32)], axis=1), dtype=BF16)


def _row_tile(m):
    return 256 if m % 256 == 0 else m


def kernel(x_prompt, x_sample, cache_l0_k, cache_l0_v, state_l0_shift, state_l0_wkv, cache_l1_k, cache_l1_v, rel_bias, norm_l0, w_in_l0, w_out_l0, da_q_norm, da_k_norm, da_lambda_q1, da_lambda_k1, da_lambda_q2, da_lambda_k2, da_subln, rw_mu, rw_w0, rw_w_up, rw_a0, rw_a_up, rw_k_k, rw_k_a, rw_r_k, rw_lnx_g, rw_lnx_b, norm_l1, w_in_l1, w_out_l1):
    bp, t, d = x_prompt.shape
    nb, ts, _ = x_sample.shape
    past = cache_l0_k.shape[1]
    assert bp == 1 and ts <= LANES and t % 256 == 0 and past % 256 == 0

    row = lambda v: v.reshape(1, -1).astype(F32)
    w_in0 = w_in_l0.astype(BF16)
    w_in1 = w_in_l1.astype(BF16)
    w_out0a = w_out_l0[:DA_WIDTH].astype(BF16)
    w_out0b = w_out_l0[DA_WIDTH:].astype(BF16)
    w_out1 = w_out_l1.astype(BF16)
    qg = row(jnp.tile(da_q_norm, 512 // HEAD))
    kg = row(jnp.tile(da_k_norm, 512 // HEAD))
    lamv = jnp.stack([da_lambda_q1, da_lambda_k1, da_lambda_q2, da_lambda_k2]).astype(F32)
    sg = row(da_subln)
    zeros = jnp.zeros((HEAD, RW_WIDTH), F32)
    wwa_f = jnp.concatenate([jnp.concatenate([rw_w_up, zeros], axis=1),
                             jnp.concatenate([zeros, rw_a_up], axis=1)], axis=0)
    wwa_hi = wwa_f.astype(BF16)
    wwa = jnp.stack([wwa_hi, (wwa_f - wwa_hi.astype(F32)).astype(BF16)])
    e4 = np.equal(np.arange(RW_STATE_SHAPE[0])[:, None] % HEAD,
                  np.arange(RW_STATE_SHAPE[1])[None, :] % HEAD).astype(np.float32)
    rw = dict(mu=row(rw_mu), w0=row(rw_w0), a0=row(rw_a0), wwa=wwa, kkw=row(rw_k_k), kaw=row(rw_k_a),
              rk=row(rw_r_k), lng=row(rw_lnx_g), lnb=row(rw_lnx_b), g2=_group_matrix(RW_STATE_LANES, 1.0),
              avg=_group_matrix(RW_WIDTH, 1.0 / HEAD), ones=_group_matrix(RW_WIDTH, 1.0), e4=jnp.asarray(e4))
    tq_da = 256
    prompt_tab, tabc, tabn = _bias_tables(rel_bias.astype(F32), tq_da, ts, past)
    u = _suffix_matrix()

    def layer0(x, first, s0, seq_len, attend):
        m = x.shape[0]
        tm = _row_tile(m)
        qn, kn, knb, v, vb, za, prw, zb = _inproj0(x, row(norm_l0), w_in0, qg, kg, tm)
        oa = attend(qn, kn, knb, v, vb, za)
        ob, sfin = _rwkv(prw, first, zb, _state_to_rows(s0), rw, seq_len, min(seq_len, 256))
        y = _outproj(x, [oa, ob], [w_out0a, w_out0b], tm)
        shift = prw.reshape(m // seq_len, seq_len, RW_SHIFT_COLS)[:, -1:]
        return y, kn, v, shift, _rows_to_state(sfin)

    def layer1(x, attend):
        m = x.shape[0]
        tm = _row_tile(m)
        qb, k, kb, v, vb, z = _inproj1(x, row(norm_l1), w_in1, tm)
        o = attend(qb, kb, vb, z)
        return _outproj(x, [o], [w_out1], tm), k, v

    xp = x_prompt.reshape(t, d)
    yp, k0p, v0p, shp, wkvp = layer0(
        xp, jnp.zeros((1, 1, RW_SHIFT_COLS), F32), jnp.zeros((1, RW_HEADS, HEAD, HEAD), F32), t,
        lambda qn, kn, knb, v, vb, za: _da_prompt(lamv, qn, knb, vb, za, prompt_tab, sg, tq_da))
    yp, k1p, v1p = layer1(yp, lambda qb, kb, vb, z: _sb_prompt(qb, kb, vb, z, u, 512))

    xs = x_sample.reshape(nb * ts, d)
    ck0 = cache_l0_k.reshape(nb, past, DA_WIDTH)
    cv0 = cache_l0_v.reshape(nb, past, DA_WIDTH)
    ys, k0s, v0s, shs, wkvs = layer0(
        xs, state_l0_shift, state_l0_wkv, ts,
        lambda qn, kn, knb, v, vb, za: _da_sample(lamv, qn, ck0, cv0, kn, v, za, tabc, tabn, sg, ts))
    ck1 = cache_l1_k.reshape(nb, past, SB_WIDTH)
    cv1 = cache_l1_v.reshape(nb, past, SB_WIDTH)
    ys, k1s, v1s = layer1(ys, lambda qb, kb, vb, z: _sb_sample(qb, ck1, cv1, kb, vb, z, u, ts))

    return (yp.reshape(1, t, d), ys.reshape(nb, ts, d),
            k0p.reshape(1, t, DA_HEADS, LANES), v0p.reshape(1, t, DA_HEADS, LANES), shp, wkvp,
            k1p.reshape(1, t, SB_HEADS, HEAD), v1p.reshape(1, t, SB_HEADS, HEAD),
            k0s.reshape(nb, ts, DA_HEADS, LANES), v0s.reshape(nb, ts, DA_HEADS, LANES), shs, wkvs,
            k1s.reshape(nb, ts, SB_HEADS, HEAD), v1s.reshape(nb, ts, SB_HEADS, HEAD))
```

```python
import functools
import math

import numpy as np
import jax
import jax.numpy as jnp
from jax import lax
from jax.experimental import pallas as pl
from jax.experimental.pallas import tpu as pltpu

F32 = jnp.float32
BF16 = jnp.bfloat16

EPS = 1e-6
NEG = -1e30
CHUNK = 64
LANES = 128
SUBLANES = 8
HEAD = 64
DA_HEADS = 4
DA_WIDTH = 512
RW_WIDTH = 512
RW_HEADS = 8
RW_SHIFT_COLS = 3 * RW_WIDTH + 128
RW_GN_EPS = 64e-5
RW_STATE_LANES = 256
RW_STATE_GROUPS = RW_WIDTH // RW_STATE_LANES
RW_STATE_SHAPE = (RW_STATE_GROUPS * HEAD, RW_STATE_LANES)
SB_HEADS = 16
SB_WIDTH = 1024
N_BUCKETS = 32
MAX_DISTANCE = 128
LAMBDA_INIT = 0.8 - 0.6 * math.exp(-0.3 * 0)
LOG2E = math.log2(math.e)
VMEM_LIMIT = 56 * 1024 * 1024


def _params(sem):
    return pltpu.CompilerParams(dimension_semantics=sem, vmem_limit_bytes=VMEM_LIMIT)


def _dot(a, b):
    return jnp.dot(a, b, preferred_element_type=F32)


def _dot_nt(a, b):
    return lax.dot_general(a, b, (((1,), (1,)), ((), ())), preferred_element_type=F32)


def _dot2(x, w):
    hi = x.astype(BF16)
    lo = (x - hi.astype(F32)).astype(BF16)
    return _dot(hi, w) + _dot(lo, w)


def _silu(z):
    return z / (1.0 + jnp.exp(-z))


def _rms(x, g):
    return x * lax.rsqrt(jnp.mean(x * x, axis=-1, keepdims=True) + EPS) * g


def _group_matrix(n, scale):
    idx = np.arange(n) // HEAD
    return jnp.asarray((idx[:, None] == idx[None, :]).astype(np.float32) * scale, dtype=BF16)


def _inproj0_kernel(x_ref, g_ref, w_ref, qg_ref, kg_ref, avg_ref,
                    qn_ref, kn_ref, knb_ref, v_ref, vb_ref, za_ref, prw_ref, zb_ref):
    y = _dot(_rms(x_ref[...], g_ref[...]).astype(BF16), w_ref[...])

    def head_norm(t, g):
        return t * lax.rsqrt(_dot2(t * t, avg_ref[...]) + EPS) * g

    qn_ref[...] = (head_norm(y[:, 0:512], qg_ref[...]) * (HEAD ** -0.5 * LOG2E)).astype(BF16)
    kn = head_norm(y[:, 512:1024], kg_ref[...])
    kn_ref[...] = kn
    knb_ref[...] = kn.astype(BF16)
    v = y[:, 1024:1536]
    v_ref[...] = v
    vb_ref[...] = v.astype(BF16)
    za_ref[...] = y[:, 1536:2048]
    prw_ref[...] = y[:, 2048:2048 + RW_SHIFT_COLS]
    zb_ref[...] = y[:, 2048 + RW_SHIFT_COLS:]


def _inproj0(x, g, w, qg, kg, tm):
    m, d = x.shape
    n = w.shape[1]
    row = lambda c: pl.BlockSpec((tm, c), lambda i: (i, 0))
    full = lambda a: pl.BlockSpec(a.shape, lambda i: (0,) * a.ndim)
    avg = _group_matrix(512, 1.0 / HEAD)
    widths = [(512, BF16), (512, F32), (512, BF16), (512, F32), (512, BF16), (512, F32),
              (RW_SHIFT_COLS, F32), (512, F32)]
    return pl.pallas_call(
        _inproj0_kernel,
        grid=(m // tm,),
        in_specs=[row(d), full(g), full(w), full(qg), full(kg), full(avg)],
        out_specs=[row(c) for c, _ in widths],
        out_shape=[jax.ShapeDtypeStruct((m, c), dt) for c, dt in widths],
        compiler_params=_params(("arbitrary",)),
        name="inproj0",
    )(x, g, w, qg, kg, avg)


def _inproj1_kernel(x_ref, g_ref, w_ref, qb_ref, k_ref, kb_ref, v_ref, vb_ref, z_ref):
    y = _dot(_rms(x_ref[...], g_ref[...]).astype(BF16), w_ref[...])
    qb_ref[...] = (y[:, 0:1024] * (HEAD ** -0.5)).astype(BF16)
    k = y[:, 1024:2048]
    k_ref[...] = k
    kb_ref[...] = k.astype(BF16)
    v = y[:, 2048:3072]
    v_ref[...] = v
    vb_ref[...] = v.astype(BF16)
    z_ref[...] = y[:, 3072:4096]


def _inproj1(x, g, w, tm):
    m, d = x.shape
    row = lambda c: pl.BlockSpec((tm, c), lambda i: (i, 0))
    full = lambda a: pl.BlockSpec(a.shape, lambda i: (0,) * a.ndim)
    dts = [BF16, F32, BF16, F32, BF16, F32]
    return pl.pallas_call(
        _inproj1_kernel,
        grid=(m // tm,),
        in_specs=[row(d), full(g), full(w)],
        out_specs=[row(1024) for _ in dts],
        out_shape=[jax.ShapeDtypeStruct((m, 1024), dt) for dt in dts],
        compiler_params=_params(("arbitrary",)),
        name="inproj1",
    )(x, g, w)


def _outproj_kernel(*refs):
    n = (len(refs) - 2) // 2
    x_ref, o_ref = refs[0], refs[-1]
    acc = x_ref[...]
    for a_ref, w_ref in zip(refs[1:1 + n], refs[1 + n:1 + 2 * n]):
        acc = acc + _dot(a_ref[...], w_ref[...])
    o_ref[...] = acc


def _outproj(x, acts, ws, tm):
    m, d = x.shape
    row = lambda c: pl.BlockSpec((tm, c), lambda i: (i, 0))
    full = lambda a: pl.BlockSpec(a.shape, lambda i: (0,) * a.ndim)
    return pl.pallas_call(
        _outproj_kernel,
        grid=(m // tm,),
        in_specs=[row(d)] + [row(a.shape[1]) for a in acts] + [full(w) for w in ws],
        out_specs=row(d),
        out_shape=jax.ShapeDtypeStruct((m, d), F32),
        compiler_params=_params(("arbitrary",)),
        name="outproj",
    )(x, *acts, *ws)


def _t5_bucket_np(rel):
    nb = N_BUCKETS // 2
    max_exact = nb // 2
    n = np.abs(rel)
    nf = np.maximum(n, 1).astype(np.float32)
    large = max_exact + (np.log(nf / np.float32(max_exact)) / np.float32(math.log(MAX_DISTANCE / max_exact))
                         * np.float32(nb - max_exact)).astype(np.int32)
    large = np.minimum(large, nb - 1)
    return np.where(rel > 0, nb, 0) + np.where(n < max_exact, n, large)


FAR_BUCKET = N_BUCKETS // 2 - 1


def _stack_halves(q):
    lane = lax.broadcasted_iota(jnp.int32, q.shape, 1)
    zero = jnp.zeros_like(q)
    return jnp.concatenate([jnp.where(lane < HEAD, q, zero), jnp.where(lane >= HEAD, q, zero)], axis=0)


def _softmax_step(qs, k, v, bias, m_sc, l_sc, acc_sc):
    s = _dot_nt(qs, k)
    if bias is not None:
        s = s + bias
    reps = s.shape[1] // LANES
    m_old = m_sc[...]
    m_new = jnp.maximum(m_old, jnp.max(s, axis=-1, keepdims=True))
    alpha = jnp.exp2(m_old - m_new)
    p = jnp.exp2(s - jnp.concatenate([m_new] * reps, axis=1))
    psum = p[:, :LANES]
    for r in range(1, reps):
        psum = psum + p[:, r * LANES:(r + 1) * LANES]
    l_sc[...] = alpha * l_sc[...] + psum
    acc_sc[...] = alpha * acc_sc[...] + _dot(p.astype(BF16), v)
    m_sc[...] = m_new


def _diff_finish(lam_ref, za, sg, l_sc, acc_sc, tq):
    lv = lam_ref[...]
    lam = (jnp.exp(jnp.sum(lv[0:1] * lv[1:2], axis=-1, keepdims=True))
           - jnp.exp(jnp.sum(lv[2:3] * lv[3:4], axis=-1, keepdims=True)) + LAMBDA_INIT)
    o = acc_sc[...] / jnp.sum(l_sc[...], axis=-1, keepdims=True)
    attn = o[:tq] - lam * o[tq:]
    return (_rms(attn, sg) * (1.0 - LAMBDA_INIT)) * _silu(za)


def _da_prompt_kernel(lam_ref, q_ref, k_ref, v_ref, za_ref, tab_ref, sg_ref, o_ref,
                      m_sc, l_sc, acc_sc, *, tq, tk):
    i = pl.program_id(1)
    qs = _stack_halves(q_ref[...])
    m_sc[...] = jnp.full(m_sc.shape, NEG, F32)
    l_sc[...] = jnp.zeros(l_sc.shape, F32)
    acc_sc[...] = jnp.zeros(acc_sc.shape, F32)

    def tile(j, width, bias):
        off = pl.multiple_of(j * tk, tk)
        _softmax_step(qs, k_ref[pl.ds(off, width), :], v_ref[pl.ds(off, width), :], bias, m_sc, l_sc, acc_sc)

    nfar = jnp.maximum(i - 1, 0)

    def far(j, c):
        tile(2 * j, 2 * tk, None)
        return c

    lax.fori_loop(0, nfar // 2, far, 0)

    @pl.when(nfar % 2 == 1)
    def _():
        tile(nfar - 1, tk, None)

    @pl.when(i >= 1)
    def _():
        b = tab_ref[0, 1]
        tile(i - 1, tk, jnp.concatenate([b, b], axis=0))

    b = tab_ref[0, 0]
    tile(i, tk, jnp.concatenate([b, b], axis=0))
    o_ref[...] = _diff_finish(lam_ref, za_ref[...], sg_ref[...], l_sc, acc_sc, tq).astype(BF16)


def _da_prompt(lamv, qn, knb, vb, za, tab, sg, tq):
    t = qn.shape[0]
    tk = tq
    kern = functools.partial(_da_prompt_kernel, tq=tq, tk=tk)
    return pl.pallas_call(
        kern,
        grid=(DA_HEADS, t // tq),
        in_specs=[
            pl.BlockSpec(lamv.shape, lambda h, i: (0, 0)),
            pl.BlockSpec((tq, LANES), lambda h, i: (i, h)),
            pl.BlockSpec((t, LANES), lambda h, i: (0, h)),
            pl.BlockSpec((t, LANES), lambda h, i: (0, h)),
            pl.BlockSpec((tq, LANES), lambda h, i: (i, h)),
            pl.BlockSpec((1, 2, tq, tk), lambda h, i: (h, 0, 0, 0)),
            pl.BlockSpec(sg.shape, lambda h, i: (0, 0)),
        ],
        out_specs=pl.BlockSpec((tq, LANES), lambda h, i: (i, h)),
        out_shape=jax.ShapeDtypeStruct((t, DA_WIDTH), BF16),
        scratch_shapes=[pltpu.VMEM((2 * tq, LANES), F32)] * 3,
        compiler_params=_params(("arbitrary", "arbitrary")),
        name="diff_attn_prompt",
    )(lamv, qn, knb, vb, za, tab, sg)


def _da_sample_kernel(lam_ref, q_ref, ck_ref, cv_ref, kn_ref, vn_ref, za_ref, tabc_ref, tabn_ref, sg_ref,
                      o_ref, m_sc, l_sc, acc_sc, *, ts, tk):
    qs = _stack_halves(q_ref[...])
    m_sc[...] = jnp.full(m_sc.shape, NEG, F32)
    l_sc[...] = jnp.zeros(l_sc.shape, F32)
    acc_sc[...] = jnp.zeros(acc_sc.shape, F32)
    past = ck_ref.shape[1]
    for j in range(past // tk):
        b = tabc_ref[0, :, j * tk:(j + 1) * tk]
        _softmax_step(qs, ck_ref[0, j * tk:(j + 1) * tk, :].astype(BF16),
                      cv_ref[0, j * tk:(j + 1) * tk, :].astype(BF16),
                      jnp.concatenate([b, b], axis=0), m_sc, l_sc, acc_sc)
    pad = jnp.zeros((LANES - ts, LANES), BF16)
    kn = jnp.concatenate([kn_ref[...].astype(BF16), pad], axis=0)
    vn = jnp.concatenate([vn_ref[...].astype(BF16), pad], axis=0)
    b = tabn_ref[0]
    _softmax_step(qs, kn, vn, jnp.concatenate([b, b], axis=0), m_sc, l_sc, acc_sc)
    o_ref[...] = _diff_finish(lam_ref, za_ref[...], sg_ref[...], l_sc, acc_sc, ts).astype(BF16)


def _da_sample(lamv, qn, ck, cv, kn, vn, za, tabc, tabn, sg, ts):
    nb, past, _ = ck.shape
    kern = functools.partial(_da_sample_kernel, ts=ts, tk=256)
    rows = pl.BlockSpec((ts, LANES), lambda b, h: (b, h))
    return pl.pallas_call(
        kern,
        grid=(nb, DA_HEADS),
        in_specs=[
            pl.BlockSpec(lamv.shape, lambda b, h: (0, 0)),
            rows,
            pl.BlockSpec((1, past, LANES), lambda b, h: (b, 0, h)),
            pl.BlockSpec((1, past, LANES), lambda b, h: (b, 0, h)),
            rows, rows, rows,
            pl.BlockSpec((1, ts, past), lambda b, h: (h, 0, 0)),
            pl.BlockSpec((1, ts, LANES), lambda b, h: (h, 0, 0)),
            pl.BlockSpec(sg.shape, lambda b, h: (0, 0)),
        ],
        out_specs=rows,
        out_shape=jax.ShapeDtypeStruct((nb * ts, DA_WIDTH), BF16),
        scratch_shapes=[pltpu.VMEM((2 * ts, LANES), F32)] * 3,
        compiler_params=_params(("arbitrary", "arbitrary")),
        name="diff_attn_sample",
    )(lamv, qn, ck, cv, kn, vn, za, tabc, tabn, sg)


def _bias_kernel(idx_ref, bt_ref, o_ref):
    idx = idx_ref[...]
    bt = bt_ref[...]
    bt = bt - bt[:, FAR_BUCKET:FAR_BUCKET + 1]
    onehot = jnp.where(lax.broadcasted_iota(jnp.int32, (N_BUCKETS, idx.shape[1]), 0) == idx, 1.0, 0.0).astype(BF16)
    hi = bt.astype(BF16)
    rest = bt - hi.astype(F32)
    mid = rest.astype(BF16)
    lo = (rest - mid.astype(F32)).astype(BF16)
    val = _dot(hi, onehot) + _dot(mid, onehot) + _dot(lo, onehot)
    o_ref[...] = jnp.where(idx < 0, NEG, val * LOG2E)


def _bias_tables(rel_bias, tq, ts, past):
    def buckets(rel, mask):
        return np.where(mask, _t5_bucket_np(rel), -1).astype(np.int32).reshape(-1)

    r = np.arange(tq)[:, None]
    c = np.arange(tq)[None, :]
    qpos = past + np.arange(ts)[:, None]
    kc = np.arange(past)[None, :]
    kn = past + np.arange(LANES)[None, :]
    parts = [buckets(c - r, (c // CHUNK) <= (r // CHUNK)),
             buckets(c - r - tq, np.ones((tq, tq), bool)),
             buckets(kc - qpos, (kc // CHUNK) <= (qpos // CHUNK)),
             buckets(kn - qpos, ((kn // CHUNK) <= (qpos // CHUNK)) & (kn < past + ts))]
    idx = np.concatenate(parts)
    n = idx.size
    bt = jnp.zeros((SUBLANES, N_BUCKETS), F32).at[:DA_HEADS].set(rel_bias.T)
    blk = 4096
    assert n % blk == 0
    tab = pl.pallas_call(
        _bias_kernel,
        grid=(n // blk,),
        in_specs=[pl.BlockSpec((1, blk), lambda i: (0, i)), pl.BlockSpec(bt.shape, lambda i: (0, 0))],
        out_specs=pl.BlockSpec((SUBLANES, blk), lambda i: (0, i)),
        out_shape=jax.ShapeDtypeStruct((SUBLANES, n), F32),
        compiler_params=_params(("arbitrary",)),
        name="t5_bias_tables",
    )(jnp.asarray(idx).reshape(1, n), bt)[:DA_HEADS]
    o1 = 2 * tq * tq
    o2 = o1 + ts * past
    prompt_tab = tab[:, :o1].reshape(DA_HEADS, 2, tq, tq)
    tabc = tab[:, o1:o2].reshape(DA_HEADS, ts, past)
    tabn = tab[:, o2:].reshape(DA_HEADS, ts, LANES)
    return prompt_tab, tabc, tabn


def _rwkv_kernel(prw_ref, look_ref, first_ref, zb_ref, s0_ref, mu_ref, w0_ref, a0_ref, wwa_ref,
                 kkw_ref, kaw_ref, rk_ref, lng_ref, lnb_ref, g2_ref, avg_ref, ones_ref, e_ref,
                 ob_ref, sfin_ref,
                 s_sc, w_sc, kk_sc, kka_sc, k_sc, v_sc, r_sc, wkk_sc, al_sc, be_sc, o_sc, *, tm):
    j = pl.program_id(1)

    @pl.when(j == 0)
    def _():
        s_sc[...] = s0_ref[0]

    x = prw_ref[...]
    row0 = jnp.where(j == 0, first_ref[0], look_ref[7:8, :])
    rows = lax.broadcasted_iota(jnp.int32, x.shape, 0)
    prev = jnp.where(rows == 0, row0, pltpu.roll(x, 1, axis=0))
    mix = x + (prev - x) * mu_ref[...]
    r = mix[:, 0:512]
    kr = mix[:, 512:1024]
    vr = mix[:, 1024:1536]
    la = mix[:, 1536:1664]
    lane = lax.broadcasted_iota(jnp.int32, la.shape, 1)
    la = jnp.where(lane < HEAD, jnp.tanh(la), la)
    hi = la.astype(BF16)
    lo = (la - hi.astype(F32)).astype(BF16)
    wwa = wwa_ref[...]
    lora = _dot(hi, wwa[0]) + _dot(lo, wwa[0]) + _dot(hi, wwa[1])
    wpre = w0_ref[...] + lora[:, 0:512]
    nw = -wpre
    w = -(jnp.maximum(nw, 0.0) + jnp.log(1.0 + jnp.exp(-jnp.abs(nw)))) - 0.5
    decay = jnp.exp(-jnp.exp(w))
    w_sc[...] = decay
    a = 1.0 / (1.0 + jnp.exp(-(a0_ref[...] + lora[:, 512:1024])))
    kk = kr * kkw_ref[...]
    nrm = jnp.sqrt(_dot2(kk * kk, ones_ref[...]))
    kk = kk / jnp.maximum(nrm, 1e-12)
    kk_sc[...] = kk
    kka = kk * a
    kka_sc[...] = kka
    kh = kr * (1.0 + (a - 1.0) * kaw_ref[...])
    k_sc[...] = kh
    v_sc[...] = vr
    r_sc[...] = r
    kk_next = pltpu.roll(kk, tm - 1, axis=0)
    wkk_sc[...] = decay * kk_next
    al_sc[...] = _dot2(kka * kk_next, ones_ref[...])
    be_sc[...] = _dot2(kh * kk_next, ones_ref[...])

    g2 = g2_ref[...]
    e4 = e_ref[...]

    def rowvec(blk, i):
        return jnp.concatenate(
            [jnp.broadcast_to(blk[i:i + 1, c * RW_STATE_LANES:(c + 1) * RW_STATE_LANES], (HEAD, RW_STATE_LANES))
             for c in range(RW_STATE_GROUPS)], axis=0)

    def group_sums(x):
        return _dot(x.astype(BF16), g2)

    nrow = RW_STATE_SHAPE[0]

    def steps(g, s):
        base = pl.multiple_of(g * SUBLANES, SUBLANES)
        kk_b, kka_b, k_b, v_b, w_b, r_b, wkk_b, al_b, be_b = (
            ref[pl.ds(base, SUBLANES), :]
            for ref in (kk_sc, kka_sc, k_sc, v_sc, w_sc, r_sc, wkk_sc, al_sc, be_sc))
        vcols = group_sums(jnp.concatenate([rowvec(v_b, i) * e4 for i in range(SUBLANES)], axis=0))

        def out_row(s_new, i):
            oe = group_sums(s_new * rowvec(r_b, i)) * e4
            return jnp.concatenate(
                [jnp.sum(oe[c * HEAD:(c + 1) * HEAD], axis=0, keepdims=True) for c in range(RW_STATE_GROUPS)],
                axis=1)

        out_rows = []
        for i in range(0, SUBLANES, 2):
            v0 = vcols[i * nrow:(i + 1) * nrow]
            v1 = vcols[(i + 1) * nrow:(i + 2) * nrow]
            red = group_sums(jnp.concatenate([s * rowvec(kk_b, i), s * rowvec(wkk_b, i)], axis=0))
            sa0 = red[:nrow]
            sa1 = red[nrow:] - sa0 * rowvec(al_b, i) + v0 * rowvec(be_b, i)
            s = s * rowvec(w_b, i) - sa0 * rowvec(kka_b, i) + v0 * rowvec(k_b, i)
            out_rows.append(out_row(s, i))
            s = s * rowvec(w_b, i + 1) - sa1 * rowvec(kka_b, i + 1) + v1 * rowvec(k_b, i + 1)
            out_rows.append(out_row(s, i + 1))
        o_sc[pl.ds(base, SUBLANES), :] = jnp.concatenate(out_rows, axis=0)
        return s

    s_sc[...] = lax.fori_loop(0, tm // SUBLANES, steps, s_sc[...])

    o = o_sc[...]
    mean = _dot2(o, avg_ref[...])
    d = o - mean
    var = _dot2(d * d, avg_ref[...])
    y = d * lax.rsqrt(var + RW_GN_EPS) * lng_ref[...] + lnb_ref[...]
    bonus = _dot2(r * kh * rk_ref[...], ones_ref[...]) * vr
    ob_ref[...] = ((y + bonus) * _silu(zb_ref[...])).astype(BF16)

    @pl.when(j == pl.num_programs(1) - 1)
    def _():
        sfin_ref[0] = s_sc[...]


def _rwkv(prw, first, zb, s0, p, seq_len, tm):
    m = prw.shape[0]
    nseq = m // seq_len
    ntile = seq_len // tm
    kern = functools.partial(_rwkv_kernel, tm=tm)
    full = lambda a: pl.BlockSpec(a.shape, lambda b, j: (0,) * a.ndim)
    rows = lambda c: pl.BlockSpec((tm, c), lambda b, j: (b * ntile + j, 0))
    look = pl.BlockSpec((8, RW_SHIFT_COLS), lambda b, j: (jnp.maximum((b * ntile + j) * (tm // 8) - 1, 0), 0))
    consts = [p["mu"], p["w0"], p["a0"], p["wwa"], p["kkw"], p["kaw"], p["rk"], p["lng"], p["lnb"],
              p["g2"], p["avg"], p["ones"], p["e4"]]
    return pl.pallas_call(
        kern,
        grid=(nseq, ntile),
        in_specs=[rows(RW_SHIFT_COLS), look,
                  pl.BlockSpec((1, 1, RW_SHIFT_COLS), lambda b, j: (b, 0, 0)),
                  rows(RW_WIDTH),
                  pl.BlockSpec((1,) + RW_STATE_SHAPE, lambda b, j: (b, 0, 0))] + [full(c) for c in consts],
        out_specs=[rows(RW_WIDTH), pl.BlockSpec((1,) + RW_STATE_SHAPE, lambda b, j: (b, 0, 0))],
        out_shape=[jax.ShapeDtypeStruct((m, RW_WIDTH), BF16), jax.ShapeDtypeStruct((nseq,) + RW_STATE_SHAPE, F32)],
        scratch_shapes=[pltpu.VMEM(RW_STATE_SHAPE, F32)] + [pltpu.VMEM((tm, RW_WIDTH), F32)] * 10,
        compiler_params=_params(("arbitrary", "arbitrary")),
        name="rwkv7",
    )(prw, prw, first, zb, s0, *consts)


def _state_to_rows(s):
    b = s.shape[0]
    per = RW_HEADS // RW_STATE_GROUPS
    return s.reshape(b, RW_STATE_GROUPS, per, HEAD, HEAD).transpose(0, 1, 3, 2, 4).reshape((b,) + RW_STATE_SHAPE)


def _rows_to_state(s):
    b = s.shape[0]
    per = RW_HEADS // RW_STATE_GROUPS
    return s.reshape(b, RW_STATE_GROUPS, HEAD, per, HEAD).transpose(0, 1, 3, 2, 4).reshape(b, RW_HEADS, HEAD, HEAD)


SKIP_LOG = -104.0


def _sb_tile(qs, k, v, u, c, mask):
    tk = k.shape[0]
    z = _dot_nt(qs, k)
    lg = -(jnp.maximum(z, 0.0) + jnp.log(1.0 + jnp.exp(-jnp.abs(z))))
    if mask is not None:
        lg = jnp.where(mask, lg, 0.0)
    it = _dot2(lg, u)
    a = jnp.exp(z + it[:, :tk] + c)
    if mask is not None:
        a = jnp.where(mask, a, 0.0)
    return _dot(a.astype(BF16), v), c + it[:, tk:]


def _sb_finish(acc, z):
    t = z.shape[0]
    lane = lax.broadcasted_iota(jnp.int32, z.shape, 1)
    return jnp.where(lane < HEAD, acc[:t], acc[t:]) * _silu(z)


def _sb_prompt_kernel(q_ref, k_ref, v_ref, z_ref, u_ref, o_ref, acc_sc, c_sc, *, tq):
    i = pl.program_id(1)
    tk = LANES
    r = tq // tk
    qs = _stack_halves(q_ref[...])
    u = u_ref[...]
    acc_sc[...] = jnp.zeros(acc_sc.shape, F32)
    c_sc[...] = jnp.zeros(c_sc.shape, F32)
    qrow = lax.broadcasted_iota(jnp.int32, (2 * tq, tk), 0) % tq
    col = lax.broadcasted_iota(jnp.int32, (2 * tq, tk), 1)

    def tile(j, mask):
        off = pl.multiple_of(j * tk, tk)
        da, c = _sb_tile(qs, k_ref[pl.ds(off, tk), :], v_ref[pl.ds(off, tk), :], u, c_sc[...], mask)
        acc_sc[...] += da
        c_sc[...] = c

    for d in range(r):
        jj = r - 1 - d
        tile(i * r + jj, (col + jj * tk) < qrow)

    def body(st):
        tile(i * r - 1 - st[0], None)
        return st[0] + 1, jnp.max(c_sc[...])

    lax.while_loop(lambda st: jnp.logical_and(st[0] < i * r, st[1] > SKIP_LOG), body,
                   (jnp.int32(0), jnp.max(c_sc[...])))
    o_ref[...] = _sb_finish(acc_sc[...], z_ref[...]).astype(BF16)


def _sb_prompt(qb, kb, vb, z, u, tq):
    t = qb.shape[0]
    kern = functools.partial(_sb_prompt_kernel, tq=tq)
    rows = pl.BlockSpec((tq, LANES), lambda h, i: (i, h))
    whole = pl.BlockSpec((t, LANES), lambda h, i: (0, h))
    return pl.pallas_call(
        kern,
        grid=(SB_HEADS // 2, t // tq),
        in_specs=[rows, whole, whole, rows, pl.BlockSpec(u.shape, lambda h, i: (0, 0))],
        out_specs=rows,
        out_shape=jax.ShapeDtypeStruct((t, SB_WIDTH), BF16),
        scratch_shapes=[pltpu.VMEM((2 * tq, LANES), F32)] * 2,
        compiler_params=_params(("arbitrary", "arbitrary")),
        name="stick_breaking_prompt",
    )(qb, kb, vb, z, u)


def _sb_sample_kernel(q_ref, ck_ref, cv_ref, kn_ref, vn_ref, z_ref, u_ref, o_ref, acc_sc, c_sc, *, ts):
    tk = LANES
    ntile = ck_ref.shape[1] // tk
    qs = _stack_halves(q_ref[...])
    u = u_ref[...]
    pad = jnp.zeros((tk - ts, LANES), BF16)
    kn = jnp.concatenate([kn_ref[...], pad], axis=0)
    vn = jnp.concatenate([vn_ref[...], pad], axis=0)
    qrow = lax.broadcasted_iota(jnp.int32, (2 * ts, tk), 0) % ts
    col = lax.broadcasted_iota(jnp.int32, (2 * ts, tk), 1)
    acc, c = _sb_tile(qs, kn, vn, u, jnp.zeros((2 * ts, LANES), F32), col < qrow)
    acc_sc[...] = acc
    c_sc[...] = c

    def body(st):
        off = pl.multiple_of((ntile - 1 - st[0]) * tk, tk)
        da, c = _sb_tile(qs, ck_ref[0, pl.ds(off, tk), :].astype(BF16),
                         cv_ref[0, pl.ds(off, tk), :].astype(BF16), u, c_sc[...], None)
        acc_sc[...] += da
        c_sc[...] = c
        return st[0] + 1, jnp.max(c)

    lax.while_loop(lambda st: jnp.logical_and(st[0] < ntile, st[1] > SKIP_LOG), body,
                   (jnp.int32(0), jnp.max(c)))
    o_ref[...] = _sb_finish(acc_sc[...], z_ref[...]).astype(BF16)


def _sb_sample(qb, ck, cv, kb, vb, z, u, ts):
    nb, past, _ = ck.shape
    kern = functools.partial(_sb_sample_kernel, ts=ts)
    rows = pl.BlockSpec((ts, LANES), lambda b, h: (b, h))
    cache = pl.BlockSpec((1, past, LANES), lambda b, h: (b, 0, h))
    return pl.pallas_call(
        kern,
        grid=(nb, SB_HEADS // 2),
        in_specs=[rows, cache, cache, rows, rows, rows, pl.BlockSpec(u.shape, lambda b, h: (0, 0))],
        out_specs=rows,
        out_shape=jax.ShapeDtypeStruct((nb * ts, SB_WIDTH), BF16),
        scratch_shapes=[pltpu.VMEM((2 * ts, LANES), F32)] * 2,
        compiler_params=_params(("arbitrary", "arbitrary")),
        name="stick_breaking_sample",
    )(qb, ck, cv, kb, vb, z, u)


def _suffix_matrix():
    j = np.arange(LANES)[:, None]
    s = np.arange(LANES)[None, :]
    u = (j >= s).astype(np.float32)
    return jnp.asarray(np.concatenate([u, np.ones((LANES, LANES), np.float32)], axis=1), dtype=BF16)


def _row_tile(m):
    return 256 if m % 256 == 0 else m


def kernel(x_prompt, x_sample, cache_l0_k, cache_l0_v, state_l0_shift, state_l0_wkv, cache_l1_k, cache_l1_v, rel_bias, norm_l0, w_in_l0, w_out_l0, da_q_norm, da_k_norm, da_lambda_q1, da_lambda_k1, da_lambda_q2, da_lambda_k2, da_subln, rw_mu, rw_w0, rw_w_up, rw_a0, rw_a_up, rw_k_k, rw_k_a, rw_r_k, rw_lnx_g, rw_lnx_b, norm_l1, w_in_l1, w_out_l1):
    bp, t, d = x_prompt.shape
    nb, ts, _ = x_sample.shape
    past = cache_l0_k.shape[1]
    assert bp == 1 and ts <= LANES and t % 256 == 0 and past % 256 == 0

    row = lambda v: v.reshape(1, -1).astype(F32)
    w_in0 = w_in_l0.astype(BF16)
    w_in1 = w_in_l1.astype(BF16)
    w_out0a = w_out_l0[:DA_WIDTH].astype(BF16)
    w_out0b = w_out_l0[DA_WIDTH:].astype(BF16)
    w_out1 = w_out_l1.astype(BF16)
    qg = row(jnp.tile(da_q_norm, 512 // HEAD))
    kg = row(jnp.tile(da_k_norm, 512 // HEAD))
    lamv = jnp.stack([da_lambda_q1, da_lambda_k1, da_lambda_q2, da_lambda_k2]).astype(F32)
    sg = row(da_subln)
    zeros = jnp.zeros((HEAD, RW_WIDTH), F32)
    wwa_f = jnp.concatenate([jnp.concatenate([rw_w_up, zeros], axis=1),
                             jnp.concatenate([zeros, rw_a_up], axis=1)], axis=0)
    wwa_hi = wwa_f.astype(BF16)
    wwa = jnp.stack([wwa_hi, (wwa_f - wwa_hi.astype(F32)).astype(BF16)])
    e4 = np.equal(np.arange(RW_STATE_SHAPE[0])[:, None] % HEAD,
                  np.arange(RW_STATE_SHAPE[1])[None, :] % HEAD).astype(np.float32)
    rw = dict(mu=row(rw_mu), w0=row(rw_w0), a0=row(rw_a0), wwa=wwa, kkw=row(rw_k_k), kaw=row(rw_k_a),
              rk=row(rw_r_k), lng=row(rw_lnx_g), lnb=row(rw_lnx_b), g2=_group_matrix(RW_STATE_LANES, 1.0),
              avg=_group_matrix(RW_WIDTH, 1.0 / HEAD), ones=_group_matrix(RW_WIDTH, 1.0), e4=jnp.asarray(e4))
    tq_da = 256
    prompt_tab, tabc, tabn = _bias_tables(rel_bias.astype(F32), tq_da, ts, past)
    u = _suffix_matrix()

    def layer0(x, first, s0, seq_len, attend):
        m = x.shape[0]
        tm = _row_tile(m)
        qn, kn, knb, v, vb, za, prw, zb = _inproj0(x, row(norm_l0), w_in0, qg, kg, tm)
        oa = attend(qn, kn, knb, v, vb, za)
        ob, sfin = _rwkv(prw, first, zb, _state_to_rows(s0), rw, seq_len, min(seq_len, 256))
        y = _outproj(x, [oa, ob], [w_out0a, w_out0b], tm)
        shift = prw.reshape(m // seq_len, seq_len, RW_SHIFT_COLS)[:, -1:]
        return y, kn, v, shift, _rows_to_state(sfin)

    def layer1(x, attend):
        m = x.shape[0]
        tm = _row_tile(m)
        qb, k, kb, v, vb, z = _inproj1(x, row(norm_l1), w_in1, tm)
        o = attend(qb, kb, vb, z)
        return _outproj(x, [o], [w_out1], tm), k, v

    xp = x_prompt.reshape(t, d)
    yp, k0p, v0p, shp, wkvp = layer0(
        xp, jnp.zeros((1, 1, RW_SHIFT_COLS), F32), jnp.zeros((1, RW_HEADS, HEAD, HEAD), F32), t,
        lambda qn, kn, knb, v, vb, za: _da_prompt(lamv, qn, knb, vb, za, prompt_tab, sg, tq_da))
    yp, k1p, v1p = layer1(yp, lambda qb, kb, vb, z: _sb_prompt(qb, kb, vb, z, u, 512))

    xs = x_sample.reshape(nb * ts, d)
    ck0 = cache_l0_k.reshape(nb, past, DA_WIDTH)
    cv0 = cache_l0_v.reshape(nb, past, DA_WIDTH)
    ys, k0s, v0s, shs, wkvs = layer0(
        xs, state_l0_shift, state_l0_wkv, ts,
        lambda qn, kn, knb, v, vb, za: _da_sample(lamv, qn, ck0, cv0, kn, v, za, tabc, tabn, sg, ts))
    ck1 = cache_l1_k.reshape(nb, past, SB_WIDTH)
    cv1 = cache_l1_v.reshape(nb, past, SB_WIDTH)
    ys, k1s, v1s = layer1(ys, lambda qb, kb, vb, z: _sb_sample(qb, ck1, cv1, kb, vb, z, u, ts))

    return (yp.reshape(1, t, d), ys.reshape(nb, ts, d),
            k0p.reshape(1, t, DA_HEADS, LANES), v0p.reshape(1, t, DA_HEADS, LANES), shp, wkvp,
            k1p.reshape(1, t, SB_HEADS, HEAD), v1p.reshape(1, t, SB_HEADS, HEAD),
            k0s.reshape(nb, ts, DA_HEADS, LANES), v0s.reshape(nb, ts, DA_HEADS, LANES), shs, wkvs,
            k1s.reshape(nb, ts, SB_HEADS, HEAD), v1s.reshape(nb, ts, SB_HEADS, HEAD))
```

```python
import functools
import math

import numpy as np
import jax
import jax.numpy as jnp
from jax import lax
from jax.experimental import pallas as pl
from jax.experimental.pallas import tpu as pltpu

F32 = jnp.float32
BF16 = jnp.bfloat16

EPS = 1e-6
NEG = -1e30
CHUNK = 64
LANES = 128
SUBLANES = 8
HEAD = 64
DA_HEADS = 4
DA_WIDTH = 512
RW_WIDTH = 512
RW_HEADS = 8
RW_SHIFT_COLS = 3 * RW_WIDTH + 128
RW_GN_EPS = 64e-5
RW_STATE_LANES = 256
RW_STATE_GROUPS = RW_WIDTH // RW_STATE_LANES
RW_STATE_SHAPE = (RW_STATE_GROUPS * HEAD, RW_STATE_LANES)
SB_HEADS = 16
SB_WIDTH = 1024
N_BUCKETS = 32
MAX_DISTANCE = 128
LAMBDA_INIT = 0.8 - 0.6 * math.exp(-0.3 * 0)
LOG2E = math.log2(math.e)
VMEM_LIMIT = 56 * 1024 * 1024


def _params(sem):
    return pltpu.CompilerParams(dimension_semantics=sem, vmem_limit_bytes=VMEM_LIMIT)


def _dot(a, b):
    return jnp.dot(a, b, preferred_element_type=F32)


def _dot_nt(a, b):
    return lax.dot_general(a, b, (((1,), (1,)), ((), ())), preferred_element_type=F32)


def _dot2(x, w):
    hi = x.astype(BF16)
    lo = (x - hi.astype(F32)).astype(BF16)
    return _dot(hi, w) + _dot(lo, w)


def _silu(z):
    return z / (1.0 + jnp.exp(-z))


def _rms(x, g):
    return x * lax.rsqrt(jnp.mean(x * x, axis=-1, keepdims=True) + EPS) * g


def _group_matrix(n, scale):
    idx = np.arange(n) // HEAD
    return jnp.asarray((idx[:, None] == idx[None, :]).astype(np.float32) * scale, dtype=BF16)


def _inproj0_kernel(x_ref, g_ref, w_ref, qg_ref, kg_ref, avg_ref,
                    qn_ref, kn_ref, knb_ref, v_ref, vb_ref, za_ref, prw_ref, zb_ref):
    y = _dot(_rms(x_ref[...], g_ref[...]).astype(BF16), w_ref[...])

    def head_norm(t, g):
        return t * lax.rsqrt(_dot2(t * t, avg_ref[...]) + EPS) * g

    qn_ref[...] = (head_norm(y[:, 0:512], qg_ref[...]) * (HEAD ** -0.5 * LOG2E)).astype(BF16)
    kn = head_norm(y[:, 512:1024], kg_ref[...])
    kn_ref[...] = kn
    knb_ref[...] = kn.astype(BF16)
    v = y[:, 1024:1536]
    v_ref[...] = v
    vb_ref[...] = v.astype(BF16)
    za_ref[...] = y[:, 1536:2048]
    prw_ref[...] = y[:, 2048:2048 + RW_SHIFT_COLS]
    zb_ref[...] = y[:, 2048 + RW_SHIFT_COLS:]


def _inproj0(x, g, w, qg, kg, tm):
    m, d = x.shape
    n = w.shape[1]
    row = lambda c: pl.BlockSpec((tm, c), lambda i: (i, 0))
    full = lambda a: pl.BlockSpec(a.shape, lambda i: (0,) * a.ndim)
    avg = _group_matrix(512, 1.0 / HEAD)
    widths = [(512, BF16), (512, F32), (512, BF16), (512, F32), (512, BF16), (512, F32),
              (RW_SHIFT_COLS, F32), (512, F32)]
    return pl.pallas_call(
        _inproj0_kernel,
        grid=(m // tm,),
        in_specs=[row(d), full(g), full(w), full(qg), full(kg), full(avg)],
        out_specs=[row(c) for c, _ in widths],
        out_shape=[jax.ShapeDtypeStruct((m, c), dt) for c, dt in widths],
        compiler_params=_params(("arbitrary",)),
        name="inproj0",
    )(x, g, w, qg, kg, avg)


def _inproj1_kernel(x_ref, g_ref, w_ref, qb_ref, k_ref, kb_ref, v_ref, vb_ref, z_ref):
    y = _dot(_rms(x_ref[...], g_ref[...]).astype(BF16), w_ref[...])
    qb_ref[...] = (y[:, 0:1024] * (HEAD ** -0.5)).astype(BF16)
    k = y[:, 1024:2048]
    k_ref[...] = k
    kb_ref[...] = k.astype(BF16)
    v = y[:, 2048:3072]
    v_ref[...] = v
    vb_ref[...] = v.astype(BF16)
    z_ref[...] = y[:, 3072:4096]


def _inproj1(x, g, w, tm):
    m, d = x.shape
    row = lambda c: pl.BlockSpec((tm, c), lambda i: (i, 0))
    full = lambda a: pl.BlockSpec(a.shape, lambda i: (0,) * a.ndim)
    dts = [BF16, F32, BF16, F32, BF16, F32]
    return pl.pallas_call(
        _inproj1_kernel,
        grid=(m // tm,),
        in_specs=[row(d), full(g), full(w)],
        out_specs=[row(1024) for _ in dts],
        out_shape=[jax.ShapeDtypeStruct((m, 1024), dt) for dt in dts],
        compiler_params=_params(("arbitrary",)),
        name="inproj1",
    )(x, g, w)


def _outproj_kernel(*refs):
    n = (len(refs) - 2) // 2
    x_ref, o_ref = refs[0], refs[-1]
    acc = x_ref[...]
    for a_ref, w_ref in zip(refs[1:1 + n], refs[1 + n:1 + 2 * n]):
        acc = acc + _dot(a_ref[...], w_ref[...])
    o_ref[...] = acc


def _outproj(x, acts, ws, tm):
    m, d = x.shape
    row = lambda c: pl.BlockSpec((tm, c), lambda i: (i, 0))
    full = lambda a: pl.BlockSpec(a.shape, lambda i: (0,) * a.ndim)
    return pl.pallas_call(
        _outproj_kernel,
        grid=(m // tm,),
        in_specs=[row(d)] + [row(a.shape[1]) for a in acts] + [full(w) for w in ws],
        out_specs=row(d),
        out_shape=jax.ShapeDtypeStruct((m, d), F32),
        compiler_params=_params(("arbitrary",)),
        name="outproj",
    )(x, *acts, *ws)


def _t5_bucket_np(rel):
    nb = N_BUCKETS // 2
    max_exact = nb // 2
    n = np.abs(rel)
    nf = np.maximum(n, 1).astype(np.float32)
    large = max_exact + (np.log(nf / np.float32(max_exact)) / np.float32(math.log(MAX_DISTANCE / max_exact))
                         * np.float32(nb - max_exact)).astype(np.int32)
    large = np.minimum(large, nb - 1)
    return np.where(rel > 0, nb, 0) + np.where(n < max_exact, n, large)


FAR_BUCKET = N_BUCKETS // 2 - 1
FAR_TILES = (8, 2, 1)


def _stack_halves(q):
    lane = lax.broadcasted_iota(jnp.int32, q.shape, 1)
    zero = jnp.zeros_like(q)
    return jnp.concatenate([jnp.where(lane < HEAD, q, zero), jnp.where(lane >= HEAD, q, zero)], axis=0)


def _softmax_step(qs, k, v, bias, m_sc, l_sc, acc_sc):
    s = jnp.concatenate([_dot_nt(q_h, k_h) for q_h, k_h in zip(qs, k)], axis=0)
    if bias is not None:
        s = s + bias
    _softmax_update(s, v, m_sc, l_sc, acc_sc)


def _softmax_update(s, v, m_sc, l_sc, acc_sc):
    reps = s.shape[1] // LANES
    m_old = m_sc[...]
    m_new = jnp.maximum(m_old, jnp.max(s, axis=-1, keepdims=True))
    alpha = jnp.exp2(m_old - m_new)
    p = jnp.exp2(s - jnp.concatenate([m_new] * reps, axis=1))
    psum = p[:, :LANES]
    for r in range(1, reps):
        psum = psum + p[:, r * LANES:(r + 1) * LANES]
    l_sc[...] = alpha * l_sc[...] + psum
    p = p.astype(BF16)
    rows = p.shape[0] // len(v)
    pv = jnp.concatenate([_dot(p[h * rows:(h + 1) * rows], v_h) for h, v_h in enumerate(v)], axis=0)
    acc_sc[...] = alpha * acc_sc[...] + pv
    m_sc[...] = m_new


def _diff_finish(lam_ref, za, sg, l, acc, tq):
    lv = lam_ref[...]
    lam = (jnp.exp(jnp.sum(lv[0:1] * lv[1:2], axis=-1, keepdims=True))
           - jnp.exp(jnp.sum(lv[2:3] * lv[3:4], axis=-1, keepdims=True)) + LAMBDA_INIT)
    o = acc / jnp.sum(l, axis=-1, keepdims=True)
    attn = o[:tq] - lam * o[tq:]
    return (_rms(attn, sg) * (1.0 - LAMBDA_INIT)) * _silu(za)


def _da_prompt_kernel(lam_ref, q_ref, k_ref, v_ref, za_ref, tab_ref, sg_ref, o_ref,
                      m_sc, l_sc, acc_sc, *, tq, tk):
    i = pl.program_id(1)
    qs = _stack_halves(q_ref[...])
    m_sc[...] = jnp.full(m_sc.shape, NEG, F32)
    l_sc[...] = jnp.zeros(l_sc.shape, F32)
    acc_sc[...] = jnp.zeros(acc_sc.shape, F32)

    def tile(j, width, bias):
        off = pl.multiple_of(j * tk, tk)
        _softmax_step([qs], [k_ref[pl.ds(off, width), :]], [v_ref[pl.ds(off, width), :]], bias,
                      m_sc, l_sc, acc_sc)

    nfar = jnp.maximum(i - 1, 0)

    done = 0
    for width in FAR_TILES:
        def far(j, c, width=width, done=done):
            tile(done + width * j, width * tk, None)
            return c

        trips = (nfar - done) // width
        lax.fori_loop(0, trips, far, 0)
        done = done + trips * width

    @pl.when(i >= 1)
    def _():
        b = tab_ref[0, 1]
        tile(i - 1, tk, jnp.concatenate([b, b], axis=0))

    b = tab_ref[0, 0]
    tile(i, tk, jnp.concatenate([b, b], axis=0))
    o_ref[...] = _diff_finish(lam_ref, za_ref[...], sg_ref[...], l_sc[...], acc_sc[...], tq).astype(BF16)


def _da_prompt(lamv, qn, knb, vb, za, tab, sg, tq):
    t = qn.shape[0]
    tk = tq
    kern = functools.partial(_da_prompt_kernel, tq=tq, tk=tk)
    return pl.pallas_call(
        kern,
        grid=(DA_HEADS, t // tq),
        in_specs=[
            pl.BlockSpec(lamv.shape, lambda h, i: (0, 0)),
            pl.BlockSpec((tq, LANES), lambda h, i: (i, h)),
            pl.BlockSpec((t, LANES), lambda h, i: (0, h)),
            pl.BlockSpec((t, LANES), lambda h, i: (0, h)),
            pl.BlockSpec((tq, LANES), lambda h, i: (i, h)),
            pl.BlockSpec((1, 2, tq, tk), lambda h, i: (h, 0, 0, 0)),
            pl.BlockSpec(sg.shape, lambda h, i: (0, 0)),
        ],
        out_specs=pl.BlockSpec((tq, LANES), lambda h, i: (i, h)),
        out_shape=jax.ShapeDtypeStruct((t, DA_WIDTH), BF16),
        scratch_shapes=[pltpu.VMEM((2 * tq, LANES), F32)] * 3,
        compiler_params=_params(("arbitrary", "arbitrary")),
        name="diff_attn_prompt",
    )(lamv, qn, knb, vb, za, tab, sg)


def _da_sample_kernel(lam_ref, q_ref, ck_ref, cv_ref, kn_ref, vn_ref, za_ref, tabc_ref, tabn_ref, sg_ref,
                      o_ref, m_sc, l_sc, acc_sc, *, ts):
    cols = lambda h: slice(h * LANES, (h + 1) * LANES)
    heads = range(DA_HEADS)
    qs = [_stack_halves(q_ref[:, cols(h)]) for h in heads]
    m_sc[...] = jnp.full(m_sc.shape, NEG, F32)
    l_sc[...] = jnp.zeros(l_sc.shape, F32)
    acc_sc[...] = jnp.zeros(acc_sc.shape, F32)
    kc = ck_ref[0].astype(BF16)
    vc = cv_ref[0].astype(BF16)
    _softmax_step(qs, [kc[:, cols(h)] for h in heads], [vc[:, cols(h)] for h in heads], tabc_ref[...],
                  m_sc, l_sc, acc_sc)
    pad = jnp.zeros((LANES - ts, LANES), BF16)
    kn = [jnp.concatenate([kn_ref[:, cols(h)].astype(BF16), pad], axis=0) for h in heads]
    vn = [jnp.concatenate([vn_ref[:, cols(h)].astype(BF16), pad], axis=0) for h in heads]
    _softmax_step(qs, kn, vn, tabn_ref[...], m_sc, l_sc, acc_sc)
    l = l_sc[...]
    acc = acc_sc[...]
    za = za_ref[...]
    rows = lambda h: slice(h * 2 * ts, (h + 1) * 2 * ts)
    o_ref[...] = jnp.concatenate(
        [_diff_finish(lam_ref, za[:, cols(h)], sg_ref[...], l[rows(h)], acc[rows(h)], ts) for h in heads],
        axis=1).astype(BF16)


def _da_sample(lamv, qn, ck, cv, kn, vn, za, tabc, tabn, sg, ts):
    nb, past, _ = ck.shape
    kern = functools.partial(_da_sample_kernel, ts=ts)
    stack = lambda tab: jnp.concatenate([tab, tab], axis=1).reshape(DA_HEADS * 2 * ts, tab.shape[-1])
    tabc, tabn = stack(tabc), stack(tabn)
    rows = pl.BlockSpec((ts, DA_WIDTH), lambda b: (b, 0))
    cache = pl.BlockSpec((1, past, DA_WIDTH), lambda b: (b, 0, 0))
    full = lambda a: pl.BlockSpec(a.shape, lambda b: (0,) * a.ndim)
    return pl.pallas_call(
        kern,
        grid=(nb,),
        in_specs=[full(lamv), rows, cache, cache, rows, rows, rows, full(tabc), full(tabn), full(sg)],
        out_specs=rows,
        out_shape=jax.ShapeDtypeStruct((nb * ts, DA_WIDTH), BF16),
        scratch_shapes=[pltpu.VMEM((DA_HEADS * 2 * ts, LANES), F32)] * 3,
        compiler_params=_params(("arbitrary",)),
        name="diff_attn_sample",
    )(lamv, qn, ck, cv, kn, vn, za, tabc, tabn, sg)


def _bias_kernel(idx_ref, bt_ref, o_ref):
    idx = idx_ref[...]
    bt = bt_ref[...]
    bt = bt - bt[:, FAR_BUCKET:FAR_BUCKET + 1]
    onehot = jnp.where(lax.broadcasted_iota(jnp.int32, (N_BUCKETS, idx.shape[1]), 0) == idx, 1.0, 0.0).astype(BF16)
    hi = bt.astype(BF16)
    rest = bt - hi.astype(F32)
    mid = rest.astype(BF16)
    lo = (rest - mid.astype(F32)).astype(BF16)
    val = _dot(hi, onehot) + _dot(mid, onehot) + _dot(lo, onehot)
    o_ref[...] = jnp.where(idx < 0, NEG, val * LOG2E)


def _bias_tables(rel_bias, tq, ts, past):
    def buckets(rel, mask):
        return np.where(mask, _t5_bucket_np(rel), -1).astype(np.int32).reshape(-1)

    r = np.arange(tq)[:, None]
    c = np.arange(tq)[None, :]
    qpos = past + np.arange(ts)[:, None]
    kc = np.arange(past)[None, :]
    kn = past + np.arange(LANES)[None, :]
    parts = [buckets(c - r, (c // CHUNK) <= (r // CHUNK)),
             buckets(c - r - tq, np.ones((tq, tq), bool)),
             buckets(kc - qpos, (kc // CHUNK) <= (qpos // CHUNK)),
             buckets(kn - qpos, ((kn // CHUNK) <= (qpos // CHUNK)) & (kn < past + ts))]
    idx = np.concatenate(parts)
    n = idx.size
    bt = jnp.zeros((SUBLANES, N_BUCKETS), F32).at[:DA_HEADS].set(rel_bias.T)
    blk = 4096
    assert n % blk == 0
    tab = pl.pallas_call(
        _bias_kernel,
        grid=(n // blk,),
        in_specs=[pl.BlockSpec((1, blk), lambda i: (0, i)), pl.BlockSpec(bt.shape, lambda i: (0, 0))],
        out_specs=pl.BlockSpec((SUBLANES, blk), lambda i: (0, i)),
        out_shape=jax.ShapeDtypeStruct((SUBLANES, n), F32),
        compiler_params=_params(("arbitrary",)),
        name="t5_bias_tables",
    )(jnp.asarray(idx).reshape(1, n), bt)[:DA_HEADS]
    o1 = 2 * tq * tq
    o2 = o1 + ts * past
    prompt_tab = tab[:, :o1].reshape(DA_HEADS, 2, tq, tq)
    tabc = tab[:, o1:o2].reshape(DA_HEADS, ts, past)
    tabn = tab[:, o2:].reshape(DA_HEADS, ts, LANES)
    return prompt_tab, tabc, tabn


def _rwkv_kernel(prw_ref, look_ref, first_ref, zb_ref, s0_ref, mu_ref, w0_ref, a0_ref, wwa_ref,
                 kkw_ref, kaw_ref, rk_ref, lng_ref, lnb_ref, g2_ref, avg_ref, ones_ref, e_ref,
                 ob_ref, sfin_ref,
                 s_sc, w_sc, kk_sc, kka_sc, k_sc, v_sc, r_sc, wkk_sc, al_sc, be_sc, o_sc, *, tm):
    j = pl.program_id(1)

    @pl.when(j == 0)
    def _():
        s_sc[...] = s0_ref[0]

    x = prw_ref[...]
    row0 = jnp.where(j == 0, first_ref[0], look_ref[7:8, :])
    rows = lax.broadcasted_iota(jnp.int32, x.shape, 0)
    prev = jnp.where(rows == 0, row0, pltpu.roll(x, 1, axis=0))
    mix = x + (prev - x) * mu_ref[...]
    r = mix[:, 0:512]
    kr = mix[:, 512:1024]
    vr = mix[:, 1024:1536]
    la = mix[:, 1536:1664]
    lane = lax.broadcasted_iota(jnp.int32, la.shape, 1)
    la = jnp.where(lane < HEAD, jnp.tanh(la), la)
    hi = la.astype(BF16)
    lo = (la - hi.astype(F32)).astype(BF16)
    wwa = wwa_ref[...]
    lora = _dot(hi, wwa[0]) + _dot(lo, wwa[0]) + _dot(hi, wwa[1])
    wpre = w0_ref[...] + lora[:, 0:512]
    nw = -wpre
    w = -(jnp.maximum(nw, 0.0) + jnp.log(1.0 + jnp.exp(-jnp.abs(nw)))) - 0.5
    decay = jnp.exp(-jnp.exp(w))
    w_sc[...] = decay
    a = 1.0 / (1.0 + jnp.exp(-(a0_ref[...] + lora[:, 512:1024])))
    kk = kr * kkw_ref[...]
    nrm = jnp.sqrt(_dot2(kk * kk, ones_ref[...]))
    kk = kk / jnp.maximum(nrm, 1e-12)
    kk_sc[...] = kk
    kka = kk * a
    kka_sc[...] = kka
    kh = kr * (1.0 + (a - 1.0) * kaw_ref[...])
    k_sc[...] = kh
    v_sc[...] = vr
    r_sc[...] = r
    kk_next = pltpu.roll(kk, tm - 1, axis=0)
    wkk_sc[...] = decay * kk_next
    al_sc[...] = _dot2(kka * kk_next, ones_ref[...])
    be_sc[...] = _dot2(kh * kk_next, ones_ref[...])

    g2 = g2_ref[...]
    e4 = e_ref[...]

    def rowvec(blk, i):
        return jnp.concatenate(
            [jnp.broadcast_to(blk[i:i + 1, c * RW_STATE_LANES:(c + 1) * RW_STATE_LANES], (HEAD, RW_STATE_LANES))
             for c in range(RW_STATE_GROUPS)], axis=0)

    def group_sums(x):
        return _dot(x.astype(BF16), g2)

    nrow = RW_STATE_SHAPE[0]

    def steps(g, s):
        base = pl.multiple_of(g * SUBLANES, SUBLANES)
        kk_b, kka_b, k_b, v_b, w_b, r_b, wkk_b, al_b, be_b = (
            ref[pl.ds(base, SUBLANES), :]
            for ref in (kk_sc, kka_sc, k_sc, v_sc, w_sc, r_sc, wkk_sc, al_sc, be_sc))
        vcols = group_sums(jnp.concatenate([rowvec(v_b, i) * e4 for i in range(SUBLANES)], axis=0))

        def out_row(s_new, i):
            oe = group_sums(s_new * rowvec(r_b, i)) * e4
            return jnp.concatenate(
                [jnp.sum(oe[c * HEAD:(c + 1) * HEAD], axis=0, keepdims=True) for c in range(RW_STATE_GROUPS)],
                axis=1)

        out_rows = []
        for i in range(0, SUBLANES, 2):
            v0 = vcols[i * nrow:(i + 1) * nrow]
            v1 = vcols[(i + 1) * nrow:(i + 2) * nrow]
            red = group_sums(jnp.concatenate([s * rowvec(kk_b, i), s * rowvec(wkk_b, i)], axis=0))
            sa0 = red[:nrow]
            sa1 = red[nrow:] - sa0 * rowvec(al_b, i) + v0 * rowvec(be_b, i)
            s = s * rowvec(w_b, i) - sa0 * rowvec(kka_b, i) + v0 * rowvec(k_b, i)
            out_rows.append(out_row(s, i))
            s = s * rowvec(w_b, i + 1) - sa1 * rowvec(kka_b, i + 1) + v1 * rowvec(k_b, i + 1)
            out_rows.append(out_row(s, i + 1))
        o_sc[pl.ds(base, SUBLANES), :] = jnp.concatenate(out_rows, axis=0)
        return s

    s_sc[...] = lax.fori_loop(0, tm // SUBLANES, steps, s_sc[...])

    o = o_sc[...]
    mean = _dot2(o, avg_ref[...])
    d = o - mean
    var = _dot2(d * d, avg_ref[...])
    y = d * lax.rsqrt(var + RW_GN_EPS) * lng_ref[...] + lnb_ref[...]
    bonus = _dot2(r * kh * rk_ref[...], ones_ref[...]) * vr
    ob_ref[...] = ((y + bonus) * _silu(zb_ref[...])).astype(BF16)

    @pl.when(j == pl.num_programs(1) - 1)
    def _():
        sfin_ref[0] = s_sc[...]


def _rwkv(prw, first, zb, s0, p, seq_len, tm):
    m = prw.shape[0]
    nseq = m // seq_len
    ntile = seq_len // tm
    kern = functools.partial(_rwkv_kernel, tm=tm)
    full = lambda a: pl.BlockSpec(a.shape, lambda b, j: (0,) * a.ndim)
    rows = lambda c: pl.BlockSpec((tm, c), lambda b, j: (b * ntile + j, 0))
    look = pl.BlockSpec((8, RW_SHIFT_COLS), lambda b, j: (jnp.maximum((b * ntile + j) * (tm // 8) - 1, 0), 0))
    consts = [p["mu"], p["w0"], p["a0"], p["wwa"], p["kkw"], p["kaw"], p["rk"], p["lng"], p["lnb"],
              p["g2"], p["avg"], p["ones"], p["e4"]]
    return pl.pallas_call(
        kern,
        grid=(nseq, ntile),
        in_specs=[rows(RW_SHIFT_COLS), look,
                  pl.BlockSpec((1, 1, RW_SHIFT_COLS), lambda b, j: (b, 0, 0)),
                  rows(RW_WIDTH),
                  pl.BlockSpec((1,) + RW_STATE_SHAPE, lambda b, j: (b, 0, 0))] + [full(c) for c in consts],
        out_specs=[rows(RW_WIDTH), pl.BlockSpec((1,) + RW_STATE_SHAPE, lambda b, j: (b, 0, 0))],
        out_shape=[jax.ShapeDtypeStruct((m, RW_WIDTH), BF16), jax.ShapeDtypeStruct((nseq,) + RW_STATE_SHAPE, F32)],
        scratch_shapes=[pltpu.VMEM(RW_STATE_SHAPE, F32)] + [pltpu.VMEM((tm, RW_WIDTH), F32)] * 10,
        compiler_params=_params(("arbitrary", "arbitrary")),
        name="rwkv7",
    )(prw, prw, first, zb, s0, *consts)


def _state_to_rows(s):
    b = s.shape[0]
    per = RW_HEADS // RW_STATE_GROUPS
    return s.reshape(b, RW_STATE_GROUPS, per, HEAD, HEAD).transpose(0, 1, 3, 2, 4).reshape((b,) + RW_STATE_SHAPE)


def _rows_to_state(s):
    b = s.shape[0]
    per = RW_HEADS // RW_STATE_GROUPS
    return s.reshape(b, RW_STATE_GROUPS, HEAD, per, HEAD).transpose(0, 1, 3, 2, 4).reshape(b, RW_HEADS, HEAD, HEAD)


SKIP_LOG = -104.0


def _sb_tile(qs, k, v, u, c, mask):
    tk = k[0].shape[0]
    rows = qs[0].shape[0]
    z = jnp.concatenate([_dot_nt(q_p, k_p) for q_p, k_p in zip(qs, k)], axis=0)
    lg = -(jnp.maximum(z, 0.0) + jnp.log(1.0 + jnp.exp(-jnp.abs(z))))
    if mask is not None:
        lg = jnp.where(mask, lg, 0.0)
    hi = lg.astype(BF16)
    lo = (lg - hi.astype(F32)).astype(BF16)
    it = _dot(jnp.concatenate([hi, lo], axis=1), u)
    a = jnp.exp(z + it + jnp.concatenate([c] * (tk // LANES), axis=1))
    if mask is not None:
        a = jnp.where(mask, a, 0.0)
    a = a.astype(BF16)
    da = jnp.concatenate([_dot(a[p * rows:(p + 1) * rows], v_p) for p, v_p in enumerate(v)], axis=0)
    return da, c + jnp.broadcast_to(it[:, 0:1], c.shape)


def _sb_finish(acc, z):
    t = z.shape[0]
    lane = lax.broadcasted_iota(jnp.int32, z.shape, 1)
    return jnp.where(lane < HEAD, acc[:t], acc[t:]) * _silu(z)


def _sb_prompt_kernel(q_ref, k_ref, v_ref, z_ref, u_ref, o_ref, acc_sc, c_sc, *, tq):
    i = pl.program_id(1)
    tk = u_ref.shape[0] // 2
    r = tq // tk
    qs = _stack_halves(q_ref[...])
    u = u_ref[...]
    acc_sc[...] = jnp.zeros(acc_sc.shape, F32)
    c_sc[...] = jnp.zeros(c_sc.shape, F32)
    qrow = lax.broadcasted_iota(jnp.int32, (2 * tq, tk), 0) % tq
    col = lax.broadcasted_iota(jnp.int32, (2 * tq, tk), 1)

    def tile(j, mask):
        off = pl.multiple_of(j * tk, tk)
        da, c = _sb_tile([qs], [k_ref[pl.ds(off, tk), :]], [v_ref[pl.ds(off, tk), :]], u, c_sc[...], mask)
        acc_sc[...] += da
        c_sc[...] = c

    for d in range(r):
        jj = r - 1 - d
        tile(i * r + jj, (col + jj * tk) < qrow)

    def body(st):
        tile(i * r - 1 - st[0], None)
        return st[0] + 1, jnp.max(c_sc[...])

    lax.while_loop(lambda st: jnp.logical_and(st[0] < i * r, st[1] > SKIP_LOG), body,
                   (jnp.int32(0), jnp.max(c_sc[...])))
    o_ref[...] = _sb_finish(acc_sc[...], z_ref[...]).astype(BF16)


def _sb_prompt(qb, kb, vb, z, u, tq):
    t = qb.shape[0]
    kern = functools.partial(_sb_prompt_kernel, tq=tq)
    rows = pl.BlockSpec((tq, LANES), lambda h, i: (i, h))
    whole = pl.BlockSpec((t, LANES), lambda h, i: (0, h))
    return pl.pallas_call(
        kern,
        grid=(SB_HEADS // 2, t // tq),
        in_specs=[rows, whole, whole, rows, pl.BlockSpec(u.shape, lambda h, i: (0, 0))],
        out_specs=rows,
        out_shape=jax.ShapeDtypeStruct((t, SB_WIDTH), BF16),
        scratch_shapes=[pltpu.VMEM((2 * tq, LANES), F32)] * 2,
        compiler_params=_params(("arbitrary", "arbitrary")),
        name="stick_breaking_prompt",
    )(qb, kb, vb, z, u)


def _sb_sample_kernel(q_ref, ck_ref, cv_ref, kn_ref, vn_ref, z_ref, u_ref, o_ref, acc_sc, c_sc, *, ts):
    tk = u_ref.shape[0] // 2
    ntile = ck_ref.shape[1] // tk
    npair = SB_HEADS // 2
    cols = lambda p: slice(p * LANES, (p + 1) * LANES)
    qs = [_stack_halves(q_ref[:, cols(p)]) for p in range(npair)]
    u = u_ref[...]
    pad = jnp.zeros((tk - ts, LANES), BF16)
    kn = [jnp.concatenate([kn_ref[:, cols(p)], pad], axis=0) for p in range(npair)]
    vn = [jnp.concatenate([vn_ref[:, cols(p)], pad], axis=0) for p in range(npair)]
    rows = npair * 2 * ts
    qrow = lax.broadcasted_iota(jnp.int32, (rows, tk), 0) % ts
    col = lax.broadcasted_iota(jnp.int32, (rows, tk), 1)
    acc, c = _sb_tile(qs, kn, vn, u, jnp.zeros((rows, LANES), F32), col < qrow)
    acc_sc[...] = acc
    c_sc[...] = c

    def body(st):
        off = pl.multiple_of((ntile - 1 - st[0]) * tk, tk)
        kc = ck_ref[0, pl.ds(off, tk), :].astype(BF16)
        vc = cv_ref[0, pl.ds(off, tk), :].astype(BF16)
        da, c = _sb_tile(qs, [kc[:, cols(p)] for p in range(npair)], [vc[:, cols(p)] for p in range(npair)],
                         u, c_sc[...], None)
        acc_sc[...] += da
        c_sc[...] = c
        return st[0] + 1, jnp.max(c)

    lax.while_loop(lambda st: jnp.logical_and(st[0] < ntile, st[1] > SKIP_LOG), body,
                   (jnp.int32(0), jnp.max(c)))
    acc = acc_sc[...]
    z = z_ref[...]
    o_ref[...] = jnp.concatenate(
        [_sb_finish(acc[p * 2 * ts:(p + 1) * 2 * ts], z[:, cols(p)]) for p in range(npair)], axis=1).astype(BF16)


def _sb_sample(qb, ck, cv, kb, vb, z, u, ts):
    nb, past, _ = ck.shape
    kern = functools.partial(_sb_sample_kernel, ts=ts)
    rows = pl.BlockSpec((ts, SB_WIDTH), lambda b: (b, 0))
    cache = pl.BlockSpec((1, past, SB_WIDTH), lambda b: (b, 0, 0))
    return pl.pallas_call(
        kern,
        grid=(nb,),
        in_specs=[rows, cache, cache, rows, rows, rows, pl.BlockSpec(u.shape, lambda b: (0, 0))],
        out_specs=rows,
        out_shape=jax.ShapeDtypeStruct((nb * ts, SB_WIDTH), BF16),
        scratch_shapes=[pltpu.VMEM((SB_HEADS * ts, LANES), F32)] * 2,
        compiler_params=_params(("arbitrary",)),
        name="stick_breaking_sample",
    )(qb, ck, cv, kb, vb, z, u)


def _suffix_matrix(tk):
    j = np.arange(tk)[:, None]
    s = np.arange(tk)[None, :]
    u = (j >= s).astype(np.float32)
    return jnp.asarray(np.concatenate([u, u], axis=0), dtype=BF16)


def _row_tile(m):
    return 256 if m % 256 == 0 else m


def kernel(x_prompt, x_sample, cache_l0_k, cache_l0_v, state_l0_shift, state_l0_wkv, cache_l1_k, cache_l1_v, rel_bias, norm_l0, w_in_l0, w_out_l0, da_q_norm, da_k_norm, da_lambda_q1, da_lambda_k1, da_lambda_q2, da_lambda_k2, da_subln, rw_mu, rw_w0, rw_w_up, rw_a0, rw_a_up, rw_k_k, rw_k_a, rw_r_k, rw_lnx_g, rw_lnx_b, norm_l1, w_in_l1, w_out_l1):
    bp, t, d = x_prompt.shape
    nb, ts, _ = x_sample.shape
    past = cache_l0_k.shape[1]
    assert bp == 1 and ts <= LANES and t % 256 == 0 and past % 256 == 0

    row = lambda v: v.reshape(1, -1).astype(F32)
    w_in0 = w_in_l0.astype(BF16)
    w_in1 = w_in_l1.astype(BF16)
    w_out0a = w_out_l0[:DA_WIDTH].astype(BF16)
    w_out0b = w_out_l0[DA_WIDTH:].astype(BF16)
    w_out1 = w_out_l1.astype(BF16)
    qg = row(jnp.tile(da_q_norm, 512 // HEAD))
    kg = row(jnp.tile(da_k_norm, 512 // HEAD))
    lamv = jnp.stack([da_lambda_q1, da_lambda_k1, da_lambda_q2, da_lambda_k2]).astype(F32)
    sg = row(da_subln)
    zeros = jnp.zeros((HEAD, RW_WIDTH), F32)
    wwa_f = jnp.concatenate([jnp.concatenate([rw_w_up, zeros], axis=1),
                             jnp.concatenate([zeros, rw_a_up], axis=1)], axis=0)
    wwa_hi = wwa_f.astype(BF16)
    wwa = jnp.stack([wwa_hi, (wwa_f - wwa_hi.astype(F32)).astype(BF16)])
    e4 = np.equal(np.arange(RW_STATE_SHAPE[0])[:, None] % HEAD,
                  np.arange(RW_STATE_SHAPE[1])[None, :] % HEAD).astype(np.float32)
    rw = dict(mu=row(rw_mu), w0=row(rw_w0), a0=row(rw_a0), wwa=wwa, kkw=row(rw_k_k), kaw=row(rw_k_a),
              rk=row(rw_r_k), lng=row(rw_lnx_g), lnb=row(rw_lnx_b), g2=_group_matrix(RW_STATE_LANES, 1.0),
              avg=_group_matrix(RW_WIDTH, 1.0 / HEAD), ones=_group_matrix(RW_WIDTH, 1.0), e4=jnp.asarray(e4))
    tq_da = 256
    prompt_tab, tabc, tabn = _bias_tables(rel_bias.astype(F32), tq_da, ts, past)
    u_prompt = _suffix_matrix(2 * LANES)
    u_sample = _suffix_matrix(LANES)

    def layer0(x, first, s0, seq_len, attend):
        m = x.shape[0]
        tm = _row_tile(m)
        qn, kn, knb, v, vb, za, prw, zb = _inproj0(x, row(norm_l0), w_in0, qg, kg, tm)
        oa = attend(qn, kn, knb, v, vb, za)
        ob, sfin = _rwkv(prw, first, zb, _state_to_rows(s0), rw, seq_len, min(seq_len, 256))
        y = _outproj(x, [oa, ob], [w_out0a, w_out0b], tm)
        shift = prw.reshape(m // seq_len, seq_len, RW_SHIFT_COLS)[:, -1:]
        return y, kn, v, shift, _rows_to_state(sfin)

    def layer1(x, attend):
        m = x.shape[0]
        tm = _row_tile(m)
        qb, k, kb, v, vb, z = _inproj1(x, row(norm_l1), w_in1, tm)
        o = attend(qb, kb, vb, z)
        return _outproj(x, [o], [w_out1], tm), k, v

    xp = x_prompt.reshape(t, d)
    yp, k0p, v0p, shp, wkvp = layer0(
        xp, jnp.zeros((1, 1, RW_SHIFT_COLS), F32), jnp.zeros((1, RW_HEADS, HEAD, HEAD), F32), t,
        lambda qn, kn, knb, v, vb, za: _da_prompt(lamv, qn, knb, vb, za, prompt_tab, sg, tq_da))
    yp, k1p, v1p = layer1(yp, lambda qb, kb, vb, z: _sb_prompt(qb, kb, vb, z, u_prompt, 512))

    xs = x_sample.reshape(nb * ts, d)
    ck0 = cache_l0_k.reshape(nb, past, DA_WIDTH)
    cv0 = cache_l0_v.reshape(nb, past, DA_WIDTH)
    ys, k0s, v0s, shs, wkvs = layer0(
        xs, state_l0_shift, state_l0_wkv, ts,
        lambda qn, kn, knb, v, vb, za: _da_sample(lamv, qn, ck0, cv0, kn, v, za, tabc, tabn, sg, ts))
    ck1 = cache_l1_k.reshape(nb, past, SB_WIDTH)
    cv1 = cache_l1_v.reshape(nb, past, SB_WIDTH)
    ys, k1s, v1s = layer1(ys, lambda qb, kb, vb, z: _sb_sample(qb, ck1, cv1, kb, vb, z, u_sample, ts))

    return (yp.reshape(1, t, d), ys.reshape(nb, ts, d),
            k0p.reshape(1, t, DA_HEADS, LANES), v0p.reshape(1, t, DA_HEADS, LANES), shp, wkvp,
            k1p.reshape(1, t, SB_HEADS, HEAD), v1p.reshape(1, t, SB_HEADS, HEAD),
            k0s.reshape(nb, ts, DA_HEADS, LANES), v0s.reshape(nb, ts, DA_HEADS, LANES), shs, wkvs,
            k1s.reshape(nb, ts, SB_HEADS, HEAD), v1s.reshape(nb, ts, SB_HEADS, HEAD))
```

```python
import functools
import math

import numpy as np
import jax
import jax.numpy as jnp
from jax import lax
from jax.experimental import pallas as pl
from jax.experimental.pallas import tpu as pltpu

F32 = jnp.float32
BF16 = jnp.bfloat16

EPS = 1e-6
NEG = -1e30
CHUNK = 64
LANES = 128
SUBLANES = 8
HEAD = 64
DA_HEADS = 4
DA_WIDTH = 512
RW_WIDTH = 512
RW_HEADS = 8
RW_SHIFT_COLS = 3 * RW_WIDTH + 128
RW_GN_EPS = 64e-5
RW_STATE_LANES = 256
RW_STATE_GROUPS = RW_WIDTH // RW_STATE_LANES
RW_STATE_SHAPE = (RW_STATE_GROUPS * HEAD, RW_STATE_LANES)
SB_HEADS = 16
SB_WIDTH = 1024
N_BUCKETS = 32
MAX_DISTANCE = 128
LAMBDA_INIT = 0.8 - 0.6 * math.exp(-0.3 * 0)
LOG2E = math.log2(math.e)
VMEM_LIMIT = 56 * 1024 * 1024


def _params(sem):
    return pltpu.CompilerParams(dimension_semantics=sem, vmem_limit_bytes=VMEM_LIMIT)


def _dot(a, b):
    return jnp.dot(a, b, preferred_element_type=F32)


def _dot_nt(a, b):
    return lax.dot_general(a, b, (((1,), (1,)), ((), ())), preferred_element_type=F32)


def _dot2(x, w):
    hi = x.astype(BF16)
    lo = (x - hi.astype(F32)).astype(BF16)
    return _dot(hi, w) + _dot(lo, w)


def _silu(z):
    return z / (1.0 + jnp.exp(-z))


def _rms(x, g):
    return x * lax.rsqrt(jnp.mean(x * x, axis=-1, keepdims=True) + EPS) * g


def _group_matrix(n, scale):
    idx = np.arange(n) // HEAD
    return jnp.asarray((idx[:, None] == idx[None, :]).astype(np.float32) * scale, dtype=BF16)


def _inproj0_kernel(x_ref, g_ref, w_ref, qg_ref, kg_ref, avg_ref,
                    qn_ref, kn_ref, knb_ref, v_ref, vb_ref, za_ref, prw_ref, zb_ref):
    y = _dot(_rms(x_ref[...], g_ref[...]).astype(BF16), w_ref[...])

    def head_norm(t, g):
        sq = t * t
        half = avg_ref.shape[0]
        ms = jnp.concatenate([_dot2(sq[:, c:c + half], avg_ref[...]) for c in range(0, t.shape[1], half)], axis=1)
        return t * lax.rsqrt(ms + EPS) * g

    qn_ref[...] = (head_norm(y[:, 0:512], qg_ref[...]) * (HEAD ** -0.5 * LOG2E)).astype(BF16)
    kn = head_norm(y[:, 512:1024], kg_ref[...])
    knb_ref[...] = kn.astype(BF16)
    v = y[:, 1024:1536]
    vb_ref[...] = v.astype(BF16)
    for h in range(DA_HEADS):
        kn_ref[:, h, :] = kn[:, h * LANES:(h + 1) * LANES]
        v_ref[:, h, :] = v[:, h * LANES:(h + 1) * LANES]
    za_ref[...] = y[:, 1536:2048]
    prw_ref[...] = y[:, 2048:2048 + RW_SHIFT_COLS]
    zb_ref[...] = y[:, 2048 + RW_SHIFT_COLS:]


def _inproj0(x, g, w, qg, kg, tm):
    m, d = x.shape
    n = w.shape[1]
    row = lambda c: pl.BlockSpec((tm, c), lambda i: (i, 0))
    full = lambda a: pl.BlockSpec(a.shape, lambda i: (0,) * a.ndim)
    avg = _group_matrix(2 * LANES, 1.0 / HEAD)
    widths = [(512, BF16), (None, F32), (512, BF16), (None, F32), (512, BF16), (512, F32),
              (RW_SHIFT_COLS, F32), (512, F32)]
    heads = (DA_HEADS, LANES)
    return pl.pallas_call(
        _inproj0_kernel,
        grid=(m // tm,),
        in_specs=[row(d), full(g), full(w), full(qg), full(kg), full(avg)],
        out_specs=[row(c) if c else pl.BlockSpec((tm,) + heads, lambda i: (i, 0, 0)) for c, _ in widths],
        out_shape=[jax.ShapeDtypeStruct((m, c) if c else (m,) + heads, dt) for c, dt in widths],
        compiler_params=_params(("arbitrary",)),
        name="inproj0",
    )(x, g, w, qg, kg, avg)


def _inproj1_kernel(x_ref, g_ref, w_ref, qb_ref, k_ref, kb_ref, v_ref, vb_ref, z_ref):
    y = _dot(_rms(x_ref[...], g_ref[...]).astype(BF16), w_ref[...])
    qb_ref[...] = (y[:, 0:1024] * (HEAD ** -0.5)).astype(BF16)
    k = y[:, 1024:2048]
    k_ref[...] = k
    kb_ref[...] = k.astype(BF16)
    v = y[:, 2048:3072]
    v_ref[...] = v
    vb_ref[...] = v.astype(BF16)
    z_ref[...] = y[:, 3072:4096]


def _inproj1(x, g, w, tm):
    m, d = x.shape
    row = lambda c: pl.BlockSpec((tm, c), lambda i: (i, 0))
    full = lambda a: pl.BlockSpec(a.shape, lambda i: (0,) * a.ndim)
    dts = [BF16, F32, BF16, F32, BF16, F32]
    return pl.pallas_call(
        _inproj1_kernel,
        grid=(m // tm,),
        in_specs=[row(d), full(g), full(w)],
        out_specs=[row(1024) for _ in dts],
        out_shape=[jax.ShapeDtypeStruct((m, 1024), dt) for dt in dts],
        compiler_params=_params(("arbitrary",)),
        name="inproj1",
    )(x, g, w)


def _outproj_kernel(*refs):
    n = (len(refs) - 2) // 2
    x_ref, o_ref = refs[0], refs[-1]
    acc = x_ref[...]
    for a_ref, w_ref in zip(refs[1:1 + n], refs[1 + n:1 + 2 * n]):
        acc = acc + _dot(a_ref[...], w_ref[...])
    o_ref[...] = acc


def _outproj(x, acts, ws, tm):
    m, d = x.shape
    row = lambda c: pl.BlockSpec((tm, c), lambda i: (i, 0))
    full = lambda a: pl.BlockSpec(a.shape, lambda i: (0,) * a.ndim)
    return pl.pallas_call(
        _outproj_kernel,
        grid=(m // tm,),
        in_specs=[row(d)] + [row(a.shape[1]) for a in acts] + [full(w) for w in ws],
        out_specs=row(d),
        out_shape=jax.ShapeDtypeStruct((m, d), F32),
        compiler_params=_params(("arbitrary",)),
        name="outproj",
    )(x, *acts, *ws)


def _t5_bucket_np(rel):
    nb = N_BUCKETS // 2
    max_exact = nb // 2
    n = np.abs(rel)
    nf = np.maximum(n, 1).astype(np.float32)
    large = max_exact + (np.log(nf / np.float32(max_exact)) / np.float32(math.log(MAX_DISTANCE / max_exact))
                         * np.float32(nb - max_exact)).astype(np.int32)
    large = np.minimum(large, nb - 1)
    return np.where(rel > 0, nb, 0) + np.where(n < max_exact, n, large)


FAR_BUCKET = N_BUCKETS // 2 - 1
FAR_TILES = (8, 2, 1)


def _stack_halves(q):
    lane = lax.broadcasted_iota(jnp.int32, q.shape, 1)
    zero = jnp.zeros_like(q)
    return jnp.concatenate([jnp.where(lane < HEAD, q, zero), jnp.where(lane >= HEAD, q, zero)], axis=0)


def _softmax_step(qs, k, v, bias, m_sc, l_sc, acc_sc):
    s = jnp.concatenate([_dot_nt(q_h, k_h) for q_h, k_h in zip(qs, k)], axis=0)
    if bias is not None:
        s = s + bias
    _softmax_update(s, v, m_sc, l_sc, acc_sc)


def _softmax_update(s, v, m_sc, l_sc, acc_sc):
    reps = s.shape[1] // LANES
    m_old = m_sc[...]
    m_new = jnp.maximum(m_old, jnp.max(s, axis=-1, keepdims=True))
    alpha = jnp.exp2(m_old - m_new)
    p = jnp.exp2(s - jnp.concatenate([m_new] * reps, axis=1))
    psum = p[:, :LANES]
    for r in range(1, reps):
        psum = psum + p[:, r * LANES:(r + 1) * LANES]
    l_sc[...] = alpha * l_sc[...] + psum
    p = p.astype(BF16)
    rows = p.shape[0] // len(v)
    pv = jnp.concatenate([_dot(p[h * rows:(h + 1) * rows], v_h) for h, v_h in enumerate(v)], axis=0)
    acc_sc[...] = alpha * acc_sc[...] + pv
    m_sc[...] = m_new


def _diff_finish(lam_ref, za, sg, l, acc, tq):
    lv = lam_ref[...]
    lam = (jnp.exp(jnp.sum(lv[0:1] * lv[1:2], axis=-1, keepdims=True))
           - jnp.exp(jnp.sum(lv[2:3] * lv[3:4], axis=-1, keepdims=True)) + LAMBDA_INIT)
    o = acc / jnp.sum(l, axis=-1, keepdims=True)
    attn = o[:tq] - lam * o[tq:]
    return (_rms(attn, sg) * (1.0 - LAMBDA_INIT)) * _silu(za)


def _da_prompt_kernel(lam_ref, q_ref, k_ref, v_ref, za_ref, tab_ref, sg_ref, o_ref,
                      m_sc, l_sc, acc_sc, *, tq, tk):
    i = pl.program_id(1)
    qs = _stack_halves(q_ref[...])
    m_sc[...] = jnp.full(m_sc.shape, NEG, F32)
    l_sc[...] = jnp.zeros(l_sc.shape, F32)
    acc_sc[...] = jnp.zeros(acc_sc.shape, F32)

    def tile(j, width, bias):
        off = pl.multiple_of(j * tk, tk)
        _softmax_step([qs], [k_ref[pl.ds(off, width), :]], [v_ref[pl.ds(off, width), :]], bias,
                      m_sc, l_sc, acc_sc)

    nfar = jnp.maximum(i - 1, 0)

    done = 0
    for width in FAR_TILES:
        def far(j, c, width=width, done=done):
            tile(done + width * j, width * tk, None)
            return c

        trips = (nfar - done) // width
        lax.fori_loop(0, trips, far, 0)
        done = done + trips * width

    @pl.when(i >= 1)
    def _():
        b = jnp.concatenate([tab_ref[0, 1], tab_ref[0, 0]], axis=1)
        tile(i - 1, 2 * tk, jnp.concatenate([b, b], axis=0))

    @pl.when(i == 0)
    def _():
        b = tab_ref[0, 0]
        tile(0, tk, jnp.concatenate([b, b], axis=0))
    o_ref[...] = _diff_finish(lam_ref, za_ref[...], sg_ref[...], l_sc[...], acc_sc[...], tq).astype(BF16)


def _da_prompt(lamv, qn, knb, vb, za, tab, sg, tq):
    t = qn.shape[0]
    tk = tq
    kern = functools.partial(_da_prompt_kernel, tq=tq, tk=tk)
    return pl.pallas_call(
        kern,
        grid=(DA_HEADS, t // tq),
        in_specs=[
            pl.BlockSpec(lamv.shape, lambda h, i: (0, 0)),
            pl.BlockSpec((tq, LANES), lambda h, i: (i, h)),
            pl.BlockSpec((t, LANES), lambda h, i: (0, h)),
            pl.BlockSpec((t, LANES), lambda h, i: (0, h)),
            pl.BlockSpec((tq, LANES), lambda h, i: (i, h)),
            pl.BlockSpec((1, 2, tq, tk), lambda h, i: (h, 0, 0, 0)),
            pl.BlockSpec(sg.shape, lambda h, i: (0, 0)),
        ],
        out_specs=pl.BlockSpec((tq, LANES), lambda h, i: (i, h)),
        out_shape=jax.ShapeDtypeStruct((t, DA_WIDTH), BF16),
        scratch_shapes=[pltpu.VMEM((2 * tq, LANES), F32)] * 3,
        compiler_params=_params(("arbitrary", "arbitrary")),
        name="diff_attn_prompt",
    )(lamv, qn, knb, vb, za, tab, sg)


def _da_sample_kernel(lam_ref, q_ref, ck_ref, cv_ref, kn_ref, vn_ref, za_ref, tabc_ref, tabn_ref, sg_ref,
                      o_ref, m_sc, l_sc, acc_sc, *, ts):
    cols = lambda h: slice(h * LANES, (h + 1) * LANES)
    heads = range(DA_HEADS)
    qs = [_stack_halves(q_ref[:, cols(h)]) for h in heads]
    m_sc[...] = jnp.full(m_sc.shape, NEG, F32)
    l_sc[...] = jnp.zeros(l_sc.shape, F32)
    acc_sc[...] = jnp.zeros(acc_sc.shape, F32)
    kc = ck_ref[0].astype(BF16)
    vc = cv_ref[0].astype(BF16)
    _softmax_step(qs, [kc[:, cols(h)] for h in heads], [vc[:, cols(h)] for h in heads], tabc_ref[...],
                  m_sc, l_sc, acc_sc)
    pad = jnp.zeros((LANES - ts, LANES), BF16)
    kn = [jnp.concatenate([kn_ref[:, cols(h)].astype(BF16), pad], axis=0) for h in heads]
    vn = [jnp.concatenate([vn_ref[:, cols(h)].astype(BF16), pad], axis=0) for h in heads]
    _softmax_step(qs, kn, vn, tabn_ref[...], m_sc, l_sc, acc_sc)
    l = l_sc[...]
    acc = acc_sc[...]
    za = za_ref[...]
    rows = lambda h: slice(h * 2 * ts, (h + 1) * 2 * ts)
    o_ref[...] = jnp.concatenate(
        [_diff_finish(lam_ref, za[:, cols(h)], sg_ref[...], l[rows(h)], acc[rows(h)], ts) for h in heads],
        axis=1).astype(BF16)


def _da_sample(lamv, qn, ck, cv, kn, vn, za, tabc, tabn, sg, ts):
    nb, past, _ = ck.shape
    kern = functools.partial(_da_sample_kernel, ts=ts)
    stack = lambda tab: jnp.concatenate([tab, tab], axis=1).reshape(DA_HEADS * 2 * ts, tab.shape[-1])
    tabc, tabn = stack(tabc), stack(tabn)
    rows = pl.BlockSpec((ts, DA_WIDTH), lambda b: (b, 0))
    cache = pl.BlockSpec((1, past, DA_WIDTH), lambda b: (b, 0, 0))
    full = lambda a: pl.BlockSpec(a.shape, lambda b: (0,) * a.ndim)
    return pl.pallas_call(
        kern,
        grid=(nb,),
        in_specs=[full(lamv), rows, cache, cache, rows, rows, rows, full(tabc), full(tabn), full(sg)],
        out_specs=rows,
        out_shape=jax.ShapeDtypeStruct((nb * ts, DA_WIDTH), BF16),
        scratch_shapes=[pltpu.VMEM((DA_HEADS * 2 * ts, LANES), F32)] * 3,
        compiler_params=_params(("arbitrary",)),
        name="diff_attn_sample",
    )(lamv, qn, ck, cv, kn, vn, za, tabc, tabn, sg)


def _bias_kernel(idx_ref, bt_ref, o_ref):
    idx = idx_ref[...]
    bt = bt_ref[...]
    bt = bt - bt[:, FAR_BUCKET:FAR_BUCKET + 1]
    onehot = jnp.where(lax.broadcasted_iota(jnp.int32, (N_BUCKETS, idx.shape[1]), 0) == idx, 1.0, 0.0).astype(BF16)
    hi = bt.astype(BF16)
    rest = bt - hi.astype(F32)
    mid = rest.astype(BF16)
    lo = (rest - mid.astype(F32)).astype(BF16)
    val = _dot(hi, onehot) + _dot(mid, onehot) + _dot(lo, onehot)
    o_ref[...] = jnp.where(idx < 0, NEG, val * LOG2E)


def _bias_tables(rel_bias, tq, ts, past):
    def buckets(rel, mask):
        return np.where(mask, _t5_bucket_np(rel), -1).astype(np.int32).reshape(-1)

    r = np.arange(tq)[:, None]
    c = np.arange(tq)[None, :]
    qpos = past + np.arange(ts)[:, None]
    kc = np.arange(past)[None, :]
    kn = past + np.arange(LANES)[None, :]
    parts = [buckets(c - r, (c // CHUNK) <= (r // CHUNK)),
             buckets(c - r - tq, np.ones((tq, tq), bool)),
             buckets(kc - qpos, (kc // CHUNK) <= (qpos // CHUNK)),
             buckets(kn - qpos, ((kn // CHUNK) <= (qpos // CHUNK)) & (kn < past + ts))]
    idx = np.concatenate(parts)
    n = idx.size
    bt = jnp.zeros((SUBLANES, N_BUCKETS), F32).at[:DA_HEADS].set(rel_bias.T)
    blk = 4096
    assert n % blk == 0
    tab = pl.pallas_call(
        _bias_kernel,
        grid=(n // blk,),
        in_specs=[pl.BlockSpec((1, blk), lambda i: (0, i)), pl.BlockSpec(bt.shape, lambda i: (0, 0))],
        out_specs=pl.BlockSpec((SUBLANES, blk), lambda i: (0, i)),
        out_shape=jax.ShapeDtypeStruct((SUBLANES, n), F32),
        compiler_params=_params(("arbitrary",)),
        name="t5_bias_tables",
    )(jnp.asarray(idx).reshape(1, n), bt)[:DA_HEADS]
    o1 = 2 * tq * tq
    o2 = o1 + ts * past
    prompt_tab = tab[:, :o1].reshape(DA_HEADS, 2, tq, tq)
    tabc = tab[:, o1:o2].reshape(DA_HEADS, ts, past)
    tabn = tab[:, o2:].reshape(DA_HEADS, ts, LANES)
    return prompt_tab, tabc, tabn


def _rwkv_kernel(prw_ref, look_ref, first_ref, zb_ref, s0_ref, mu_ref, w0_ref, a0_ref, wwa_ref,
                 kkw_ref, kaw_ref, rk_ref, lng_ref, lnb_ref, g2_ref, e_ref,
                 ob_ref, sfin_ref,
                 s_sc, w_sc, kk_sc, kka_sc, k_sc, v_sc, r_sc, wkk_sc, al_sc, be_sc, o_sc, *, tm):
    j = pl.program_id(1)

    @pl.when(j == 0)
    def _():
        s_sc[...] = s0_ref[0]

    g2 = g2_ref[...]

    def head_sums(t):
        return jnp.concatenate([_dot2(t[:, c * RW_STATE_LANES:(c + 1) * RW_STATE_LANES], g2)
                                for c in range(RW_STATE_GROUPS)], axis=1)

    x = prw_ref[...]
    row0 = jnp.where(j == 0, first_ref[0], look_ref[7:8, :])
    rows = lax.broadcasted_iota(jnp.int32, x.shape, 0)
    prev = jnp.where(rows == 0, row0, pltpu.roll(x, 1, axis=0))
    mix = x + (prev - x) * mu_ref[...]
    r = mix[:, 0:512]
    kr = mix[:, 512:1024]
    vr = mix[:, 1024:1536]
    la = mix[:, 1536:1664]
    lane = lax.broadcasted_iota(jnp.int32, la.shape, 1)
    la = jnp.where(lane < HEAD, jnp.tanh(la), la)
    hi = la.astype(BF16)
    lo = (la - hi.astype(F32)).astype(BF16)
    wwa = wwa_ref[...]
    lora = _dot(hi, wwa[0]) + _dot(lo, wwa[0]) + _dot(hi, wwa[1])
    wpre = w0_ref[...] + lora[:, 0:512]
    nw = -wpre
    w = -(jnp.maximum(nw, 0.0) + jnp.log(1.0 + jnp.exp(-jnp.abs(nw)))) - 0.5
    decay = jnp.exp(-jnp.exp(w))
    w_sc[...] = decay
    a = 1.0 / (1.0 + jnp.exp(-(a0_ref[...] + lora[:, 512:1024])))
    kk = kr * kkw_ref[...]
    nrm = jnp.sqrt(head_sums(kk * kk))
    kk = kk / jnp.maximum(nrm, 1e-12)
    kk_sc[...] = kk
    kka = kk * a
    kka_sc[...] = kka
    kh = kr * (1.0 + (a - 1.0) * kaw_ref[...])
    k_sc[...] = kh
    v_sc[...] = vr
    r_sc[...] = r
    kk_next = pltpu.roll(kk, tm - 1, axis=0)
    wkk_sc[...] = decay * kk_next
    al_sc[...] = head_sums(kka * kk_next)
    be_sc[...] = head_sums(kh * kk_next)

    e4 = e_ref[...]

    def rowvec(blk, i):
        return jnp.concatenate(
            [jnp.broadcast_to(blk[i:i + 1, c * RW_STATE_LANES:(c + 1) * RW_STATE_LANES], (HEAD, RW_STATE_LANES))
             for c in range(RW_STATE_GROUPS)], axis=0)

    def group_sums(x):
        return _dot(x.astype(BF16), g2)

    nrow = RW_STATE_SHAPE[0]

    def steps(g, s):
        base = pl.multiple_of(g * SUBLANES, SUBLANES)
        kk_b, kka_b, k_b, v_b, w_b, r_b, wkk_b, al_b, be_b = (
            ref[pl.ds(base, SUBLANES), :]
            for ref in (kk_sc, kka_sc, k_sc, v_sc, w_sc, r_sc, wkk_sc, al_sc, be_sc))
        vcols = group_sums(jnp.concatenate([rowvec(v_b, i) * e4 for i in range(SUBLANES)], axis=0))

        def out_row(s_new, i):
            oe = group_sums(s_new * rowvec(r_b, i)) * e4
            return jnp.concatenate(
                [jnp.sum(oe[c * HEAD:(c + 1) * HEAD], axis=0, keepdims=True) for c in range(RW_STATE_GROUPS)],
                axis=1)

        out_rows = []
        for i in range(0, SUBLANES, 2):
            v0 = vcols[i * nrow:(i + 1) * nrow]
            v1 = vcols[(i + 1) * nrow:(i + 2) * nrow]
            red = group_sums(jnp.concatenate([s * rowvec(kk_b, i), s * rowvec(wkk_b, i)], axis=0))
            sa0 = red[:nrow]
            sa1 = red[nrow:] - sa0 * rowvec(al_b, i) + v0 * rowvec(be_b, i)
            s = s * rowvec(w_b, i) - sa0 * rowvec(kka_b, i) + v0 * rowvec(k_b, i)
            out_rows.append(out_row(s, i))
            s = s * rowvec(w_b, i + 1) - sa1 * rowvec(kka_b, i + 1) + v1 * rowvec(k_b, i + 1)
            out_rows.append(out_row(s, i + 1))
        o_sc[pl.ds(base, SUBLANES), :] = jnp.concatenate(out_rows, axis=0)
        return s

    s_sc[...] = lax.fori_loop(0, tm // SUBLANES, steps, s_sc[...])

    o = o_sc[...]
    mean = head_sums(o) * (1.0 / HEAD)
    d = o - mean
    var = head_sums(d * d) * (1.0 / HEAD)
    y = d * lax.rsqrt(var + RW_GN_EPS) * lng_ref[...] + lnb_ref[...]
    bonus = head_sums(r * kh * rk_ref[...]) * vr
    ob_ref[...] = ((y + bonus) * _silu(zb_ref[...])).astype(BF16)

    @pl.when(j == pl.num_programs(1) - 1)
    def _():
        sfin_ref[0] = s_sc[...]


def _rwkv(prw, first, zb, s0, p, seq_len, tm):
    m = prw.shape[0]
    nseq = m // seq_len
    ntile = seq_len // tm
    kern = functools.partial(_rwkv_kernel, tm=tm)
    full = lambda a: pl.BlockSpec(a.shape, lambda b, j: (0,) * a.ndim)
    rows = lambda c: pl.BlockSpec((tm, c), lambda b, j: (b * ntile + j, 0))
    look = pl.BlockSpec((8, RW_SHIFT_COLS), lambda b, j: (jnp.maximum((b * ntile + j) * (tm // 8) - 1, 0), 0))
    consts = [p["mu"], p["w0"], p["a0"], p["wwa"], p["kkw"], p["kaw"], p["rk"], p["lng"], p["lnb"],
              p["g2"], p["e4"]]
    return pl.pallas_call(
        kern,
        grid=(nseq, ntile),
        in_specs=[rows(RW_SHIFT_COLS), look,
                  pl.BlockSpec((1, 1, RW_SHIFT_COLS), lambda b, j: (b, 0, 0)),
                  rows(RW_WIDTH),
                  pl.BlockSpec((1,) + RW_STATE_SHAPE, lambda b, j: (b, 0, 0))] + [full(c) for c in consts],
        out_specs=[rows(RW_WIDTH), pl.BlockSpec((1,) + RW_STATE_SHAPE, lambda b, j: (b, 0, 0))],
        out_shape=[jax.ShapeDtypeStruct((m, RW_WIDTH), BF16), jax.ShapeDtypeStruct((nseq,) + RW_STATE_SHAPE, F32)],
        scratch_shapes=[pltpu.VMEM(RW_STATE_SHAPE, F32)] + [pltpu.VMEM((tm, RW_WIDTH), F32)] * 10,
        compiler_params=_params(("arbitrary", "arbitrary")),
        name="rwkv7",
    )(prw, prw, first, zb, s0, *consts)


def _state_to_rows(s):
    b = s.shape[0]
    per = RW_HEADS // RW_STATE_GROUPS
    return s.reshape(b, RW_STATE_GROUPS, per, HEAD, HEAD).transpose(0, 1, 3, 2, 4).reshape((b,) + RW_STATE_SHAPE)


def _rows_to_state(s):
    b = s.shape[0]
    per = RW_HEADS // RW_STATE_GROUPS
    return s.reshape(b, RW_STATE_GROUPS, HEAD, per, HEAD).transpose(0, 1, 3, 2, 4).reshape(b, RW_HEADS, HEAD, HEAD)


SKIP_LOG = -104.0


def _sb_tile(qs, k, v, u, c, mask):
    tk = k[0].shape[0]
    rows = qs[0].shape[0]
    z = jnp.concatenate([_dot_nt(q_p, k_p) for q_p, k_p in zip(qs, k)], axis=0)
    lg = -(jnp.maximum(z, 0.0) + jnp.log(1.0 + jnp.exp(-jnp.abs(z))))
    if mask is not None:
        lg = jnp.where(mask, lg, 0.0)
    hi = lg.astype(BF16)
    lo = (lg - hi.astype(F32)).astype(BF16)
    it = _dot(jnp.concatenate([hi, lo], axis=1), u)
    a = jnp.exp(z + it + jnp.concatenate([c] * (tk // LANES), axis=1))
    if mask is not None:
        a = jnp.where(mask, a, 0.0)
    a = a.astype(BF16)
    da = jnp.concatenate([_dot(a[p * rows:(p + 1) * rows], v_p) for p, v_p in enumerate(v)], axis=0)
    return da, c + jnp.broadcast_to(it[:, 0:1], c.shape)


def _sb_finish(acc, z):
    t = z.shape[0]
    lane = lax.broadcasted_iota(jnp.int32, z.shape, 1)
    return jnp.where(lane < HEAD, acc[:t], acc[t:]) * _silu(z)


def _sb_prompt_kernel(q_ref, k_ref, v_ref, z_ref, u_ref, o_ref, acc_sc, c_sc, *, tq):
    i = pl.program_id(1)
    tk = u_ref.shape[0] // 2
    r = tq // tk
    qs = _stack_halves(q_ref[...])
    u = u_ref[...]
    acc_sc[...] = jnp.zeros(acc_sc.shape, F32)
    c_sc[...] = jnp.zeros(c_sc.shape, F32)
    def tile(j, lo, masked):
        n = tq - lo
        halves = lambda x: jnp.concatenate([x[lo:tq], x[tq + lo:2 * tq]], axis=0)
        off = pl.multiple_of(j * tk, tk)
        mask = None
        if masked:
            qrow = lax.broadcasted_iota(jnp.int32, (2 * n, tk), 0) % n
            col = lax.broadcasted_iota(jnp.int32, (2 * n, tk), 1)
            mask = col < qrow
        da, c = _sb_tile([halves(qs)], [k_ref[pl.ds(off, tk), :]], [v_ref[pl.ds(off, tk), :]], u,
                         halves(c_sc[...]), mask)
        for h in range(2):
            acc_sc[h * tq + lo:(h + 1) * tq, :] += da[h * n:(h + 1) * n]
            c_sc[h * tq + lo:(h + 1) * tq, :] = c[h * n:(h + 1) * n]

    for jj in reversed(range(r)):
        tile(i * r + jj, jj * tk, True)

    def body(st):
        tile(i * r - 1 - st[0], 0, False)
        return st[0] + 1, jnp.max(c_sc[...])

    lax.while_loop(lambda st: jnp.logical_and(st[0] < i * r, st[1] > SKIP_LOG), body,
                   (jnp.int32(0), jnp.max(c_sc[...])))
    o_ref[...] = _sb_finish(acc_sc[...], z_ref[...]).astype(BF16)


def _sb_prompt(qb, kb, vb, z, u, tq):
    t = qb.shape[0]
    kern = functools.partial(_sb_prompt_kernel, tq=tq)
    rows = pl.BlockSpec((tq, LANES), lambda h, i: (i, h))
    whole = pl.BlockSpec((t, LANES), lambda h, i: (0, h))
    return pl.pallas_call(
        kern,
        grid=(SB_HEADS // 2, t // tq),
        in_specs=[rows, whole, whole, rows, pl.BlockSpec(u.shape, lambda h, i: (0, 0))],
        out_specs=rows,
        out_shape=jax.ShapeDtypeStruct((t, SB_WIDTH), BF16),
        scratch_shapes=[pltpu.VMEM((2 * tq, LANES), F32)] * 2,
        compiler_params=_params(("arbitrary", "arbitrary")),
        name="stick_breaking_prompt",
    )(qb, kb, vb, z, u)


def _sb_sample_kernel(q_ref, ck_ref, cv_ref, kn_ref, vn_ref, z_ref, u_ref, o_ref, acc_sc, c_sc, *, ts):
    tk = u_ref.shape[0] // 2
    ntile = ck_ref.shape[1] // tk
    npair = SB_HEADS // 2
    cols = lambda p: slice(p * LANES, (p + 1) * LANES)
    qs = [_stack_halves(q_ref[:, cols(p)]) for p in range(npair)]
    u = u_ref[...]
    pad = jnp.zeros((tk - ts, LANES), BF16)
    kn = [jnp.concatenate([kn_ref[:, cols(p)], pad], axis=0) for p in range(npair)]
    vn = [jnp.concatenate([vn_ref[:, cols(p)], pad], axis=0) for p in range(npair)]
    rows = npair * 2 * ts
    qrow = lax.broadcasted_iota(jnp.int32, (rows, tk), 0) % ts
    col = lax.broadcasted_iota(jnp.int32, (rows, tk), 1)
    acc, c = _sb_tile(qs, kn, vn, u, jnp.zeros((rows, LANES), F32), col < qrow)
    acc_sc[...] = acc
    c_sc[...] = c

    def body(st):
        off = pl.multiple_of((ntile - 1 - st[0]) * tk, tk)
        kc = ck_ref[0, pl.ds(off, tk), :].astype(BF16)
        vc = cv_ref[0, pl.ds(off, tk), :].astype(BF16)
        da, c = _sb_tile(qs, [kc[:, cols(p)] for p in range(npair)], [vc[:, cols(p)] for p in range(npair)],
                         u, c_sc[...], None)
        acc_sc[...] += da
        c_sc[...] = c
        return st[0] + 1, jnp.max(c)

    lax.while_loop(lambda st: jnp.logical_and(st[0] < ntile, st[1] > SKIP_LOG), body,
                   (jnp.int32(0), jnp.max(c)))
    acc = acc_sc[...]
    z = z_ref[...]
    o_ref[...] = jnp.concatenate(
        [_sb_finish(acc[p * 2 * ts:(p + 1) * 2 * ts], z[:, cols(p)]) for p in range(npair)], axis=1).astype(BF16)


def _sb_sample(qb, ck, cv, kb, vb, z, u, ts):
    nb, past, _ = ck.shape
    kern = functools.partial(_sb_sample_kernel, ts=ts)
    rows = pl.BlockSpec((ts, SB_WIDTH), lambda b: (b, 0))
    cache = pl.BlockSpec((1, past, SB_WIDTH), lambda b: (b, 0, 0))
    return pl.pallas_call(
        kern,
        grid=(nb,),
        in_specs=[rows, cache, cache, rows, rows, rows, pl.BlockSpec(u.shape, lambda b: (0, 0))],
        out_specs=rows,
        out_shape=jax.ShapeDtypeStruct((nb * ts, SB_WIDTH), BF16),
        scratch_shapes=[pltpu.VMEM((SB_HEADS * ts, LANES), F32)] * 2,
        compiler_params=_params(("arbitrary",)),
        name="stick_breaking_sample",
    )(qb, ck, cv, kb, vb, z, u)


def _suffix_matrix(tk):
    j = np.arange(tk)[:, None]
    s = np.arange(tk)[None, :]
    u = (j >= s).astype(np.float32)
    return jnp.asarray(np.concatenate([u, u], axis=0), dtype=BF16)


def _row_tile(m):
    return 256 if m % 256 == 0 else m


def kernel(x_prompt, x_sample, cache_l0_k, cache_l0_v, state_l0_shift, state_l0_wkv, cache_l1_k, cache_l1_v, rel_bias, norm_l0, w_in_l0, w_out_l0, da_q_norm, da_k_norm, da_lambda_q1, da_lambda_k1, da_lambda_q2, da_lambda_k2, da_subln, rw_mu, rw_w0, rw_w_up, rw_a0, rw_a_up, rw_k_k, rw_k_a, rw_r_k, rw_lnx_g, rw_lnx_b, norm_l1, w_in_l1, w_out_l1):
    bp, t, d = x_prompt.shape
    nb, ts, _ = x_sample.shape
    past = cache_l0_k.shape[1]
    assert bp == 1 and ts <= LANES and t % 256 == 0 and past % 256 == 0

    row = lambda v: v.reshape(1, -1).astype(F32)
    w_in0 = w_in_l0.astype(BF16)
    w_in1 = w_in_l1.astype(BF16)
    w_out0a = w_out_l0[:DA_WIDTH].astype(BF16)
    w_out0b = w_out_l0[DA_WIDTH:].astype(BF16)
    w_out1 = w_out_l1.astype(BF16)
    qg = row(jnp.tile(da_q_norm, 512 // HEAD))
    kg = row(jnp.tile(da_k_norm, 512 // HEAD))
    lamv = jnp.stack([da_lambda_q1, da_lambda_k1, da_lambda_q2, da_lambda_k2]).astype(F32)
    sg = row(da_subln)
    zeros = jnp.zeros((HEAD, RW_WIDTH), F32)
    wwa_f = jnp.concatenate([jnp.concatenate([rw_w_up, zeros], axis=1),
                             jnp.concatenate([zeros, rw_a_up], axis=1)], axis=0)
    wwa_hi = wwa_f.astype(BF16)
    wwa = jnp.stack([wwa_hi, (wwa_f - wwa_hi.astype(F32)).astype(BF16)])
    e4 = np.equal(np.arange(RW_STATE_SHAPE[0])[:, None] % HEAD,
                  np.arange(RW_STATE_SHAPE[1])[None, :] % HEAD).astype(np.float32)
    rw = dict(mu=row(rw_mu), w0=row(rw_w0), a0=row(rw_a0), wwa=wwa, kkw=row(rw_k_k), kaw=row(rw_k_a),
              rk=row(rw_r_k), lng=row(rw_lnx_g), lnb=row(rw_lnx_b), g2=_group_matrix(RW_STATE_LANES, 1.0),
              e4=jnp.asarray(e4))
    tq_da = 256
    prompt_tab, tabc, tabn = _bias_tables(rel_bias.astype(F32), tq_da, ts, past)
    u_prompt = _suffix_matrix(2 * LANES)
    u_sample = _suffix_matrix(LANES)

    def layer0(x, first, s0, seq_len, attend):
        m = x.shape[0]
        tm = _row_tile(m)
        qn, kn, knb, v, vb, za, prw, zb = _inproj0(x, row(norm_l0), w_in0, qg, kg, tm)
        oa = attend(qn, kn, knb, v, vb, za)
        ob, sfin = _rwkv(prw, first, zb, _state_to_rows(s0), rw, seq_len, min(seq_len, 256))
        y = _outproj(x, [oa, ob], [w_out0a, w_out0b], tm)
        shift = prw.reshape(m // seq_len, seq_len, RW_SHIFT_COLS)[:, -1:]
        return y, kn, v, shift, _rows_to_state(sfin)

    def layer1(x, attend):
        m = x.shape[0]
        tm = _row_tile(m)
        qb, k, kb, v, vb, z = _inproj1(x, row(norm_l1), w_in1, tm)
        o = attend(qb, kb, vb, z)
        return _outproj(x, [o], [w_out1], tm), k, v

    xp = x_prompt.reshape(t, d)
    yp, k0p, v0p, shp, wkvp = layer0(
        xp, jnp.zeros((1, 1, RW_SHIFT_COLS), F32), jnp.zeros((1, RW_HEADS, HEAD, HEAD), F32), t,
        lambda qn, kn, knb, v, vb, za: _da_prompt(lamv, qn, knb, vb, za, prompt_tab, sg, tq_da))
    yp, k1p, v1p = layer1(yp, lambda qb, kb, vb, z: _sb_prompt(qb, kb, vb, z, u_prompt, 512))

    xs = x_sample.reshape(nb * ts, d)
    ck0 = cache_l0_k.reshape(nb, past, DA_WIDTH)
    cv0 = cache_l0_v.reshape(nb, past, DA_WIDTH)
    ys, k0s, v0s, shs, wkvs = layer0(
        xs, state_l0_shift, state_l0_wkv, ts,
        lambda qn, kn, knb, v, vb, za: _da_sample(lamv, qn, ck0, cv0, knb, vb, za, tabc, tabn, sg, ts))
    ck1 = cache_l1_k.reshape(nb, past, SB_WIDTH)
    cv1 = cache_l1_v.reshape(nb, past, SB_WIDTH)
    ys, k1s, v1s = layer1(ys, lambda qb, kb, vb, z: _sb_sample(qb, ck1, cv1, kb, vb, z, u_sample, ts))

    return (yp.reshape(1, t, d), ys.reshape(nb, ts, d),
            k0p.reshape(1, t, DA_HEADS, LANES), v0p.reshape(1, t, DA_HEADS, LANES), shp, wkvp,
            k1p.reshape(1, t, SB_HEADS, HEAD), v1p.reshape(1, t, SB_HEADS, HEAD),
            k0s.reshape(nb, ts, DA_HEADS, LANES), v0s.reshape(nb, ts, DA_HEADS, LANES), shs, wkvs,
            k1s.reshape(nb, ts, SB_HEADS, HEAD), v1s.reshape(nb, ts, SB_HEADS, HEAD))
```

```python
import functools
import math

import numpy as np
import jax
import jax.numpy as jnp
from jax import lax
from jax.experimental import pallas as pl
from jax.experimental.pallas import tpu as pltpu

F32 = jnp.float32
BF16 = jnp.bfloat16

EPS = 1e-6
NEG = -1e30
CHUNK = 64
LANES = 128
SUBLANES = 8
HEAD = 64
DA_HEADS = 4
DA_WIDTH = 512
RW_WIDTH = 512
RW_HEADS = 8
RW_SHIFT_COLS = 3 * RW_WIDTH + 128
RW_GN_EPS = 64e-5
RW_STATE_LANES = 256
RW_STATE_GROUPS = RW_WIDTH // RW_STATE_LANES
RW_STATE_SHAPE = (RW_STATE_GROUPS * HEAD, RW_STATE_LANES)
RW_BLOCK = 8
SB_HEADS = 16
SB_WIDTH = 1024
N_BUCKETS = 32
MAX_DISTANCE = 128
LAMBDA_INIT = 0.8 - 0.6 * math.exp(-0.3 * 0)
LOG2E = math.log2(math.e)
VMEM_LIMIT = 56 * 1024 * 1024


def _params(sem):
    return pltpu.CompilerParams(dimension_semantics=sem, vmem_limit_bytes=VMEM_LIMIT)


def _dot(a, b):
    return jnp.dot(a, b, preferred_element_type=F32)


def _dot_nt(a, b):
    return lax.dot_general(a, b, (((1,), (1,)), ((), ())), preferred_element_type=F32)


def _dot2(x, w):
    hi = x.astype(BF16)
    lo = (x - hi.astype(F32)).astype(BF16)
    return _dot(hi, w) + _dot(lo, w)


def _silu(z):
    return z / (1.0 + jnp.exp(-z))


def _rms(x, g):
    return x * lax.rsqrt(jnp.mean(x * x, axis=-1, keepdims=True) + EPS) * g


def _group_matrix(n, scale):
    idx = np.arange(n) // HEAD
    return jnp.asarray((idx[:, None] == idx[None, :]).astype(np.float32) * scale, dtype=BF16)


def _inproj0_kernel(x_ref, g_ref, w_ref, qg_ref, kg_ref, avg_ref,
                    qn_ref, kn_ref, knb_ref, v_ref, vb_ref, za_ref, prw_ref, zb_ref):
    y = _dot(_rms(x_ref[...], g_ref[...]).astype(BF16), w_ref[...])

    def head_norm(t, g):
        sq = t * t
        half = avg_ref.shape[0]
        ms = jnp.concatenate([_dot2(sq[:, c:c + half], avg_ref[...]) for c in range(0, t.shape[1], half)], axis=1)
        return t * lax.rsqrt(ms + EPS) * g

    qn_ref[...] = (head_norm(y[:, 0:512], qg_ref[...]) * (HEAD ** -0.5 * LOG2E)).astype(BF16)
    kn = head_norm(y[:, 512:1024], kg_ref[...])
    knb_ref[...] = kn.astype(BF16)
    v = y[:, 1024:1536]
    vb_ref[...] = v.astype(BF16)
    for h in range(DA_HEADS):
        kn_ref[:, h, :] = kn[:, h * LANES:(h + 1) * LANES]
        v_ref[:, h, :] = v[:, h * LANES:(h + 1) * LANES]
    za_ref[...] = y[:, 1536:2048]
    prw_ref[...] = y[:, 2048:2048 + RW_SHIFT_COLS]
    zb_ref[...] = y[:, 2048 + RW_SHIFT_COLS:]


def _inproj0(x, g, w, qg, kg, tm):
    m, d = x.shape
    n = w.shape[1]
    row = lambda c: pl.BlockSpec((tm, c), lambda i: (i, 0))
    full = lambda a: pl.BlockSpec(a.shape, lambda i: (0,) * a.ndim)
    avg = _group_matrix(2 * LANES, 1.0 / HEAD)
    widths = [(512, BF16), (None, F32), (512, BF16), (None, F32), (512, BF16), (512, F32),
              (RW_SHIFT_COLS, F32), (512, F32)]
    heads = (DA_HEADS, LANES)
    return pl.pallas_call(
        _inproj0_kernel,
        grid=(m // tm,),
        in_specs=[row(d), full(g), full(w), full(qg), full(kg), full(avg)],
        out_specs=[row(c) if c else pl.BlockSpec((tm,) + heads, lambda i: (i, 0, 0)) for c, _ in widths],
        out_shape=[jax.ShapeDtypeStruct((m, c) if c else (m,) + heads, dt) for c, dt in widths],
        compiler_params=_params(("arbitrary",)),
        name="inproj0",
    )(x, g, w, qg, kg, avg)


def _inproj1_kernel(x_ref, g_ref, w_ref, qb_ref, k_ref, kb_ref, v_ref, vb_ref, z_ref):
    y = _dot(_rms(x_ref[...], g_ref[...]).astype(BF16), w_ref[...])
    qb_ref[...] = (y[:, 0:1024] * (HEAD ** -0.5)).astype(BF16)
    k = y[:, 1024:2048]
    k_ref[...] = k
    kb_ref[...] = k.astype(BF16)
    v = y[:, 2048:3072]
    v_ref[...] = v
    vb_ref[...] = v.astype(BF16)
    z_ref[...] = y[:, 3072:4096]


def _inproj1(x, g, w, tm):
    m, d = x.shape
    row = lambda c: pl.BlockSpec((tm, c), lambda i: (i, 0))
    full = lambda a: pl.BlockSpec(a.shape, lambda i: (0,) * a.ndim)
    dts = [BF16, F32, BF16, F32, BF16, F32]
    return pl.pallas_call(
        _inproj1_kernel,
        grid=(m // tm,),
        in_specs=[row(d), full(g), full(w)],
        out_specs=[row(1024) for _ in dts],
        out_shape=[jax.ShapeDtypeStruct((m, 1024), dt) for dt in dts],
        compiler_params=_params(("arbitrary",)),
        name="inproj1",
    )(x, g, w)


def _outproj_kernel(*refs):
    n = (len(refs) - 2) // 2
    x_ref, o_ref = refs[0], refs[-1]
    acc = x_ref[...]
    for a_ref, w_ref in zip(refs[1:1 + n], refs[1 + n:1 + 2 * n]):
        acc = acc + _dot(a_ref[...], w_ref[...])
    o_ref[...] = acc


def _outproj(x, acts, ws, tm):
    m, d = x.shape
    row = lambda c: pl.BlockSpec((tm, c), lambda i: (i, 0))
    full = lambda a: pl.BlockSpec(a.shape, lambda i: (0,) * a.ndim)
    return pl.pallas_call(
        _outproj_kernel,
        grid=(m // tm,),
        in_specs=[row(d)] + [row(a.shape[1]) for a in acts] + [full(w) for w in ws],
        out_specs=row(d),
        out_shape=jax.ShapeDtypeStruct((m, d), F32),
        compiler_params=_params(("arbitrary",)),
        name="outproj",
    )(x, *acts, *ws)


def _t5_bucket_np(rel):
    nb = N_BUCKETS // 2
    max_exact = nb // 2
    n = np.abs(rel)
    nf = np.maximum(n, 1).astype(np.float32)
    large = max_exact + (np.log(nf / np.float32(max_exact)) / np.float32(math.log(MAX_DISTANCE / max_exact))
                         * np.float32(nb - max_exact)).astype(np.int32)
    large = np.minimum(large, nb - 1)
    return np.where(rel > 0, nb, 0) + np.where(n < max_exact, n, large)


FAR_BUCKET = N_BUCKETS // 2 - 1
FAR_TILES = (8, 2, 1)


def _stack_halves(q):
    lane = lax.broadcasted_iota(jnp.int32, q.shape, 1)
    zero = jnp.zeros_like(q)
    return jnp.concatenate([jnp.where(lane < HEAD, q, zero), jnp.where(lane >= HEAD, q, zero)], axis=0)


def _softmax_step(qs, k, v, bias, m_sc, l_sc, acc_sc):
    s = jnp.concatenate([_dot_nt(q_h, k_h) for q_h, k_h in zip(qs, k)], axis=0)
    if bias is not None:
        s = s + bias
    _softmax_update(s, v, m_sc, l_sc, acc_sc)


def _softmax_update(s, v, m_sc, l_sc, acc_sc):
    reps = s.shape[1] // LANES
    m_old = m_sc[...]
    m_new = jnp.maximum(m_old, jnp.max(s, axis=-1, keepdims=True))
    alpha = jnp.exp2(m_old - m_new)
    p = jnp.exp2(s - jnp.concatenate([m_new] * reps, axis=1))
    psum = p[:, :LANES]
    for r in range(1, reps):
        psum = psum + p[:, r * LANES:(r + 1) * LANES]
    l_sc[...] = alpha * l_sc[...] + psum
    p = p.astype(BF16)
    rows = p.shape[0] // len(v)
    pv = jnp.concatenate([_dot(p[h * rows:(h + 1) * rows], v_h) for h, v_h in enumerate(v)], axis=0)
    acc_sc[...] = alpha * acc_sc[...] + pv
    m_sc[...] = m_new


def _diff_finish(lam_ref, za, sg, l, acc, tq):
    lv = lam_ref[...]
    lam = (jnp.exp(jnp.sum(lv[0:1] * lv[1:2], axis=-1, keepdims=True))
           - jnp.exp(jnp.sum(lv[2:3] * lv[3:4], axis=-1, keepdims=True)) + LAMBDA_INIT)
    o = acc / jnp.sum(l, axis=-1, keepdims=True)
    attn = o[:tq] - lam * o[tq:]
    return (_rms(attn, sg) * (1.0 - LAMBDA_INIT)) * _silu(za)


def _da_prompt_kernel(lam_ref, q_ref, k_ref, v_ref, za_ref, tab_ref, sg_ref, o_ref,
                      m_sc, l_sc, acc_sc, *, tq, tk):
    i = pl.program_id(1)
    qs = _stack_halves(q_ref[...])
    m_sc[...] = jnp.full(m_sc.shape, NEG, F32)
    l_sc[...] = jnp.zeros(l_sc.shape, F32)
    acc_sc[...] = jnp.zeros(acc_sc.shape, F32)

    def tile(j, width, bias):
        off = pl.multiple_of(j * tk, tk)
        _softmax_step([qs], [k_ref[pl.ds(off, width), :]], [v_ref[pl.ds(off, width), :]], bias,
                      m_sc, l_sc, acc_sc)

    nfar = jnp.maximum(i - 1, 0)

    done = 0
    for width in FAR_TILES:
        def far(j, c, width=width, done=done):
            tile(done + width * j, width * tk, None)
            return c

        trips = (nfar - done) // width
        lax.fori_loop(0, trips, far, 0)
        done = done + trips * width

    @pl.when(i >= 1)
    def _():
        b = jnp.concatenate([tab_ref[0, 1], tab_ref[0, 0]], axis=1)
        tile(i - 1, 2 * tk, jnp.concatenate([b, b], axis=0))

    @pl.when(i == 0)
    def _():
        b = tab_ref[0, 0]
        tile(0, tk, jnp.concatenate([b, b], axis=0))
    o_ref[...] = _diff_finish(lam_ref, za_ref[...], sg_ref[...], l_sc[...], acc_sc[...], tq).astype(BF16)


def _da_prompt(lamv, qn, knb, vb, za, tab, sg, tq):
    t = qn.shape[0]
    tk = tq
    kern = functools.partial(_da_prompt_kernel, tq=tq, tk=tk)
    return pl.pallas_call(
        kern,
        grid=(DA_HEADS, t // tq),
        in_specs=[
            pl.BlockSpec(lamv.shape, lambda h, i: (0, 0)),
            pl.BlockSpec((tq, LANES), lambda h, i: (i, h)),
            pl.BlockSpec((t, LANES), lambda h, i: (0, h)),
            pl.BlockSpec((t, LANES), lambda h, i: (0, h)),
            pl.BlockSpec((tq, LANES), lambda h, i: (i, h)),
            pl.BlockSpec((1, 2, tq, tk), lambda h, i: (h, 0, 0, 0)),
            pl.BlockSpec(sg.shape, lambda h, i: (0, 0)),
        ],
        out_specs=pl.BlockSpec((tq, LANES), lambda h, i: (i, h)),
        out_shape=jax.ShapeDtypeStruct((t, DA_WIDTH), BF16),
        scratch_shapes=[pltpu.VMEM((2 * tq, LANES), F32)] * 3,
        compiler_params=_params(("arbitrary", "arbitrary")),
        name="diff_attn_prompt",
    )(lamv, qn, knb, vb, za, tab, sg)


def _da_sample_kernel(lam_ref, q_ref, ck_ref, cv_ref, kn_ref, vn_ref, za_ref, tabc_ref, tabn_ref, sg_ref,
                      o_ref, m_sc, l_sc, acc_sc, *, ts):
    cols = lambda h: slice(h * LANES, (h + 1) * LANES)
    heads = range(DA_HEADS)
    qs = [_stack_halves(q_ref[:, cols(h)]) for h in heads]
    m_sc[...] = jnp.full(m_sc.shape, NEG, F32)
    l_sc[...] = jnp.zeros(l_sc.shape, F32)
    acc_sc[...] = jnp.zeros(acc_sc.shape, F32)
    kc = ck_ref[0].astype(BF16)
    vc = cv_ref[0].astype(BF16)
    _softmax_step(qs, [kc[:, cols(h)] for h in heads], [vc[:, cols(h)] for h in heads], tabc_ref[...],
                  m_sc, l_sc, acc_sc)
    pad = jnp.zeros((LANES - ts, LANES), BF16)
    kn = [jnp.concatenate([kn_ref[:, cols(h)].astype(BF16), pad], axis=0) for h in heads]
    vn = [jnp.concatenate([vn_ref[:, cols(h)].astype(BF16), pad], axis=0) for h in heads]
    _softmax_step(qs, kn, vn, tabn_ref[...], m_sc, l_sc, acc_sc)
    l = l_sc[...]
    acc = acc_sc[...]
    za = za_ref[...]
    rows = lambda h: slice(h * 2 * ts, (h + 1) * 2 * ts)
    o_ref[...] = jnp.concatenate(
        [_diff_finish(lam_ref, za[:, cols(h)], sg_ref[...], l[rows(h)], acc[rows(h)], ts) for h in heads],
        axis=1).astype(BF16)


def _da_sample(lamv, qn, ck, cv, kn, vn, za, tabc, tabn, sg, ts):
    nb, past, _ = ck.shape
    kern = functools.partial(_da_sample_kernel, ts=ts)
    stack = lambda tab: jnp.concatenate([tab, tab], axis=1).reshape(DA_HEADS * 2 * ts, tab.shape[-1])
    tabc, tabn = stack(tabc), stack(tabn)
    rows = pl.BlockSpec((ts, DA_WIDTH), lambda b: (b, 0))
    cache = pl.BlockSpec((1, past, DA_WIDTH), lambda b: (b, 0, 0))
    full = lambda a: pl.BlockSpec(a.shape, lambda b: (0,) * a.ndim)
    return pl.pallas_call(
        kern,
        grid=(nb,),
        in_specs=[full(lamv), rows, cache, cache, rows, rows, rows, full(tabc), full(tabn), full(sg)],
        out_specs=rows,
        out_shape=jax.ShapeDtypeStruct((nb * ts, DA_WIDTH), BF16),
        scratch_shapes=[pltpu.VMEM((DA_HEADS * 2 * ts, LANES), F32)] * 3,
        compiler_params=_params(("arbitrary",)),
        name="diff_attn_sample",
    )(lamv, qn, ck, cv, kn, vn, za, tabc, tabn, sg)


def _bias_kernel(idx_ref, bt_ref, o_ref):
    idx = idx_ref[...]
    bt = bt_ref[...]
    bt = bt - bt[:, FAR_BUCKET:FAR_BUCKET + 1]
    onehot = jnp.where(lax.broadcasted_iota(jnp.int32, (N_BUCKETS, idx.shape[1]), 0) == idx, 1.0, 0.0).astype(BF16)
    hi = bt.astype(BF16)
    rest = bt - hi.astype(F32)
    mid = rest.astype(BF16)
    lo = (rest - mid.astype(F32)).astype(BF16)
    val = _dot(hi, onehot) + _dot(mid, onehot) + _dot(lo, onehot)
    o_ref[...] = jnp.where(idx < 0, NEG, val * LOG2E)


def _bias_tables(rel_bias, tq, ts, past):
    def buckets(rel, mask):
        return np.where(mask, _t5_bucket_np(rel), -1).astype(np.int32).reshape(-1)

    r = np.arange(tq)[:, None]
    c = np.arange(tq)[None, :]
    qpos = past + np.arange(ts)[:, None]
    kc = np.arange(past)[None, :]
    kn = past + np.arange(LANES)[None, :]
    parts = [buckets(c - r, (c // CHUNK) <= (r // CHUNK)),
             buckets(c - r - tq, np.ones((tq, tq), bool)),
             buckets(kc - qpos, (kc // CHUNK) <= (qpos // CHUNK)),
             buckets(kn - qpos, ((kn // CHUNK) <= (qpos // CHUNK)) & (kn < past + ts))]
    idx = np.concatenate(parts)
    n = idx.size
    bt = jnp.zeros((SUBLANES, N_BUCKETS), F32).at[:DA_HEADS].set(rel_bias.T)
    blk = 4096
    assert n % blk == 0
    tab = pl.pallas_call(
        _bias_kernel,
        grid=(n // blk,),
        in_specs=[pl.BlockSpec((1, blk), lambda i: (0, i)), pl.BlockSpec(bt.shape, lambda i: (0, 0))],
        out_specs=pl.BlockSpec((SUBLANES, blk), lambda i: (0, i)),
        out_shape=jax.ShapeDtypeStruct((SUBLANES, n), F32),
        compiler_params=_params(("arbitrary",)),
        name="t5_bias_tables",
    )(jnp.asarray(idx).reshape(1, n), bt)[:DA_HEADS]
    o1 = 2 * tq * tq
    o2 = o1 + ts * past
    prompt_tab = tab[:, :o1].reshape(DA_HEADS, 2, tq, tq)
    tabc = tab[:, o1:o2].reshape(DA_HEADS, ts, past)
    tabn = tab[:, o2:].reshape(DA_HEADS, ts, LANES)
    return prompt_tab, tabc, tabn


def _rwkv_kernel(prw_ref, look_ref, first_ref, zb_ref, s0_ref, mu_ref, w0_ref, a0_ref, wwa_ref,
                 kkw_ref, kaw_ref, rk_ref, lng_ref, lnb_ref, g2_ref, hm_ref, dm_ref, pm_ref, ti_ref, to_ref,
                 ob_ref, sfin_ref,
                 s_sc, om_sc, rho_sc, at_sc, kt_sc, atb_sc, ktb_sc, wb_sc, vr_sc, o_sc, *, tm):
    j = pl.program_id(1)

    @pl.when(j == 0)
    def _():
        s_sc[...] = s0_ref[0]

    g2 = g2_ref[...]

    def head_sums(t):
        return jnp.concatenate([_dot2(t[:, c * RW_STATE_LANES:(c + 1) * RW_STATE_LANES], g2)
                                for c in range(RW_STATE_GROUPS)], axis=1)

    x = prw_ref[...]
    row0 = jnp.where(j == 0, first_ref[0], look_ref[7:8, :])
    rows = lax.broadcasted_iota(jnp.int32, x.shape, 0)
    prev = jnp.where(rows == 0, row0, pltpu.roll(x, 1, axis=0))
    mix = x + (prev - x) * mu_ref[...]
    r = mix[:, 0:512]
    kr = mix[:, 512:1024]
    vr = mix[:, 1024:1536]
    la = mix[:, 1536:1664]
    lane = lax.broadcasted_iota(jnp.int32, la.shape, 1)
    la = jnp.where(lane < HEAD, jnp.tanh(la), la)
    hi = la.astype(BF16)
    lo = (la - hi.astype(F32)).astype(BF16)
    wwa = wwa_ref[...]
    lora = _dot(hi, wwa[0]) + _dot(lo, wwa[0]) + _dot(hi, wwa[1])
    wpre = w0_ref[...] + lora[:, 0:512]
    nw = -wpre
    w = -(jnp.maximum(nw, 0.0) + jnp.log(1.0 + jnp.exp(-jnp.abs(nw)))) - 0.5
    ld = -jnp.exp(w)
    a = 1.0 / (1.0 + jnp.exp(-(a0_ref[...] + lora[:, 512:1024])))
    kk = kr * kkw_ref[...]
    nrm = jnp.sqrt(head_sums(kk * kk))
    kk = kk / jnp.maximum(nrm, 1e-12)
    kka = kk * a
    kh = kr * (1.0 + (a - 1.0) * kaw_ref[...])

    def dot3(m, t):
        hi = t.astype(BF16)
        r1 = t - hi.astype(F32)
        mid = r1.astype(BF16)
        lo = (r1 - mid.astype(F32)).astype(BF16)
        return _dot(m, hi) + _dot(m, mid) + _dot(m, lo)

    incl = dot3(ti_ref[...], ld)
    tot = dot3(to_ref[...], ld)
    e_rem = jnp.exp(tot - incl)
    e_inv = jnp.exp(-incl)
    om_sc[...] = jnp.exp(incl - ld) * kk
    rho_sc[...] = jnp.exp(incl) * r
    at_sc[...] = kka * e_inv
    kt_sc[...] = kh * e_inv
    atb_sc[...] = kka * e_rem
    ktb_sc[...] = kh * e_rem
    wb_sc[...] = jnp.exp(tot)
    hm = hm_ref[...]
    vx = (jnp.broadcast_to(vr[:, None, :], (tm, RW_HEADS, RW_WIDTH)) * hm[None]).reshape(tm * RW_HEADS, RW_WIDTH)
    vr_sc[...] = _dot(vx.astype(BF16), pm_ref[...])

    nb = RW_BLOCK
    heads_per_group = RW_HEADS // RW_STATE_GROUPS
    dm = dm_ref[...]

    def head_rows(blk):
        return jnp.concatenate([jnp.broadcast_to(blk[j:j + 1], (RW_HEADS, RW_WIDTH)) * hm for j in range(nb)],
                               axis=0)

    def fold(x):
        return x[:, :RW_STATE_LANES] + x[:, RW_STATE_LANES:]

    def block(b, s):
        base = pl.multiple_of(b * nb, nb)
        om_b, rho_b, at_b, kt_b, atb_b, ktb_b, wb_b = (
            ref[pl.ds(base, nb), :] for ref in (om_sc, rho_sc, at_sc, kt_sc, atb_sc, ktb_sc, wb_sc))
        vr_b = vr_sc[pl.ds(pl.multiple_of(b * (nb * RW_HEADS), nb * RW_HEADS), nb * RW_HEADS), :]
        x = jnp.concatenate([head_rows(om_b), head_rows(rho_b)], axis=0)
        rhs = jnp.concatenate([at_b, kt_b, jnp.zeros((LANES - 2 * nb, RW_WIDTH), F32)], axis=0)
        coef = _dot_nt(x.astype(BF16), rhs.astype(BF16))
        g = _dot_nt(fold(x).astype(BF16), s.astype(BF16))

        def rows(a, j):
            return a[j * RW_HEADS:(j + 1) * RW_HEADS]

        def cf(j, lane):
            return jnp.broadcast_to(rows(coef, j)[:, lane:lane + 1], (RW_HEADS, LANES))

        sa = []
        for j in range(nb):
            acc = rows(g, j) * rows(dm, j)
            for i in range(j):
                acc = acc + cf(j, nb + i) * rows(vr_b, i)
            for i in range(j):
                acc = acc - cf(j, i) * sa[i]
            sa.append(acc)
        outs = []
        for j in range(nb):
            acc = rows(g, nb + j) * rows(dm, j)
            for i in range(j + 1):
                acc = acc + cf(nb + j, nb + i) * rows(vr_b, i) - cf(nb + j, i) * sa[i]
            outs.append(acc)
        lhs = jnp.concatenate([-t for t in sa] + [vr_b], axis=0)
        rhs2 = jnp.concatenate([fold(head_rows(atb_b)), fold(head_rows(ktb_b))], axis=0)
        ds = _dot(jnp.transpose(lhs).astype(BF16), rhs2.astype(BF16))
        wb = jnp.concatenate(
            [jnp.broadcast_to(wb_b[0:1, c * RW_STATE_LANES:(c + 1) * RW_STATE_LANES], (HEAD, RW_STATE_LANES))
             for c in range(RW_STATE_GROUPS)], axis=0)
        for h in range(RW_HEADS):
            o_sc[h, pl.ds(base, nb), :] = jnp.concatenate([t[h:h + 1] for t in outs], axis=0)
        return s * wb + ds

    s_sc[...] = lax.fori_loop(0, tm // nb, block, s_sc[...])

    o = jnp.concatenate([o_sc[c * heads_per_group + hh][:, c * HEAD:(c + 1) * HEAD]
                         for c in range(RW_STATE_GROUPS) for hh in range(heads_per_group)], axis=1)
    mean = head_sums(o) * (1.0 / HEAD)
    d = o - mean
    var = head_sums(d * d) * (1.0 / HEAD)
    y = d * lax.rsqrt(var + RW_GN_EPS) * lng_ref[...] + lnb_ref[...]
    bonus = head_sums(r * kh * rk_ref[...]) * vr
    ob_ref[...] = ((y + bonus) * _silu(zb_ref[...])).astype(BF16)

    @pl.when(j == pl.num_programs(1) - 1)
    def _():
        sfin_ref[0] = s_sc[...]


def _block_sum_matrices(tm):
    t = np.arange(tm)
    same = (t[:, None] // RW_BLOCK) == (t[None, :] // RW_BLOCK)
    incl = same & (t[None, :] <= t[:, None])
    return [jnp.asarray(incl.astype(np.float32), dtype=BF16), jnp.asarray(same.astype(np.float32), dtype=BF16)]


def _rwkv(prw, first, zb, s0, p, seq_len, tm):
    m = prw.shape[0]
    nseq = m // seq_len
    ntile = seq_len // tm
    kern = functools.partial(_rwkv_kernel, tm=tm)
    full = lambda a: pl.BlockSpec(a.shape, lambda b, j: (0,) * a.ndim)
    rows = lambda c: pl.BlockSpec((tm, c), lambda b, j: (b * ntile + j, 0))
    look = pl.BlockSpec((8, RW_SHIFT_COLS), lambda b, j: (jnp.maximum((b * ntile + j) * (tm // 8) - 1, 0), 0))
    consts = [p["mu"], p["w0"], p["a0"], p["wwa"], p["kkw"], p["kaw"], p["rk"], p["lng"], p["lnb"],
              p["g2"], p["hm"], p["dm"], p["pm"]] + _block_sum_matrices(tm)
    return pl.pallas_call(
        kern,
        grid=(nseq, ntile),
        in_specs=[rows(RW_SHIFT_COLS), look,
                  pl.BlockSpec((1, 1, RW_SHIFT_COLS), lambda b, j: (b, 0, 0)),
                  rows(RW_WIDTH),
                  pl.BlockSpec((1,) + RW_STATE_SHAPE, lambda b, j: (b, 0, 0))] + [full(c) for c in consts],
        out_specs=[rows(RW_WIDTH), pl.BlockSpec((1,) + RW_STATE_SHAPE, lambda b, j: (b, 0, 0))],
        out_shape=[jax.ShapeDtypeStruct((m, RW_WIDTH), BF16), jax.ShapeDtypeStruct((nseq,) + RW_STATE_SHAPE, F32)],
        scratch_shapes=([pltpu.VMEM(RW_STATE_SHAPE, F32)] + [pltpu.VMEM((tm, RW_WIDTH), F32)] * 7
                        + [pltpu.VMEM((tm * RW_HEADS, RW_STATE_SHAPE[0]), F32),
                           pltpu.VMEM((RW_HEADS, tm, RW_STATE_SHAPE[0]), F32)]),
        compiler_params=_params(("arbitrary", "arbitrary")),
        name="rwkv7",
    )(prw, prw, first, zb, s0, *consts)


def _state_to_rows(s):
    b = s.shape[0]
    per = RW_HEADS // RW_STATE_GROUPS
    return s.reshape(b, RW_STATE_GROUPS, per, HEAD, HEAD).transpose(0, 1, 3, 2, 4).reshape((b,) + RW_STATE_SHAPE)


def _rows_to_state(s):
    b = s.shape[0]
    per = RW_HEADS // RW_STATE_GROUPS
    return s.reshape(b, RW_STATE_GROUPS, HEAD, per, HEAD).transpose(0, 1, 3, 2, 4).reshape(b, RW_HEADS, HEAD, HEAD)


SKIP_LOG = -104.0


def _sb_tile(qs, k, v, u, c, mask):
    tk = k[0].shape[0]
    rows = qs[0].shape[0]
    z = jnp.concatenate([_dot_nt(q_p, k_p) for q_p, k_p in zip(qs, k)], axis=0)
    lg = -(jnp.maximum(z, 0.0) + jnp.log(1.0 + jnp.exp(-jnp.abs(z))))
    if mask is not None:
        lg = jnp.where(mask, lg, 0.0)
    hi = lg.astype(BF16)
    lo = (lg - hi.astype(F32)).astype(BF16)
    it = _dot(jnp.concatenate([hi, lo], axis=1), u)
    a = jnp.exp(z + it + jnp.concatenate([c] * (tk // LANES), axis=1))
    if mask is not None:
        a = jnp.where(mask, a, 0.0)
    a = a.astype(BF16)
    da = jnp.concatenate([_dot(a[p * rows:(p + 1) * rows], v_p) for p, v_p in enumerate(v)], axis=0)
    return da, c + jnp.broadcast_to(it[:, 0:1], c.shape)


def _sb_finish(acc, z):
    t = z.shape[0]
    lane = lax.broadcasted_iota(jnp.int32, z.shape, 1)
    return jnp.where(lane < HEAD, acc[:t], acc[t:]) * _silu(z)


def _sb_prompt_kernel(q_ref, k_ref, v_ref, z_ref, u_ref, o_ref, acc_sc, c_sc, *, tq):
    i = pl.program_id(1)
    tk = u_ref.shape[0] // 2
    r = tq // tk
    qs = _stack_halves(q_ref[...])
    u = u_ref[...]
    acc_sc[...] = jnp.zeros(acc_sc.shape, F32)
    c_sc[...] = jnp.zeros(c_sc.shape, F32)
    def tile(j, lo, masked):
        n = tq - lo
        halves = lambda x: jnp.concatenate([x[lo:tq], x[tq + lo:2 * tq]], axis=0)
        off = pl.multiple_of(j * tk, tk)
        mask = None
        if masked:
            qrow = lax.broadcasted_iota(jnp.int32, (2 * n, tk), 0) % n
            col = lax.broadcasted_iota(jnp.int32, (2 * n, tk), 1)
            mask = col < qrow
        da, c = _sb_tile([halves(qs)], [k_ref[pl.ds(off, tk), :]], [v_ref[pl.ds(off, tk), :]], u,
                         halves(c_sc[...]), mask)
        for h in range(2):
            acc_sc[h * tq + lo:(h + 1) * tq, :] += da[h * n:(h + 1) * n]
            c_sc[h * tq + lo:(h + 1) * tq, :] = c[h * n:(h + 1) * n]

    for jj in reversed(range(r)):
        tile(i * r + jj, jj * tk, True)

    def body(st):
        tile(i * r - 1 - st[0], 0, False)
        return st[0] + 1, jnp.max(c_sc[...])

    lax.while_loop(lambda st: jnp.logical_and(st[0] < i * r, st[1] > SKIP_LOG), body,
                   (jnp.int32(0), jnp.max(c_sc[...])))
    o_ref[...] = _sb_finish(acc_sc[...], z_ref[...]).astype(BF16)


def _sb_prompt(qb, kb, vb, z, u, tq):
    t = qb.shape[0]
    kern = functools.partial(_sb_prompt_kernel, tq=tq)
    rows = pl.BlockSpec((tq, LANES), lambda h, i: (i, h))
    whole = pl.BlockSpec((t, LANES), lambda h, i: (0, h))
    return pl.pallas_call(
        kern,
        grid=(SB_HEADS // 2, t // tq),
        in_specs=[rows, whole, whole, rows, pl.BlockSpec(u.shape, lambda h, i: (0, 0))],
        out_specs=rows,
        out_shape=jax.ShapeDtypeStruct((t, SB_WIDTH), BF16),
        scratch_shapes=[pltpu.VMEM((2 * tq, LANES), F32)] * 2,
        compiler_params=_params(("arbitrary", "arbitrary")),
        name="stick_breaking_prompt",
    )(qb, kb, vb, z, u)


def _sb_sample_kernel(q_ref, ck_ref, cv_ref, kn_ref, vn_ref, z_ref, u_ref, o_ref, acc_sc, c_sc, *, ts):
    tk = u_ref.shape[0] // 2
    ntile = ck_ref.shape[1] // tk
    npair = SB_HEADS // 2
    cols = lambda p: slice(p * LANES, (p + 1) * LANES)
    qs = [_stack_halves(q_ref[:, cols(p)]) for p in range(npair)]
    u = u_ref[...]
    pad = jnp.zeros((tk - ts, LANES), BF16)
    kn = [jnp.concatenate([kn_ref[:, cols(p)], pad], axis=0) for p in range(npair)]
    vn = [jnp.concatenate([vn_ref[:, cols(p)], pad], axis=0) for p in range(npair)]
    rows = npair * 2 * ts
    qrow = lax.broadcasted_iota(jnp.int32, (rows, tk), 0) % ts
    col = lax.broadcasted_iota(jnp.int32, (rows, tk), 1)
    acc, c = _sb_tile(qs, kn, vn, u, jnp.zeros((rows, LANES), F32), col < qrow)
    acc_sc[...] = acc
    c_sc[...] = c

    def body(st):
        off = pl.multiple_of((ntile - 1 - st[0]) * tk, tk)
        kc = ck_ref[0, pl.ds(off, tk), :].astype(BF16)
        vc = cv_ref[0, pl.ds(off, tk), :].astype(BF16)
        da, c = _sb_tile(qs, [kc[:, cols(p)] for p in range(npair)], [vc[:, cols(p)] for p in range(npair)],
                         u, c_sc[...], None)
        acc_sc[...] += da
        c_sc[...] = c
        return st[0] + 1, jnp.max(c)

    lax.while_loop(lambda st: jnp.logical_and(st[0] < ntile, st[1] > SKIP_LOG), body,
                   (jnp.int32(0), jnp.max(c)))
    acc = acc_sc[...]
    z = z_ref[...]
    o_ref[...] = jnp.concatenate(
        [_sb_finish(acc[p * 2 * ts:(p + 1) * 2 * ts], z[:, cols(p)]) for p in range(npair)], axis=1).astype(BF16)


def _sb_sample(qb, ck, cv, kb, vb, z, u, ts):
    nb, past, _ = ck.shape
    kern = functools.partial(_sb_sample_kernel, ts=ts)
    rows = pl.BlockSpec((ts, SB_WIDTH), lambda b: (b, 0))
    cache = pl.BlockSpec((1, past, SB_WIDTH), lambda b: (b, 0, 0))
    return pl.pallas_call(
        kern,
        grid=(nb,),
        in_specs=[rows, cache, cache, rows, rows, rows, pl.BlockSpec(u.shape, lambda b: (0, 0))],
        out_specs=rows,
        out_shape=jax.ShapeDtypeStruct((nb * ts, SB_WIDTH), BF16),
        scratch_shapes=[pltpu.VMEM((SB_HEADS * ts, LANES), F32)] * 2,
        compiler_params=_params(("arbitrary",)),
        name="stick_breaking_sample",
    )(qb, ck, cv, kb, vb, z, u)


def _suffix_matrix(tk):
    j = np.arange(tk)[:, None]
    s = np.arange(tk)[None, :]
    u = (j >= s).astype(np.float32)
    return jnp.asarray(np.concatenate([u, u], axis=0), dtype=BF16)


def _row_tile(m):
    return 256 if m % 256 == 0 else m


def kernel(x_prompt, x_sample, cache_l0_k, cache_l0_v, state_l0_shift, state_l0_wkv, cache_l1_k, cache_l1_v, rel_bias, norm_l0, w_in_l0, w_out_l0, da_q_norm, da_k_norm, da_lambda_q1, da_lambda_k1, da_lambda_q2, da_lambda_k2, da_subln, rw_mu, rw_w0, rw_w_up, rw_a0, rw_a_up, rw_k_k, rw_k_a, rw_r_k, rw_lnx_g, rw_lnx_b, norm_l1, w_in_l1, w_out_l1):
    bp, t, d = x_prompt.shape
    nb, ts, _ = x_sample.shape
    past = cache_l0_k.shape[1]
    assert bp == 1 and ts <= LANES and t % 256 == 0 and past % 256 == 0

    row = lambda v: v.reshape(1, -1).astype(F32)
    w_in0 = w_in_l0.astype(BF16)
    w_in1 = w_in_l1.astype(BF16)
    w_out0a = w_out_l0[:DA_WIDTH].astype(BF16)
    w_out0b = w_out_l0[DA_WIDTH:].astype(BF16)
    w_out1 = w_out_l1.astype(BF16)
    qg = row(jnp.tile(da_q_norm, 512 // HEAD))
    kg = row(jnp.tile(da_k_norm, 512 // HEAD))
    lamv = jnp.stack([da_lambda_q1, da_lambda_k1, da_lambda_q2, da_lambda_k2]).astype(F32)
    sg = row(da_subln)
    zeros = jnp.zeros((HEAD, RW_WIDTH), F32)
    wwa_f = jnp.concatenate([jnp.concatenate([rw_w_up, zeros], axis=1),
                             jnp.concatenate([zeros, rw_a_up], axis=1)], axis=0)
    wwa_hi = wwa_f.astype(BF16)
    wwa = jnp.stack([wwa_hi, (wwa_f - wwa_hi.astype(F32)).astype(BF16)])
    heads = np.arange(RW_HEADS)
    per_group = RW_HEADS // RW_STATE_GROUPS
    chan = np.arange(RW_WIDTH)
    out_lane = np.arange(RW_STATE_SHAPE[0])
    hm = np.equal(heads[:, None], chan[None, :] // HEAD)
    dm = np.equal(heads[:, None] // per_group, out_lane[None, :] // HEAD)
    pm = (chan[:, None] // HEAD // per_group == out_lane[None, :] // HEAD) & (chan[:, None] % HEAD == out_lane[None, :] % HEAD)
    rw = dict(mu=row(rw_mu), w0=row(rw_w0), a0=row(rw_a0), wwa=wwa, kkw=row(rw_k_k), kaw=row(rw_k_a),
              rk=row(rw_r_k), lng=row(rw_lnx_g), lnb=row(rw_lnx_b), g2=_group_matrix(RW_STATE_LANES, 1.0),
              hm=jnp.asarray(hm.astype(np.float32)),
              dm=jnp.asarray(np.tile(dm.astype(np.float32), (RW_BLOCK, 1))),
              pm=jnp.asarray(pm.astype(np.float32), dtype=BF16))
    tq_da = 256
    prompt_tab, tabc, tabn = _bias_tables(rel_bias.astype(F32), tq_da, ts, past)
    u_prompt = _suffix_matrix(2 * LANES)
    u_sample = _suffix_matrix(LANES)

    def layer0(x, first, s0, seq_len, attend):
        m = x.shape[0]
        tm = _row_tile(m)
        qn, kn, knb, v, vb, za, prw, zb = _inproj0(x, row(norm_l0), w_in0, qg, kg, tm)
        oa = attend(qn, kn, knb, v, vb, za)
        ob, sfin = _rwkv(prw, first, zb, _state_to_rows(s0), rw, seq_len, min(seq_len, 256))
        y = _outproj(x, [oa, ob], [w_out0a, w_out0b], tm)
        shift = prw.reshape(m // seq_len, seq_len, RW_SHIFT_COLS)[:, -1:]
        return y, kn, v, shift, _rows_to_state(sfin)

    def layer1(x, attend):
        m = x.shape[0]
        tm = _row_tile(m)
        qb, k, kb, v, vb, z = _inproj1(x, row(norm_l1), w_in1, tm)
        o = attend(qb, kb, vb, z)
        return _outproj(x, [o], [w_out1], tm), k, v

    xp = x_prompt.reshape(t, d)
    yp, k0p, v0p, shp, wkvp = layer0(
        xp, jnp.zeros((1, 1, RW_SHIFT_COLS), F32), jnp.zeros((1, RW_HEADS, HEAD, HEAD), F32), t,
        lambda qn, kn, knb, v, vb, za: _da_prompt(lamv, qn, knb, vb, za, prompt_tab, sg, tq_da))
    yp, k1p, v1p = layer1(yp, lambda qb, kb, vb, z: _sb_prompt(qb, kb, vb, z, u_prompt, 512))

    xs = x_sample.reshape(nb * ts, d)
    ck0 = cache_l0_k.reshape(nb, past, DA_WIDTH)
    cv0 = cache_l0_v.reshape(nb, past, DA_WIDTH)
    ys, k0s, v0s, shs, wkvs = layer0(
        xs, state_l0_shift, state_l0_wkv, ts,
        lambda qn, kn, knb, v, vb, za: _da_sample(lamv, qn, ck0, cv0, knb, vb, za, tabc, tabn, sg, ts))
    ck1 = cache_l1_k.reshape(nb, past, SB_WIDTH)
    cv1 = cache_l1_v.reshape(nb, past, SB_WIDTH)
    ys, k1s, v1s = layer1(ys, lambda qb, kb, vb, z: _sb_sample(qb, ck1, cv1, kb, vb, z, u_sample, ts))

    return (yp.reshape(1, t, d), ys.reshape(nb, ts, d),
            k0p.reshape(1, t, DA_HEADS, LANES), v0p.reshape(1, t, DA_HEADS, LANES), shp, wkvp,
            k1p.reshape(1, t, SB_HEADS, HEAD), v1p.reshape(1, t, SB_HEADS, HEAD),
            k0s.reshape(nb, ts, DA_HEADS, LANES), v0s.reshape(nb, ts, DA_HEADS, LANES), shs, wkvs,
            k1s.reshape(nb, ts, SB_HEADS, HEAD), v1s.reshape(nb, ts, SB_HEADS, HEAD))
```

```python
import functools
import math

import numpy as np
import jax
import jax.numpy as jnp
from jax import lax
from jax.experimental import pallas as pl
from jax.experimental.pallas import tpu as pltpu

F32 = jnp.float32
BF16 = jnp.bfloat16

EPS = 1e-6
NEG = -1e30
CHUNK = 64
LANES = 128
SUBLANES = 8
HEAD = 64
DA_HEADS = 4
DA_WIDTH = 512
RW_WIDTH = 512
RW_HEADS = 8
RW_SHIFT_COLS = 3 * RW_WIDTH + 128
RW_GN_EPS = 64e-5
RW_STATE_LANES = 256
RW_STATE_GROUPS = RW_WIDTH // RW_STATE_LANES
RW_STATE_SHAPE = (RW_STATE_GROUPS * HEAD, RW_STATE_LANES)
RW_BLOCK = 8
SB_HEADS = 16
SB_WIDTH = 1024
N_BUCKETS = 32
MAX_DISTANCE = 128
LAMBDA_INIT = 0.8 - 0.6 * math.exp(-0.3 * 0)
LOG2E = math.log2(math.e)
VMEM_LIMIT = 56 * 1024 * 1024


def _params(sem):
    return pltpu.CompilerParams(dimension_semantics=sem, vmem_limit_bytes=VMEM_LIMIT)


def _dot(a, b):
    return jnp.dot(a, b, preferred_element_type=F32)


def _dot_nt(a, b):
    return lax.dot_general(a, b, (((1,), (1,)), ((), ())), preferred_element_type=F32)


def _dot2(x, w):
    hi = x.astype(BF16)
    lo = (x - hi.astype(F32)).astype(BF16)
    return _dot(hi, w) + _dot(lo, w)


def _silu(z):
    return z / (1.0 + jnp.exp(-z))


def _rms(x, g):
    return x * lax.rsqrt(jnp.mean(x * x, axis=-1, keepdims=True) + EPS) * g


def _group_matrix(n, scale):
    idx = np.arange(n) // HEAD
    return jnp.asarray((idx[:, None] == idx[None, :]).astype(np.float32) * scale, dtype=BF16)


def _inproj0_kernel(x_ref, g_ref, w_ref, qg_ref, kg_ref, avg_ref,
                    qn_ref, kn_ref, knb_ref, v_ref, vb_ref, za_ref, prw_ref, zb_ref):
    y = _dot(_rms(x_ref[...], g_ref[...]).astype(BF16), w_ref[...])

    def head_norm(t, g):
        sq = t * t
        half = avg_ref.shape[0]
        ms = jnp.concatenate([_dot2(sq[:, c:c + half], avg_ref[...]) for c in range(0, t.shape[1], half)], axis=1)
        return t * lax.rsqrt(ms + EPS) * g

    qn_ref[...] = (head_norm(y[:, 0:512], qg_ref[...]) * (HEAD ** -0.5 * LOG2E)).astype(BF16)
    kn = head_norm(y[:, 512:1024], kg_ref[...])
    knb_ref[...] = kn.astype(BF16)
    v = y[:, 1024:1536]
    vb_ref[...] = v.astype(BF16)
    for h in range(DA_HEADS):
        kn_ref[:, h, :] = kn[:, h * LANES:(h + 1) * LANES]
        v_ref[:, h, :] = v[:, h * LANES:(h + 1) * LANES]
    za_ref[...] = y[:, 1536:2048]
    prw_ref[...] = y[:, 2048:2048 + RW_SHIFT_COLS]
    zb_ref[...] = y[:, 2048 + RW_SHIFT_COLS:]


def _inproj0(x, g, w, qg, kg, tm):
    m, d = x.shape
    n = w.shape[1]
    row = lambda c: pl.BlockSpec((tm, c), lambda i: (i, 0))
    full = lambda a: pl.BlockSpec(a.shape, lambda i: (0,) * a.ndim)
    avg = _group_matrix(2 * LANES, 1.0 / HEAD)
    widths = [(512, BF16), (None, F32), (512, BF16), (None, F32), (512, BF16), (512, F32),
              (RW_SHIFT_COLS, F32), (512, F32)]
    heads = (DA_HEADS, LANES)
    return pl.pallas_call(
        _inproj0_kernel,
        grid=(m // tm,),
        in_specs=[row(d), full(g), full(w), full(qg), full(kg), full(avg)],
        out_specs=[row(c) if c else pl.BlockSpec((tm,) + heads, lambda i: (i, 0, 0)) for c, _ in widths],
        out_shape=[jax.ShapeDtypeStruct((m, c) if c else (m,) + heads, dt) for c, dt in widths],
        compiler_params=_params(("arbitrary",)),
        name="inproj0",
    )(x, g, w, qg, kg, avg)


def _inproj1_kernel(x_ref, g_ref, w_ref, qb_ref, k_ref, kb_ref, v_ref, vb_ref, z_ref):
    y = _dot(_rms(x_ref[...], g_ref[...]).astype(BF16), w_ref[...])
    qb_ref[...] = (y[:, 0:1024] * (HEAD ** -0.5)).astype(BF16)
    k = y[:, 1024:2048]
    k_ref[...] = k
    kb_ref[...] = k.astype(BF16)
    v = y[:, 2048:3072]
    v_ref[...] = v
    vb_ref[...] = v.astype(BF16)
    z_ref[...] = y[:, 3072:4096]


def _inproj1(x, g, w, tm):
    m, d = x.shape
    row = lambda c: pl.BlockSpec((tm, c), lambda i: (i, 0))
    full = lambda a: pl.BlockSpec(a.shape, lambda i: (0,) * a.ndim)
    dts = [BF16, F32, BF16, F32, BF16, F32]
    return pl.pallas_call(
        _inproj1_kernel,
        grid=(m // tm,),
        in_specs=[row(d), full(g), full(w)],
        out_specs=[row(1024) for _ in dts],
        out_shape=[jax.ShapeDtypeStruct((m, 1024), dt) for dt in dts],
        compiler_params=_params(("arbitrary",)),
        name="inproj1",
    )(x, g, w)


def _outproj_kernel(*refs):
    n = (len(refs) - 2) // 2
    x_ref, o_ref = refs[0], refs[-1]
    acc = x_ref[...]
    for a_ref, w_ref in zip(refs[1:1 + n], refs[1 + n:1 + 2 * n]):
        acc = acc + _dot(a_ref[...], w_ref[...])
    o_ref[...] = acc


def _outproj(x, acts, ws, tm):
    m, d = x.shape
    row = lambda c: pl.BlockSpec((tm, c), lambda i: (i, 0))
    full = lambda a: pl.BlockSpec(a.shape, lambda i: (0,) * a.ndim)
    return pl.pallas_call(
        _outproj_kernel,
        grid=(m // tm,),
        in_specs=[row(d)] + [row(a.shape[1]) for a in acts] + [full(w) for w in ws],
        out_specs=row(d),
        out_shape=jax.ShapeDtypeStruct((m, d), F32),
        compiler_params=_params(("arbitrary",)),
        name="outproj",
    )(x, *acts, *ws)


def _t5_bucket_np(rel):
    nb = N_BUCKETS // 2
    max_exact = nb // 2
    n = np.abs(rel)
    nf = np.maximum(n, 1).astype(np.float32)
    large = max_exact + (np.log(nf / np.float32(max_exact)) / np.float32(math.log(MAX_DISTANCE / max_exact))
                         * np.float32(nb - max_exact)).astype(np.int32)
    large = np.minimum(large, nb - 1)
    return np.where(rel > 0, nb, 0) + np.where(n < max_exact, n, large)


FAR_BUCKET = N_BUCKETS // 2 - 1
FAR_TILES = (8, 2, 1)


def _stack_halves(q):
    lane = lax.broadcasted_iota(jnp.int32, q.shape, 1)
    zero = jnp.zeros_like(q)
    return jnp.concatenate([jnp.where(lane < HEAD, q, zero), jnp.where(lane >= HEAD, q, zero)], axis=0)


def _softmax_step(qs, k, v, bias, m_sc, l_sc, acc_sc):
    s = jnp.concatenate([_dot_nt(q_h, k_h) for q_h, k_h in zip(qs, k)], axis=0)
    if bias is not None:
        s = s + bias
    _softmax_update(s, v, m_sc, l_sc, acc_sc)


def _softmax_update(s, v, m_sc, l_sc, acc_sc):
    reps = s.shape[1] // LANES
    m_old = m_sc[...]
    m_new = jnp.maximum(m_old, jnp.max(s, axis=-1, keepdims=True))
    alpha = jnp.exp2(m_old - m_new)
    p = jnp.exp2(s - jnp.concatenate([m_new] * reps, axis=1))
    psum = p[:, :LANES]
    for r in range(1, reps):
        psum = psum + p[:, r * LANES:(r + 1) * LANES]
    l_sc[...] = alpha * l_sc[...] + psum
    p = p.astype(BF16)
    rows = p.shape[0] // len(v)
    pv = jnp.concatenate([_dot(p[h * rows:(h + 1) * rows], v_h) for h, v_h in enumerate(v)], axis=0)
    acc_sc[...] = alpha * acc_sc[...] + pv
    m_sc[...] = m_new


def _diff_finish(lam_ref, za, sg, l, acc, tq):
    lv = lam_ref[...]
    lam = (jnp.exp(jnp.sum(lv[0:1] * lv[1:2], axis=-1, keepdims=True))
           - jnp.exp(jnp.sum(lv[2:3] * lv[3:4], axis=-1, keepdims=True)) + LAMBDA_INIT)
    o = acc / jnp.sum(l, axis=-1, keepdims=True)
    attn = o[:tq] - lam * o[tq:]
    return (_rms(attn, sg) * (1.0 - LAMBDA_INIT)) * _silu(za)


def _da_prompt_kernel(lam_ref, q_ref, k_ref, v_ref, za_ref, tab_ref, sg_ref, o_ref,
                      m_sc, l_sc, acc_sc, *, tq, tk):
    i = pl.program_id(1)
    qs = _stack_halves(q_ref[...])
    m_sc[...] = jnp.full(m_sc.shape, NEG, F32)
    l_sc[...] = jnp.zeros(l_sc.shape, F32)
    acc_sc[...] = jnp.zeros(acc_sc.shape, F32)

    def tile(j, width, bias):
        off = pl.multiple_of(j * tk, tk)
        _softmax_step([qs], [k_ref[pl.ds(off, width), :]], [v_ref[pl.ds(off, width), :]], bias,
                      m_sc, l_sc, acc_sc)

    nfar = jnp.maximum(i - 1, 0)

    done = 0
    for width in FAR_TILES:
        def far(j, c, width=width, done=done):
            tile(done + width * j, width * tk, None)
            return c

        trips = (nfar - done) // width
        lax.fori_loop(0, trips, far, 0)
        done = done + trips * width

    @pl.when(i >= 1)
    def _():
        b = jnp.concatenate([tab_ref[0, 1], tab_ref[0, 0]], axis=1)
        tile(i - 1, 2 * tk, jnp.concatenate([b, b], axis=0))

    @pl.when(i == 0)
    def _():
        b = tab_ref[0, 0]
        tile(0, tk, jnp.concatenate([b, b], axis=0))
    o_ref[...] = _diff_finish(lam_ref, za_ref[...], sg_ref[...], l_sc[...], acc_sc[...], tq).astype(BF16)


def _da_prompt(lamv, qn, knb, vb, za, tab, sg, tq):
    t = qn.shape[0]
    tk = tq
    kern = functools.partial(_da_prompt_kernel, tq=tq, tk=tk)
    return pl.pallas_call(
        kern,
        grid=(DA_HEADS, t // tq),
        in_specs=[
            pl.BlockSpec(lamv.shape, lambda h, i: (0, 0)),
            pl.BlockSpec((tq, LANES), lambda h, i: (i, h)),
            pl.BlockSpec((t, LANES), lambda h, i: (0, h)),
            pl.BlockSpec((t, LANES), lambda h, i: (0, h)),
            pl.BlockSpec((tq, LANES), lambda h, i: (i, h)),
            pl.BlockSpec((1, 2, tq, tk), lambda h, i: (h, 0, 0, 0)),
            pl.BlockSpec(sg.shape, lambda h, i: (0, 0)),
        ],
        out_specs=pl.BlockSpec((tq, LANES), lambda h, i: (i, h)),
        out_shape=jax.ShapeDtypeStruct((t, DA_WIDTH), BF16),
        scratch_shapes=[pltpu.VMEM((2 * tq, LANES), F32)] * 3,
        compiler_params=_params(("arbitrary", "arbitrary")),
        name="diff_attn_prompt",
    )(lamv, qn, knb, vb, za, tab, sg)


def _da_sample_kernel(lam_ref, q_ref, ck_ref, cv_ref, kn_ref, vn_ref, za_ref, tabc_ref, tabn_ref, sg_ref,
                      o_ref, m_sc, l_sc, acc_sc, *, ts):
    cols = lambda h: slice(h * LANES, (h + 1) * LANES)
    heads = range(DA_HEADS)
    qs = [_stack_halves(q_ref[:, cols(h)]) for h in heads]
    m_sc[...] = jnp.full(m_sc.shape, NEG, F32)
    l_sc[...] = jnp.zeros(l_sc.shape, F32)
    acc_sc[...] = jnp.zeros(acc_sc.shape, F32)
    kc = ck_ref[0].astype(BF16)
    vc = cv_ref[0].astype(BF16)
    _softmax_step(qs, [kc[:, cols(h)] for h in heads], [vc[:, cols(h)] for h in heads], tabc_ref[...],
                  m_sc, l_sc, acc_sc)
    pad = jnp.zeros((LANES - ts, LANES), BF16)
    kn = [jnp.concatenate([kn_ref[:, cols(h)].astype(BF16), pad], axis=0) for h in heads]
    vn = [jnp.concatenate([vn_ref[:, cols(h)].astype(BF16), pad], axis=0) for h in heads]
    _softmax_step(qs, kn, vn, tabn_ref[...], m_sc, l_sc, acc_sc)
    l = l_sc[...]
    acc = acc_sc[...]
    za = za_ref[...]
    rows = lambda h: slice(h * 2 * ts, (h + 1) * 2 * ts)
    o_ref[...] = jnp.concatenate(
        [_diff_finish(lam_ref, za[:, cols(h)], sg_ref[...], l[rows(h)], acc[rows(h)], ts) for h in heads],
        axis=1).astype(BF16)


def _da_sample(lamv, qn, ck, cv, kn, vn, za, tabc, tabn, sg, ts):
    nb, past, _ = ck.shape
    kern = functools.partial(_da_sample_kernel, ts=ts)
    stack = lambda tab: jnp.concatenate([tab, tab], axis=1).reshape(DA_HEADS * 2 * ts, tab.shape[-1])
    tabc, tabn = stack(tabc), stack(tabn)
    rows = pl.BlockSpec((ts, DA_WIDTH), lambda b: (b, 0))
    cache = pl.BlockSpec((1, past, DA_WIDTH), lambda b: (b, 0, 0))
    full = lambda a: pl.BlockSpec(a.shape, lambda b: (0,) * a.ndim)
    return pl.pallas_call(
        kern,
        grid=(nb,),
        in_specs=[full(lamv), rows, cache, cache, rows, rows, rows, full(tabc), full(tabn), full(sg)],
        out_specs=rows,
        out_shape=jax.ShapeDtypeStruct((nb * ts, DA_WIDTH), BF16),
        scratch_shapes=[pltpu.VMEM((DA_HEADS * 2 * ts, LANES), F32)] * 3,
        compiler_params=_params(("arbitrary",)),
        name="diff_attn_sample",
    )(lamv, qn, ck, cv, kn, vn, za, tabc, tabn, sg)


def _bias_kernel(idx_ref, bt_ref, o_ref):
    idx = idx_ref[...]
    bt = bt_ref[...]
    bt = bt - bt[:, FAR_BUCKET:FAR_BUCKET + 1]
    onehot = jnp.where(lax.broadcasted_iota(jnp.int32, (N_BUCKETS, idx.shape[1]), 0) == idx, 1.0, 0.0).astype(BF16)
    hi = bt.astype(BF16)
    rest = bt - hi.astype(F32)
    mid = rest.astype(BF16)
    lo = (rest - mid.astype(F32)).astype(BF16)
    val = _dot(hi, onehot) + _dot(mid, onehot) + _dot(lo, onehot)
    o_ref[...] = jnp.where(idx < 0, NEG, val * LOG2E)


def _bias_tables(rel_bias, tq, ts, past):
    def buckets(rel, mask):
        return np.where(mask, _t5_bucket_np(rel), -1).astype(np.int32).reshape(-1)

    r = np.arange(tq)[:, None]
    c = np.arange(tq)[None, :]
    qpos = past + np.arange(ts)[:, None]
    kc = np.arange(past)[None, :]
    kn = past + np.arange(LANES)[None, :]
    parts = [buckets(c - r, (c // CHUNK) <= (r // CHUNK)),
             buckets(c - r - tq, np.ones((tq, tq), bool)),
             buckets(kc - qpos, (kc // CHUNK) <= (qpos // CHUNK)),
             buckets(kn - qpos, ((kn // CHUNK) <= (qpos // CHUNK)) & (kn < past + ts))]
    idx = np.concatenate(parts)
    n = idx.size
    bt = jnp.zeros((SUBLANES, N_BUCKETS), F32).at[:DA_HEADS].set(rel_bias.T)
    blk = 4096
    assert n % blk == 0
    tab = pl.pallas_call(
        _bias_kernel,
        grid=(n // blk,),
        in_specs=[pl.BlockSpec((1, blk), lambda i: (0, i)), pl.BlockSpec(bt.shape, lambda i: (0, 0))],
        out_specs=pl.BlockSpec((SUBLANES, blk), lambda i: (0, i)),
        out_shape=jax.ShapeDtypeStruct((SUBLANES, n), F32),
        compiler_params=_params(("arbitrary",)),
        name="t5_bias_tables",
    )(jnp.asarray(idx).reshape(1, n), bt)[:DA_HEADS]
    o1 = 2 * tq * tq
    o2 = o1 + ts * past
    prompt_tab = tab[:, :o1].reshape(DA_HEADS, 2, tq, tq)
    tabc = tab[:, o1:o2].reshape(DA_HEADS, ts, past)
    tabn = tab[:, o2:].reshape(DA_HEADS, ts, LANES)
    return prompt_tab, tabc, tabn


def _rwkv_kernel(prw_ref, look_ref, first_ref, zb_ref, s0_ref, mu_ref, w0_ref, a0_ref, wwa_ref,
                 kkw_ref, kaw_ref, rk_ref, lng_ref, lnb_ref, g2_ref, hm_ref, dm_ref, pm_ref, ti_ref, to_ref,
                 ob_ref, sfin_ref,
                 s_sc, om_sc, rho_sc, at_sc, kt_sc, atb_sc, ktb_sc, wb_sc, vr_sc, o_sc, *, tm):
    j = pl.program_id(1)

    @pl.when(j == 0)
    def _():
        s_sc[...] = s0_ref[0]

    g2 = g2_ref[...]

    def head_sums(t):
        return jnp.concatenate([_dot2(t[:, c * RW_STATE_LANES:(c + 1) * RW_STATE_LANES], g2)
                                for c in range(RW_STATE_GROUPS)], axis=1)

    x = prw_ref[...]
    row0 = jnp.where(j == 0, first_ref[0], look_ref[7:8, :])
    rows = lax.broadcasted_iota(jnp.int32, x.shape, 0)
    prev = jnp.where(rows == 0, row0, pltpu.roll(x, 1, axis=0))
    mix = x + (prev - x) * mu_ref[...]
    r = mix[:, 0:512]
    kr = mix[:, 512:1024]
    vr = mix[:, 1024:1536]
    la = mix[:, 1536:1664]
    lane = lax.broadcasted_iota(jnp.int32, la.shape, 1)
    la = jnp.where(lane < HEAD, jnp.tanh(la), la)
    hi = la.astype(BF16)
    lo = (la - hi.astype(F32)).astype(BF16)
    wwa = wwa_ref[...]
    lora = _dot(hi, wwa[0]) + _dot(lo, wwa[0]) + _dot(hi, wwa[1])
    wpre = w0_ref[...] + lora[:, 0:512]
    nw = -wpre
    w = -(jnp.maximum(nw, 0.0) + jnp.log(1.0 + jnp.exp(-jnp.abs(nw)))) - 0.5
    ld = -jnp.exp(w)
    a = 1.0 / (1.0 + jnp.exp(-(a0_ref[...] + lora[:, 512:1024])))
    kk = kr * kkw_ref[...]
    nrm = jnp.sqrt(head_sums(kk * kk))
    kk = kk / jnp.maximum(nrm, 1e-12)
    kka = kk * a
    kh = kr * (1.0 + (a - 1.0) * kaw_ref[...])

    def dot3(m, t):
        hi = t.astype(BF16)
        r1 = t - hi.astype(F32)
        mid = r1.astype(BF16)
        lo = (r1 - mid.astype(F32)).astype(BF16)
        return _dot(m, hi) + _dot(m, mid) + _dot(m, lo)

    incl = dot3(ti_ref[...], ld)
    tot = dot3(to_ref[...], ld)
    e_rem = jnp.exp(tot - incl)
    e_inv = jnp.exp(-incl)
    om_sc[...] = jnp.exp(incl - ld) * kk
    rho_sc[...] = jnp.exp(incl) * r
    at_sc[...] = kka * e_inv
    kt_sc[...] = kh * e_inv
    atb_sc[...] = kka * e_rem
    ktb_sc[...] = kh * e_rem
    wb_sc[...] = jnp.exp(tot)
    hm = hm_ref[...]
    vx = (jnp.broadcast_to(vr[:, None, :], (tm, RW_HEADS, RW_WIDTH)) * hm[None]).reshape(tm * RW_HEADS, RW_WIDTH)
    vr_sc[...] = _dot(vx.astype(BF16), pm_ref[...])

    nb = RW_BLOCK
    heads_per_group = RW_HEADS // RW_STATE_GROUPS
    dm = dm_ref[...]

    def head_rows(blk):
        return jnp.concatenate([jnp.broadcast_to(blk[j:j + 1], (RW_HEADS, RW_WIDTH)) * hm for j in range(nb)],
                               axis=0)

    def fold(x):
        return x[:, :RW_STATE_LANES] + x[:, RW_STATE_LANES:]

    def query_rows(b):
        base = pl.multiple_of(b * nb, nb)
        return jnp.concatenate([head_rows(om_sc[pl.ds(base, nb), :]), head_rows(rho_sc[pl.ds(base, nb), :])], axis=0)

    nblk = tm // nb

    def coefficients(b, x):
        base = pl.multiple_of(b * nb, nb)
        rhs = jnp.concatenate([at_sc[pl.ds(base, nb), :], kt_sc[pl.ds(base, nb), :],
                               jnp.zeros((LANES - 2 * nb, RW_WIDTH), F32)], axis=0)
        return _dot_nt(x.astype(BF16), rhs.astype(BF16))

    def block(b, carry):
        s, g, coef = carry
        base = pl.multiple_of(b * nb, nb)
        atb_b, ktb_b, wb_b = (ref[pl.ds(base, nb), :] for ref in (atb_sc, ktb_sc, wb_sc))
        vr_b = vr_sc[pl.ds(pl.multiple_of(b * (nb * RW_HEADS), nb * RW_HEADS), nb * RW_HEADS), :]
        bn = jnp.minimum(b + 1, nblk - 1)
        xn = query_rows(bn)
        coef_next = coefficients(bn, xn)
        rhs2 = jnp.concatenate([fold(head_rows(atb_b)), fold(head_rows(ktb_b))], axis=0).astype(BF16)
        cross = _dot_nt(fold(xn).astype(BF16), rhs2).astype(BF16)
        g_decayed = _dot_nt(fold(xn * wb_b[0:1]).astype(BF16), s.astype(BF16))

        def rows(a, j):
            return a[j * RW_HEADS:(j + 1) * RW_HEADS]

        def cf(j, lane):
            return jnp.broadcast_to(rows(coef, j)[:, lane:lane + 1], (RW_HEADS, LANES))

        sa = []
        for j in range(nb):
            acc = rows(g, j) * rows(dm, j)
            for i in range(j):
                acc = acc + cf(j, nb + i) * rows(vr_b, i)
            for i in range(j):
                acc = acc - cf(j, i) * sa[i]
            sa.append(acc)
        outs = []
        for j in range(nb):
            acc = rows(g, nb + j) * rows(dm, j)
            for i in range(j + 1):
                acc = acc + cf(nb + j, nb + i) * rows(vr_b, i) - cf(nb + j, i) * sa[i]
            outs.append(acc)
        lhs = jnp.concatenate([-t for t in sa] + [vr_b], axis=0).astype(BF16)
        g_next = g_decayed + _dot(cross, lhs)
        ds = _dot(jnp.transpose(lhs.astype(F32)).astype(BF16), rhs2)
        wb = jnp.concatenate(
            [jnp.broadcast_to(wb_b[0:1, c * RW_STATE_LANES:(c + 1) * RW_STATE_LANES], (HEAD, RW_STATE_LANES))
             for c in range(RW_STATE_GROUPS)], axis=0)
        for h in range(RW_HEADS):
            o_sc[h, pl.ds(base, nb), :] = jnp.concatenate([t[h:h + 1] for t in outs], axis=0)
        return s * wb + ds, g_next, coef_next

    s0 = s_sc[...]
    x0 = query_rows(0)
    g0 = _dot_nt(fold(x0).astype(BF16), s0.astype(BF16))
    s_sc[...] = lax.fori_loop(0, nblk, block, (s0, g0, coefficients(0, x0)))[0]

    o = jnp.concatenate([o_sc[c * heads_per_group + hh][:, c * HEAD:(c + 1) * HEAD]
                         for c in range(RW_STATE_GROUPS) for hh in range(heads_per_group)], axis=1)
    mean = head_sums(o) * (1.0 / HEAD)
    d = o - mean
    var = head_sums(d * d) * (1.0 / HEAD)
    y = d * lax.rsqrt(var + RW_GN_EPS) * lng_ref[...] + lnb_ref[...]
    bonus = head_sums(r * kh * rk_ref[...]) * vr
    ob_ref[...] = ((y + bonus) * _silu(zb_ref[...])).astype(BF16)

    @pl.when(j == pl.num_programs(1) - 1)
    def _():
        sfin_ref[0] = s_sc[...]


def _block_sum_matrices(tm):
    t = np.arange(tm)
    same = (t[:, None] // RW_BLOCK) == (t[None, :] // RW_BLOCK)
    incl = same & (t[None, :] <= t[:, None])
    return [jnp.asarray(incl.astype(np.float32), dtype=BF16), jnp.asarray(same.astype(np.float32), dtype=BF16)]


def _rwkv(prw, first, zb, s0, p, seq_len, tm):
    m = prw.shape[0]
    nseq = m // seq_len
    ntile = seq_len // tm
    kern = functools.partial(_rwkv_kernel, tm=tm)
    full = lambda a: pl.BlockSpec(a.shape, lambda b, j: (0,) * a.ndim)
    rows = lambda c: pl.BlockSpec((tm, c), lambda b, j: (b * ntile + j, 0))
    look = pl.BlockSpec((8, RW_SHIFT_COLS), lambda b, j: (jnp.maximum((b * ntile + j) * (tm // 8) - 1, 0), 0))
    consts = [p["mu"], p["w0"], p["a0"], p["wwa"], p["kkw"], p["kaw"], p["rk"], p["lng"], p["lnb"],
              p["g2"], p["hm"], p["dm"], p["pm"]] + _block_sum_matrices(tm)
    return pl.pallas_call(
        kern,
        grid=(nseq, ntile),
        in_specs=[rows(RW_SHIFT_COLS), look,
                  pl.BlockSpec((1, 1, RW_SHIFT_COLS), lambda b, j: (b, 0, 0)),
                  rows(RW_WIDTH),
                  pl.BlockSpec((1,) + RW_STATE_SHAPE, lambda b, j: (b, 0, 0))] + [full(c) for c in consts],
        out_specs=[rows(RW_WIDTH), pl.BlockSpec((1,) + RW_STATE_SHAPE, lambda b, j: (b, 0, 0))],
        out_shape=[jax.ShapeDtypeStruct((m, RW_WIDTH), BF16), jax.ShapeDtypeStruct((nseq,) + RW_STATE_SHAPE, F32)],
        scratch_shapes=([pltpu.VMEM(RW_STATE_SHAPE, F32)] + [pltpu.VMEM((tm, RW_WIDTH), F32)] * 7
                        + [pltpu.VMEM((tm * RW_HEADS, RW_STATE_SHAPE[0]), F32),
                           pltpu.VMEM((RW_HEADS, tm, RW_STATE_SHAPE[0]), F32)]),
        compiler_params=_params(("arbitrary", "arbitrary")),
        name="rwkv7",
    )(prw, prw, first, zb, s0, *consts)


def _state_to_rows(s):
    b = s.shape[0]
    per = RW_HEADS // RW_STATE_GROUPS
    return s.reshape(b, RW_STATE_GROUPS, per, HEAD, HEAD).transpose(0, 1, 3, 2, 4).reshape((b,) + RW_STATE_SHAPE)


def _rows_to_state(s):
    b = s.shape[0]
    per = RW_HEADS // RW_STATE_GROUPS
    return s.reshape(b, RW_STATE_GROUPS, HEAD, per, HEAD).transpose(0, 1, 3, 2, 4).reshape(b, RW_HEADS, HEAD, HEAD)


SKIP_LOG = -104.0


def _sb_tile(qs, k, v, u, c, mask):
    tk = k[0].shape[0]
    rows = qs[0].shape[0]
    z = jnp.concatenate([_dot_nt(q_p, k_p) for q_p, k_p in zip(qs, k)], axis=0)
    lg = -(jnp.maximum(z, 0.0) + jnp.log(1.0 + jnp.exp(-jnp.abs(z))))
    if mask is not None:
        lg = jnp.where(mask, lg, 0.0)
    hi = lg.astype(BF16)
    lo = (lg - hi.astype(F32)).astype(BF16)
    it = _dot(jnp.concatenate([hi, lo], axis=1), u)
    a = jnp.exp(z + it + jnp.concatenate([c] * (tk // LANES), axis=1))
    if mask is not None:
        a = jnp.where(mask, a, 0.0)
    a = a.astype(BF16)
    da = jnp.concatenate([_dot(a[p * rows:(p + 1) * rows], v_p) for p, v_p in enumerate(v)], axis=0)
    return da, c + jnp.broadcast_to(it[:, 0:1], c.shape)


def _sb_finish(acc, z):
    t = z.shape[0]
    lane = lax.broadcasted_iota(jnp.int32, z.shape, 1)
    return jnp.where(lane < HEAD, acc[:t], acc[t:]) * _silu(z)


def _sb_prompt_kernel(q_ref, k_ref, v_ref, z_ref, u_ref, o_ref, acc_sc, c_sc, *, tq):
    i = pl.program_id(1)
    tk = u_ref.shape[0] // 2
    r = tq // tk
    qs = _stack_halves(q_ref[...])
    u = u_ref[...]
    acc_sc[...] = jnp.zeros(acc_sc.shape, F32)
    c_sc[...] = jnp.zeros(c_sc.shape, F32)
    def tile(j, lo, masked):
        n = tq - lo
        halves = lambda x: jnp.concatenate([x[lo:tq], x[tq + lo:2 * tq]], axis=0)
        off = pl.multiple_of(j * tk, tk)
        mask = None
        if masked:
            qrow = lax.broadcasted_iota(jnp.int32, (2 * n, tk), 0) % n
            col = lax.broadcasted_iota(jnp.int32, (2 * n, tk), 1)
            mask = col < qrow
        da, c = _sb_tile([halves(qs)], [k_ref[pl.ds(off, tk), :]], [v_ref[pl.ds(off, tk), :]], u,
                         halves(c_sc[...]), mask)
        for h in range(2):
            acc_sc[h * tq + lo:(h + 1) * tq, :] += da[h * n:(h + 1) * n]
            c_sc[h * tq + lo:(h + 1) * tq, :] = c[h * n:(h + 1) * n]

    for jj in reversed(range(r)):
        tile(i * r + jj, jj * tk, True)

    def body(st):
        tile(i * r - 1 - st[0], 0, False)
        return st[0] + 1, jnp.max(c_sc[...])

    lax.while_loop(lambda st: jnp.logical_and(st[0] < i * r, st[1] > SKIP_LOG), body,
                   (jnp.int32(0), jnp.max(c_sc[...])))
    o_ref[...] = _sb_finish(acc_sc[...], z_ref[...]).astype(BF16)


def _sb_prompt(qb, kb, vb, z, u, tq):
    t = qb.shape[0]
    kern = functools.partial(_sb_prompt_kernel, tq=tq)
    rows = pl.BlockSpec((tq, LANES), lambda h, i: (i, h))
    whole = pl.BlockSpec((t, LANES), lambda h, i: (0, h))
    return pl.pallas_call(
        kern,
        grid=(SB_HEADS // 2, t // tq),
        in_specs=[rows, whole, whole, rows, pl.BlockSpec(u.shape, lambda h, i: (0, 0))],
        out_specs=rows,
        out_shape=jax.ShapeDtypeStruct((t, SB_WIDTH), BF16),
        scratch_shapes=[pltpu.VMEM((2 * tq, LANES), F32)] * 2,
        compiler_params=_params(("arbitrary", "arbitrary")),
        name="stick_breaking_prompt",
    )(qb, kb, vb, z, u)


def _sb_sample_kernel(q_ref, ck_ref, cv_ref, kn_ref, vn_ref, z_ref, u_ref, o_ref, acc_sc, c_sc, *, ts):
    tk = u_ref.shape[0] // 2
    ntile = ck_ref.shape[1] // tk
    npair = SB_HEADS // 2
    cols = lambda p: slice(p * LANES, (p + 1) * LANES)
    qs = [_stack_halves(q_ref[:, cols(p)]) for p in range(npair)]
    u = u_ref[...]
    pad = jnp.zeros((tk - ts, LANES), BF16)
    kn = [jnp.concatenate([kn_ref[:, cols(p)], pad], axis=0) for p in range(npair)]
    vn = [jnp.concatenate([vn_ref[:, cols(p)], pad], axis=0) for p in range(npair)]
    rows = npair * 2 * ts
    qrow = lax.broadcasted_iota(jnp.int32, (rows, tk), 0) % ts
    col = lax.broadcasted_iota(jnp.int32, (rows, tk), 1)
    acc, c = _sb_tile(qs, kn, vn, u, jnp.zeros((rows, LANES), F32), col < qrow)
    acc_sc[...] = acc
    c_sc[...] = c

    def body(st):
        off = pl.multiple_of((ntile - 1 - st[0]) * tk, tk)
        kc = ck_ref[0, pl.ds(off, tk), :].astype(BF16)
        vc = cv_ref[0, pl.ds(off, tk), :].astype(BF16)
        da, c = _sb_tile(qs, [kc[:, cols(p)] for p in range(npair)], [vc[:, cols(p)] for p in range(npair)],
                         u, c_sc[...], None)
        acc_sc[...] += da
        c_sc[...] = c
        return st[0] + 1, jnp.max(c)

    lax.while_loop(lambda st: jnp.logical_and(st[0] < ntile, st[1] > SKIP_LOG), body,
                   (jnp.int32(0), jnp.max(c)))
    acc = acc_sc[...]
    z = z_ref[...]
    o_ref[...] = jnp.concatenate(
        [_sb_finish(acc[p * 2 * ts:(p + 1) * 2 * ts], z[:, cols(p)]) for p in range(npair)], axis=1).astype(BF16)


def _sb_sample(qb, ck, cv, kb, vb, z, u, ts):
    nb, past, _ = ck.shape
    kern = functools.partial(_sb_sample_kernel, ts=ts)
    rows = pl.BlockSpec((ts, SB_WIDTH), lambda b: (b, 0))
    cache = pl.BlockSpec((1, past, SB_WIDTH), lambda b: (b, 0, 0))
    return pl.pallas_call(
        kern,
        grid=(nb,),
        in_specs=[rows, cache, cache, rows, rows, rows, pl.BlockSpec(u.shape, lambda b: (0, 0))],
        out_specs=rows,
        out_shape=jax.ShapeDtypeStruct((nb * ts, SB_WIDTH), BF16),
        scratch_shapes=[pltpu.VMEM((SB_HEADS * ts, LANES), F32)] * 2,
        compiler_params=_params(("arbitrary",)),
        name="stick_breaking_sample",
    )(qb, ck, cv, kb, vb, z, u)


def _suffix_matrix(tk):
    j = np.arange(tk)[:, None]
    s = np.arange(tk)[None, :]
    u = (j >= s).astype(np.float32)
    return jnp.asarray(np.concatenate([u, u], axis=0), dtype=BF16)


def _row_tile(m):
    return 256 if m % 256 == 0 else m


def kernel(x_prompt, x_sample, cache_l0_k, cache_l0_v, state_l0_shift, state_l0_wkv, cache_l1_k, cache_l1_v, rel_bias, norm_l0, w_in_l0, w_out_l0, da_q_norm, da_k_norm, da_lambda_q1, da_lambda_k1, da_lambda_q2, da_lambda_k2, da_subln, rw_mu, rw_w0, rw_w_up, rw_a0, rw_a_up, rw_k_k, rw_k_a, rw_r_k, rw_lnx_g, rw_lnx_b, norm_l1, w_in_l1, w_out_l1):
    bp, t, d = x_prompt.shape
    nb, ts, _ = x_sample.shape
    past = cache_l0_k.shape[1]
    assert bp == 1 and ts <= LANES and t % 256 == 0 and past % 256 == 0

    row = lambda v: v.reshape(1, -1).astype(F32)
    w_in0 = w_in_l0.astype(BF16)
    w_in1 = w_in_l1.astype(BF16)
    w_out0a = w_out_l0[:DA_WIDTH].astype(BF16)
    w_out0b = w_out_l0[DA_WIDTH:].astype(BF16)
    w_out1 = w_out_l1.astype(BF16)
    qg = row(jnp.tile(da_q_norm, 512 // HEAD))
    kg = row(jnp.tile(da_k_norm, 512 // HEAD))
    lamv = jnp.stack([da_lambda_q1, da_lambda_k1, da_lambda_q2, da_lambda_k2]).astype(F32)
    sg = row(da_subln)
    zeros = jnp.zeros((HEAD, RW_WIDTH), F32)
    wwa_f = jnp.concatenate([jnp.concatenate([rw_w_up, zeros], axis=1),
                             jnp.concatenate([zeros, rw_a_up], axis=1)], axis=0)
    wwa_hi = wwa_f.astype(BF16)
    wwa = jnp.stack([wwa_hi, (wwa_f - wwa_hi.astype(F32)).astype(BF16)])
    heads = np.arange(RW_HEADS)
    per_group = RW_HEADS // RW_STATE_GROUPS
    chan = np.arange(RW_WIDTH)
    out_lane = np.arange(RW_STATE_SHAPE[0])
    hm = np.equal(heads[:, None], chan[None, :] // HEAD)
    dm = np.equal(heads[:, None] // per_group, out_lane[None, :] // HEAD)
    pm = (chan[:, None] // HEAD // per_group == out_lane[None, :] // HEAD) & (chan[:, None] % HEAD == out_lane[None, :] % HEAD)
    rw = dict(mu=row(rw_mu), w0=row(rw_w0), a0=row(rw_a0), wwa=wwa, kkw=row(rw_k_k), kaw=row(rw_k_a),
              rk=row(rw_r_k), lng=row(rw_lnx_g), lnb=row(rw_lnx_b), g2=_group_matrix(RW_STATE_LANES, 1.0),
              hm=jnp.asarray(hm.astype(np.float32)),
              dm=jnp.asarray(np.tile(dm.astype(np.float32), (RW_BLOCK, 1))),
              pm=jnp.asarray(pm.astype(np.float32), dtype=BF16))
    tq_da = 256
    prompt_tab, tabc, tabn = _bias_tables(rel_bias.astype(F32), tq_da, ts, past)
    u_prompt = _suffix_matrix(2 * LANES)
    u_sample = _suffix_matrix(LANES)

    def layer0(x, first, s0, seq_len, attend):
        m = x.shape[0]
        tm = _row_tile(m)
        qn, kn, knb, v, vb, za, prw, zb = _inproj0(x, row(norm_l0), w_in0, qg, kg, tm)
        oa = attend(qn, kn, knb, v, vb, za)
        ob, sfin = _rwkv(prw, first, zb, _state_to_rows(s0), rw, seq_len, min(seq_len, 256))
        y = _outproj(x, [oa, ob], [w_out0a, w_out0b], tm)
        shift = prw.reshape(m // seq_len, seq_len, RW_SHIFT_COLS)[:, -1:]
        return y, kn, v, shift, _rows_to_state(sfin)

    def layer1(x, attend):
        m = x.shape[0]
        tm = _row_tile(m)
        qb, k, kb, v, vb, z = _inproj1(x, row(norm_l1), w_in1, tm)
        o = attend(qb, kb, vb, z)
        return _outproj(x, [o], [w_out1], tm), k, v

    xp = x_prompt.reshape(t, d)
    yp, k0p, v0p, shp, wkvp = layer0(
        xp, jnp.zeros((1, 1, RW_SHIFT_COLS), F32), jnp.zeros((1, RW_HEADS, HEAD, HEAD), F32), t,
        lambda qn, kn, knb, v, vb, za: _da_prompt(lamv, qn, knb, vb, za, prompt_tab, sg, tq_da))
    yp, k1p, v1p = layer1(yp, lambda qb, kb, vb, z: _sb_prompt(qb, kb, vb, z, u_prompt, 512))

    xs = x_sample.reshape(nb * ts, d)
    ck0 = cache_l0_k.reshape(nb, past, DA_WIDTH)
    cv0 = cache_l0_v.reshape(nb, past, DA_WIDTH)
    ys, k0s, v0s, shs, wkvs = layer0(
        xs, state_l0_shift, state_l0_wkv, ts,
        lambda qn, kn, knb, v, vb, za: _da_sample(lamv, qn, ck0, cv0, knb, vb, za, tabc, tabn, sg, ts))
    ck1 = cache_l1_k.reshape(nb, past, SB_WIDTH)
    cv1 = cache_l1_v.reshape(nb, past, SB_WIDTH)
    ys, k1s, v1s = layer1(ys, lambda qb, kb, vb, z: _sb_sample(qb, ck1, cv1, kb, vb, z, u_sample, ts))

    return (yp.reshape(1, t, d), ys.reshape(nb, ts, d),
            k0p.reshape(1, t, DA_HEADS, LANES), v0p.reshape(1, t, DA_HEADS, LANES), shp, wkvp,
            k1p.reshape(1, t, SB_HEADS, HEAD), v1p.reshape(1, t, SB_HEADS, HEAD),
            k0s.reshape(nb, ts, DA_HEADS, LANES), v0s.reshape(nb, ts, DA_HEADS, LANES), shs, wkvs,
            k1s.reshape(nb, ts, SB_HEADS, HEAD), v1s.reshape(nb, ts, SB_HEADS, HEAD))
```

```python
import functools
import math

import numpy as np
import jax
import jax.numpy as jnp
from jax import lax
from jax.experimental import pallas as pl
from jax.experimental.pallas import tpu as pltpu

F32 = jnp.float32
BF16 = jnp.bfloat16

EPS = 1e-6
NEG = -1e30
CHUNK = 64
LANES = 128
SUBLANES = 8
HEAD = 64
DA_HEADS = 4
DA_WIDTH = 512
RW_WIDTH = 512
RW_HEADS = 8
RW_SHIFT_COLS = 3 * RW_WIDTH + 128
RW_GN_EPS = 64e-5
RW_STATE_LANES = 256
RW_STATE_GROUPS = RW_WIDTH // RW_STATE_LANES
RW_STATE_SHAPE = (RW_STATE_GROUPS * HEAD, RW_STATE_LANES)
RW_BLOCK = 8
SB_HEADS = 16
SB_WIDTH = 1024
N_BUCKETS = 32
MAX_DISTANCE = 128
LAMBDA_INIT = 0.8 - 0.6 * math.exp(-0.3 * 0)
LOG2E = math.log2(math.e)
VMEM_LIMIT = 56 * 1024 * 1024


def _params(sem):
    return pltpu.CompilerParams(dimension_semantics=sem, vmem_limit_bytes=VMEM_LIMIT)


def _dot(a, b):
    return jnp.dot(a, b, preferred_element_type=F32)


def _dot_nt(a, b):
    return lax.dot_general(a, b, (((1,), (1,)), ((), ())), preferred_element_type=F32)


def _dot2(x, w):
    hi = x.astype(BF16)
    lo = (x - hi.astype(F32)).astype(BF16)
    return _dot(hi, w) + _dot(lo, w)


def _silu(z):
    return z / (1.0 + jnp.exp(-z))


def _rms(x, g):
    return x * lax.rsqrt(jnp.mean(x * x, axis=-1, keepdims=True) + EPS) * g


def _group_matrix(n, scale):
    idx = np.arange(n) // HEAD
    return jnp.asarray((idx[:, None] == idx[None, :]).astype(np.float32) * scale, dtype=BF16)


def _inproj0_kernel(x_ref, g_ref, w_ref, qg_ref, kg_ref, avg_ref,
                    qn_ref, kn_ref, knb_ref, v_ref, vb_ref, za_ref, prw_ref, zb_ref):
    y = _dot(_rms(x_ref[...], g_ref[...]).astype(BF16), w_ref[...])

    def head_norm(t, g):
        sq = t * t
        half = avg_ref.shape[0]
        ms = jnp.concatenate([_dot2(sq[:, c:c + half], avg_ref[...]) for c in range(0, t.shape[1], half)], axis=1)
        return t * lax.rsqrt(ms + EPS) * g

    qn_ref[...] = (head_norm(y[:, 0:512], qg_ref[...]) * (HEAD ** -0.5 * LOG2E)).astype(BF16)
    kn = head_norm(y[:, 512:1024], kg_ref[...])
    knb_ref[...] = kn.astype(BF16)
    v = y[:, 1024:1536]
    vb_ref[...] = v.astype(BF16)
    for h in range(DA_HEADS):
        kn_ref[:, h, :] = kn[:, h * LANES:(h + 1) * LANES]
        v_ref[:, h, :] = v[:, h * LANES:(h + 1) * LANES]
    za_ref[...] = y[:, 1536:2048]
    prw_ref[...] = y[:, 2048:2048 + RW_SHIFT_COLS]
    zb_ref[...] = y[:, 2048 + RW_SHIFT_COLS:]


def _inproj0(x, g, w, qg, kg, tm):
    m, d = x.shape
    n = w.shape[1]
    row = lambda c: pl.BlockSpec((tm, c), lambda i: (i, 0))
    full = lambda a: pl.BlockSpec(a.shape, lambda i: (0,) * a.ndim)
    avg = _group_matrix(2 * LANES, 1.0 / HEAD)
    widths = [(512, BF16), (None, F32), (512, BF16), (None, F32), (512, BF16), (512, F32),
              (RW_SHIFT_COLS, F32), (512, F32)]
    heads = (DA_HEADS, LANES)
    return pl.pallas_call(
        _inproj0_kernel,
        grid=(m // tm,),
        in_specs=[row(d), full(g), full(w), full(qg), full(kg), full(avg)],
        out_specs=[row(c) if c else pl.BlockSpec((tm,) + heads, lambda i: (i, 0, 0)) for c, _ in widths],
        out_shape=[jax.ShapeDtypeStruct((m, c) if c else (m,) + heads, dt) for c, dt in widths],
        compiler_params=_params(("arbitrary",)),
        name="inproj0",
    )(x, g, w, qg, kg, avg)


def _inproj1_kernel(x_ref, g_ref, w_ref, qb_ref, k_ref, kb_ref, v_ref, vb_ref, z_ref):
    y = _dot(_rms(x_ref[...], g_ref[...]).astype(BF16), w_ref[...])
    qb_ref[...] = (y[:, 0:1024] * (HEAD ** -0.5)).astype(BF16)
    k = y[:, 1024:2048]
    k_ref[...] = k
    kb_ref[...] = k.astype(BF16)
    v = y[:, 2048:3072]
    v_ref[...] = v
    vb_ref[...] = v.astype(BF16)
    z_ref[...] = y[:, 3072:4096]


def _inproj1(x, g, w, tm):
    m, d = x.shape
    row = lambda c: pl.BlockSpec((tm, c), lambda i: (i, 0))
    full = lambda a: pl.BlockSpec(a.shape, lambda i: (0,) * a.ndim)
    dts = [BF16, F32, BF16, F32, BF16, F32]
    return pl.pallas_call(
        _inproj1_kernel,
        grid=(m // tm,),
        in_specs=[row(d), full(g), full(w)],
        out_specs=[row(1024) for _ in dts],
        out_shape=[jax.ShapeDtypeStruct((m, 1024), dt) for dt in dts],
        compiler_params=_params(("arbitrary",)),
        name="inproj1",
    )(x, g, w)


def _outproj_kernel(*refs):
    n = (len(refs) - 2) // 2
    x_ref, o_ref = refs[0], refs[-1]
    acc = x_ref[...]
    for a_ref, w_ref in zip(refs[1:1 + n], refs[1 + n:1 + 2 * n]):
        acc = acc + _dot(a_ref[...], w_ref[...])
    o_ref[...] = acc


def _outproj(x, acts, ws, tm):
    m, d = x.shape
    row = lambda c: pl.BlockSpec((tm, c), lambda i: (i, 0))
    full = lambda a: pl.BlockSpec(a.shape, lambda i: (0,) * a.ndim)
    return pl.pallas_call(
        _outproj_kernel,
        grid=(m // tm,),
        in_specs=[row(d)] + [row(a.shape[1]) for a in acts] + [full(w) for w in ws],
        out_specs=row(d),
        out_shape=jax.ShapeDtypeStruct((m, d), F32),
        compiler_params=_params(("arbitrary",)),
        name="outproj",
    )(x, *acts, *ws)


def _t5_bucket_np(rel):
    nb = N_BUCKETS // 2
    max_exact = nb // 2
    n = np.abs(rel)
    nf = np.maximum(n, 1).astype(np.float32)
    large = max_exact + (np.log(nf / np.float32(max_exact)) / np.float32(math.log(MAX_DISTANCE / max_exact))
                         * np.float32(nb - max_exact)).astype(np.int32)
    large = np.minimum(large, nb - 1)
    return np.where(rel > 0, nb, 0) + np.where(n < max_exact, n, large)


FAR_BUCKET = N_BUCKETS // 2 - 1
FAR_TILES = (8, 2, 1)


def _stack_halves(q):
    lane = lax.broadcasted_iota(jnp.int32, q.shape, 1)
    zero = jnp.zeros_like(q)
    return jnp.concatenate([jnp.where(lane < HEAD, q, zero), jnp.where(lane >= HEAD, q, zero)], axis=0)


def _softmax_step(qs, k, v, bias, m_sc, l_sc, acc_sc):
    s = jnp.concatenate([_dot_nt(q_h, k_h) for q_h, k_h in zip(qs, k)], axis=0)
    if bias is not None:
        s = s + bias
    _softmax_update(s, v, m_sc, l_sc, acc_sc)


def _softmax_update(s, v, m_sc, l_sc, acc_sc):
    reps = s.shape[1] // LANES
    m_old = m_sc[...]
    m_new = jnp.maximum(m_old, jnp.max(s, axis=-1, keepdims=True))
    alpha = jnp.exp2(m_old - m_new)
    p = jnp.exp2(s - jnp.concatenate([m_new] * reps, axis=1))
    psum = p[:, :LANES]
    for r in range(1, reps):
        psum = psum + p[:, r * LANES:(r + 1) * LANES]
    l_sc[...] = alpha * l_sc[...] + psum
    p = p.astype(BF16)
    rows = p.shape[0] // len(v)
    pv = jnp.concatenate([_dot(p[h * rows:(h + 1) * rows], v_h) for h, v_h in enumerate(v)], axis=0)
    acc_sc[...] = alpha * acc_sc[...] + pv
    m_sc[...] = m_new


def _diff_finish(lam_ref, za, sg, l, acc, tq):
    lv = lam_ref[...]
    lam = (jnp.exp(jnp.sum(lv[0:1] * lv[1:2], axis=-1, keepdims=True))
           - jnp.exp(jnp.sum(lv[2:3] * lv[3:4], axis=-1, keepdims=True)) + LAMBDA_INIT)
    o = acc / jnp.sum(l, axis=-1, keepdims=True)
    attn = o[:tq] - lam * o[tq:]
    return (_rms(attn, sg) * (1.0 - LAMBDA_INIT)) * _silu(za)


def _da_prompt_kernel(lam_ref, q_ref, k_ref, v_ref, za_ref, tab_ref, sg_ref, o_ref,
                      m_sc, l_sc, acc_sc, *, tq, tk):
    i = pl.program_id(1)
    qs = _stack_halves(q_ref[...])
    m_sc[...] = jnp.full(m_sc.shape, NEG, F32)
    l_sc[...] = jnp.zeros(l_sc.shape, F32)
    acc_sc[...] = jnp.zeros(acc_sc.shape, F32)

    def tile(j, width, bias):
        off = pl.multiple_of(j * tk, tk)
        _softmax_step([qs], [k_ref[pl.ds(off, width), :]], [v_ref[pl.ds(off, width), :]], bias,
                      m_sc, l_sc, acc_sc)

    nfar = jnp.maximum(i - 1, 0)

    done = 0
    for width in FAR_TILES:
        def far(j, c, width=width, done=done):
            tile(done + width * j, width * tk, None)
            return c

        trips = (nfar - done) // width
        lax.fori_loop(0, trips, far, 0)
        done = done + trips * width

    @pl.when(i >= 1)
    def _():
        b = jnp.concatenate([tab_ref[0, 1], tab_ref[0, 0]], axis=1)
        tile(i - 1, 2 * tk, jnp.concatenate([b, b], axis=0))

    @pl.when(i == 0)
    def _():
        b = tab_ref[0, 0]
        tile(0, tk, jnp.concatenate([b, b], axis=0))
    o_ref[...] = _diff_finish(lam_ref, za_ref[...], sg_ref[...], l_sc[...], acc_sc[...], tq).astype(BF16)


def _da_prompt(lamv, qn, knb, vb, za, tab, sg, tq):
    t = qn.shape[0]
    tk = tq
    kern = functools.partial(_da_prompt_kernel, tq=tq, tk=tk)
    return pl.pallas_call(
        kern,
        grid=(DA_HEADS, t // tq),
        in_specs=[
            pl.BlockSpec(lamv.shape, lambda h, i: (0, 0)),
            pl.BlockSpec((tq, LANES), lambda h, i: (i, h)),
            pl.BlockSpec((t, LANES), lambda h, i: (0, h)),
            pl.BlockSpec((t, LANES), lambda h, i: (0, h)),
            pl.BlockSpec((tq, LANES), lambda h, i: (i, h)),
            pl.BlockSpec((1, 2, tq, tk), lambda h, i: (h, 0, 0, 0)),
            pl.BlockSpec(sg.shape, lambda h, i: (0, 0)),
        ],
        out_specs=pl.BlockSpec((tq, LANES), lambda h, i: (i, h)),
        out_shape=jax.ShapeDtypeStruct((t, DA_WIDTH), BF16),
        scratch_shapes=[pltpu.VMEM((2 * tq, LANES), F32)] * 3,
        compiler_params=_params(("arbitrary", "arbitrary")),
        name="diff_attn_prompt",
    )(lamv, qn, knb, vb, za, tab, sg)


def _da_sample_kernel(lam_ref, q_ref, ck_ref, cv_ref, kn_ref, vn_ref, za_ref, tabc_ref, tabn_ref, sg_ref,
                      o_ref, m_sc, l_sc, acc_sc, *, ts):
    cols = lambda h: slice(h * LANES, (h + 1) * LANES)
    heads = range(DA_HEADS)
    qs = [_stack_halves(q_ref[:, cols(h)]) for h in heads]
    m_sc[...] = jnp.full(m_sc.shape, NEG, F32)
    l_sc[...] = jnp.zeros(l_sc.shape, F32)
    acc_sc[...] = jnp.zeros(acc_sc.shape, F32)
    kc = ck_ref[0].astype(BF16)
    vc = cv_ref[0].astype(BF16)
    _softmax_step(qs, [kc[:, cols(h)] for h in heads], [vc[:, cols(h)] for h in heads], tabc_ref[...],
                  m_sc, l_sc, acc_sc)
    pad = jnp.zeros((LANES - ts, LANES), BF16)
    kn = [jnp.concatenate([kn_ref[:, cols(h)].astype(BF16), pad], axis=0) for h in heads]
    vn = [jnp.concatenate([vn_ref[:, cols(h)].astype(BF16), pad], axis=0) for h in heads]
    _softmax_step(qs, kn, vn, tabn_ref[...], m_sc, l_sc, acc_sc)
    l = l_sc[...]
    acc = acc_sc[...]
    za = za_ref[...]
    rows = lambda h: slice(h * 2 * ts, (h + 1) * 2 * ts)
    o_ref[...] = jnp.concatenate(
        [_diff_finish(lam_ref, za[:, cols(h)], sg_ref[...], l[rows(h)], acc[rows(h)], ts) for h in heads],
        axis=1).astype(BF16)


def _da_sample(lamv, qn, ck, cv, kn, vn, za, tabc, tabn, sg, ts):
    nb, past, _ = ck.shape
    kern = functools.partial(_da_sample_kernel, ts=ts)
    stack = lambda tab: jnp.concatenate([tab, tab], axis=1).reshape(DA_HEADS * 2 * ts, tab.shape[-1])
    tabc, tabn = stack(tabc), stack(tabn)
    rows = pl.BlockSpec((ts, DA_WIDTH), lambda b: (b, 0))
    cache = pl.BlockSpec((1, past, DA_WIDTH), lambda b: (b, 0, 0))
    full = lambda a: pl.BlockSpec(a.shape, lambda b: (0,) * a.ndim)
    return pl.pallas_call(
        kern,
        grid=(nb,),
        in_specs=[full(lamv), rows, cache, cache, rows, rows, rows, full(tabc), full(tabn), full(sg)],
        out_specs=rows,
        out_shape=jax.ShapeDtypeStruct((nb * ts, DA_WIDTH), BF16),
        scratch_shapes=[pltpu.VMEM((DA_HEADS * 2 * ts, LANES), F32)] * 3,
        compiler_params=_params(("arbitrary",)),
        name="diff_attn_sample",
    )(lamv, qn, ck, cv, kn, vn, za, tabc, tabn, sg)


def _bias_kernel(idx_ref, bt_ref, o_ref):
    idx = idx_ref[...]
    bt = bt_ref[...]
    bt = bt - bt[:, FAR_BUCKET:FAR_BUCKET + 1]
    onehot = jnp.where(lax.broadcasted_iota(jnp.int32, (N_BUCKETS, idx.shape[1]), 0) == idx, 1.0, 0.0).astype(BF16)
    hi = bt.astype(BF16)
    rest = bt - hi.astype(F32)
    mid = rest.astype(BF16)
    lo = (rest - mid.astype(F32)).astype(BF16)
    val = _dot(hi, onehot) + _dot(mid, onehot) + _dot(lo, onehot)
    o_ref[...] = jnp.where(idx < 0, NEG, val * LOG2E)


def _bias_tables(rel_bias, tq, ts, past):
    def buckets(rel, mask):
        return np.where(mask, _t5_bucket_np(rel), -1).astype(np.int32).reshape(-1)

    r = np.arange(tq)[:, None]
    c = np.arange(tq)[None, :]
    qpos = past + np.arange(ts)[:, None]
    kc = np.arange(past)[None, :]
    kn = past + np.arange(LANES)[None, :]
    parts = [buckets(c - r, (c // CHUNK) <= (r // CHUNK)),
             buckets(c - r - tq, np.ones((tq, tq), bool)),
             buckets(kc - qpos, (kc // CHUNK) <= (qpos // CHUNK)),
             buckets(kn - qpos, ((kn // CHUNK) <= (qpos // CHUNK)) & (kn < past + ts))]
    idx = np.concatenate(parts)
    n = idx.size
    bt = jnp.zeros((SUBLANES, N_BUCKETS), F32).at[:DA_HEADS].set(rel_bias.T)
    blk = 4096
    assert n % blk == 0
    tab = pl.pallas_call(
        _bias_kernel,
        grid=(n // blk,),
        in_specs=[pl.BlockSpec((1, blk), lambda i: (0, i)), pl.BlockSpec(bt.shape, lambda i: (0, 0))],
        out_specs=pl.BlockSpec((SUBLANES, blk), lambda i: (0, i)),
        out_shape=jax.ShapeDtypeStruct((SUBLANES, n), F32),
        compiler_params=_params(("arbitrary",)),
        name="t5_bias_tables",
    )(jnp.asarray(idx).reshape(1, n), bt)[:DA_HEADS]
    o1 = 2 * tq * tq
    o2 = o1 + ts * past
    prompt_tab = tab[:, :o1].reshape(DA_HEADS, 2, tq, tq)
    tabc = tab[:, o1:o2].reshape(DA_HEADS, ts, past)
    tabn = tab[:, o2:].reshape(DA_HEADS, ts, LANES)
    return prompt_tab, tabc, tabn


def _rwkv_kernel(prw_ref, look_ref, first_ref, zb_ref, s0_ref, mu_ref, w0_ref, a0_ref, wwa_ref,
                 kkw_ref, kaw_ref, rk_ref, lng_ref, lnb_ref, g2_ref, hm_ref, dm_ref, pm_ref, ti_ref, to_ref,
                 ob_ref, sfin_ref,
                 s_sc, om_sc, rho_sc, at_sc, kt_sc, atb_sc, ktb_sc, wb_sc, vr_sc, o_sc, *, tm):
    j = pl.program_id(1)

    @pl.when(j == 0)
    def _():
        s_sc[...] = s0_ref[0]

    g2 = g2_ref[...]

    def head_sums(t):
        return jnp.concatenate([_dot2(t[:, c * RW_STATE_LANES:(c + 1) * RW_STATE_LANES], g2)
                                for c in range(RW_STATE_GROUPS)], axis=1)

    x = prw_ref[...]
    row0 = jnp.where(j == 0, first_ref[0], look_ref[7:8, :])
    rows = lax.broadcasted_iota(jnp.int32, x.shape, 0)
    prev = jnp.where(rows == 0, row0, pltpu.roll(x, 1, axis=0))
    mix = x + (prev - x) * mu_ref[...]
    r = mix[:, 0:512]
    kr = mix[:, 512:1024]
    vr = mix[:, 1024:1536]
    la = mix[:, 1536:1664]
    lane = lax.broadcasted_iota(jnp.int32, la.shape, 1)
    la = jnp.where(lane < HEAD, jnp.tanh(la), la)
    hi = la.astype(BF16)
    lo = (la - hi.astype(F32)).astype(BF16)
    wwa = wwa_ref[...]
    lora = _dot(hi, wwa[0]) + _dot(lo, wwa[0]) + _dot(hi, wwa[1])
    wpre = w0_ref[...] + lora[:, 0:512]
    nw = -wpre
    w = -(jnp.maximum(nw, 0.0) + jnp.log(1.0 + jnp.exp(-jnp.abs(nw)))) - 0.5
    ld = -jnp.exp(w)
    a = 1.0 / (1.0 + jnp.exp(-(a0_ref[...] + lora[:, 512:1024])))
    kk = kr * kkw_ref[...]
    nrm = jnp.sqrt(head_sums(kk * kk))
    kk = kk / jnp.maximum(nrm, 1e-12)
    kka = kk * a
    kh = kr * (1.0 + (a - 1.0) * kaw_ref[...])

    def dot3(m, t):
        hi = t.astype(BF16)
        r1 = t - hi.astype(F32)
        mid = r1.astype(BF16)
        lo = (r1 - mid.astype(F32)).astype(BF16)
        return _dot(m, hi) + _dot(m, mid) + _dot(m, lo)

    incl = dot3(ti_ref[...], ld)
    tot = dot3(to_ref[...], ld)
    e_rem = jnp.exp(tot - incl)
    e_inv = jnp.exp(-incl)
    om_sc[...] = jnp.exp(incl - ld) * kk
    rho_sc[...] = jnp.exp(incl) * r
    at_sc[...] = kka * e_inv
    kt_sc[...] = kh * e_inv
    atb_sc[...] = kka * e_rem
    ktb_sc[...] = kh * e_rem
    wb_sc[...] = jnp.exp(tot)
    hm = hm_ref[...]
    vx = (jnp.broadcast_to(vr[:, None, :], (tm, RW_HEADS, RW_WIDTH)) * hm[None]).reshape(tm * RW_HEADS, RW_WIDTH)
    vr_sc[...] = _dot(vx.astype(BF16), pm_ref[...])

    nb = RW_BLOCK
    heads_per_group = RW_HEADS // RW_STATE_GROUPS
    dm = dm_ref[...]

    def head_rows(blk):
        return jnp.concatenate([jnp.broadcast_to(blk[j:j + 1], (RW_HEADS, RW_WIDTH)) * hm for j in range(nb)],
                               axis=0)

    def fold(x):
        return x[:, :RW_STATE_LANES] + x[:, RW_STATE_LANES:]

    def query_rows(b):
        base = pl.multiple_of(b * nb, nb)
        return jnp.concatenate([head_rows(om_sc[pl.ds(base, nb), :]), head_rows(rho_sc[pl.ds(base, nb), :])], axis=0)

    nblk = tm // nb

    def coefficients(b, x):
        base = pl.multiple_of(b * nb, nb)
        rhs = jnp.concatenate([at_sc[pl.ds(base, nb), :], kt_sc[pl.ds(base, nb), :],
                               jnp.zeros((LANES - 2 * nb, RW_WIDTH), F32)], axis=0)
        return _dot_nt(x.astype(BF16), rhs.astype(BF16))

    def block(b, carry):
        s, g, coef = carry
        base = pl.multiple_of(b * nb, nb)
        atb_b, ktb_b, wb_b = (ref[pl.ds(base, nb), :] for ref in (atb_sc, ktb_sc, wb_sc))
        vr_b = vr_sc[pl.ds(pl.multiple_of(b * (nb * RW_HEADS), nb * RW_HEADS), nb * RW_HEADS), :]
        bn = jnp.minimum(b + 1, nblk - 1)
        xn = query_rows(bn)
        coef_next = coefficients(bn, xn)
        rhs2 = jnp.concatenate([fold(head_rows(atb_b)), fold(head_rows(ktb_b))], axis=0).astype(BF16)
        cross = _dot_nt(fold(xn).astype(BF16), rhs2).astype(BF16)
        g_decayed = _dot_nt(fold(xn * wb_b[0:1]).astype(BF16), s.astype(BF16))

        def rows(a, j):
            return a[j * RW_HEADS:(j + 1) * RW_HEADS]

        def cf(j, lane):
            return jnp.broadcast_to(rows(coef, j)[:, lane:lane + 1], (RW_HEADS, LANES))

        sa = []
        for j in range(nb):
            acc = rows(g, j) * rows(dm, j)
            for i in range(j):
                acc = acc + cf(j, nb + i) * rows(vr_b, i)
            for i in range(j):
                acc = acc - cf(j, i) * sa[i]
            sa.append(acc)
        outs = []
        for j in range(nb):
            acc = rows(g, nb + j) * rows(dm, j)
            for i in range(j + 1):
                acc = acc + cf(nb + j, nb + i) * rows(vr_b, i) - cf(nb + j, i) * sa[i]
            outs.append(acc)
        lhs = jnp.concatenate([-t for t in sa] + [vr_b], axis=0).astype(BF16)
        g_next = g_decayed + _dot(cross, lhs)
        ds = _dot(jnp.transpose(lhs.astype(F32)).astype(BF16), rhs2)
        wb = jnp.concatenate(
            [jnp.broadcast_to(wb_b[0:1, c * RW_STATE_LANES:(c + 1) * RW_STATE_LANES], (HEAD, RW_STATE_LANES))
             for c in range(RW_STATE_GROUPS)], axis=0)
        for h in range(RW_HEADS):
            o_sc[h, pl.ds(base, nb), :] = jnp.concatenate([t[h:h + 1] for t in outs], axis=0)
        return s * wb + ds, g_next, coef_next

    s0 = s_sc[...]
    x0 = query_rows(0)
    g0 = _dot_nt(fold(x0).astype(BF16), s0.astype(BF16))
    s_sc[...] = lax.fori_loop(0, nblk, block, (s0, g0, coefficients(0, x0)))[0]

    o = jnp.concatenate([o_sc[c * heads_per_group + hh][:, c * HEAD:(c + 1) * HEAD]
                         for c in range(RW_STATE_GROUPS) for hh in range(heads_per_group)], axis=1)
    mean = head_sums(o) * (1.0 / HEAD)
    d = o - mean
    var = head_sums(d * d) * (1.0 / HEAD)
    y = d * lax.rsqrt(var + RW_GN_EPS) * lng_ref[...] + lnb_ref[...]
    bonus = head_sums(r * kh * rk_ref[...]) * vr
    ob_ref[...] = ((y + bonus) * _silu(zb_ref[...])).astype(BF16)

    @pl.when(j == pl.num_programs(1) - 1)
    def _():
        sfin_ref[0] = s_sc[...]


def _block_sum_matrices(tm):
    t = np.arange(tm)
    same = (t[:, None] // RW_BLOCK) == (t[None, :] // RW_BLOCK)
    incl = same & (t[None, :] <= t[:, None])
    return [jnp.asarray(incl.astype(np.float32), dtype=BF16), jnp.asarray(same.astype(np.float32), dtype=BF16)]


def _rwkv(prw, first, zb, s0, p, seq_len, tm):
    m = prw.shape[0]
    nseq = m // seq_len
    ntile = seq_len // tm
    kern = functools.partial(_rwkv_kernel, tm=tm)
    full = lambda a: pl.BlockSpec(a.shape, lambda b, j: (0,) * a.ndim)
    rows = lambda c: pl.BlockSpec((tm, c), lambda b, j: (b * ntile + j, 0))
    look = pl.BlockSpec((8, RW_SHIFT_COLS), lambda b, j: (jnp.maximum((b * ntile + j) * (tm // 8) - 1, 0), 0))
    consts = [p["mu"], p["w0"], p["a0"], p["wwa"], p["kkw"], p["kaw"], p["rk"], p["lng"], p["lnb"],
              p["g2"], p["hm"], p["dm"], p["pm"]] + _block_sum_matrices(tm)
    return pl.pallas_call(
        kern,
        grid=(nseq, ntile),
        in_specs=[rows(RW_SHIFT_COLS), look,
                  pl.BlockSpec((1, 1, RW_SHIFT_COLS), lambda b, j: (b, 0, 0)),
                  rows(RW_WIDTH),
                  pl.BlockSpec((1,) + RW_STATE_SHAPE, lambda b, j: (b, 0, 0))] + [full(c) for c in consts],
        out_specs=[rows(RW_WIDTH), pl.BlockSpec((1,) + RW_STATE_SHAPE, lambda b, j: (b, 0, 0))],
        out_shape=[jax.ShapeDtypeStruct((m, RW_WIDTH), BF16), jax.ShapeDtypeStruct((nseq,) + RW_STATE_SHAPE, F32)],
        scratch_shapes=([pltpu.VMEM(RW_STATE_SHAPE, F32)] + [pltpu.VMEM((tm, RW_WIDTH), F32)] * 7
                        + [pltpu.VMEM((tm * RW_HEADS, RW_STATE_SHAPE[0]), F32),
                           pltpu.VMEM((RW_HEADS, tm, RW_STATE_SHAPE[0]), F32)]),
        compiler_params=_params(("arbitrary", "arbitrary")),
        name="rwkv7",
    )(prw, prw, first, zb, s0, *consts)


def _state_to_rows(s):
    b = s.shape[0]
    per = RW_HEADS // RW_STATE_GROUPS
    return s.reshape(b, RW_STATE_GROUPS, per, HEAD, HEAD).transpose(0, 1, 3, 2, 4).reshape((b,) + RW_STATE_SHAPE)


def _rows_to_state(s):
    b = s.shape[0]
    per = RW_HEADS // RW_STATE_GROUPS
    return s.reshape(b, RW_STATE_GROUPS, HEAD, per, HEAD).transpose(0, 1, 3, 2, 4).reshape(b, RW_HEADS, HEAD, HEAD)


SKIP_LOG = -104.0
SB_RECENT_KEYS = 256


def _sb_tile(qs, k, v, u, c, mask):
    tk = k[0].shape[0]
    rows = qs[0].shape[0]
    z = jnp.concatenate([_dot_nt(q_p, k_p) for q_p, k_p in zip(qs, k)], axis=0)
    lg = -(jnp.maximum(z, 0.0) + jnp.log(1.0 + jnp.exp(-jnp.abs(z))))
    if mask is not None:
        lg = jnp.where(mask, lg, 0.0)
    hi = lg.astype(BF16)
    lo = (lg - hi.astype(F32)).astype(BF16)
    it = _dot(jnp.concatenate([hi, lo], axis=1), u)
    a = jnp.exp(z + it + jnp.concatenate([c] * (tk // LANES), axis=1))
    if mask is not None:
        a = jnp.where(mask, a, 0.0)
    a = a.astype(BF16)
    da = jnp.concatenate([_dot(a[p * rows:(p + 1) * rows], v_p) for p, v_p in enumerate(v)], axis=0)
    return da, c + jnp.broadcast_to(it[:, 0:1], c.shape)


def _sb_finish(acc, z):
    t = z.shape[0]
    lane = lax.broadcasted_iota(jnp.int32, z.shape, 1)
    return jnp.where(lane < HEAD, acc[:t], acc[t:]) * _silu(z)


def _sb_prompt_kernel(q_ref, k_ref, v_ref, z_ref, u_ref, o_ref, acc_sc, c_sc, *, tq):
    i = pl.program_id(1)
    tk = u_ref.shape[0] // 2
    r = tq // tk
    qs = _stack_halves(q_ref[...])
    u = u_ref[...]
    acc_sc[...] = jnp.zeros(acc_sc.shape, F32)
    c_sc[...] = jnp.zeros(c_sc.shape, F32)
    def tile(j, lo, hi, masked):
        n = hi - lo
        halves = lambda x: jnp.concatenate([x[lo:hi], x[tq + lo:tq + hi]], axis=0)
        off = pl.multiple_of(j * tk, tk)
        mask = None
        if masked:
            qrow = lax.broadcasted_iota(jnp.int32, (2 * n, tk), 0) % n
            col = lax.broadcasted_iota(jnp.int32, (2 * n, tk), 1)
            mask = col < qrow
        da, c = _sb_tile([halves(qs)], [k_ref[pl.ds(off, tk), :]], [v_ref[pl.ds(off, tk), :]], u,
                         halves(c_sc[...]), mask)
        for h in range(2):
            acc_sc[h * tq + lo:h * tq + hi, :] += da[h * n:(h + 1) * n]
            c_sc[h * tq + lo:h * tq + hi, :] = c[h * n:(h + 1) * n]

    for jj in reversed(range(r)):
        tile(i * r + jj, jj * tk, tq, True)

    def body(st):
        for lo in range(0, tq, tq // 2):
            hi = lo + tq // 2
            left = jnp.max(jnp.maximum(c_sc[lo:hi, :], c_sc[tq + lo:tq + hi, :]))

            @pl.when(left > SKIP_LOG)
            def _():
                tile(i * r - 1 - st[0], lo, hi, False)
        return st[0] + 1, jnp.max(c_sc[...])

    lax.while_loop(lambda st: jnp.logical_and(st[0] < i * r, st[1] > SKIP_LOG), body,
                   (jnp.int32(0), jnp.max(c_sc[...])))
    o_ref[...] = _sb_finish(acc_sc[...], z_ref[...]).astype(BF16)


def _sb_prompt(qb, kb, vb, z, u, tq):
    t = qb.shape[0]
    kern = functools.partial(_sb_prompt_kernel, tq=tq)
    rows = pl.BlockSpec((tq, LANES), lambda h, i: (i, h))
    whole = pl.BlockSpec((t, LANES), lambda h, i: (0, h))
    return pl.pallas_call(
        kern,
        grid=(SB_HEADS // 2, t // tq),
        in_specs=[rows, whole, whole, rows, pl.BlockSpec(u.shape, lambda h, i: (0, 0))],
        out_specs=rows,
        out_shape=jax.ShapeDtypeStruct((t, SB_WIDTH), BF16),
        scratch_shapes=[pltpu.VMEM((2 * tq, LANES), F32)] * 2,
        compiler_params=_params(("arbitrary", "arbitrary")),
        name="stick_breaking_prompt",
    )(qb, kb, vb, z, u)


def _sb_sample_kernel(q_ref, ck_ref, cv_ref, kn_ref, vn_ref, z_ref, u_ref, o_ref, left_ref, acc_sc, c_sc, *, ts):
    tk = u_ref.shape[0] // 2
    ntile = ck_ref.shape[1] // tk
    npair = SB_HEADS // 2
    cols = lambda p: slice(p * LANES, (p + 1) * LANES)
    qs = [_stack_halves(q_ref[:, cols(p)]) for p in range(npair)]
    u = u_ref[...]
    pad = jnp.zeros((tk - ts, LANES), BF16)
    kn = [jnp.concatenate([kn_ref[:, cols(p)], pad], axis=0) for p in range(npair)]
    vn = [jnp.concatenate([vn_ref[:, cols(p)], pad], axis=0) for p in range(npair)]
    rows = npair * 2 * ts
    qrow = lax.broadcasted_iota(jnp.int32, (rows, tk), 0) % ts
    col = lax.broadcasted_iota(jnp.int32, (rows, tk), 1)
    acc, c = _sb_tile(qs, kn, vn, u, jnp.zeros((rows, LANES), F32), col < qrow)
    acc_sc[...] = acc
    c_sc[...] = c

    def body(st):
        off = pl.multiple_of((ntile - 1 - st[0]) * tk, tk)
        kc = ck_ref[0, pl.ds(off, tk), :].astype(BF16)
        vc = cv_ref[0, pl.ds(off, tk), :].astype(BF16)
        da, c = _sb_tile(qs, [kc[:, cols(p)] for p in range(npair)], [vc[:, cols(p)] for p in range(npair)],
                         u, c_sc[...], None)
        acc_sc[...] += da
        c_sc[...] = c
        return st[0] + 1, jnp.max(c)

    lax.while_loop(lambda st: jnp.logical_and(st[0] < ntile, st[1] > SKIP_LOG), body,
                   (jnp.int32(0), jnp.max(c)))
    left_ref[0] = jnp.broadcast_to(jnp.max(c_sc[...], axis=0, keepdims=True), (SUBLANES, LANES))
    acc = acc_sc[...]
    z = z_ref[...]
    o_ref[...] = jnp.concatenate(
        [_sb_finish(acc[p * 2 * ts:(p + 1) * 2 * ts], z[:, cols(p)]) for p in range(npair)], axis=1).astype(BF16)


def _sb_sample(qb, ck, cv, kb, vb, z, u, ts):
    nb, past, _ = ck.shape
    kern = functools.partial(_sb_sample_kernel, ts=ts)
    rows = pl.BlockSpec((ts, SB_WIDTH), lambda b: (b, 0))
    cache = pl.BlockSpec((1, past, SB_WIDTH), lambda b: (b, 0, 0))
    return pl.pallas_call(
        kern,
        grid=(nb,),
        in_specs=[rows, cache, cache, rows, rows, rows, pl.BlockSpec(u.shape, lambda b: (0, 0))],
        out_specs=[rows, pl.BlockSpec((1, SUBLANES, LANES), lambda b: (b, 0, 0))],
        out_shape=[jax.ShapeDtypeStruct((nb * ts, SB_WIDTH), BF16),
                   jax.ShapeDtypeStruct((nb, SUBLANES, LANES), F32)],
        scratch_shapes=[pltpu.VMEM((SB_HEADS * ts, LANES), F32)] * 2,
        compiler_params=_params(("arbitrary",)),
        name="stick_breaking_sample",
    )(qb, ck, cv, kb, vb, z, u)


def _sb_sample_two_phase(qb, cache_k, cache_v, kb, vb, z, u, ts):
    nb, past = cache_k.shape[:2]
    dense = lambda c: c.reshape(nb, c.shape[1], SB_WIDTH)
    recent = min(past, SB_RECENT_KEYS)
    out, left = _sb_sample(qb, dense(cache_k[:, past - recent:]), dense(cache_v[:, past - recent:]), kb, vb, z, u, ts)
    if recent == past:
        return out
    return lax.cond(jnp.max(left) > SKIP_LOG,
                    lambda: _sb_sample(qb, dense(cache_k), dense(cache_v), kb, vb, z, u, ts)[0],
                    lambda: out)


def _suffix_matrix(tk):
    j = np.arange(tk)[:, None]
    s = np.arange(tk)[None, :]
    u = (j >= s).astype(np.float32)
    return jnp.asarray(np.concatenate([u, u], axis=0), dtype=BF16)


def _row_tile(m):
    for tm in (512, 256):
        if m % tm == 0:
            return tm
    return m


def kernel(x_prompt, x_sample, cache_l0_k, cache_l0_v, state_l0_shift, state_l0_wkv, cache_l1_k, cache_l1_v, rel_bias, norm_l0, w_in_l0, w_out_l0, da_q_norm, da_k_norm, da_lambda_q1, da_lambda_k1, da_lambda_q2, da_lambda_k2, da_subln, rw_mu, rw_w0, rw_w_up, rw_a0, rw_a_up, rw_k_k, rw_k_a, rw_r_k, rw_lnx_g, rw_lnx_b, norm_l1, w_in_l1, w_out_l1):
    bp, t, d = x_prompt.shape
    nb, ts, _ = x_sample.shape
    past = cache_l0_k.shape[1]
    assert bp == 1 and ts <= LANES and t % 256 == 0 and past % 256 == 0

    row = lambda v: v.reshape(1, -1).astype(F32)
    w_in0 = w_in_l0.astype(BF16)
    w_in1 = w_in_l1.astype(BF16)
    w_out0a = w_out_l0[:DA_WIDTH].astype(BF16)
    w_out0b = w_out_l0[DA_WIDTH:].astype(BF16)
    w_out1 = w_out_l1.astype(BF16)
    qg = row(jnp.tile(da_q_norm, 512 // HEAD))
    kg = row(jnp.tile(da_k_norm, 512 // HEAD))
    lamv = jnp.stack([da_lambda_q1, da_lambda_k1, da_lambda_q2, da_lambda_k2]).astype(F32)
    sg = row(da_subln)
    zeros = jnp.zeros((HEAD, RW_WIDTH), F32)
    wwa_f = jnp.concatenate([jnp.concatenate([rw_w_up, zeros], axis=1),
                             jnp.concatenate([zeros, rw_a_up], axis=1)], axis=0)
    wwa_hi = wwa_f.astype(BF16)
    wwa = jnp.stack([wwa_hi, (wwa_f - wwa_hi.astype(F32)).astype(BF16)])
    heads = np.arange(RW_HEADS)
    per_group = RW_HEADS // RW_STATE_GROUPS
    chan = np.arange(RW_WIDTH)
    out_lane = np.arange(RW_STATE_SHAPE[0])
    hm = np.equal(heads[:, None], chan[None, :] // HEAD)
    dm = np.equal(heads[:, None] // per_group, out_lane[None, :] // HEAD)
    pm = (chan[:, None] // HEAD // per_group == out_lane[None, :] // HEAD) & (chan[:, None] % HEAD == out_lane[None, :] % HEAD)
    rw = dict(mu=row(rw_mu), w0=row(rw_w0), a0=row(rw_a0), wwa=wwa, kkw=row(rw_k_k), kaw=row(rw_k_a),
              rk=row(rw_r_k), lng=row(rw_lnx_g), lnb=row(rw_lnx_b), g2=_group_matrix(RW_STATE_LANES, 1.0),
              hm=jnp.asarray(hm.astype(np.float32)),
              dm=jnp.asarray(np.tile(dm.astype(np.float32), (RW_BLOCK, 1))),
              pm=jnp.asarray(pm.astype(np.float32), dtype=BF16))
    tq_da = 256
    prompt_tab, tabc, tabn = _bias_tables(rel_bias.astype(F32), tq_da, ts, past)
    u_prompt = _suffix_matrix(2 * LANES)
    u_sample = _suffix_matrix(LANES)

    def layer0(x, first, s0, seq_len, attend):
        m = x.shape[0]
        tm = _row_tile(m)
        qn, kn, knb, v, vb, za, prw, zb = _inproj0(x, row(norm_l0), w_in0, qg, kg, tm)
        oa = attend(qn, kn, knb, v, vb, za)
        ob, sfin = _rwkv(prw, first, zb, _state_to_rows(s0), rw, seq_len, min(seq_len, 256))
        y = _outproj(x, [oa, ob], [w_out0a, w_out0b], tm)
        shift = prw.reshape(m // seq_len, seq_len, RW_SHIFT_COLS)[:, -1:]
        return y, kn, v, shift, _rows_to_state(sfin)

    def layer1(x, attend):
        m = x.shape[0]
        tm = _row_tile(m)
        qb, k, kb, v, vb, z = _inproj1(x, row(norm_l1), w_in1, tm)
        o = attend(qb, kb, vb, z)
        return _outproj(x, [o], [w_out1], tm), k, v

    xp = x_prompt.reshape(t, d)
    yp, k0p, v0p, shp, wkvp = layer0(
        xp, jnp.zeros((1, 1, RW_SHIFT_COLS), F32), jnp.zeros((1, RW_HEADS, HEAD, HEAD), F32), t,
        lambda qn, kn, knb, v, vb, za: _da_prompt(lamv, qn, knb, vb, za, prompt_tab, sg, tq_da))
    yp, k1p, v1p = layer1(yp, lambda qb, kb, vb, z: _sb_prompt(qb, kb, vb, z, u_prompt, 512))

    xs = x_sample.reshape(nb * ts, d)
    ck0 = cache_l0_k.reshape(nb, past, DA_WIDTH)
    cv0 = cache_l0_v.reshape(nb, past, DA_WIDTH)
    ys, k0s, v0s, shs, wkvs = layer0(
        xs, state_l0_shift, state_l0_wkv, ts,
        lambda qn, kn, knb, v, vb, za: _da_sample(lamv, qn, ck0, cv0, knb, vb, za, tabc, tabn, sg, ts))
    ys, k1s, v1s = layer1(ys, lambda qb, kb, vb, z: _sb_sample_two_phase(
        qb, cache_l1_k, cache_l1_v, kb, vb, z, u_sample, ts))

    return (yp.reshape(1, t, d), ys.reshape(nb, ts, d),
            k0p.reshape(1, t, DA_HEADS, LANES), v0p.reshape(1, t, DA_HEADS, LANES), shp, wkvp,
            k1p.reshape(1, t, SB_HEADS, HEAD), v1p.reshape(1, t, SB_HEADS, HEAD),
            k0s.reshape(nb, ts, DA_HEADS, LANES), v0s.reshape(nb, ts, DA_HEADS, LANES), shs, wkvs,
            k1s.reshape(nb, ts, SB_HEADS, HEAD), v1s.reshape(nb, ts, SB_HEADS, HEAD))
```

```python
import functools
import math

import numpy as np
import jax
import jax.numpy as jnp
from jax import lax
from jax.experimental import pallas as pl
from jax.experimental.pallas import tpu as pltpu

F32 = jnp.float32
BF16 = jnp.bfloat16

EPS = 1e-6
NEG = -1e30
CHUNK = 64
LANES = 128
SUBLANES = 8
HEAD = 64
DA_HEADS = 4
DA_WIDTH = 512
RW_WIDTH = 512
RW_HEADS = 8
RW_SHIFT_COLS = 3 * RW_WIDTH + 128
RW_GN_EPS = 64e-5
RW_STATE_LANES = 256
RW_STATE_GROUPS = RW_WIDTH // RW_STATE_LANES
RW_STATE_SHAPE = (RW_STATE_GROUPS * HEAD, RW_STATE_LANES)
RW_BLOCK = 8
SB_HEADS = 16
SB_WIDTH = 1024
N_BUCKETS = 32
MAX_DISTANCE = 128
LAMBDA_INIT = 0.8 - 0.6 * math.exp(-0.3 * 0)
LOG2E = math.log2(math.e)
VMEM_LIMIT = 56 * 1024 * 1024


def _params(sem):
    return pltpu.CompilerParams(dimension_semantics=sem, vmem_limit_bytes=VMEM_LIMIT)


def _dot(a, b):
    return jnp.dot(a, b, preferred_element_type=F32)


def _dot_nt(a, b):
    return lax.dot_general(a, b, (((1,), (1,)), ((), ())), preferred_element_type=F32)


def _dot2(x, w):
    hi = x.astype(BF16)
    lo = (x - hi.astype(F32)).astype(BF16)
    return _dot(hi, w) + _dot(lo, w)


def _silu(z):
    return z / (1.0 + jnp.exp(-z))


def _rms(x, g):
    return x * lax.rsqrt(jnp.mean(x * x, axis=-1, keepdims=True) + EPS) * g


def _group_matrix(n, scale):
    idx = np.arange(n) // HEAD
    return jnp.asarray((idx[:, None] == idx[None, :]).astype(np.float32) * scale, dtype=BF16)


def _inproj0_kernel(x_ref, g_ref, w_ref, qg_ref, kg_ref, avg_ref,
                    qn_ref, kn_ref, knb_ref, v_ref, vb_ref, za_ref, prw_ref, zb_ref):
    y = _dot(_rms(x_ref[...], g_ref[...]).astype(BF16), w_ref[...])

    def head_norm(t, g):
        sq = t * t
        half = avg_ref.shape[0]
        ms = jnp.concatenate([_dot2(sq[:, c:c + half], avg_ref[...]) for c in range(0, t.shape[1], half)], axis=1)
        return t * lax.rsqrt(ms + EPS) * g

    qn_ref[...] = (head_norm(y[:, 0:512], qg_ref[...]) * (HEAD ** -0.5 * LOG2E)).astype(BF16)
    kn = head_norm(y[:, 512:1024], kg_ref[...])
    knb_ref[...] = kn.astype(BF16)
    v = y[:, 1024:1536]
    vb_ref[...] = v.astype(BF16)
    for h in range(DA_HEADS):
        kn_ref[:, h, :] = kn[:, h * LANES:(h + 1) * LANES]
        v_ref[:, h, :] = v[:, h * LANES:(h + 1) * LANES]
    za_ref[...] = y[:, 1536:2048]
    prw_ref[...] = y[:, 2048:2048 + RW_SHIFT_COLS]
    zb_ref[...] = y[:, 2048 + RW_SHIFT_COLS:]


def _inproj0(x, g, w, qg, kg, tm):
    m, d = x.shape
    n = w.shape[1]
    row = lambda c: pl.BlockSpec((tm, c), lambda i: (i, 0))
    full = lambda a: pl.BlockSpec(a.shape, lambda i: (0,) * a.ndim)
    avg = _group_matrix(2 * LANES, 1.0 / HEAD)
    widths = [(512, BF16), (None, F32), (512, BF16), (None, F32), (512, BF16), (512, F32),
              (RW_SHIFT_COLS, F32), (512, F32)]
    heads = (DA_HEADS, LANES)
    return pl.pallas_call(
        _inproj0_kernel,
        grid=(m // tm,),
        in_specs=[row(d), full(g), full(w), full(qg), full(kg), full(avg)],
        out_specs=[row(c) if c else pl.BlockSpec((tm,) + heads, lambda i: (i, 0, 0)) for c, _ in widths],
        out_shape=[jax.ShapeDtypeStruct((m, c) if c else (m,) + heads, dt) for c, dt in widths],
        compiler_params=_params(("arbitrary",)),
        name="inproj0",
    )(x, g, w, qg, kg, avg)


def _inproj1_kernel(x_ref, g_ref, w_ref, qb_ref, k_ref, kb_ref, v_ref, vb_ref, z_ref):
    y = _dot(_rms(x_ref[...], g_ref[...]).astype(BF16), w_ref[...])
    qb_ref[...] = (y[:, 0:1024] * (HEAD ** -0.5)).astype(BF16)
    k = y[:, 1024:2048]
    k_ref[...] = k
    kb_ref[...] = k.astype(BF16)
    v = y[:, 2048:3072]
    v_ref[...] = v
    vb_ref[...] = v.astype(BF16)
    z_ref[...] = y[:, 3072:4096]


def _inproj1(x, g, w, tm):
    m, d = x.shape
    row = lambda c: pl.BlockSpec((tm, c), lambda i: (i, 0))
    full = lambda a: pl.BlockSpec(a.shape, lambda i: (0,) * a.ndim)
    dts = [BF16, F32, BF16, F32, BF16, F32]
    return pl.pallas_call(
        _inproj1_kernel,
        grid=(m // tm,),
        in_specs=[row(d), full(g), full(w)],
        out_specs=[row(1024) for _ in dts],
        out_shape=[jax.ShapeDtypeStruct((m, 1024), dt) for dt in dts],
        compiler_params=_params(("arbitrary",)),
        name="inproj1",
    )(x, g, w)


def _outproj_kernel(*refs):
    n = (len(refs) - 2) // 2
    x_ref, o_ref = refs[0], refs[-1]
    acc = x_ref[...]
    for a_ref, w_ref in zip(refs[1:1 + n], refs[1 + n:1 + 2 * n]):
        acc = acc + _dot(a_ref[...], w_ref[...])
    o_ref[...] = acc


def _outproj(x, acts, ws, tm):
    m, d = x.shape
    row = lambda c: pl.BlockSpec((tm, c), lambda i: (i, 0))
    full = lambda a: pl.BlockSpec(a.shape, lambda i: (0,) * a.ndim)
    return pl.pallas_call(
        _outproj_kernel,
        grid=(m // tm,),
        in_specs=[row(d)] + [row(a.shape[1]) for a in acts] + [full(w) for w in ws],
        out_specs=row(d),
        out_shape=jax.ShapeDtypeStruct((m, d), F32),
        compiler_params=_params(("arbitrary",)),
        name="outproj",
    )(x, *acts, *ws)


def _t5_bucket_np(rel):
    nb = N_BUCKETS // 2
    max_exact = nb // 2
    n = np.abs(rel)
    nf = np.maximum(n, 1).astype(np.float32)
    large = max_exact + (np.log(nf / np.float32(max_exact)) / np.float32(math.log(MAX_DISTANCE / max_exact))
                         * np.float32(nb - max_exact)).astype(np.int32)
    large = np.minimum(large, nb - 1)
    return np.where(rel > 0, nb, 0) + np.where(n < max_exact, n, large)


FAR_BUCKET = N_BUCKETS // 2 - 1
FAR_TILES = (8, 4, 2, 1)


def _stack_halves(q):
    lane = lax.broadcasted_iota(jnp.int32, q.shape, 1)
    zero = jnp.zeros_like(q)
    return jnp.concatenate([jnp.where(lane < HEAD, q, zero), jnp.where(lane >= HEAD, q, zero)], axis=0)


def _softmax_step(qs, k, v, bias, m_sc, l_sc, acc_sc):
    s = jnp.concatenate([_dot_nt(q_h, k_h) for q_h, k_h in zip(qs, k)], axis=0)
    if bias is not None:
        s = s + bias
    _softmax_update(s, v, m_sc, l_sc, acc_sc)


def _softmax_update(s, v, m_sc, l_sc, acc_sc):
    reps = s.shape[1] // LANES
    m_old = m_sc[...]
    m_new = jnp.maximum(m_old, jnp.max(s, axis=-1, keepdims=True))
    alpha = jnp.exp2(m_old - m_new)
    p = jnp.exp2(s - jnp.concatenate([m_new] * reps, axis=1))
    psum = p[:, :LANES]
    for r in range(1, reps):
        psum = psum + p[:, r * LANES:(r + 1) * LANES]
    l_sc[...] = alpha * l_sc[...] + psum
    p = p.astype(BF16)
    rows = p.shape[0] // len(v)
    pv = jnp.concatenate([_dot(p[h * rows:(h + 1) * rows], v_h) for h, v_h in enumerate(v)], axis=0)
    acc_sc[...] = alpha * acc_sc[...] + pv
    m_sc[...] = m_new


def _diff_finish(lam_ref, za, sg, l, acc, tq):
    lv = lam_ref[...]
    lam = (jnp.exp(jnp.sum(lv[0:1] * lv[1:2], axis=-1, keepdims=True))
           - jnp.exp(jnp.sum(lv[2:3] * lv[3:4], axis=-1, keepdims=True)) + LAMBDA_INIT)
    o = acc / jnp.sum(l, axis=-1, keepdims=True)
    attn = o[:tq] - lam * o[tq:]
    return (_rms(attn, sg) * (1.0 - LAMBDA_INIT)) * _silu(za)


def _da_prompt_kernel(lam_ref, q_ref, k_ref, v_ref, za_ref, tab_ref, sg_ref, o_ref,
                      m_sc, l_sc, acc_sc, *, tq, tk):
    i = pl.program_id(1)
    qs = _stack_halves(q_ref[...])
    m_sc[...] = jnp.full(m_sc.shape, NEG, F32)
    l_sc[...] = jnp.zeros(l_sc.shape, F32)
    acc_sc[...] = jnp.zeros(acc_sc.shape, F32)

    def tile(j, width, bias):
        off = pl.multiple_of(j * tk, tk)
        _softmax_step([qs], [k_ref[pl.ds(off, width), :]], [v_ref[pl.ds(off, width), :]], bias,
                      m_sc, l_sc, acc_sc)

    nfar = jnp.maximum(i - 1, 0)

    done = 0
    for width in FAR_TILES:
        def far(j, c, width=width, done=done):
            tile(done + width * j, width * tk, None)
            return c

        trips = (nfar - done) // width
        lax.fori_loop(0, trips, far, 0)
        done = done + trips * width

    @pl.when(i >= 1)
    def _():
        b = jnp.concatenate([tab_ref[0, 1], tab_ref[0, 0]], axis=1)
        tile(i - 1, 2 * tk, jnp.concatenate([b, b], axis=0))

    @pl.when(i == 0)
    def _():
        b = tab_ref[0, 0]
        tile(0, tk, jnp.concatenate([b, b], axis=0))
    o_ref[...] = _diff_finish(lam_ref, za_ref[...], sg_ref[...], l_sc[...], acc_sc[...], tq).astype(BF16)


def _da_prompt(lamv, qn, knb, vb, za, tab, sg, tq):
    t = qn.shape[0]
    tk = tq
    kern = functools.partial(_da_prompt_kernel, tq=tq, tk=tk)
    return pl.pallas_call(
        kern,
        grid=(DA_HEADS, t // tq),
        in_specs=[
            pl.BlockSpec(lamv.shape, lambda h, i: (0, 0)),
            pl.BlockSpec((tq, LANES), lambda h, i: (i, h)),
            pl.BlockSpec((t, LANES), lambda h, i: (0, h)),
            pl.BlockSpec((t, LANES), lambda h, i: (0, h)),
            pl.BlockSpec((tq, LANES), lambda h, i: (i, h)),
            pl.BlockSpec((1, 2, tq, tk), lambda h, i: (h, 0, 0, 0)),
            pl.BlockSpec(sg.shape, lambda h, i: (0, 0)),
        ],
        out_specs=pl.BlockSpec((tq, LANES), lambda h, i: (i, h)),
        out_shape=jax.ShapeDtypeStruct((t, DA_WIDTH), BF16),
        scratch_shapes=[pltpu.VMEM((2 * tq, LANES), F32)] * 3,
        compiler_params=_params(("arbitrary", "arbitrary")),
        name="diff_attn_prompt",
    )(lamv, qn, knb, vb, za, tab, sg)


def _da_sample_kernel(lam_ref, q_ref, ck_ref, cv_ref, kn_ref, vn_ref, za_ref, tabc_ref, tabn_ref, sg_ref,
                      o_ref, m_sc, l_sc, acc_sc, *, ts):
    cols = lambda h: slice(h * LANES, (h + 1) * LANES)
    heads = range(DA_HEADS)
    qs = [_stack_halves(q_ref[:, cols(h)]) for h in heads]
    m_sc[...] = jnp.full(m_sc.shape, NEG, F32)
    l_sc[...] = jnp.zeros(l_sc.shape, F32)
    acc_sc[...] = jnp.zeros(acc_sc.shape, F32)
    kc = ck_ref[0].astype(BF16)
    vc = cv_ref[0].astype(BF16)
    _softmax_step(qs, [kc[:, cols(h)] for h in heads], [vc[:, cols(h)] for h in heads], tabc_ref[...],
                  m_sc, l_sc, acc_sc)
    pad = jnp.zeros((LANES - ts, LANES), BF16)
    kn = [jnp.concatenate([kn_ref[:, cols(h)].astype(BF16), pad], axis=0) for h in heads]
    vn = [jnp.concatenate([vn_ref[:, cols(h)].astype(BF16), pad], axis=0) for h in heads]
    _softmax_step(qs, kn, vn, tabn_ref[...], m_sc, l_sc, acc_sc)
    l = l_sc[...]
    acc = acc_sc[...]
    za = za_ref[...]
    rows = lambda h: slice(h * 2 * ts, (h + 1) * 2 * ts)
    o_ref[...] = jnp.concatenate(
        [_diff_finish(lam_ref, za[:, cols(h)], sg_ref[...], l[rows(h)], acc[rows(h)], ts) for h in heads],
        axis=1).astype(BF16)


def _da_sample(lamv, qn, ck, cv, kn, vn, za, tabc, tabn, sg, ts):
    nb, past, _ = ck.shape
    kern = functools.partial(_da_sample_kernel, ts=ts)
    stack = lambda tab: jnp.concatenate([tab, tab], axis=1).reshape(DA_HEADS * 2 * ts, tab.shape[-1])
    tabc, tabn = stack(tabc), stack(tabn)
    rows = pl.BlockSpec((ts, DA_WIDTH), lambda b: (b, 0))
    cache = pl.BlockSpec((1, past, DA_WIDTH), lambda b: (b, 0, 0))
    full = lambda a: pl.BlockSpec(a.shape, lambda b: (0,) * a.ndim)
    return pl.pallas_call(
        kern,
        grid=(nb,),
        in_specs=[full(lamv), rows, cache, cache, rows, rows, rows, full(tabc), full(tabn), full(sg)],
        out_specs=rows,
        out_shape=jax.ShapeDtypeStruct((nb * ts, DA_WIDTH), BF16),
        scratch_shapes=[pltpu.VMEM((DA_HEADS * 2 * ts, LANES), F32)] * 3,
        compiler_params=_params(("arbitrary",)),
        name="diff_attn_sample",
    )(lamv, qn, ck, cv, kn, vn, za, tabc, tabn, sg)


def _bias_kernel(idx_ref, bt_ref, o_ref):
    idx = idx_ref[...]
    bt = bt_ref[...]
    bt = bt - bt[:, FAR_BUCKET:FAR_BUCKET + 1]
    onehot = jnp.where(lax.broadcasted_iota(jnp.int32, (N_BUCKETS, idx.shape[1]), 0) == idx, 1.0, 0.0).astype(BF16)
    hi = bt.astype(BF16)
    rest = bt - hi.astype(F32)
    mid = rest.astype(BF16)
    lo = (rest - mid.astype(F32)).astype(BF16)
    val = _dot(hi, onehot) + _dot(mid, onehot) + _dot(lo, onehot)
    o_ref[...] = jnp.where(idx < 0, NEG, val * LOG2E)


def _bias_tables(rel_bias, tq, ts, past):
    def buckets(rel, mask):
        return np.where(mask, _t5_bucket_np(rel), -1).astype(np.int32).reshape(-1)

    r = np.arange(tq)[:, None]
    c = np.arange(tq)[None, :]
    qpos = past + np.arange(ts)[:, None]
    kc = np.arange(past)[None, :]
    kn = past + np.arange(LANES)[None, :]
    parts = [buckets(c - r, (c // CHUNK) <= (r // CHUNK)),
             buckets(c - r - tq, np.ones((tq, tq), bool)),
             buckets(kc - qpos, (kc // CHUNK) <= (qpos // CHUNK)),
             buckets(kn - qpos, ((kn // CHUNK) <= (qpos // CHUNK)) & (kn < past + ts))]
    idx = np.concatenate(parts)
    n = idx.size
    bt = jnp.zeros((SUBLANES, N_BUCKETS), F32).at[:DA_HEADS].set(rel_bias.T)
    blk = 4096
    assert n % blk == 0
    tab = pl.pallas_call(
        _bias_kernel,
        grid=(n // blk,),
        in_specs=[pl.BlockSpec((1, blk), lambda i: (0, i)), pl.BlockSpec(bt.shape, lambda i: (0, 0))],
        out_specs=pl.BlockSpec((SUBLANES, blk), lambda i: (0, i)),
        out_shape=jax.ShapeDtypeStruct((SUBLANES, n), F32),
        compiler_params=_params(("arbitrary",)),
        name="t5_bias_tables",
    )(jnp.asarray(idx).reshape(1, n), bt)[:DA_HEADS]
    o1 = 2 * tq * tq
    o2 = o1 + ts * past
    prompt_tab = tab[:, :o1].reshape(DA_HEADS, 2, tq, tq)
    tabc = tab[:, o1:o2].reshape(DA_HEADS, ts, past)
    tabn = tab[:, o2:].reshape(DA_HEADS, ts, LANES)
    return prompt_tab, tabc, tabn


def _rwkv_kernel(prw_ref, look_ref, first_ref, zb_ref, s0_ref, mu_ref, w0_ref, a0_ref, wwa_ref,
                 kkw_ref, kaw_ref, rk_ref, lng_ref, lnb_ref, g2_ref, hm_ref, dm_ref, pm_ref, ti_ref, to_ref,
                 ob_ref, sfin_ref,
                 s_sc, om_sc, rho_sc, at_sc, kt_sc, atb_sc, ktb_sc, wb_sc, vr_sc, o_sc, *, tm):
    j = pl.program_id(1)

    @pl.when(j == 0)
    def _():
        s_sc[...] = s0_ref[0]

    g2 = g2_ref[...]

    def head_sums(t):
        tb = t.astype(BF16)
        return jnp.concatenate([_dot(tb[:, c * RW_STATE_LANES:(c + 1) * RW_STATE_LANES], g2)
                                for c in range(RW_STATE_GROUPS)], axis=1)

    x = prw_ref[...]
    row0 = jnp.where(j == 0, first_ref[0], look_ref[7:8, :])
    rows = lax.broadcasted_iota(jnp.int32, x.shape, 0)
    prev = jnp.where(rows == 0, row0, pltpu.roll(x, 1, axis=0))
    mix = x + (prev - x) * mu_ref[...]
    r = mix[:, 0:512]
    kr = mix[:, 512:1024]
    vr = mix[:, 1024:1536]
    la = mix[:, 1536:1664]
    lane = lax.broadcasted_iota(jnp.int32, la.shape, 1)
    la = jnp.where(lane < HEAD, jnp.tanh(la), la)
    hi = la.astype(BF16)
    lo = (la - hi.astype(F32)).astype(BF16)
    wwa = wwa_ref[...]
    lora = _dot(hi, wwa[0]) + _dot(lo, wwa[0]) + _dot(hi, wwa[1])
    wpre = w0_ref[...] + lora[:, 0:512]
    nw = -wpre
    w = -(jnp.maximum(nw, 0.0) + jnp.log(1.0 + jnp.exp(-jnp.abs(nw)))) - 0.5
    ld = -jnp.exp(w)
    a = 1.0 / (1.0 + jnp.exp(-(a0_ref[...] + lora[:, 512:1024])))
    kk = kr * kkw_ref[...]
    nrm = jnp.sqrt(head_sums(kk * kk))
    kk = kk / jnp.maximum(nrm, 1e-12)
    kka = kk * a
    kh = kr * (1.0 + (a - 1.0) * kaw_ref[...])

    def dot3(m, t):
        hi = t.astype(BF16)
        r1 = t - hi.astype(F32)
        mid = r1.astype(BF16)
        lo = (r1 - mid.astype(F32)).astype(BF16)
        return _dot(m, hi) + _dot(m, mid) + _dot(m, lo)

    incl = dot3(ti_ref[...], ld)
    tot = dot3(to_ref[...], ld)
    e_rem = jnp.exp(tot - incl)
    e_inv = jnp.exp(-incl)
    om_sc[...] = jnp.exp(incl - ld) * kk
    rho_sc[...] = jnp.exp(incl) * r
    at_sc[...] = kka * e_inv
    kt_sc[...] = kh * e_inv
    atb_sc[...] = kka * e_rem
    ktb_sc[...] = kh * e_rem
    wb_sc[...] = jnp.exp(tot)
    hm = hm_ref[...]
    vx = (jnp.broadcast_to(vr[:, None, :], (tm, RW_HEADS, RW_WIDTH)) * hm[None]).reshape(tm * RW_HEADS, RW_WIDTH)
    vr_sc[...] = _dot(vx.astype(BF16), pm_ref[...])

    nb = RW_BLOCK
    heads_per_group = RW_HEADS // RW_STATE_GROUPS
    dm = dm_ref[...]

    def head_rows(blk):
        return jnp.concatenate([jnp.broadcast_to(blk[j:j + 1], (RW_HEADS, RW_WIDTH)) * hm for j in range(nb)],
                               axis=0)

    def fold(x):
        return x[:, :RW_STATE_LANES] + x[:, RW_STATE_LANES:]

    def query_rows(b):
        base = pl.multiple_of(b * nb, nb)
        return jnp.concatenate([head_rows(om_sc[pl.ds(base, nb), :]), head_rows(rho_sc[pl.ds(base, nb), :])], axis=0)

    nblk = tm // nb

    def coefficients(b, x):
        base = pl.multiple_of(b * nb, nb)
        rhs = jnp.concatenate([at_sc[pl.ds(base, nb), :], kt_sc[pl.ds(base, nb), :],
                               jnp.zeros((LANES - 2 * nb, RW_WIDTH), F32)], axis=0)
        return _dot_nt(x.astype(BF16), rhs.astype(BF16))

    def block(b, carry):
        s, g, coef = carry
        base = pl.multiple_of(b * nb, nb)
        atb_b, ktb_b, wb_b = (ref[pl.ds(base, nb), :] for ref in (atb_sc, ktb_sc, wb_sc))
        vr_b = vr_sc[pl.ds(pl.multiple_of(b * (nb * RW_HEADS), nb * RW_HEADS), nb * RW_HEADS), :]
        bn = jnp.minimum(b + 1, nblk - 1)
        xn = query_rows(bn)
        coef_next = coefficients(bn, xn)
        rhs2 = jnp.concatenate([fold(head_rows(atb_b)), fold(head_rows(ktb_b))], axis=0).astype(BF16)
        cross = _dot_nt(fold(xn).astype(BF16), rhs2).astype(BF16)
        g_decayed = _dot_nt(fold(xn * wb_b[0:1]).astype(BF16), s.astype(BF16))

        def rows(a, j):
            return a[j * RW_HEADS:(j + 1) * RW_HEADS]

        def cf(j, lane):
            return jnp.broadcast_to(rows(coef, j)[:, lane:lane + 1], (RW_HEADS, LANES))

        sa = []
        for j in range(nb):
            acc = rows(g, j) * rows(dm, j)
            for i in range(j):
                acc = acc + cf(j, nb + i) * rows(vr_b, i)
            for i in range(j):
                acc = acc - cf(j, i) * sa[i]
            sa.append(acc)
        outs = []
        for j in range(nb):
            acc = rows(g, nb + j) * rows(dm, j)
            for i in range(j + 1):
                acc = acc + cf(nb + j, nb + i) * rows(vr_b, i) - cf(nb + j, i) * sa[i]
            outs.append(acc)
        lhs = jnp.concatenate([-t for t in sa] + [vr_b], axis=0).astype(BF16)
        g_next = g_decayed + _dot(cross, lhs)
        ds = _dot(jnp.transpose(lhs.astype(F32)).astype(BF16), rhs2)
        wb = jnp.concatenate(
            [jnp.broadcast_to(wb_b[0:1, c * RW_STATE_LANES:(c + 1) * RW_STATE_LANES], (HEAD, RW_STATE_LANES))
             for c in range(RW_STATE_GROUPS)], axis=0)
        for h in range(RW_HEADS):
            o_sc[h, pl.ds(base, nb), :] = jnp.concatenate([t[h:h + 1] for t in outs], axis=0)
        return s * wb + ds, g_next, coef_next

    s0 = s_sc[...]
    x0 = query_rows(0)
    g0 = _dot_nt(fold(x0).astype(BF16), s0.astype(BF16))
    s_sc[...] = lax.fori_loop(0, nblk, block, (s0, g0, coefficients(0, x0)))[0]

    o = jnp.concatenate([o_sc[c * heads_per_group + hh][:, c * HEAD:(c + 1) * HEAD]
                         for c in range(RW_STATE_GROUPS) for hh in range(heads_per_group)], axis=1)
    mean = head_sums(o) * (1.0 / HEAD)
    d = o - mean
    var = head_sums(d * d) * (1.0 / HEAD)
    y = d * lax.rsqrt(var + RW_GN_EPS) * lng_ref[...] + lnb_ref[...]
    bonus = head_sums(r * kh * rk_ref[...]) * vr
    ob_ref[...] = ((y + bonus) * _silu(zb_ref[...])).astype(BF16)

    @pl.when(j == pl.num_programs(1) - 1)
    def _():
        sfin_ref[0] = s_sc[...]


def _block_sum_matrices(tm):
    t = np.arange(tm)
    same = (t[:, None] // RW_BLOCK) == (t[None, :] // RW_BLOCK)
    incl = same & (t[None, :] <= t[:, None])
    return [jnp.asarray(incl.astype(np.float32), dtype=BF16), jnp.asarray(same.astype(np.float32), dtype=BF16)]


def _rwkv(prw, first, zb, s0, p, seq_len, tm):
    m = prw.shape[0]
    nseq = m // seq_len
    ntile = seq_len // tm
    kern = functools.partial(_rwkv_kernel, tm=tm)
    full = lambda a: pl.BlockSpec(a.shape, lambda b, j: (0,) * a.ndim)
    rows = lambda c: pl.BlockSpec((tm, c), lambda b, j: (b * ntile + j, 0))
    look = pl.BlockSpec((8, RW_SHIFT_COLS), lambda b, j: (jnp.maximum((b * ntile + j) * (tm // 8) - 1, 0), 0))
    consts = [p["mu"], p["w0"], p["a0"], p["wwa"], p["kkw"], p["kaw"], p["rk"], p["lng"], p["lnb"],
              p["g2"], p["hm"], p["dm"], p["pm"]] + _block_sum_matrices(tm)
    return pl.pallas_call(
        kern,
        grid=(nseq, ntile),
        in_specs=[rows(RW_SHIFT_COLS), look,
                  pl.BlockSpec((1, 1, RW_SHIFT_COLS), lambda b, j: (b, 0, 0)),
                  rows(RW_WIDTH),
                  pl.BlockSpec((1,) + RW_STATE_SHAPE, lambda b, j: (b, 0, 0))] + [full(c) for c in consts],
        out_specs=[rows(RW_WIDTH), pl.BlockSpec((1,) + RW_STATE_SHAPE, lambda b, j: (b, 0, 0))],
        out_shape=[jax.ShapeDtypeStruct((m, RW_WIDTH), BF16), jax.ShapeDtypeStruct((nseq,) + RW_STATE_SHAPE, F32)],
        scratch_shapes=([pltpu.VMEM(RW_STATE_SHAPE, F32)] + [pltpu.VMEM((tm, RW_WIDTH), F32)] * 7
                        + [pltpu.VMEM((tm * RW_HEADS, RW_STATE_SHAPE[0]), F32),
                           pltpu.VMEM((RW_HEADS, tm, RW_STATE_SHAPE[0]), F32)]),
        compiler_params=_params(("arbitrary", "arbitrary")),
        name="rwkv7",
    )(prw, prw, first, zb, s0, *consts)


def _state_to_rows(s):
    b = s.shape[0]
    per = RW_HEADS // RW_STATE_GROUPS
    return s.reshape(b, RW_STATE_GROUPS, per, HEAD, HEAD).transpose(0, 1, 3, 2, 4).reshape((b,) + RW_STATE_SHAPE)


def _rows_to_state(s):
    b = s.shape[0]
    per = RW_HEADS // RW_STATE_GROUPS
    return s.reshape(b, RW_STATE_GROUPS, HEAD, per, HEAD).transpose(0, 1, 3, 2, 4).reshape(b, RW_HEADS, HEAD, HEAD)


SKIP_LOG = -104.0
SB_RECENT_KEYS = 256


def _sb_tile(qs, k, v, u, c, mask):
    tk = k[0].shape[0]
    rows = qs[0].shape[0]
    z = jnp.concatenate([_dot_nt(q_p, k_p) for q_p, k_p in zip(qs, k)], axis=0)
    lg = -(jnp.maximum(z, 0.0) + jnp.log(1.0 + jnp.exp(-jnp.abs(z))))
    if mask is not None:
        lg = jnp.where(mask, lg, 0.0)
    it = _dot(lg.astype(BF16), u)
    a = jnp.exp(z + it + jnp.concatenate([c] * (tk // LANES), axis=1))
    if mask is not None:
        a = jnp.where(mask, a, 0.0)
    a = a.astype(BF16)
    da = jnp.concatenate([_dot(a[p * rows:(p + 1) * rows], v_p) for p, v_p in enumerate(v)], axis=0)
    return da, c + jnp.broadcast_to(it[:, 0:1], c.shape)


def _sb_finish(acc, z):
    t = z.shape[0]
    lane = lax.broadcasted_iota(jnp.int32, z.shape, 1)
    return jnp.where(lane < HEAD, acc[:t], acc[t:]) * _silu(z)


def _sb_prompt_kernel(q_ref, k_ref, v_ref, z_ref, u_ref, o_ref, acc_sc, c_sc, *, tq):
    i = pl.program_id(1)
    tk = u_ref.shape[0]
    r = tq // tk
    qs = _stack_halves(q_ref[...])
    u = u_ref[...]
    acc_sc[...] = jnp.zeros(acc_sc.shape, F32)
    c_sc[...] = jnp.zeros(c_sc.shape, F32)
    def tile(j, lo, hi, masked):
        n = hi - lo
        halves = lambda x: jnp.concatenate([x[lo:hi], x[tq + lo:tq + hi]], axis=0)
        off = pl.multiple_of(j * tk, tk)
        mask = None
        if masked:
            qrow = lax.broadcasted_iota(jnp.int32, (2 * n, tk), 0) % n
            col = lax.broadcasted_iota(jnp.int32, (2 * n, tk), 1)
            mask = col < qrow
        da, c = _sb_tile([halves(qs)], [k_ref[pl.ds(off, tk), :]], [v_ref[pl.ds(off, tk), :]], u,
                         halves(c_sc[...]), mask)
        for h in range(2):
            acc_sc[h * tq + lo:h * tq + hi, :] += da[h * n:(h + 1) * n]
            c_sc[h * tq + lo:h * tq + hi, :] = c[h * n:(h + 1) * n]

    for jj in reversed(range(r)):
        tile(i * r + jj, jj * tk, tq, True)

    def body(st):
        tile(i * r - 1 - st[0], 0, tq, False)
        return st[0] + 1, jnp.max(c_sc[...])

    lax.while_loop(lambda st: jnp.logical_and(st[0] < i * r, st[1] > SKIP_LOG), body,
                   (jnp.int32(0), jnp.max(c_sc[...])))
    o_ref[...] = _sb_finish(acc_sc[...], z_ref[...]).astype(BF16)


def _sb_prompt(qb, kb, vb, z, u, tq):
    t = qb.shape[0]
    kern = functools.partial(_sb_prompt_kernel, tq=tq)
    rows = pl.BlockSpec((tq, LANES), lambda h, i: (i, h))
    whole = pl.BlockSpec((t, LANES), lambda h, i: (0, h))
    return pl.pallas_call(
        kern,
        grid=(SB_HEADS // 2, t // tq),
        in_specs=[rows, whole, whole, rows, pl.BlockSpec(u.shape, lambda h, i: (0, 0))],
        out_specs=rows,
        out_shape=jax.ShapeDtypeStruct((t, SB_WIDTH), BF16),
        scratch_shapes=[pltpu.VMEM((2 * tq, LANES), F32)] * 2,
        compiler_params=_params(("arbitrary", "arbitrary")),
        name="stick_breaking_prompt",
    )(qb, kb, vb, z, u)


def _sb_sample_kernel(q_ref, ck_ref, cv_ref, kn_ref, vn_ref, z_ref, u_ref, o_ref, left_ref, acc_sc, c_sc, *, ts):
    tk = u_ref.shape[0]
    ntile = ck_ref.shape[1] // tk
    npair = SB_HEADS // 2
    cols = lambda p: slice(p * LANES, (p + 1) * LANES)
    qs = [_stack_halves(q_ref[:, cols(p)]) for p in range(npair)]
    u = u_ref[...]
    pad = jnp.zeros((tk - ts, LANES), BF16)
    kn = [jnp.concatenate([kn_ref[:, cols(p)], pad], axis=0) for p in range(npair)]
    vn = [jnp.concatenate([vn_ref[:, cols(p)], pad], axis=0) for p in range(npair)]
    rows = npair * 2 * ts
    qrow = lax.broadcasted_iota(jnp.int32, (rows, tk), 0) % ts
    col = lax.broadcasted_iota(jnp.int32, (rows, tk), 1)
    acc, c = _sb_tile(qs, kn, vn, u, jnp.zeros((rows, LANES), F32), col < qrow)
    acc_sc[...] = acc
    c_sc[...] = c

    def body(st):
        off = pl.multiple_of((ntile - 1 - st[0]) * tk, tk)
        kc = ck_ref[0, pl.ds(off, tk), :].astype(BF16)
        vc = cv_ref[0, pl.ds(off, tk), :].astype(BF16)
        da, c = _sb_tile(qs, [kc[:, cols(p)] for p in range(npair)], [vc[:, cols(p)] for p in range(npair)],
                         u, c_sc[...], None)
        acc_sc[...] += da
        c_sc[...] = c
        return st[0] + 1, jnp.max(c)

    lax.while_loop(lambda st: jnp.logical_and(st[0] < ntile, st[1] > SKIP_LOG), body,
                   (jnp.int32(0), jnp.max(c)))
    left_ref[0] = jnp.broadcast_to(jnp.max(c_sc[...], axis=0, keepdims=True), (SUBLANES, LANES))
    acc = acc_sc[...]
    z = z_ref[...]
    o_ref[...] = jnp.concatenate(
        [_sb_finish(acc[p * 2 * ts:(p + 1) * 2 * ts], z[:, cols(p)]) for p in range(npair)], axis=1).astype(BF16)


def _sb_sample(qb, ck, cv, kb, vb, z, u, ts):
    nb, past, _ = ck.shape
    kern = functools.partial(_sb_sample_kernel, ts=ts)
    rows = pl.BlockSpec((ts, SB_WIDTH), lambda b: (b, 0))
    cache = pl.BlockSpec((1, past, SB_WIDTH), lambda b: (b, 0, 0))
    return pl.pallas_call(
        kern,
        grid=(nb,),
        in_specs=[rows, cache, cache, rows, rows, rows, pl.BlockSpec(u.shape, lambda b: (0, 0))],
        out_specs=[rows, pl.BlockSpec((1, SUBLANES, LANES), lambda b: (b, 0, 0))],
        out_shape=[jax.ShapeDtypeStruct((nb * ts, SB_WIDTH), BF16),
                   jax.ShapeDtypeStruct((nb, SUBLANES, LANES), F32)],
        scratch_shapes=[pltpu.VMEM((SB_HEADS * ts, LANES), F32)] * 2,
        compiler_params=_params(("arbitrary",)),
        name="stick_breaking_sample",
    )(qb, ck, cv, kb, vb, z, u)


def _sb_sample_two_phase(qb, cache_k, cache_v, kb, vb, z, u, ts):
    nb, past = cache_k.shape[:2]
    dense = lambda c: c.reshape(nb, c.shape[1], SB_WIDTH)
    recent = min(past, SB_RECENT_KEYS)
    out, left = _sb_sample(qb, dense(cache_k[:, past - recent:]), dense(cache_v[:, past - recent:]), kb, vb, z, u, ts)
    if recent == past:
        return out
    return lax.cond(jnp.max(left) > SKIP_LOG,
                    lambda: _sb_sample(qb, dense(cache_k), dense(cache_v), kb, vb, z, u, ts)[0],
                    lambda: out)


def _suffix_matrix(tk):
    j = np.arange(tk)[:, None]
    s = np.arange(tk)[None, :]
    return jnp.asarray((j >= s).astype(np.float32), dtype=BF16)


def _row_tile(m):
    for tm in (512, 256):
        if m % tm == 0:
            return tm
    return m


def kernel(x_prompt, x_sample, cache_l0_k, cache_l0_v, state_l0_shift, state_l0_wkv, cache_l1_k, cache_l1_v, rel_bias, norm_l0, w_in_l0, w_out_l0, da_q_norm, da_k_norm, da_lambda_q1, da_lambda_k1, da_lambda_q2, da_lambda_k2, da_subln, rw_mu, rw_w0, rw_w_up, rw_a0, rw_a_up, rw_k_k, rw_k_a, rw_r_k, rw_lnx_g, rw_lnx_b, norm_l1, w_in_l1, w_out_l1):
    bp, t, d = x_prompt.shape
    nb, ts, _ = x_sample.shape
    past = cache_l0_k.shape[1]
    assert bp == 1 and ts <= LANES and t % 256 == 0 and past % 256 == 0

    row = lambda v: v.reshape(1, -1).astype(F32)
    w_in0 = w_in_l0.astype(BF16)
    w_in1 = w_in_l1.astype(BF16)
    w_out0a = w_out_l0[:DA_WIDTH].astype(BF16)
    w_out0b = w_out_l0[DA_WIDTH:].astype(BF16)
    w_out1 = w_out_l1.astype(BF16)
    qg = row(jnp.tile(da_q_norm, 512 // HEAD))
    kg = row(jnp.tile(da_k_norm, 512 // HEAD))
    lamv = jnp.stack([da_lambda_q1, da_lambda_k1, da_lambda_q2, da_lambda_k2]).astype(F32)
    sg = row(da_subln)
    zeros = jnp.zeros((HEAD, RW_WIDTH), F32)
    wwa_f = jnp.concatenate([jnp.concatenate([rw_w_up, zeros], axis=1),
                             jnp.concatenate([zeros, rw_a_up], axis=1)], axis=0)
    wwa_hi = wwa_f.astype(BF16)
    wwa = jnp.stack([wwa_hi, (wwa_f - wwa_hi.astype(F32)).astype(BF16)])
    heads = np.arange(RW_HEADS)
    per_group = RW_HEADS // RW_STATE_GROUPS
    chan = np.arange(RW_WIDTH)
    out_lane = np.arange(RW_STATE_SHAPE[0])
    hm = np.equal(heads[:, None], chan[None, :] // HEAD)
    dm = np.equal(heads[:, None] // per_group, out_lane[None, :] // HEAD)
    pm = (chan[:, None] // HEAD // per_group == out_lane[None, :] // HEAD) & (chan[:, None] % HEAD == out_lane[None, :] % HEAD)
    rw = dict(mu=row(rw_mu), w0=row(rw_w0), a0=row(rw_a0), wwa=wwa, kkw=row(rw_k_k), kaw=row(rw_k_a),
              rk=row(rw_r_k), lng=row(rw_lnx_g), lnb=row(rw_lnx_b), g2=_group_matrix(RW_STATE_LANES, 1.0),
              hm=jnp.asarray(hm.astype(np.float32)),
              dm=jnp.asarray(np.tile(dm.astype(np.float32), (RW_BLOCK, 1))),
              pm=jnp.asarray(pm.astype(np.float32), dtype=BF16))
    tq_da = 256
    prompt_tab, tabc, tabn = _bias_tables(rel_bias.astype(F32), tq_da, ts, past)
    u_prompt = _suffix_matrix(2 * LANES)
    u_sample = _suffix_matrix(LANES)

    def layer0(x, first, s0, seq_len, attend):
        m = x.shape[0]
        tm = _row_tile(m)
        qn, kn, knb, v, vb, za, prw, zb = _inproj0(x, row(norm_l0), w_in0, qg, kg, tm)
        oa = attend(qn, kn, knb, v, vb, za)
        ob, sfin = _rwkv(prw, first, zb, _state_to_rows(s0), rw, seq_len, min(seq_len, 256))
        y = _outproj(x, [oa, ob], [w_out0a, w_out0b], tm)
        shift = prw.reshape(m // seq_len, seq_len, RW_SHIFT_COLS)[:, -1:]
        return y, kn, v, shift, _rows_to_state(sfin)

    def layer1(x, attend):
        m = x.shape[0]
        tm = _row_tile(m)
        qb, k, kb, v, vb, z = _inproj1(x, row(norm_l1), w_in1, tm)
        o = attend(qb, kb, vb, z)
        return _outproj(x, [o], [w_out1], tm), k, v

    xp = x_prompt.reshape(t, d)
    yp, k0p, v0p, shp, wkvp = layer0(
        xp, jnp.zeros((1, 1, RW_SHIFT_COLS), F32), jnp.zeros((1, RW_HEADS, HEAD, HEAD), F32), t,
        lambda qn, kn, knb, v, vb, za: _da_prompt(lamv, qn, knb, vb, za, prompt_tab, sg, tq_da))
    yp, k1p, v1p = layer1(yp, lambda qb, kb, vb, z: _sb_prompt(qb, kb, vb, z, u_prompt, 512))

    xs = x_sample.reshape(nb * ts, d)
    ck0 = cache_l0_k.reshape(nb, past, DA_WIDTH)
    cv0 = cache_l0_v.reshape(nb, past, DA_WIDTH)
    ys, k0s, v0s, shs, wkvs = layer0(
        xs, state_l0_shift, state_l0_wkv, ts,
        lambda qn, kn, knb, v, vb, za: _da_sample(lamv, qn, ck0, cv0, knb, vb, za, tabc, tabn, sg, ts))
    ys, k1s, v1s = layer1(ys, lambda qb, kb, vb, z: _sb_sample_two_phase(
        qb, cache_l1_k, cache_l1_v, kb, vb, z, u_sample, ts))

    return (yp.reshape(1, t, d), ys.reshape(nb, ts, d),
            k0p.reshape(1, t, DA_HEADS, LANES), v0p.reshape(1, t, DA_HEADS, LANES), shp, wkvp,
            k1p.reshape(1, t, SB_HEADS, HEAD), v1p.reshape(1, t, SB_HEADS, HEAD),
            k0s.reshape(nb, ts, DA_HEADS, LANES), v0s.reshape(nb, ts, DA_HEADS, LANES), shs, wkvs,
            k1s.reshape(nb, ts, SB_HEADS, HEAD), v1s.reshape(nb, ts, SB_HEADS, HEAD))
```

```python
import functools
import math

import numpy as np
import jax
import jax.numpy as jnp
from jax import lax
from jax.experimental import pallas as pl
from jax.experimental.pallas import tpu as pltpu

F32 = jnp.float32
BF16 = jnp.bfloat16

EPS = 1e-6
NEG = -1e30
CHUNK = 64
LANES = 128
SUBLANES = 8
HEAD = 64
DA_HEADS = 4
DA_WIDTH = 512
RW_WIDTH = 512
RW_HEADS = 8
RW_SHIFT_COLS = 3 * RW_WIDTH + 128
RW_GN_EPS = 64e-5
RW_STATE_LANES = 256
RW_STATE_GROUPS = RW_WIDTH // RW_STATE_LANES
RW_STATE_SHAPE = (RW_STATE_GROUPS * HEAD, RW_STATE_LANES)
RW_BLOCK = 8
SB_HEADS = 16
SB_WIDTH = 1024
N_BUCKETS = 32
MAX_DISTANCE = 128
LAMBDA_INIT = 0.8 - 0.6 * math.exp(-0.3 * 0)
LOG2E = math.log2(math.e)
VMEM_LIMIT = 56 * 1024 * 1024
PROJ_TILES = (512, 256)
DA_TILE = 256
SB_TILE = 512
RW_TILE = 256


def _params(sem):
    return pltpu.CompilerParams(dimension_semantics=sem, vmem_limit_bytes=VMEM_LIMIT)


def _dot(a, b):
    return jnp.dot(a, b, preferred_element_type=F32)


def _dot_nt(a, b):
    return lax.dot_general(a, b, (((1,), (1,)), ((), ())), preferred_element_type=F32)


def _silu(z):
    return z / (1.0 + jnp.exp(-z))


def _rms(x, g):
    return x * lax.rsqrt(jnp.mean(x * x, axis=-1, keepdims=True) + EPS) * g


def _group_matrix(n, scale):
    idx = np.arange(n) // HEAD
    return jnp.asarray((idx[:, None] == idx[None, :]).astype(np.float32) * scale, dtype=BF16)


def _inproj0_kernel(x_ref, g_ref, w_ref, qg_ref, kg_ref, avg_ref,
                    qn_ref, kn_ref, knb_ref, v_ref, vb_ref, za_ref, prw_ref, zb_ref):
    y = _dot(_rms(x_ref[...], g_ref[...]).astype(BF16), w_ref[...])

    def head_norm(t, g):
        sq = (t * t).astype(BF16)
        half = avg_ref.shape[0]
        ms = jnp.concatenate([_dot(sq[:, c:c + half], avg_ref[...]) for c in range(0, t.shape[1], half)], axis=1)
        return t * lax.rsqrt(ms + EPS) * g

    qn_ref[...] = (head_norm(y[:, 0:512], qg_ref[...]) * (HEAD ** -0.5 * LOG2E)).astype(BF16)
    kn = head_norm(y[:, 512:1024], kg_ref[...])
    knb_ref[...] = kn.astype(BF16)
    v = y[:, 1024:1536]
    vb_ref[...] = v.astype(BF16)
    for h in range(DA_HEADS):
        kn_ref[:, h, :] = kn[:, h * LANES:(h + 1) * LANES]
        v_ref[:, h, :] = v[:, h * LANES:(h + 1) * LANES]
    za_ref[...] = y[:, 1536:2048]
    prw_ref[...] = y[:, 2048:2048 + RW_SHIFT_COLS]
    zb_ref[...] = y[:, 2048 + RW_SHIFT_COLS:]


def _inproj0(x, g, w, qg, kg, tm):
    m, d = x.shape
    n = w.shape[1]
    row = lambda c: pl.BlockSpec((tm, c), lambda i: (i, 0))
    full = lambda a: pl.BlockSpec(a.shape, lambda i: (0,) * a.ndim)
    avg = _group_matrix(2 * LANES, 1.0 / HEAD)
    widths = [(512, BF16), (None, F32), (512, BF16), (None, F32), (512, BF16), (512, F32),
              (RW_SHIFT_COLS, F32), (512, F32)]
    heads = (DA_HEADS, LANES)
    return pl.pallas_call(
        _inproj0_kernel,
        grid=(m // tm,),
        in_specs=[row(d), full(g), full(w), full(qg), full(kg), full(avg)],
        out_specs=[row(c) if c else pl.BlockSpec((tm,) + heads, lambda i: (i, 0, 0)) for c, _ in widths],
        out_shape=[jax.ShapeDtypeStruct((m, c) if c else (m,) + heads, dt) for c, dt in widths],
        compiler_params=_params(("arbitrary",)),
        name="inproj0",
    )(x, g, w, qg, kg, avg)


def _inproj1_kernel(x_ref, g_ref, w_ref, qb_ref, k_ref, kb_ref, v_ref, vb_ref, z_ref):
    y = _dot(_rms(x_ref[...], g_ref[...]).astype(BF16), w_ref[...])
    qb_ref[...] = (y[:, 0:1024] * (HEAD ** -0.5)).astype(BF16)
    k = y[:, 1024:2048]
    k_ref[...] = k
    kb_ref[...] = k.astype(BF16)
    v = y[:, 2048:3072]
    v_ref[...] = v
    vb_ref[...] = v.astype(BF16)
    z_ref[...] = y[:, 3072:4096]


def _inproj1(x, g, w, tm):
    m, d = x.shape
    row = lambda c: pl.BlockSpec((tm, c), lambda i: (i, 0))
    full = lambda a: pl.BlockSpec(a.shape, lambda i: (0,) * a.ndim)
    dts = [BF16, F32, BF16, F32, BF16, F32]
    return pl.pallas_call(
        _inproj1_kernel,
        grid=(m // tm,),
        in_specs=[row(d), full(g), full(w)],
        out_specs=[row(1024) for _ in dts],
        out_shape=[jax.ShapeDtypeStruct((m, 1024), dt) for dt in dts],
        compiler_params=_params(("arbitrary",)),
        name="inproj1",
    )(x, g, w)


def _outproj_kernel(*refs):
    n = (len(refs) - 2) // 2
    x_ref, o_ref = refs[0], refs[-1]
    acc = x_ref[...]
    for a_ref, w_ref in zip(refs[1:1 + n], refs[1 + n:1 + 2 * n]):
        acc = acc + _dot(a_ref[...], w_ref[...])
    o_ref[...] = acc


def _outproj(x, acts, ws, tm):
    m, d = x.shape
    row = lambda c: pl.BlockSpec((tm, c), lambda i: (i, 0))
    full = lambda a: pl.BlockSpec(a.shape, lambda i: (0,) * a.ndim)
    return pl.pallas_call(
        _outproj_kernel,
        grid=(m // tm,),
        in_specs=[row(d)] + [row(a.shape[1]) for a in acts] + [full(w) for w in ws],
        out_specs=row(d),
        out_shape=jax.ShapeDtypeStruct((m, d), F32),
        compiler_params=_params(("arbitrary",)),
        name="outproj",
    )(x, *acts, *ws)


def _t5_bucket_np(rel):
    nb = N_BUCKETS // 2
    max_exact = nb // 2
    n = np.abs(rel)
    nf = np.maximum(n, 1).astype(np.float32)
    large = max_exact + (np.log(nf / np.float32(max_exact)) / np.float32(math.log(MAX_DISTANCE / max_exact))
                         * np.float32(nb - max_exact)).astype(np.int32)
    large = np.minimum(large, nb - 1)
    return np.where(rel > 0, nb, 0) + np.where(n < max_exact, n, large)


FAR_BUCKET = N_BUCKETS // 2 - 1
FAR_TILES = (8, 4, 2, 1)


def _stack_halves(q):
    lane = lax.broadcasted_iota(jnp.int32, q.shape, 1)
    zero = jnp.zeros_like(q)
    return jnp.concatenate([jnp.where(lane < HEAD, q, zero), jnp.where(lane >= HEAD, q, zero)], axis=0)


def _softmax_step(qs, k, v, bias, m_sc, l_sc, acc_sc):
    s = jnp.concatenate([_dot_nt(q_h, k_h) for q_h, k_h in zip(qs, k)], axis=0)
    if bias is not None:
        s = s + bias
    _softmax_update(s, v, m_sc, l_sc, acc_sc)


def _softmax_update(s, v, m_sc, l_sc, acc_sc):
    reps = s.shape[1] // LANES
    m_old = m_sc[...]
    m_new = jnp.maximum(m_old, jnp.max(s, axis=-1, keepdims=True))
    alpha = jnp.exp2(m_old - m_new)
    p = jnp.exp2(s - jnp.concatenate([m_new] * reps, axis=1))
    psum = p[:, :LANES]
    for r in range(1, reps):
        psum = psum + p[:, r * LANES:(r + 1) * LANES]
    l_sc[...] = alpha * l_sc[...] + psum
    p = p.astype(BF16)
    rows = p.shape[0] // len(v)
    pv = jnp.concatenate([_dot(p[h * rows:(h + 1) * rows], v_h) for h, v_h in enumerate(v)], axis=0)
    acc_sc[...] = alpha * acc_sc[...] + pv
    m_sc[...] = m_new


def _diff_finish(lam_ref, za, sg, l, acc, tq):
    lv = lam_ref[...]
    lam = (jnp.exp(jnp.sum(lv[0:1] * lv[1:2], axis=-1, keepdims=True))
           - jnp.exp(jnp.sum(lv[2:3] * lv[3:4], axis=-1, keepdims=True)) + LAMBDA_INIT)
    o = acc / jnp.sum(l, axis=-1, keepdims=True)
    attn = o[:tq] - lam * o[tq:]
    return (_rms(attn, sg) * (1.0 - LAMBDA_INIT)) * _silu(za)


def _da_prompt_kernel(lam_ref, q_ref, k_ref, v_ref, za_ref, tab_ref, sg_ref, o_ref,
                      m_sc, l_sc, acc_sc, *, tq, tk):
    i = pl.program_id(1)
    qs = _stack_halves(q_ref[...])
    m_sc[...] = jnp.full(m_sc.shape, NEG, F32)
    l_sc[...] = jnp.zeros(l_sc.shape, F32)
    acc_sc[...] = jnp.zeros(acc_sc.shape, F32)

    def tile(j, width, bias):
        off = pl.multiple_of(j * tk, tk)
        _softmax_step([qs], [k_ref[pl.ds(off, width), :]], [v_ref[pl.ds(off, width), :]], bias,
                      m_sc, l_sc, acc_sc)

    nfar = jnp.maximum(i - 1, 0)

    done = 0
    for width in FAR_TILES:
        def far(j, c, width=width, done=done):
            tile(done + width * j, width * tk, None)
            return c

        trips = (nfar - done) // width
        lax.fori_loop(0, trips, far, 0)
        done = done + trips * width

    @pl.when(i >= 1)
    def _():
        b = jnp.concatenate([tab_ref[0, 1], tab_ref[0, 0]], axis=1)
        tile(i - 1, 2 * tk, jnp.concatenate([b, b], axis=0))

    @pl.when(i == 0)
    def _():
        b = tab_ref[0, 0]
        tile(0, tk, jnp.concatenate([b, b], axis=0))
    o_ref[...] = _diff_finish(lam_ref, za_ref[...], sg_ref[...], l_sc[...], acc_sc[...], tq).astype(BF16)


def _da_prompt(lamv, qn, knb, vb, za, tab, sg, tq):
    t = qn.shape[0]
    tk = tq
    kern = functools.partial(_da_prompt_kernel, tq=tq, tk=tk)
    return pl.pallas_call(
        kern,
        grid=(DA_HEADS, t // tq),
        in_specs=[
            pl.BlockSpec(lamv.shape, lambda h, i: (0, 0)),
            pl.BlockSpec((tq, LANES), lambda h, i: (i, h)),
            pl.BlockSpec((t, LANES), lambda h, i: (0, h)),
            pl.BlockSpec((t, LANES), lambda h, i: (0, h)),
            pl.BlockSpec((tq, LANES), lambda h, i: (i, h)),
            pl.BlockSpec((1, 2, tq, tk), lambda h, i: (h, 0, 0, 0)),
            pl.BlockSpec(sg.shape, lambda h, i: (0, 0)),
        ],
        out_specs=pl.BlockSpec((tq, LANES), lambda h, i: (i, h)),
        out_shape=jax.ShapeDtypeStruct((t, DA_WIDTH), BF16),
        scratch_shapes=[pltpu.VMEM((2 * tq, LANES), F32)] * 3,
        compiler_params=_params(("arbitrary", "arbitrary")),
        name="diff_attn_prompt",
    )(lamv, qn, knb, vb, za, tab, sg)


def _da_sample_kernel(lam_ref, q_ref, ck_ref, cv_ref, kn_ref, vn_ref, za_ref, tabc_ref, tabn_ref, sg_ref,
                      o_ref, m_sc, l_sc, acc_sc, *, ts):
    cols = lambda h: slice(h * LANES, (h + 1) * LANES)
    heads = range(DA_HEADS)
    qs = [_stack_halves(q_ref[:, cols(h)]) for h in heads]
    m_sc[...] = jnp.full(m_sc.shape, NEG, F32)
    l_sc[...] = jnp.zeros(l_sc.shape, F32)
    acc_sc[...] = jnp.zeros(acc_sc.shape, F32)
    kc = ck_ref[0].astype(BF16)
    vc = cv_ref[0].astype(BF16)
    _softmax_step(qs, [kc[:, cols(h)] for h in heads], [vc[:, cols(h)] for h in heads], tabc_ref[...],
                  m_sc, l_sc, acc_sc)
    pad = jnp.zeros((LANES - ts, LANES), BF16)
    kn = [jnp.concatenate([kn_ref[:, cols(h)].astype(BF16), pad], axis=0) for h in heads]
    vn = [jnp.concatenate([vn_ref[:, cols(h)].astype(BF16), pad], axis=0) for h in heads]
    _softmax_step(qs, kn, vn, tabn_ref[...], m_sc, l_sc, acc_sc)
    l = l_sc[...]
    acc = acc_sc[...]
    za = za_ref[...]
    rows = lambda h: slice(h * 2 * ts, (h + 1) * 2 * ts)
    o_ref[...] = jnp.concatenate(
        [_diff_finish(lam_ref, za[:, cols(h)], sg_ref[...], l[rows(h)], acc[rows(h)], ts) for h in heads],
        axis=1).astype(BF16)


def _da_sample(lamv, qn, ck, cv, kn, vn, za, tabc, tabn, sg, ts):
    nb, past, _ = ck.shape
    kern = functools.partial(_da_sample_kernel, ts=ts)
    stack = lambda tab: jnp.concatenate([tab, tab], axis=1).reshape(DA_HEADS * 2 * ts, tab.shape[-1])
    tabc, tabn = stack(tabc), stack(tabn)
    rows = pl.BlockSpec((ts, DA_WIDTH), lambda b: (b, 0))
    cache = pl.BlockSpec((1, past, DA_WIDTH), lambda b: (b, 0, 0))
    full = lambda a: pl.BlockSpec(a.shape, lambda b: (0,) * a.ndim)
    return pl.pallas_call(
        kern,
        grid=(nb,),
        in_specs=[full(lamv), rows, cache, cache, rows, rows, rows, full(tabc), full(tabn), full(sg)],
        out_specs=rows,
        out_shape=jax.ShapeDtypeStruct((nb * ts, DA_WIDTH), BF16),
        scratch_shapes=[pltpu.VMEM((DA_HEADS * 2 * ts, LANES), F32)] * 3,
        compiler_params=_params(("arbitrary",)),
        name="diff_attn_sample",
    )(lamv, qn, ck, cv, kn, vn, za, tabc, tabn, sg)


def _bias_kernel(idx_ref, bt_ref, o_ref):
    idx = idx_ref[...]
    bt = bt_ref[...]
    bt = bt - bt[:, FAR_BUCKET:FAR_BUCKET + 1]
    onehot = jnp.where(lax.broadcasted_iota(jnp.int32, (N_BUCKETS, idx.shape[1]), 0) == idx, 1.0, 0.0).astype(BF16)
    hi = bt.astype(BF16)
    rest = bt - hi.astype(F32)
    mid = rest.astype(BF16)
    lo = (rest - mid.astype(F32)).astype(BF16)
    val = _dot(hi, onehot) + _dot(mid, onehot) + _dot(lo, onehot)
    o_ref[...] = jnp.where(idx < 0, NEG, val * LOG2E)


def _bias_tables(rel_bias, tq, ts, past):
    def buckets(rel, mask):
        return np.where(mask, _t5_bucket_np(rel), -1).astype(np.int32).reshape(-1)

    r = np.arange(tq)[:, None]
    c = np.arange(tq)[None, :]
    qpos = past + np.arange(ts)[:, None]
    kc = np.arange(past)[None, :]
    kn = past + np.arange(LANES)[None, :]
    parts = [buckets(c - r, (c // CHUNK) <= (r // CHUNK)),
             buckets(c - r - tq, np.ones((tq, tq), bool)),
             buckets(kc - qpos, (kc // CHUNK) <= (qpos // CHUNK)),
             buckets(kn - qpos, ((kn // CHUNK) <= (qpos // CHUNK)) & (kn < past + ts))]
    idx = np.concatenate(parts)
    n = idx.size
    bt = jnp.zeros((SUBLANES, N_BUCKETS), F32).at[:DA_HEADS].set(rel_bias.T)
    blk = 4096
    assert n % blk == 0
    tab = pl.pallas_call(
        _bias_kernel,
        grid=(n // blk,),
        in_specs=[pl.BlockSpec((1, blk), lambda i: (0, i)), pl.BlockSpec(bt.shape, lambda i: (0, 0))],
        out_specs=pl.BlockSpec((SUBLANES, blk), lambda i: (0, i)),
        out_shape=jax.ShapeDtypeStruct((SUBLANES, n), F32),
        compiler_params=_params(("arbitrary",)),
        name="t5_bias_tables",
    )(jnp.asarray(idx).reshape(1, n), bt)[:DA_HEADS]
    o1 = 2 * tq * tq
    o2 = o1 + ts * past
    prompt_tab = tab[:, :o1].reshape(DA_HEADS, 2, tq, tq)
    tabc = tab[:, o1:o2].reshape(DA_HEADS, ts, past)
    tabn = tab[:, o2:].reshape(DA_HEADS, ts, LANES)
    return prompt_tab, tabc, tabn


def _rwkv_kernel(prw_ref, look_ref, first_ref, zb_ref, s0_ref, mu_ref, w0_ref, a0_ref, wwa_ref,
                 kkw_ref, kaw_ref, rk_ref, lng_ref, lnb_ref, g2_ref, hm_ref, dm_ref, pm_ref, ti_ref, to_ref,
                 ob_ref, sfin_ref,
                 s_sc, om_sc, rho_sc, at_sc, kt_sc, atb_sc, ktb_sc, wb_sc, vr_sc, o_sc, *, tm):
    j = pl.program_id(1)

    @pl.when(j == 0)
    def _():
        s_sc[...] = s0_ref[0]

    g2 = g2_ref[...]

    def head_sums(t):
        tb = t.astype(BF16)
        return jnp.concatenate([_dot(tb[:, c * RW_STATE_LANES:(c + 1) * RW_STATE_LANES], g2)
                                for c in range(RW_STATE_GROUPS)], axis=1)

    x = prw_ref[...]
    row0 = jnp.where(j == 0, first_ref[0], look_ref[7:8, :])
    rows = lax.broadcasted_iota(jnp.int32, x.shape, 0)
    prev = jnp.where(rows == 0, row0, pltpu.roll(x, 1, axis=0))
    mix = x + (prev - x) * mu_ref[...]
    r = mix[:, 0:512]
    kr = mix[:, 512:1024]
    vr = mix[:, 1024:1536]
    la = mix[:, 1536:1664]
    lane = lax.broadcasted_iota(jnp.int32, la.shape, 1)
    la = jnp.where(lane < HEAD, jnp.tanh(la), la)
    hi = la.astype(BF16)
    lo = (la - hi.astype(F32)).astype(BF16)
    wwa = wwa_ref[...]
    lora = _dot(hi, wwa[0]) + _dot(lo, wwa[0]) + _dot(hi, wwa[1])
    wpre = w0_ref[...] + lora[:, 0:512]
    nw = -wpre
    w = -(jnp.maximum(nw, 0.0) + jnp.log(1.0 + jnp.exp(-jnp.abs(nw)))) - 0.5
    ld = -jnp.exp(w)
    a = 1.0 / (1.0 + jnp.exp(-(a0_ref[...] + lora[:, 512:1024])))
    kk = kr * kkw_ref[...]
    nrm = jnp.sqrt(head_sums(kk * kk))
    kk = kk / jnp.maximum(nrm, 1e-12)
    kka = kk * a
    kh = kr * (1.0 + (a - 1.0) * kaw_ref[...])

    def dot3(m, t):
        hi = t.astype(BF16)
        r1 = t - hi.astype(F32)
        mid = r1.astype(BF16)
        lo = (r1 - mid.astype(F32)).astype(BF16)
        return _dot(m, hi) + _dot(m, mid) + _dot(m, lo)

    incl = dot3(ti_ref[...], ld)
    tot = dot3(to_ref[...], ld)
    e_rem = jnp.exp(tot - incl)
    e_inv = jnp.exp(-incl)
    om_sc[...] = jnp.exp(incl - ld) * kk
    rho_sc[...] = jnp.exp(incl) * r
    at_sc[...] = kka * e_inv
    kt_sc[...] = kh * e_inv
    atb_sc[...] = kka * e_rem
    ktb_sc[...] = kh * e_rem
    wb_sc[...] = jnp.exp(tot)
    hm = hm_ref[...]
    vx = (jnp.broadcast_to(vr[:, None, :], (tm, RW_HEADS, RW_WIDTH)) * hm[None]).reshape(tm * RW_HEADS, RW_WIDTH)
    vr_sc[...] = _dot(vx.astype(BF16), pm_ref[...])

    nb = RW_BLOCK
    heads_per_group = RW_HEADS // RW_STATE_GROUPS
    dm = dm_ref[...]

    def head_rows(blk):
        return jnp.concatenate([jnp.broadcast_to(blk[j:j + 1], (RW_HEADS, RW_WIDTH)) * hm for j in range(nb)],
                               axis=0)

    def fold(x):
        return x[:, :RW_STATE_LANES] + x[:, RW_STATE_LANES:]

    def query_rows(b):
        base = pl.multiple_of(b * nb, nb)
        return jnp.concatenate([head_rows(om_sc[pl.ds(base, nb), :]), head_rows(rho_sc[pl.ds(base, nb), :])], axis=0)

    nblk = tm // nb

    def coefficients(b, x):
        base = pl.multiple_of(b * nb, nb)
        rhs = jnp.concatenate([at_sc[pl.ds(base, nb), :], kt_sc[pl.ds(base, nb), :],
                               jnp.zeros((LANES - 2 * nb, RW_WIDTH), F32)], axis=0)
        return _dot_nt(x.astype(BF16), rhs.astype(BF16))

    def block(b, carry):
        s, g, coef = carry
        base = pl.multiple_of(b * nb, nb)
        atb_b, ktb_b, wb_b = (ref[pl.ds(base, nb), :] for ref in (atb_sc, ktb_sc, wb_sc))
        vr_b = vr_sc[pl.ds(pl.multiple_of(b * (nb * RW_HEADS), nb * RW_HEADS), nb * RW_HEADS), :]
        bn = jnp.minimum(b + 1, nblk - 1)
        xn = query_rows(bn)
        coef_next = coefficients(bn, xn)
        rhs2 = jnp.concatenate([fold(head_rows(atb_b)), fold(head_rows(ktb_b))], axis=0).astype(BF16)
        cross = _dot_nt(fold(xn).astype(BF16), rhs2).astype(BF16)
        g_decayed = _dot_nt(fold(xn * wb_b[0:1]).astype(BF16), s.astype(BF16))

        def rows(a, j):
            return a[j * RW_HEADS:(j + 1) * RW_HEADS]

        def cf(j, lane):
            return jnp.broadcast_to(rows(coef, j)[:, lane:lane + 1], (RW_HEADS, LANES))

        sa = []
        for j in range(nb):
            acc = rows(g, j) * rows(dm, j)
            for i in range(j):
                acc = acc + cf(j, nb + i) * rows(vr_b, i)
            for i in range(j):
                acc = acc - cf(j, i) * sa[i]
            sa.append(acc)
        outs = []
        for j in range(nb):
            acc = rows(g, nb + j) * rows(dm, j)
            for i in range(j + 1):
                acc = acc + cf(nb + j, nb + i) * rows(vr_b, i) - cf(nb + j, i) * sa[i]
            outs.append(acc)
        lhs = jnp.concatenate([-t for t in sa] + [vr_b], axis=0).astype(BF16)
        g_next = g_decayed + _dot(cross, lhs)
        ds = lax.dot_general(lhs, rhs2, (((0,), (0,)), ((), ())), preferred_element_type=F32)
        wb = jnp.concatenate(
            [jnp.broadcast_to(wb_b[0:1, c * RW_STATE_LANES:(c + 1) * RW_STATE_LANES], (HEAD, RW_STATE_LANES))
             for c in range(RW_STATE_GROUPS)], axis=0)
        for h in range(RW_HEADS):
            o_sc[h, pl.ds(base, nb), :] = jnp.concatenate([t[h:h + 1] for t in outs], axis=0)
        return s * wb + ds, g_next, coef_next

    s0 = s_sc[...]
    x0 = query_rows(0)
    g0 = _dot_nt(fold(x0).astype(BF16), s0.astype(BF16))
    s_sc[...] = lax.fori_loop(0, nblk, block, (s0, g0, coefficients(0, x0)))[0]

    o = jnp.concatenate([o_sc[c * heads_per_group + hh][:, c * HEAD:(c + 1) * HEAD]
                         for c in range(RW_STATE_GROUPS) for hh in range(heads_per_group)], axis=1)
    mean = head_sums(o) * (1.0 / HEAD)
    d = o - mean
    var = head_sums(d * d) * (1.0 / HEAD)
    y = d * lax.rsqrt(var + RW_GN_EPS) * lng_ref[...] + lnb_ref[...]
    bonus = head_sums(r * kh * rk_ref[...]) * vr
    ob_ref[...] = ((y + bonus) * _silu(zb_ref[...])).astype(BF16)

    @pl.when(j == pl.num_programs(1) - 1)
    def _():
        sfin_ref[0] = s_sc[...]


def _block_sum_matrices(tm):
    t = np.arange(tm)
    same = (t[:, None] // RW_BLOCK) == (t[None, :] // RW_BLOCK)
    incl = same & (t[None, :] <= t[:, None])
    return [jnp.asarray(incl.astype(np.float32), dtype=BF16), jnp.asarray(same.astype(np.float32), dtype=BF16)]


def _rwkv(prw, first, zb, s0, p, seq_len, tm):
    m = prw.shape[0]
    nseq = m // seq_len
    ntile = seq_len // tm
    kern = functools.partial(_rwkv_kernel, tm=tm)
    full = lambda a: pl.BlockSpec(a.shape, lambda b, j: (0,) * a.ndim)
    rows = lambda c: pl.BlockSpec((tm, c), lambda b, j: (b * ntile + j, 0))
    look = pl.BlockSpec((8, RW_SHIFT_COLS), lambda b, j: (jnp.maximum((b * ntile + j) * (tm // 8) - 1, 0), 0))
    consts = [p["mu"], p["w0"], p["a0"], p["wwa"], p["kkw"], p["kaw"], p["rk"], p["lng"], p["lnb"],
              p["g2"], p["hm"], p["dm"], p["pm"]] + _block_sum_matrices(tm)
    return pl.pallas_call(
        kern,
        grid=(nseq, ntile),
        in_specs=[rows(RW_SHIFT_COLS), look,
                  pl.BlockSpec((1, 1, RW_SHIFT_COLS), lambda b, j: (b, 0, 0)),
                  rows(RW_WIDTH),
                  pl.BlockSpec((1,) + RW_STATE_SHAPE, lambda b, j: (b, 0, 0))] + [full(c) for c in consts],
        out_specs=[rows(RW_WIDTH), pl.BlockSpec((1,) + RW_STATE_SHAPE, lambda b, j: (b, 0, 0))],
        out_shape=[jax.ShapeDtypeStruct((m, RW_WIDTH), BF16), jax.ShapeDtypeStruct((nseq,) + RW_STATE_SHAPE, F32)],
        scratch_shapes=([pltpu.VMEM(RW_STATE_SHAPE, F32)] + [pltpu.VMEM((tm, RW_WIDTH), F32)] * 7
                        + [pltpu.VMEM((tm * RW_HEADS, RW_STATE_SHAPE[0]), F32),
                           pltpu.VMEM((RW_HEADS, tm, RW_STATE_SHAPE[0]), F32)]),
        compiler_params=_params(("arbitrary", "arbitrary")),
        name="rwkv7",
    )(prw, prw, first, zb, s0, *consts)


def _state_to_rows(s):
    b = s.shape[0]
    per = RW_HEADS // RW_STATE_GROUPS
    return s.reshape(b, RW_STATE_GROUPS, per, HEAD, HEAD).transpose(0, 1, 3, 2, 4).reshape((b,) + RW_STATE_SHAPE)


def _rows_to_state(s):
    b = s.shape[0]
    per = RW_HEADS // RW_STATE_GROUPS
    return s.reshape(b, RW_STATE_GROUPS, HEAD, per, HEAD).transpose(0, 1, 3, 2, 4).reshape(b, RW_HEADS, HEAD, HEAD)


SKIP_LOG = -104.0
SB_RECENT_KEYS = 256


def _sb_tile(qs, k, v, u, c, mask):
    tk = k[0].shape[0]
    rows = qs[0].shape[0]
    z = jnp.concatenate([_dot_nt(q_p, k_p) for q_p, k_p in zip(qs, k)], axis=0)
    lg = -(jnp.maximum(z, 0.0) + jnp.log(1.0 + jnp.exp(-jnp.abs(z))))
    if mask is not None:
        lg = jnp.where(mask, lg, 0.0)
    it = _dot(lg.astype(BF16), u)
    a = jnp.exp(z + it + jnp.concatenate([c] * (tk // LANES), axis=1))
    if mask is not None:
        a = jnp.where(mask, a, 0.0)
    a = a.astype(BF16)
    da = jnp.concatenate([_dot(a[p * rows:(p + 1) * rows], v_p) for p, v_p in enumerate(v)], axis=0)
    return da, c + jnp.broadcast_to(it[:, 0:1], c.shape)


def _sb_finish(acc, z):
    t = z.shape[0]
    lane = lax.broadcasted_iota(jnp.int32, z.shape, 1)
    return jnp.where(lane < HEAD, acc[:t], acc[t:]) * _silu(z)


def _sb_prompt_kernel(q_ref, k_ref, v_ref, z_ref, u_ref, o_ref, acc_sc, c_sc, *, tq):
    i = pl.program_id(1)
    tk = u_ref.shape[0]
    r = tq // tk
    qs = _stack_halves(q_ref[...])
    u = u_ref[...]
    acc_sc[...] = jnp.zeros(acc_sc.shape, F32)
    c_sc[...] = jnp.zeros(c_sc.shape, F32)
    def tile(j, lo, hi, masked):
        n = hi - lo
        halves = lambda x: jnp.concatenate([x[lo:hi], x[tq + lo:tq + hi]], axis=0)
        off = pl.multiple_of(j * tk, tk)
        mask = None
        if masked:
            qrow = lax.broadcasted_iota(jnp.int32, (2 * n, tk), 0) % n
            col = lax.broadcasted_iota(jnp.int32, (2 * n, tk), 1)
            mask = col < qrow
        da, c = _sb_tile([halves(qs)], [k_ref[pl.ds(off, tk), :]], [v_ref[pl.ds(off, tk), :]], u,
                         halves(c_sc[...]), mask)
        for h in range(2):
            acc_sc[h * tq + lo:h * tq + hi, :] += da[h * n:(h + 1) * n]
            c_sc[h * tq + lo:h * tq + hi, :] = c[h * n:(h + 1) * n]

    for jj in reversed(range(r)):
        tile(i * r + jj, jj * tk, tq, True)

    def body(st):
        tile(i * r - 1 - st[0], 0, tq, False)
        return st[0] + 1, jnp.max(c_sc[...])

    lax.while_loop(lambda st: jnp.logical_and(st[0] < i * r, st[1] > SKIP_LOG), body,
                   (jnp.int32(0), jnp.max(c_sc[...])))
    o_ref[...] = _sb_finish(acc_sc[...], z_ref[...]).astype(BF16)


def _sb_prompt(qb, kb, vb, z, u, tq):
    t = qb.shape[0]
    kern = functools.partial(_sb_prompt_kernel, tq=tq)
    rows = pl.BlockSpec((tq, LANES), lambda h, i: (i, h))
    whole = pl.BlockSpec((t, LANES), lambda h, i: (0, h))
    return pl.pallas_call(
        kern,
        grid=(SB_HEADS // 2, t // tq),
        in_specs=[rows, whole, whole, rows, pl.BlockSpec(u.shape, lambda h, i: (0, 0))],
        out_specs=rows,
        out_shape=jax.ShapeDtypeStruct((t, SB_WIDTH), BF16),
        scratch_shapes=[pltpu.VMEM((2 * tq, LANES), F32)] * 2,
        compiler_params=_params(("arbitrary", "arbitrary")),
        name="stick_breaking_prompt",
    )(qb, kb, vb, z, u)


def _sb_sample_kernel(q_ref, ck_ref, cv_ref, kn_ref, vn_ref, z_ref, u_ref, o_ref, left_ref, acc_sc, c_sc, *, ts):
    tk = u_ref.shape[0]
    ntile = ck_ref.shape[1] // tk
    npair = SB_HEADS // 2
    cols = lambda p: slice(p * LANES, (p + 1) * LANES)
    qs = [_stack_halves(q_ref[:, cols(p)]) for p in range(npair)]
    u = u_ref[...]
    pad = jnp.zeros((tk - ts, LANES), BF16)
    kn = [jnp.concatenate([kn_ref[:, cols(p)], pad], axis=0) for p in range(npair)]
    vn = [jnp.concatenate([vn_ref[:, cols(p)], pad], axis=0) for p in range(npair)]
    rows = npair * 2 * ts
    qrow = lax.broadcasted_iota(jnp.int32, (rows, tk), 0) % ts
    col = lax.broadcasted_iota(jnp.int32, (rows, tk), 1)
    acc, c = _sb_tile(qs, kn, vn, u, jnp.zeros((rows, LANES), F32), col < qrow)
    acc_sc[...] = acc
    c_sc[...] = c

    def body(st):
        off = pl.multiple_of((ntile - 1 - st[0]) * tk, tk)
        kc = ck_ref[0, pl.ds(off, tk), :].astype(BF16)
        vc = cv_ref[0, pl.ds(off, tk), :].astype(BF16)
        da, c = _sb_tile(qs, [kc[:, cols(p)] for p in range(npair)], [vc[:, cols(p)] for p in range(npair)],
                         u, c_sc[...], None)
        acc_sc[...] += da
        c_sc[...] = c
        return st[0] + 1, jnp.max(c)

    lax.while_loop(lambda st: jnp.logical_and(st[0] < ntile, st[1] > SKIP_LOG), body,
                   (jnp.int32(0), jnp.max(c)))
    left_ref[0] = jnp.broadcast_to(jnp.max(c_sc[...], axis=0, keepdims=True), (SUBLANES, LANES))
    acc = acc_sc[...]
    z = z_ref[...]
    o_ref[...] = jnp.concatenate(
        [_sb_finish(acc[p * 2 * ts:(p + 1) * 2 * ts], z[:, cols(p)]) for p in range(npair)], axis=1).astype(BF16)


def _sb_sample(qb, ck, cv, kb, vb, z, u, ts):
    nb, past, _ = ck.shape
    kern = functools.partial(_sb_sample_kernel, ts=ts)
    rows = pl.BlockSpec((ts, SB_WIDTH), lambda b: (b, 0))
    cache = pl.BlockSpec((1, past, SB_WIDTH), lambda b: (b, 0, 0))
    return pl.pallas_call(
        kern,
        grid=(nb,),
        in_specs=[rows, cache, cache, rows, rows, rows, pl.BlockSpec(u.shape, lambda b: (0, 0))],
        out_specs=[rows, pl.BlockSpec((1, SUBLANES, LANES), lambda b: (b, 0, 0))],
        out_shape=[jax.ShapeDtypeStruct((nb * ts, SB_WIDTH), BF16),
                   jax.ShapeDtypeStruct((nb, SUBLANES, LANES), F32)],
        scratch_shapes=[pltpu.VMEM((SB_HEADS * ts, LANES), F32)] * 2,
        compiler_params=_params(("arbitrary",)),
        name="stick_breaking_sample",
    )(qb, ck, cv, kb, vb, z, u)


def _sb_sample_two_phase(qb, cache_k, cache_v, kb, vb, z, u, ts):
    nb, past = cache_k.shape[:2]
    dense = lambda c: c.reshape(nb, c.shape[1], SB_WIDTH)
    recent = min(past, SB_RECENT_KEYS)
    out, left = _sb_sample(qb, dense(cache_k[:, past - recent:]), dense(cache_v[:, past - recent:]), kb, vb, z, u, ts)
    if recent == past:
        return out
    return lax.cond(jnp.max(left) > SKIP_LOG,
                    lambda: _sb_sample(qb, dense(cache_k), dense(cache_v), kb, vb, z, u, ts)[0],
                    lambda: out)


def _suffix_matrix(tk):
    j = np.arange(tk)[:, None]
    s = np.arange(tk)[None, :]
    return jnp.asarray((j >= s).astype(np.float32), dtype=BF16)


def _row_tile(m):
    for tm in PROJ_TILES:
        if m % tm == 0:
            return tm
    return m


def kernel(x_prompt, x_sample, cache_l0_k, cache_l0_v, state_l0_shift, state_l0_wkv, cache_l1_k, cache_l1_v, rel_bias, norm_l0, w_in_l0, w_out_l0, da_q_norm, da_k_norm, da_lambda_q1, da_lambda_k1, da_lambda_q2, da_lambda_k2, da_subln, rw_mu, rw_w0, rw_w_up, rw_a0, rw_a_up, rw_k_k, rw_k_a, rw_r_k, rw_lnx_g, rw_lnx_b, norm_l1, w_in_l1, w_out_l1):
    bp, t, d = x_prompt.shape
    nb, ts, _ = x_sample.shape
    past = cache_l0_k.shape[1]
    assert bp == 1 and ts <= LANES and ts % RW_BLOCK == 0
    assert t % max(SB_TILE, DA_TILE, RW_TILE) == 0 and past % LANES == 0 and DA_TILE >= MAX_DISTANCE

    row = lambda v: v.reshape(1, -1).astype(F32)
    w_in0 = w_in_l0.astype(BF16)
    w_in1 = w_in_l1.astype(BF16)
    w_out0a = w_out_l0[:DA_WIDTH].astype(BF16)
    w_out0b = w_out_l0[DA_WIDTH:].astype(BF16)
    w_out1 = w_out_l1.astype(BF16)
    qg = row(jnp.tile(da_q_norm, 512 // HEAD))
    kg = row(jnp.tile(da_k_norm, 512 // HEAD))
    lamv = jnp.stack([da_lambda_q1, da_lambda_k1, da_lambda_q2, da_lambda_k2]).astype(F32)
    sg = row(da_subln)
    zeros = jnp.zeros((HEAD, RW_WIDTH), F32)
    wwa_f = jnp.concatenate([jnp.concatenate([rw_w_up, zeros], axis=1),
                             jnp.concatenate([zeros, rw_a_up], axis=1)], axis=0)
    wwa_hi = wwa_f.astype(BF16)
    wwa = jnp.stack([wwa_hi, (wwa_f - wwa_hi.astype(F32)).astype(BF16)])
    heads = np.arange(RW_HEADS)
    per_group = RW_HEADS // RW_STATE_GROUPS
    chan = np.arange(RW_WIDTH)
    out_lane = np.arange(RW_STATE_SHAPE[0])
    hm = np.equal(heads[:, None], chan[None, :] // HEAD)
    dm = np.equal(heads[:, None] // per_group, out_lane[None, :] // HEAD)
    pm = (chan[:, None] // HEAD // per_group == out_lane[None, :] // HEAD) & (chan[:, None] % HEAD == out_lane[None, :] % HEAD)
    rw = dict(mu=row(rw_mu), w0=row(rw_w0), a0=row(rw_a0), wwa=wwa, kkw=row(rw_k_k), kaw=row(rw_k_a),
              rk=row(rw_r_k), lng=row(rw_lnx_g), lnb=row(rw_lnx_b), g2=_group_matrix(RW_STATE_LANES, 1.0),
              hm=jnp.asarray(hm.astype(np.float32)),
              dm=jnp.asarray(np.tile(dm.astype(np.float32), (RW_BLOCK, 1))),
              pm=jnp.asarray(pm.astype(np.float32), dtype=BF16))
    prompt_tab, tabc, tabn = _bias_tables(rel_bias.astype(F32), DA_TILE, ts, past)
    u_prompt = _suffix_matrix(2 * LANES)
    u_sample = _suffix_matrix(LANES)

    def layer0(x, first, s0, seq_len, attend):
        m = x.shape[0]
        tm = _row_tile(m)
        qn, kn, knb, v, vb, za, prw, zb = _inproj0(x, row(norm_l0), w_in0, qg, kg, tm)
        oa = attend(qn, kn, knb, v, vb, za)
        ob, sfin = _rwkv(prw, first, zb, _state_to_rows(s0), rw, seq_len, min(seq_len, RW_TILE))
        y = _outproj(x, [oa, ob], [w_out0a, w_out0b], tm)
        shift = prw.reshape(m // seq_len, seq_len, RW_SHIFT_COLS)[:, -1:]
        return y, kn, v, shift, _rows_to_state(sfin)

    def layer1(x, attend):
        m = x.shape[0]
        tm = _row_tile(m)
        qb, k, kb, v, vb, z = _inproj1(x, row(norm_l1), w_in1, tm)
        o = attend(qb, kb, vb, z)
        return _outproj(x, [o], [w_out1], tm), k, v

    xp = x_prompt.reshape(t, d)
    yp, k0p, v0p, shp, wkvp = layer0(
        xp, jnp.zeros((1, 1, RW_SHIFT_COLS), F32), jnp.zeros((1, RW_HEADS, HEAD, HEAD), F32), t,
        lambda qn, kn, knb, v, vb, za: _da_prompt(lamv, qn, knb, vb, za, prompt_tab, sg, DA_TILE))
    yp, k1p, v1p = layer1(yp, lambda qb, kb, vb, z: _sb_prompt(qb, kb, vb, z, u_prompt, SB_TILE))

    xs = x_sample.reshape(nb * ts, d)
    ck0 = cache_l0_k.reshape(nb, past, DA_WIDTH)
    cv0 = cache_l0_v.reshape(nb, past, DA_WIDTH)
    ys, k0s, v0s, shs, wkvs = layer0(
        xs, state_l0_shift, state_l0_wkv, ts,
        lambda qn, kn, knb, v, vb, za: _da_sample(lamv, qn, ck0, cv0, knb, vb, za, tabc, tabn, sg, ts))
    ys, k1s, v1s = layer1(ys, lambda qb, kb, vb, z: _sb_sample_two_phase(
        qb, cache_l1_k, cache_l1_v, kb, vb, z, u_sample, ts))

    return (yp.reshape(1, t, d), ys.reshape(nb, ts, d),
            k0p.reshape(1, t, DA_HEADS, LANES), v0p.reshape(1, t, DA_HEADS, LANES), shp, wkvp,
            k1p.reshape(1, t, SB_HEADS, HEAD), v1p.reshape(1, t, SB_HEADS, HEAD),
            k0s.reshape(nb, ts, DA_HEADS, LANES), v0s.reshape(nb, ts, DA_HEADS, LANES), shs, wkvs,
            k1s.reshape(nb, ts, SB_HEADS, HEAD), v1s.reshape(nb, ts, SB_HEADS, HEAD))
```

```python
import functools
import math

import numpy as np
import jax
import jax.numpy as jnp
from jax import lax
from jax.experimental import pallas as pl
from jax.experimental.pallas import tpu as pltpu

F32 = jnp.float32
BF16 = jnp.bfloat16

EPS = 1e-6
NEG = -1e30
CHUNK = 64
LANES = 128
SUBLANES = 8
HEAD = 64
DA_HEADS = 4
DA_WIDTH = 512
RW_WIDTH = 512
RW_HEADS = 8
RW_SHIFT_COLS = 3 * RW_WIDTH + 128
RW_GN_EPS = 64e-5
RW_STATE_LANES = 256
RW_STATE_GROUPS = RW_WIDTH // RW_STATE_LANES
RW_STATE_SHAPE = (RW_STATE_GROUPS * HEAD, RW_STATE_LANES)
RW_BLOCK = 8
SB_HEADS = 16
SB_WIDTH = 1024
N_BUCKETS = 32
MAX_DISTANCE = 128
LAMBDA_INIT = 0.8 - 0.6 * math.exp(-0.3 * 0)
LOG2E = math.log2(math.e)
VMEM_LIMIT = 56 * 1024 * 1024
PROJ_TILES = (512, 256)
DA_TILE = 512
SB_TILE = 512
RW_TILE = 256


def _params(sem):
    return pltpu.CompilerParams(dimension_semantics=sem, vmem_limit_bytes=VMEM_LIMIT)


def _dot(a, b):
    return jnp.dot(a, b, preferred_element_type=F32)


def _dot_nt(a, b):
    return lax.dot_general(a, b, (((1,), (1,)), ((), ())), preferred_element_type=F32)


def _silu(z):
    return z / (1.0 + jnp.exp(-z))


def _rms(x, g):
    return x * lax.rsqrt(jnp.mean(x * x, axis=-1, keepdims=True) + EPS) * g


def _group_matrix(n, scale):
    idx = np.arange(n) // HEAD
    return jnp.asarray((idx[:, None] == idx[None, :]).astype(np.float32) * scale, dtype=BF16)


def _inproj0_kernel(x_ref, g_ref, w_ref, qg_ref, kg_ref, avg_ref,
                    qn_ref, kn_ref, knb_ref, v_ref, vb_ref, za_ref, prw_ref, zb_ref):
    y = _dot(_rms(x_ref[...], g_ref[...]).astype(BF16), w_ref[...])

    def head_norm(t, g):
        sq = (t * t).astype(BF16)
        half = avg_ref.shape[0]
        ms = jnp.concatenate([_dot(sq[:, c:c + half], avg_ref[...]) for c in range(0, t.shape[1], half)], axis=1)
        return t * lax.rsqrt(ms + EPS) * g

    qn_ref[...] = (head_norm(y[:, 0:512], qg_ref[...]) * (HEAD ** -0.5 * LOG2E)).astype(BF16)
    kn = head_norm(y[:, 512:1024], kg_ref[...])
    knb_ref[...] = kn.astype(BF16)
    v = y[:, 1024:1536]
    vb_ref[...] = v.astype(BF16)
    for h in range(DA_HEADS):
        kn_ref[:, h, :] = kn[:, h * LANES:(h + 1) * LANES]
        v_ref[:, h, :] = v[:, h * LANES:(h + 1) * LANES]
    za_ref[...] = y[:, 1536:2048]
    prw_ref[...] = y[:, 2048:2048 + RW_SHIFT_COLS]
    zb_ref[...] = y[:, 2048 + RW_SHIFT_COLS:]


def _inproj0(x, g, w, qg, kg, tm):
    m, d = x.shape
    n = w.shape[1]
    row = lambda c: pl.BlockSpec((tm, c), lambda i: (i, 0))
    full = lambda a: pl.BlockSpec(a.shape, lambda i: (0,) * a.ndim)
    avg = _group_matrix(2 * LANES, 1.0 / HEAD)
    widths = [(512, BF16), (None, F32), (512, BF16), (None, F32), (512, BF16), (512, F32),
              (RW_SHIFT_COLS, F32), (512, F32)]
    heads = (DA_HEADS, LANES)
    return pl.pallas_call(
        _inproj0_kernel,
        grid=(m // tm,),
        in_specs=[row(d), full(g), full(w), full(qg), full(kg), full(avg)],
        out_specs=[row(c) if c else pl.BlockSpec((tm,) + heads, lambda i: (i, 0, 0)) for c, _ in widths],
        out_shape=[jax.ShapeDtypeStruct((m, c) if c else (m,) + heads, dt) for c, dt in widths],
        compiler_params=_params(("arbitrary",)),
        name="inproj0",
    )(x, g, w, qg, kg, avg)


def _inproj1_kernel(x_ref, g_ref, w_ref, qb_ref, k_ref, kb_ref, v_ref, vb_ref, z_ref):
    y = _dot(_rms(x_ref[...], g_ref[...]).astype(BF16), w_ref[...])
    qb_ref[...] = (y[:, 0:1024] * (HEAD ** -0.5)).astype(BF16)
    k = y[:, 1024:2048]
    k_ref[...] = k
    kb_ref[...] = k.astype(BF16)
    v = y[:, 2048:3072]
    v_ref[...] = v
    vb_ref[...] = v.astype(BF16)
    z_ref[...] = y[:, 3072:4096]


def _inproj1(x, g, w, tm):
    m, d = x.shape
    row = lambda c: pl.BlockSpec((tm, c), lambda i: (i, 0))
    full = lambda a: pl.BlockSpec(a.shape, lambda i: (0,) * a.ndim)
    dts = [BF16, F32, BF16, F32, BF16, F32]
    return pl.pallas_call(
        _inproj1_kernel,
        grid=(m // tm,),
        in_specs=[row(d), full(g), full(w)],
        out_specs=[row(1024) for _ in dts],
        out_shape=[jax.ShapeDtypeStruct((m, 1024), dt) for dt in dts],
        compiler_params=_params(("arbitrary",)),
        name="inproj1",
    )(x, g, w)


def _outproj_kernel(*refs):
    n = (len(refs) - 2) // 2
    x_ref, o_ref = refs[0], refs[-1]
    acc = x_ref[...]
    for a_ref, w_ref in zip(refs[1:1 + n], refs[1 + n:1 + 2 * n]):
        acc = acc + _dot(a_ref[...], w_ref[...])
    o_ref[...] = acc


def _outproj(x, acts, ws, tm):
    m, d = x.shape
    row = lambda c: pl.BlockSpec((tm, c), lambda i: (i, 0))
    full = lambda a: pl.BlockSpec(a.shape, lambda i: (0,) * a.ndim)
    return pl.pallas_call(
        _outproj_kernel,
        grid=(m // tm,),
        in_specs=[row(d)] + [row(a.shape[1]) for a in acts] + [full(w) for w in ws],
        out_specs=row(d),
        out_shape=jax.ShapeDtypeStruct((m, d), F32),
        compiler_params=_params(("arbitrary",)),
        name="outproj",
    )(x, *acts, *ws)


def _t5_bucket_np(rel):
    nb = N_BUCKETS // 2
    max_exact = nb // 2
    n = np.abs(rel)
    nf = np.maximum(n, 1).astype(np.float32)
    large = max_exact + (np.log(nf / np.float32(max_exact)) / np.float32(math.log(MAX_DISTANCE / max_exact))
                         * np.float32(nb - max_exact)).astype(np.int32)
    large = np.minimum(large, nb - 1)
    return np.where(rel > 0, nb, 0) + np.where(n < max_exact, n, large)


FAR_BUCKET = N_BUCKETS // 2 - 1
FAR_TILES = (4, 2, 1)


def _stack_halves(q):
    lane = lax.broadcasted_iota(jnp.int32, q.shape, 1)
    zero = jnp.zeros_like(q)
    return jnp.concatenate([jnp.where(lane < HEAD, q, zero), jnp.where(lane >= HEAD, q, zero)], axis=0)


def _softmax_step(qs, k, v, bias, m_sc, l_sc, acc_sc):
    s = jnp.concatenate([_dot_nt(q_h, k_h) for q_h, k_h in zip(qs, k)], axis=0)
    if bias is not None:
        s = s + bias
    _softmax_update(s, v, m_sc, l_sc, acc_sc)


def _softmax_update(s, v, m_sc, l_sc, acc_sc):
    reps = s.shape[1] // LANES
    m_old = m_sc[...]
    m_new = jnp.maximum(m_old, jnp.max(s, axis=-1, keepdims=True))
    alpha = jnp.exp2(m_old - m_new)
    p = jnp.exp2(s - jnp.concatenate([m_new] * reps, axis=1))
    psum = p[:, :LANES]
    for r in range(1, reps):
        psum = psum + p[:, r * LANES:(r + 1) * LANES]
    l_sc[...] = alpha * l_sc[...] + psum
    p = p.astype(BF16)
    rows = p.shape[0] // len(v)
    pv = jnp.concatenate([_dot(p[h * rows:(h + 1) * rows], v_h) for h, v_h in enumerate(v)], axis=0)
    acc_sc[...] = alpha * acc_sc[...] + pv
    m_sc[...] = m_new


def _diff_finish(lam_ref, za, sg, l, acc, tq):
    lv = lam_ref[...]
    lam = (jnp.exp(jnp.sum(lv[0:1] * lv[1:2], axis=-1, keepdims=True))
           - jnp.exp(jnp.sum(lv[2:3] * lv[3:4], axis=-1, keepdims=True)) + LAMBDA_INIT)
    o = acc / jnp.sum(l, axis=-1, keepdims=True)
    attn = o[:tq] - lam * o[tq:]
    return (_rms(attn, sg) * (1.0 - LAMBDA_INIT)) * _silu(za)


def _da_prompt_kernel(lam_ref, q_ref, k_ref, v_ref, za_ref, tab_ref, sg_ref, o_ref,
                      m_sc, l_sc, acc_sc, *, tq, tk):
    i = pl.program_id(1)
    qs = _stack_halves(q_ref[...])
    m_sc[...] = jnp.full(m_sc.shape, NEG, F32)
    l_sc[...] = jnp.zeros(l_sc.shape, F32)
    acc_sc[...] = jnp.zeros(acc_sc.shape, F32)

    def tile(j, width, bias):
        off = pl.multiple_of(j * tk, tk)
        _softmax_step([qs], [k_ref[pl.ds(off, width), :]], [v_ref[pl.ds(off, width), :]], bias,
                      m_sc, l_sc, acc_sc)

    nfar = jnp.maximum(i - 1, 0)

    done = 0
    for width in FAR_TILES:
        def far(j, c, width=width, done=done):
            tile(done + width * j, width * tk, None)
            return c

        trips = (nfar - done) // width
        lax.fori_loop(0, trips, far, 0)
        done = done + trips * width

    @pl.when(i >= 1)
    def _():
        b = jnp.concatenate([tab_ref[0, 1], tab_ref[0, 0]], axis=1)
        tile(i - 1, 2 * tk, jnp.concatenate([b, b], axis=0))

    @pl.when(i == 0)
    def _():
        b = tab_ref[0, 0]
        tile(0, tk, jnp.concatenate([b, b], axis=0))
    o_ref[...] = _diff_finish(lam_ref, za_ref[...], sg_ref[...], l_sc[...], acc_sc[...], tq).astype(BF16)


def _da_prompt(lamv, qn, knb, vb, za, tab, sg, tq):
    t = qn.shape[0]
    tk = tq
    kern = functools.partial(_da_prompt_kernel, tq=tq, tk=tk)
    return pl.pallas_call(
        kern,
        grid=(DA_HEADS, t // tq),
        in_specs=[
            pl.BlockSpec(lamv.shape, lambda h, i: (0, 0)),
            pl.BlockSpec((tq, LANES), lambda h, i: (i, h)),
            pl.BlockSpec((t, LANES), lambda h, i: (0, h)),
            pl.BlockSpec((t, LANES), lambda h, i: (0, h)),
            pl.BlockSpec((tq, LANES), lambda h, i: (i, h)),
            pl.BlockSpec((1, 2, tq, tk), lambda h, i: (h, 0, 0, 0)),
            pl.BlockSpec(sg.shape, lambda h, i: (0, 0)),
        ],
        out_specs=pl.BlockSpec((tq, LANES), lambda h, i: (i, h)),
        out_shape=jax.ShapeDtypeStruct((t, DA_WIDTH), BF16),
        scratch_shapes=[pltpu.VMEM((2 * tq, LANES), F32)] * 3,
        compiler_params=_params(("arbitrary", "arbitrary")),
        name="diff_attn_prompt",
    )(lamv, qn, knb, vb, za, tab, sg)


def _da_sample_kernel(lam_ref, q_ref, ck_ref, cv_ref, kn_ref, vn_ref, za_ref, tabc_ref, tabn_ref, sg_ref,
                      o_ref, m_sc, l_sc, acc_sc, *, ts):
    cols = lambda h: slice(h * LANES, (h + 1) * LANES)
    heads = range(DA_HEADS)
    qs = [_stack_halves(q_ref[:, cols(h)]) for h in heads]
    m_sc[...] = jnp.full(m_sc.shape, NEG, F32)
    l_sc[...] = jnp.zeros(l_sc.shape, F32)
    acc_sc[...] = jnp.zeros(acc_sc.shape, F32)
    kc = ck_ref[0].astype(BF16)
    vc = cv_ref[0].astype(BF16)
    _softmax_step(qs, [kc[:, cols(h)] for h in heads], [vc[:, cols(h)] for h in heads], tabc_ref[...],
                  m_sc, l_sc, acc_sc)
    pad = jnp.zeros((LANES - ts, LANES), BF16)
    kn = [jnp.concatenate([kn_ref[:, cols(h)].astype(BF16), pad], axis=0) for h in heads]
    vn = [jnp.concatenate([vn_ref[:, cols(h)].astype(BF16), pad], axis=0) for h in heads]
    _softmax_step(qs, kn, vn, tabn_ref[...], m_sc, l_sc, acc_sc)
    l = l_sc[...]
    acc = acc_sc[...]
    za = za_ref[...]
    rows = lambda h: slice(h * 2 * ts, (h + 1) * 2 * ts)
    o_ref[...] = jnp.concatenate(
        [_diff_finish(lam_ref, za[:, cols(h)], sg_ref[...], l[rows(h)], acc[rows(h)], ts) for h in heads],
        axis=1).astype(BF16)


def _da_sample(lamv, qn, ck, cv, kn, vn, za, tabc, tabn, sg, ts):
    nb, past, _ = ck.shape
    kern = functools.partial(_da_sample_kernel, ts=ts)
    stack = lambda tab: jnp.concatenate([tab, tab], axis=1).reshape(DA_HEADS * 2 * ts, tab.shape[-1])
    tabc, tabn = stack(tabc), stack(tabn)
    rows = pl.BlockSpec((ts, DA_WIDTH), lambda b: (b, 0))
    cache = pl.BlockSpec((1, past, DA_WIDTH), lambda b: (b, 0, 0))
    full = lambda a: pl.BlockSpec(a.shape, lambda b: (0,) * a.ndim)
    return pl.pallas_call(
        kern,
        grid=(nb,),
        in_specs=[full(lamv), rows, cache, cache, rows, rows, rows, full(tabc), full(tabn), full(sg)],
        out_specs=rows,
        out_shape=jax.ShapeDtypeStruct((nb * ts, DA_WIDTH), BF16),
        scratch_shapes=[pltpu.VMEM((DA_HEADS * 2 * ts, LANES), F32)] * 3,
        compiler_params=_params(("arbitrary",)),
        name="diff_attn_sample",
    )(lamv, qn, ck, cv, kn, vn, za, tabc, tabn, sg)


def _bias_kernel(idx_ref, bt_ref, o_ref):
    idx = idx_ref[...]
    bt = bt_ref[...]
    bt = bt - bt[:, FAR_BUCKET:FAR_BUCKET + 1]
    onehot = jnp.where(lax.broadcasted_iota(jnp.int32, (N_BUCKETS, idx.shape[1]), 0) == idx, 1.0, 0.0).astype(BF16)
    hi = bt.astype(BF16)
    rest = bt - hi.astype(F32)
    mid = rest.astype(BF16)
    lo = (rest - mid.astype(F32)).astype(BF16)
    val = _dot(hi, onehot) + _dot(mid, onehot) + _dot(lo, onehot)
    o_ref[...] = jnp.where(idx < 0, NEG, val * LOG2E)


def _bias_tables(rel_bias, tq, ts, past):
    def buckets(rel, mask):
        return np.where(mask, _t5_bucket_np(rel), -1).astype(np.int32).reshape(-1)

    r = np.arange(tq)[:, None]
    c = np.arange(tq)[None, :]
    qpos = past + np.arange(ts)[:, None]
    kc = np.arange(past)[None, :]
    kn = past + np.arange(LANES)[None, :]
    parts = [buckets(c - r, (c // CHUNK) <= (r // CHUNK)),
             buckets(c - r - tq, np.ones((tq, tq), bool)),
             buckets(kc - qpos, (kc // CHUNK) <= (qpos // CHUNK)),
             buckets(kn - qpos, ((kn // CHUNK) <= (qpos // CHUNK)) & (kn < past + ts))]
    idx = np.concatenate(parts)
    n = idx.size
    bt = jnp.zeros((SUBLANES, N_BUCKETS), F32).at[:DA_HEADS].set(rel_bias.T)
    blk = 4096
    assert n % blk == 0
    tab = pl.pallas_call(
        _bias_kernel,
        grid=(n // blk,),
        in_specs=[pl.BlockSpec((1, blk), lambda i: (0, i)), pl.BlockSpec(bt.shape, lambda i: (0, 0))],
        out_specs=pl.BlockSpec((SUBLANES, blk), lambda i: (0, i)),
        out_shape=jax.ShapeDtypeStruct((SUBLANES, n), F32),
        compiler_params=_params(("arbitrary",)),
        name="t5_bias_tables",
    )(jnp.asarray(idx).reshape(1, n), bt)[:DA_HEADS]
    o1 = 2 * tq * tq
    o2 = o1 + ts * past
    prompt_tab = tab[:, :o1].reshape(DA_HEADS, 2, tq, tq)
    tabc = tab[:, o1:o2].reshape(DA_HEADS, ts, past)
    tabn = tab[:, o2:].reshape(DA_HEADS, ts, LANES)
    return prompt_tab, tabc, tabn


def _rwkv_kernel(prw_ref, look_ref, first_ref, zb_ref, s0_ref, mu_ref, w0_ref, a0_ref, wwa_ref,
                 kkw_ref, kaw_ref, rk_ref, lng_ref, lnb_ref, g2_ref, hm_ref, dm_ref, pm_ref, ti_ref, to_ref,
                 ob_ref, sfin_ref,
                 s_sc, om_sc, rho_sc, at_sc, kt_sc, atb_sc, ktb_sc, wb_sc, vr_sc, o_sc, *, tm):
    j = pl.program_id(1)

    @pl.when(j == 0)
    def _():
        s_sc[...] = s0_ref[0]

    g2 = g2_ref[...]

    def head_sums(t):
        tb = t.astype(BF16)
        return jnp.concatenate([_dot(tb[:, c * RW_STATE_LANES:(c + 1) * RW_STATE_LANES], g2)
                                for c in range(RW_STATE_GROUPS)], axis=1)

    x = prw_ref[...]
    row0 = jnp.where(j == 0, first_ref[0], look_ref[7:8, :])
    rows = lax.broadcasted_iota(jnp.int32, x.shape, 0)
    prev = jnp.where(rows == 0, row0, pltpu.roll(x, 1, axis=0))
    mix = x + (prev - x) * mu_ref[...]
    r = mix[:, 0:512]
    kr = mix[:, 512:1024]
    vr = mix[:, 1024:1536]
    la = mix[:, 1536:1664]
    lane = lax.broadcasted_iota(jnp.int32, la.shape, 1)
    la = jnp.where(lane < HEAD, jnp.tanh(la), la)
    hi = la.astype(BF16)
    lo = (la - hi.astype(F32)).astype(BF16)
    wwa = wwa_ref[...]
    lora = _dot(hi, wwa[0]) + _dot(lo, wwa[0]) + _dot(hi, wwa[1])
    wpre = w0_ref[...] + lora[:, 0:512]
    nw = -wpre
    w = -(jnp.maximum(nw, 0.0) + jnp.log(1.0 + jnp.exp(-jnp.abs(nw)))) - 0.5
    ld = -jnp.exp(w)
    a = 1.0 / (1.0 + jnp.exp(-(a0_ref[...] + lora[:, 512:1024])))
    kk = kr * kkw_ref[...]
    nrm = jnp.sqrt(head_sums(kk * kk))
    kk = kk / jnp.maximum(nrm, 1e-12)
    kka = kk * a
    kh = kr * (1.0 + (a - 1.0) * kaw_ref[...])

    def dot3(m, t):
        hi = t.astype(BF16)
        r1 = t - hi.astype(F32)
        mid = r1.astype(BF16)
        lo = (r1 - mid.astype(F32)).astype(BF16)
        return _dot(m, hi) + _dot(m, mid) + _dot(m, lo)

    incl = dot3(ti_ref[...], ld)
    tot = dot3(to_ref[...], ld)
    e_rem = jnp.exp(tot - incl)
    e_inv = jnp.exp(-incl)
    om_sc[...] = jnp.exp(incl - ld) * kk
    rho_sc[...] = jnp.exp(incl) * r
    at_sc[...] = kka * e_inv
    kt_sc[...] = kh * e_inv
    atb_sc[...] = kka * e_rem
    ktb_sc[...] = kh * e_rem
    wb_sc[...] = jnp.exp(tot)
    hm = hm_ref[...]
    vx = (jnp.broadcast_to(vr[:, None, :], (tm, RW_HEADS, RW_WIDTH)) * hm[None]).reshape(tm * RW_HEADS, RW_WIDTH)
    vr_sc[...] = _dot(vx.astype(BF16), pm_ref[...])

    nb = RW_BLOCK
    heads_per_group = RW_HEADS // RW_STATE_GROUPS
    dm = dm_ref[...]

    def head_rows(blk):
        return jnp.concatenate([jnp.broadcast_to(blk[j:j + 1], (RW_HEADS, RW_WIDTH)) * hm for j in range(nb)],
                               axis=0)

    def fold(x):
        return x[:, :RW_STATE_LANES] + x[:, RW_STATE_LANES:]

    def query_rows(b):
        base = pl.multiple_of(b * nb, nb)
        return jnp.concatenate([head_rows(om_sc[pl.ds(base, nb), :]), head_rows(rho_sc[pl.ds(base, nb), :])], axis=0)

    nblk = tm // nb

    def coefficients(b, x):
        base = pl.multiple_of(b * nb, nb)
        rhs = jnp.concatenate([at_sc[pl.ds(base, nb), :], kt_sc[pl.ds(base, nb), :],
                               jnp.zeros((LANES - 2 * nb, RW_WIDTH), F32)], axis=0)
        return _dot_nt(x.astype(BF16), rhs.astype(BF16))

    def block(b, carry):
        s, g, coef = carry
        base = pl.multiple_of(b * nb, nb)
        atb_b, ktb_b, wb_b = (ref[pl.ds(base, nb), :] for ref in (atb_sc, ktb_sc, wb_sc))
        vr_b = vr_sc[pl.ds(pl.multiple_of(b * (nb * RW_HEADS), nb * RW_HEADS), nb * RW_HEADS), :]
        bn = jnp.minimum(b + 1, nblk - 1)
        xn = query_rows(bn)
        coef_next = coefficients(bn, xn)
        rhs2 = jnp.concatenate([fold(head_rows(atb_b)), fold(head_rows(ktb_b))], axis=0).astype(BF16)
        cross = _dot_nt(fold(xn).astype(BF16), rhs2).astype(BF16)
        g_decayed = _dot_nt(fold(xn * wb_b[0:1]).astype(BF16), s.astype(BF16))

        def rows(a, j):
            return a[j * RW_HEADS:(j + 1) * RW_HEADS]

        def cf(j, lane):
            return jnp.broadcast_to(rows(coef, j)[:, lane:lane + 1], (RW_HEADS, LANES))

        sa = []
        for j in range(nb):
            acc = rows(g, j) * rows(dm, j)
            for i in range(j):
                acc = acc + cf(j, nb + i) * rows(vr_b, i)
            for i in range(j):
                acc = acc - cf(j, i) * sa[i]
            sa.append(acc)
        outs = []
        for j in range(nb):
            acc = rows(g, nb + j) * rows(dm, j)
            for i in range(j + 1):
                acc = acc + cf(nb + j, nb + i) * rows(vr_b, i) - cf(nb + j, i) * sa[i]
            outs.append(acc)
        lhs = jnp.concatenate([-t for t in sa] + [vr_b], axis=0).astype(BF16)
        g_next = g_decayed + _dot(cross, lhs)
        ds = lax.dot_general(lhs, rhs2, (((0,), (0,)), ((), ())), preferred_element_type=F32)
        wb = jnp.concatenate(
            [jnp.broadcast_to(wb_b[0:1, c * RW_STATE_LANES:(c + 1) * RW_STATE_LANES], (HEAD, RW_STATE_LANES))
             for c in range(RW_STATE_GROUPS)], axis=0)
        for h in range(RW_HEADS):
            o_sc[h, pl.ds(base, nb), :] = jnp.concatenate([t[h:h + 1] for t in outs], axis=0)
        return s * wb + ds, g_next, coef_next

    s0 = s_sc[...]
    x0 = query_rows(0)
    g0 = _dot_nt(fold(x0).astype(BF16), s0.astype(BF16))
    s_sc[...] = lax.fori_loop(0, nblk, block, (s0, g0, coefficients(0, x0)))[0]

    o = jnp.concatenate([o_sc[c * heads_per_group + hh][:, c * HEAD:(c + 1) * HEAD]
                         for c in range(RW_STATE_GROUPS) for hh in range(heads_per_group)], axis=1)
    mean = head_sums(o) * (1.0 / HEAD)
    d = o - mean
    var = head_sums(d * d) * (1.0 / HEAD)
    y = d * lax.rsqrt(var + RW_GN_EPS) * lng_ref[...] + lnb_ref[...]
    bonus = head_sums(r * kh * rk_ref[...]) * vr
    ob_ref[...] = ((y + bonus) * _silu(zb_ref[...])).astype(BF16)

    @pl.when(j == pl.num_programs(1) - 1)
    def _():
        sfin_ref[0] = s_sc[...]


def _block_sum_matrices(tm):
    t = np.arange(tm)
    same = (t[:, None] // RW_BLOCK) == (t[None, :] // RW_BLOCK)
    incl = same & (t[None, :] <= t[:, None])
    return [jnp.asarray(incl.astype(np.float32), dtype=BF16), jnp.asarray(same.astype(np.float32), dtype=BF16)]


def _rwkv(prw, first, zb, s0, p, seq_len, tm):
    m = prw.shape[0]
    nseq = m // seq_len
    ntile = seq_len // tm
    kern = functools.partial(_rwkv_kernel, tm=tm)
    full = lambda a: pl.BlockSpec(a.shape, lambda b, j: (0,) * a.ndim)
    rows = lambda c: pl.BlockSpec((tm, c), lambda b, j: (b * ntile + j, 0))
    look = pl.BlockSpec((8, RW_SHIFT_COLS), lambda b, j: (jnp.maximum((b * ntile + j) * (tm // 8) - 1, 0), 0))
    consts = [p["mu"], p["w0"], p["a0"], p["wwa"], p["kkw"], p["kaw"], p["rk"], p["lng"], p["lnb"],
              p["g2"], p["hm"], p["dm"], p["pm"]] + _block_sum_matrices(tm)
    return pl.pallas_call(
        kern,
        grid=(nseq, ntile),
        in_specs=[rows(RW_SHIFT_COLS), look,
                  pl.BlockSpec((1, 1, RW_SHIFT_COLS), lambda b, j: (b, 0, 0)),
                  rows(RW_WIDTH),
                  pl.BlockSpec((1,) + RW_STATE_SHAPE, lambda b, j: (b, 0, 0))] + [full(c) for c in consts],
        out_specs=[rows(RW_WIDTH), pl.BlockSpec((1,) + RW_STATE_SHAPE, lambda b, j: (b, 0, 0))],
        out_shape=[jax.ShapeDtypeStruct((m, RW_WIDTH), BF16), jax.ShapeDtypeStruct((nseq,) + RW_STATE_SHAPE, F32)],
        scratch_shapes=([pltpu.VMEM(RW_STATE_SHAPE, F32)] + [pltpu.VMEM((tm, RW_WIDTH), F32)] * 7
                        + [pltpu.VMEM((tm * RW_HEADS, RW_STATE_SHAPE[0]), F32),
                           pltpu.VMEM((RW_HEADS, tm, RW_STATE_SHAPE[0]), F32)]),
        compiler_params=_params(("arbitrary", "arbitrary")),
        name="rwkv7",
    )(prw, prw, first, zb, s0, *consts)


def _state_to_rows(s):
    b = s.shape[0]
    per = RW_HEADS // RW_STATE_GROUPS
    return s.reshape(b, RW_STATE_GROUPS, per, HEAD, HEAD).transpose(0, 1, 3, 2, 4).reshape((b,) + RW_STATE_SHAPE)


def _rows_to_state(s):
    b = s.shape[0]
    per = RW_HEADS // RW_STATE_GROUPS
    return s.reshape(b, RW_STATE_GROUPS, HEAD, per, HEAD).transpose(0, 1, 3, 2, 4).reshape(b, RW_HEADS, HEAD, HEAD)


SKIP_LOG = -104.0
SB_RECENT_KEYS = 256


def _sb_tile(qs, k, v, u, c, mask):
    tk = k[0].shape[0]
    rows = qs[0].shape[0]
    z = jnp.concatenate([_dot_nt(q_p, k_p) for q_p, k_p in zip(qs, k)], axis=0)
    lg = -(jnp.maximum(z, 0.0) + jnp.log(1.0 + jnp.exp(-jnp.abs(z))))
    if mask is not None:
        lg = jnp.where(mask, lg, 0.0)
    it = _dot(lg.astype(BF16), u)
    a = jnp.exp(z + it + jnp.concatenate([c] * (tk // LANES), axis=1))
    if mask is not None:
        a = jnp.where(mask, a, 0.0)
    a = a.astype(BF16)
    da = jnp.concatenate([_dot(a[p * rows:(p + 1) * rows], v_p) for p, v_p in enumerate(v)], axis=0)
    return da, c + jnp.broadcast_to(it[:, 0:1], c.shape)


def _sb_finish(acc, z):
    t = z.shape[0]
    lane = lax.broadcasted_iota(jnp.int32, z.shape, 1)
    return jnp.where(lane < HEAD, acc[:t], acc[t:]) * _silu(z)


def _sb_prompt_kernel(q_ref, k_ref, v_ref, z_ref, u_ref, o_ref, acc_sc, c_sc, *, tq):
    i = pl.program_id(1)
    tk = u_ref.shape[0]
    r = tq // tk
    qs = _stack_halves(q_ref[...])
    u = u_ref[...]
    acc_sc[...] = jnp.zeros(acc_sc.shape, F32)
    c_sc[...] = jnp.zeros(c_sc.shape, F32)
    def tile(j, lo, hi, masked):
        n = hi - lo
        halves = lambda x: jnp.concatenate([x[lo:hi], x[tq + lo:tq + hi]], axis=0)
        off = pl.multiple_of(j * tk, tk)
        mask = None
        if masked:
            qrow = lax.broadcasted_iota(jnp.int32, (2 * n, tk), 0) % n
            col = lax.broadcasted_iota(jnp.int32, (2 * n, tk), 1)
            mask = col < qrow
        da, c = _sb_tile([halves(qs)], [k_ref[pl.ds(off, tk), :]], [v_ref[pl.ds(off, tk), :]], u,
                         halves(c_sc[...]), mask)
        for h in range(2):
            acc_sc[h * tq + lo:h * tq + hi, :] += da[h * n:(h + 1) * n]
            c_sc[h * tq + lo:h * tq + hi, :] = c[h * n:(h + 1) * n]

    for jj in reversed(range(r)):
        tile(i * r + jj, jj * tk, tq, True)

    def body(st):
        tile(i * r - 1 - st[0], 0, tq, False)
        return st[0] + 1, jnp.max(c_sc[...])

    lax.while_loop(lambda st: jnp.logical_and(st[0] < i * r, st[1] > SKIP_LOG), body,
                   (jnp.int32(0), jnp.max(c_sc[...])))
    o_ref[...] = _sb_finish(acc_sc[...], z_ref[...]).astype(BF16)


def _sb_prompt(qb, kb, vb, z, u, tq):
    t = qb.shape[0]
    kern = functools.partial(_sb_prompt_kernel, tq=tq)
    rows = pl.BlockSpec((tq, LANES), lambda h, i: (i, h))
    whole = pl.BlockSpec((t, LANES), lambda h, i: (0, h))
    return pl.pallas_call(
        kern,
        grid=(SB_HEADS // 2, t // tq),
        in_specs=[rows, whole, whole, rows, pl.BlockSpec(u.shape, lambda h, i: (0, 0))],
        out_specs=rows,
        out_shape=jax.ShapeDtypeStruct((t, SB_WIDTH), BF16),
        scratch_shapes=[pltpu.VMEM((2 * tq, LANES), F32)] * 2,
        compiler_params=_params(("arbitrary", "arbitrary")),
        name="stick_breaking_prompt",
    )(qb, kb, vb, z, u)


def _sb_sample_kernel(q_ref, ck_ref, cv_ref, kn_ref, vn_ref, z_ref, u_ref, o_ref, left_ref, acc_sc, c_sc, *, ts):
    tk = u_ref.shape[0]
    ntile = ck_ref.shape[1] // tk
    npair = SB_HEADS // 2
    cols = lambda p: slice(p * LANES, (p + 1) * LANES)
    qs = [_stack_halves(q_ref[:, cols(p)]) for p in range(npair)]
    u = u_ref[...]
    pad = jnp.zeros((tk - ts, LANES), BF16)
    kn = [jnp.concatenate([kn_ref[:, cols(p)], pad], axis=0) for p in range(npair)]
    vn = [jnp.concatenate([vn_ref[:, cols(p)], pad], axis=0) for p in range(npair)]
    rows = npair * 2 * ts
    qrow = lax.broadcasted_iota(jnp.int32, (rows, tk), 0) % ts
    col = lax.broadcasted_iota(jnp.int32, (rows, tk), 1)
    acc, c = _sb_tile(qs, kn, vn, u, jnp.zeros((rows, LANES), F32), col < qrow)
    acc_sc[...] = acc
    c_sc[...] = c

    def body(st):
        off = pl.multiple_of((ntile - 1 - st[0]) * tk, tk)
        kc = ck_ref[0, pl.ds(off, tk), :].astype(BF16)
        vc = cv_ref[0, pl.ds(off, tk), :].astype(BF16)
        da, c = _sb_tile(qs, [kc[:, cols(p)] for p in range(npair)], [vc[:, cols(p)] for p in range(npair)],
                         u, c_sc[...], None)
        acc_sc[...] += da
        c_sc[...] = c
        return st[0] + 1, jnp.max(c)

    lax.while_loop(lambda st: jnp.logical_and(st[0] < ntile, st[1] > SKIP_LOG), body,
                   (jnp.int32(0), jnp.max(c)))
    left_ref[0] = jnp.broadcast_to(jnp.max(c_sc[...], axis=0, keepdims=True), (SUBLANES, LANES))
    acc = acc_sc[...]
    z = z_ref[...]
    o_ref[...] = jnp.concatenate(
        [_sb_finish(acc[p * 2 * ts:(p + 1) * 2 * ts], z[:, cols(p)]) for p in range(npair)], axis=1).astype(BF16)


def _sb_sample(qb, ck, cv, kb, vb, z, u, ts):
    nb, past, _ = ck.shape
    kern = functools.partial(_sb_sample_kernel, ts=ts)
    rows = pl.BlockSpec((ts, SB_WIDTH), lambda b: (b, 0))
    cache = pl.BlockSpec((1, past, SB_WIDTH), lambda b: (b, 0, 0))
    return pl.pallas_call(
        kern,
        grid=(nb,),
        in_specs=[rows, cache, cache, rows, rows, rows, pl.BlockSpec(u.shape, lambda b: (0, 0))],
        out_specs=[rows, pl.BlockSpec((1, SUBLANES, LANES), lambda b: (b, 0, 0))],
        out_shape=[jax.ShapeDtypeStruct((nb * ts, SB_WIDTH), BF16),
                   jax.ShapeDtypeStruct((nb, SUBLANES, LANES), F32)],
        scratch_shapes=[pltpu.VMEM((SB_HEADS * ts, LANES), F32)] * 2,
        compiler_params=_params(("arbitrary",)),
        name="stick_breaking_sample",
    )(qb, ck, cv, kb, vb, z, u)


def _sb_sample_two_phase(qb, cache_k, cache_v, kb, vb, z, u, ts):
    nb, past = cache_k.shape[:2]
    dense = lambda c: c.reshape(nb, c.shape[1], SB_WIDTH)
    recent = min(past, SB_RECENT_KEYS)
    out, left = _sb_sample(qb, dense(cache_k[:, past - recent:]), dense(cache_v[:, past - recent:]), kb, vb, z, u, ts)
    if recent == past:
        return out
    return lax.cond(jnp.max(left) > SKIP_LOG,
                    lambda: _sb_sample(qb, dense(cache_k), dense(cache_v), kb, vb, z, u, ts)[0],
                    lambda: out)


def _suffix_matrix(tk):
    j = np.arange(tk)[:, None]
    s = np.arange(tk)[None, :]
    return jnp.asarray((j >= s).astype(np.float32), dtype=BF16)


def _row_tile(m):
    for tm in PROJ_TILES:
        if m % tm == 0:
            return tm
    return m


def kernel(x_prompt, x_sample, cache_l0_k, cache_l0_v, state_l0_shift, state_l0_wkv, cache_l1_k, cache_l1_v, rel_bias, norm_l0, w_in_l0, w_out_l0, da_q_norm, da_k_norm, da_lambda_q1, da_lambda_k1, da_lambda_q2, da_lambda_k2, da_subln, rw_mu, rw_w0, rw_w_up, rw_a0, rw_a_up, rw_k_k, rw_k_a, rw_r_k, rw_lnx_g, rw_lnx_b, norm_l1, w_in_l1, w_out_l1):
    bp, t, d = x_prompt.shape
    nb, ts, _ = x_sample.shape
    past = cache_l0_k.shape[1]
    assert bp == 1 and ts <= LANES and ts % RW_BLOCK == 0
    assert t % max(SB_TILE, DA_TILE, RW_TILE) == 0 and past % LANES == 0 and DA_TILE >= MAX_DISTANCE

    row = lambda v: v.reshape(1, -1).astype(F32)
    w_in0 = w_in_l0.astype(BF16)
    w_in1 = w_in_l1.astype(BF16)
    w_out0a = w_out_l0[:DA_WIDTH].astype(BF16)
    w_out0b = w_out_l0[DA_WIDTH:].astype(BF16)
    w_out1 = w_out_l1.astype(BF16)
    qg = row(jnp.tile(da_q_norm, 512 // HEAD))
    kg = row(jnp.tile(da_k_norm, 512 // HEAD))
    lamv = jnp.stack([da_lambda_q1, da_lambda_k1, da_lambda_q2, da_lambda_k2]).astype(F32)
    sg = row(da_subln)
    zeros = jnp.zeros((HEAD, RW_WIDTH), F32)
    wwa_f = jnp.concatenate([jnp.concatenate([rw_w_up, zeros], axis=1),
                             jnp.concatenate([zeros, rw_a_up], axis=1)], axis=0)
    wwa_hi = wwa_f.astype(BF16)
    wwa = jnp.stack([wwa_hi, (wwa_f - wwa_hi.astype(F32)).astype(BF16)])
    heads = np.arange(RW_HEADS)
    per_group = RW_HEADS // RW_STATE_GROUPS
    chan = np.arange(RW_WIDTH)
    out_lane = np.arange(RW_STATE_SHAPE[0])
    hm = np.equal(heads[:, None], chan[None, :] // HEAD)
    dm = np.equal(heads[:, None] // per_group, out_lane[None, :] // HEAD)
    pm = (chan[:, None] // HEAD // per_group == out_lane[None, :] // HEAD) & (chan[:, None] % HEAD == out_lane[None, :] % HEAD)
    rw = dict(mu=row(rw_mu), w0=row(rw_w0), a0=row(rw_a0), wwa=wwa, kkw=row(rw_k_k), kaw=row(rw_k_a),
              rk=row(rw_r_k), lng=row(rw_lnx_g), lnb=row(rw_lnx_b), g2=_group_matrix(RW_STATE_LANES, 1.0),
              hm=jnp.asarray(hm.astype(np.float32)),
              dm=jnp.asarray(np.tile(dm.astype(np.float32), (RW_BLOCK, 1))),
              pm=jnp.asarray(pm.astype(np.float32), dtype=BF16))
    prompt_tab, tabc, tabn = _bias_tables(rel_bias.astype(F32), DA_TILE, ts, past)
    u_prompt = _suffix_matrix(2 * LANES)
    u_sample = _suffix_matrix(LANES)

    def layer0(x, first, s0, seq_len, attend):
        m = x.shape[0]
        tm = _row_tile(m)
        qn, kn, knb, v, vb, za, prw, zb = _inproj0(x, row(norm_l0), w_in0, qg, kg, tm)
        oa = attend(qn, kn, knb, v, vb, za)
        ob, sfin = _rwkv(prw, first, zb, _state_to_rows(s0), rw, seq_len, min(seq_len, RW_TILE))
        y = _outproj(x, [oa, ob], [w_out0a, w_out0b], tm)
        shift = prw.reshape(m // seq_len, seq_len, RW_SHIFT_COLS)[:, -1:]
        return y, kn, v, shift, _rows_to_state(sfin)

    def layer1(x, attend):
        m = x.shape[0]
        tm = _row_tile(m)
        qb, k, kb, v, vb, z = _inproj1(x, row(norm_l1), w_in1, tm)
        o = attend(qb, kb, vb, z)
        return _outproj(x, [o], [w_out1], tm), k, v

    xp = x_prompt.reshape(t, d)
    yp, k0p, v0p, shp, wkvp = layer0(
        xp, jnp.zeros((1, 1, RW_SHIFT_COLS), F32), jnp.zeros((1, RW_HEADS, HEAD, HEAD), F32), t,
        lambda qn, kn, knb, v, vb, za: _da_prompt(lamv, qn, knb, vb, za, prompt_tab, sg, DA_TILE))
    yp, k1p, v1p = layer1(yp, lambda qb, kb, vb, z: _sb_prompt(qb, kb, vb, z, u_prompt, SB_TILE))

    xs = x_sample.reshape(nb * ts, d)
    ck0 = cache_l0_k.reshape(nb, past, DA_WIDTH)
    cv0 = cache_l0_v.reshape(nb, past, DA_WIDTH)
    ys, k0s, v0s, shs, wkvs = layer0(
        xs, state_l0_shift, state_l0_wkv, ts,
        lambda qn, kn, knb, v, vb, za: _da_sample(lamv, qn, ck0, cv0, knb, vb, za, tabc, tabn, sg, ts))
    ys, k1s, v1s = layer1(ys, lambda qb, kb, vb, z: _sb_sample_two_phase(
        qb, cache_l1_k, cache_l1_v, kb, vb, z, u_sample, ts))

    return (yp.reshape(1, t, d), ys.reshape(nb, ts, d),
            k0p.reshape(1, t, DA_HEADS, LANES), v0p.reshape(1, t, DA_HEADS, LANES), shp, wkvp,
            k1p.reshape(1, t, SB_HEADS, HEAD), v1p.reshape(1, t, SB_HEADS, HEAD),
            k0s.reshape(nb, ts, DA_HEADS, LANES), v0s.reshape(nb, ts, DA_HEADS, LANES), shs, wkvs,
            k1s.reshape(nb, ts, SB_HEADS, HEAD), v1s.reshape(nb, ts, SB_HEADS, HEAD))
```

```python
import functools
import math

import numpy as np
import jax
import jax.numpy as jnp
from jax import lax
from jax.experimental import pallas as pl
from jax.experimental.pallas import tpu as pltpu

F32 = jnp.float32
BF16 = jnp.bfloat16

EPS = 1e-6
NEG = -1e30
CHUNK = 64
LANES = 128
SUBLANES = 8
HEAD = 64
DA_HEADS = 4
DA_WIDTH = 512
RW_WIDTH = 512
RW_HEADS = 8
RW_SHIFT_COLS = 3 * RW_WIDTH + 128
RW_GN_EPS = 64e-5
RW_STATE_LANES = 256
RW_STATE_GROUPS = RW_WIDTH // RW_STATE_LANES
RW_STATE_SHAPE = (RW_STATE_GROUPS * HEAD, RW_STATE_LANES)
RW_BLOCK = 8
RW_UNROLL = 4
SB_HEADS = 16
SB_WIDTH = 1024
N_BUCKETS = 32
MAX_DISTANCE = 128
LAMBDA_INIT = 0.8 - 0.6 * math.exp(-0.3 * 0)
LOG2E = math.log2(math.e)
VMEM_LIMIT = 56 * 1024 * 1024
PROJ_TILES = (512, 256)
DA_TILE = 512
SB_TILE = 512
RW_TILE = 256


def _params(sem):
    return pltpu.CompilerParams(dimension_semantics=sem, vmem_limit_bytes=VMEM_LIMIT)


def _dot(a, b):
    return jnp.dot(a, b, preferred_element_type=F32)


def _dot_nt(a, b):
    return lax.dot_general(a, b, (((1,), (1,)), ((), ())), preferred_element_type=F32)


def _silu(z):
    return z / (1.0 + jnp.exp(-z))


def _rms(x, g):
    return x * lax.rsqrt(jnp.mean(x * x, axis=-1, keepdims=True) + EPS) * g


def _group_matrix(n, scale):
    idx = np.arange(n) // HEAD
    return jnp.asarray((idx[:, None] == idx[None, :]).astype(np.float32) * scale, dtype=BF16)


def _inproj0_kernel(x_ref, g_ref, w_ref, qg_ref, kg_ref, avg_ref,
                    qn_ref, kn_ref, knb_ref, v_ref, vb_ref, za_ref, prw_ref, zb_ref):
    y = _dot(_rms(x_ref[...], g_ref[...]).astype(BF16), w_ref[...])

    def head_norm(t, g):
        sq = (t * t).astype(BF16)
        half = avg_ref.shape[0]
        ms = jnp.concatenate([_dot(sq[:, c:c + half], avg_ref[...]) for c in range(0, t.shape[1], half)], axis=1)
        return t * lax.rsqrt(ms + EPS) * g

    qn_ref[...] = (head_norm(y[:, 0:512], qg_ref[...]) * (HEAD ** -0.5 * LOG2E)).astype(BF16)
    kn = head_norm(y[:, 512:1024], kg_ref[...])
    knb_ref[...] = kn.astype(BF16)
    v = y[:, 1024:1536]
    vb_ref[...] = v.astype(BF16)
    for h in range(DA_HEADS):
        kn_ref[:, h, :] = kn[:, h * LANES:(h + 1) * LANES]
        v_ref[:, h, :] = v[:, h * LANES:(h + 1) * LANES]
    za_ref[...] = y[:, 1536:2048]
    prw_ref[...] = y[:, 2048:2048 + RW_SHIFT_COLS]
    zb_ref[...] = y[:, 2048 + RW_SHIFT_COLS:]


def _inproj0(x, g, w, qg, kg, tm):
    m, d = x.shape
    n = w.shape[1]
    row = lambda c: pl.BlockSpec((tm, c), lambda i: (i, 0))
    full = lambda a: pl.BlockSpec(a.shape, lambda i: (0,) * a.ndim)
    avg = _group_matrix(2 * LANES, 1.0 / HEAD)
    widths = [(512, BF16), (None, F32), (512, BF16), (None, F32), (512, BF16), (512, F32),
              (RW_SHIFT_COLS, F32), (512, F32)]
    heads = (DA_HEADS, LANES)
    return pl.pallas_call(
        _inproj0_kernel,
        grid=(m // tm,),
        in_specs=[row(d), full(g), full(w), full(qg), full(kg), full(avg)],
        out_specs=[row(c) if c else pl.BlockSpec((tm,) + heads, lambda i: (i, 0, 0)) for c, _ in widths],
        out_shape=[jax.ShapeDtypeStruct((m, c) if c else (m,) + heads, dt) for c, dt in widths],
        compiler_params=_params(("arbitrary",)),
        name="inproj0",
    )(x, g, w, qg, kg, avg)


def _inproj1_kernel(x_ref, g_ref, w_ref, qb_ref, k_ref, kb_ref, v_ref, vb_ref, z_ref):
    y = _dot(_rms(x_ref[...], g_ref[...]).astype(BF16), w_ref[...])
    qb_ref[...] = (y[:, 0:1024] * (HEAD ** -0.5)).astype(BF16)
    k = y[:, 1024:2048]
    k_ref[...] = k
    kb_ref[...] = k.astype(BF16)
    v = y[:, 2048:3072]
    v_ref[...] = v
    vb_ref[...] = v.astype(BF16)
    z_ref[...] = y[:, 3072:4096]


def _inproj1(x, g, w, tm):
    m, d = x.shape
    row = lambda c: pl.BlockSpec((tm, c), lambda i: (i, 0))
    full = lambda a: pl.BlockSpec(a.shape, lambda i: (0,) * a.ndim)
    dts = [BF16, F32, BF16, F32, BF16, F32]
    return pl.pallas_call(
        _inproj1_kernel,
        grid=(m // tm,),
        in_specs=[row(d), full(g), full(w)],
        out_specs=[row(1024) for _ in dts],
        out_shape=[jax.ShapeDtypeStruct((m, 1024), dt) for dt in dts],
        compiler_params=_params(("arbitrary",)),
        name="inproj1",
    )(x, g, w)


def _outproj_kernel(*refs):
    n = (len(refs) - 2) // 2
    x_ref, o_ref = refs[0], refs[-1]
    acc = x_ref[...]
    for a_ref, w_ref in zip(refs[1:1 + n], refs[1 + n:1 + 2 * n]):
        acc = acc + _dot(a_ref[...], w_ref[...])
    o_ref[...] = acc


def _outproj(x, acts, ws, tm):
    m, d = x.shape
    row = lambda c: pl.BlockSpec((tm, c), lambda i: (i, 0))
    full = lambda a: pl.BlockSpec(a.shape, lambda i: (0,) * a.ndim)
    return pl.pallas_call(
        _outproj_kernel,
        grid=(m // tm,),
        in_specs=[row(d)] + [row(a.shape[1]) for a in acts] + [full(w) for w in ws],
        out_specs=row(d),
        out_shape=jax.ShapeDtypeStruct((m, d), F32),
        compiler_params=_params(("arbitrary",)),
        name="outproj",
    )(x, *acts, *ws)


def _t5_bucket_np(rel):
    nb = N_BUCKETS // 2
    max_exact = nb // 2
    n = np.abs(rel)
    nf = np.maximum(n, 1).astype(np.float32)
    large = max_exact + (np.log(nf / np.float32(max_exact)) / np.float32(math.log(MAX_DISTANCE / max_exact))
                         * np.float32(nb - max_exact)).astype(np.int32)
    large = np.minimum(large, nb - 1)
    return np.where(rel > 0, nb, 0) + np.where(n < max_exact, n, large)


FAR_BUCKET = N_BUCKETS // 2 - 1
FAR_TILES = ((4, 2), (4, 1), (2, 1), (1, 1))


def _stack_halves(q):
    lane = lax.broadcasted_iota(jnp.int32, q.shape, 1)
    zero = jnp.zeros_like(q)
    return jnp.concatenate([jnp.where(lane < HEAD, q, zero), jnp.where(lane >= HEAD, q, zero)], axis=0)


def _softmax_step(qs, k, v, bias, m_sc, l_sc, acc_sc):
    s = jnp.concatenate([_dot_nt(q_h, k_h) for q_h, k_h in zip(qs, k)], axis=0)
    if bias is not None:
        s = s + bias
    _softmax_update(s, v, m_sc, l_sc, acc_sc)


def _softmax_update(s, v, m_sc, l_sc, acc_sc):
    reps = s.shape[1] // LANES
    m_old = m_sc[...]
    m_new = jnp.maximum(m_old, jnp.max(s, axis=-1, keepdims=True))
    alpha = jnp.exp2(m_old - m_new)
    p = jnp.exp2(s - jnp.concatenate([m_new] * reps, axis=1))
    psum = p[:, :LANES]
    for r in range(1, reps):
        psum = psum + p[:, r * LANES:(r + 1) * LANES]
    l_sc[...] = alpha * l_sc[...] + psum
    p = p.astype(BF16)
    rows = p.shape[0] // len(v)
    pv = jnp.concatenate([_dot(p[h * rows:(h + 1) * rows], v_h) for h, v_h in enumerate(v)], axis=0)
    acc_sc[...] = alpha * acc_sc[...] + pv
    m_sc[...] = m_new


def _diff_finish(lam_ref, za, sg, l, acc, tq):
    lv = lam_ref[...]
    lam = (jnp.exp(jnp.sum(lv[0:1] * lv[1:2], axis=-1, keepdims=True))
           - jnp.exp(jnp.sum(lv[2:3] * lv[3:4], axis=-1, keepdims=True)) + LAMBDA_INIT)
    o = acc / jnp.sum(l, axis=-1, keepdims=True)
    attn = o[:tq] - lam * o[tq:]
    return (_rms(attn, sg) * (1.0 - LAMBDA_INIT)) * _silu(za)


def _da_prompt_kernel(lam_ref, q_ref, k_ref, v_ref, za_ref, tab_ref, sg_ref, o_ref,
                      m_sc, l_sc, acc_sc, *, tq, tk):
    i = pl.program_id(1)
    qs = _stack_halves(q_ref[...])
    m_sc[...] = jnp.full(m_sc.shape, NEG, F32)
    l_sc[...] = jnp.zeros(l_sc.shape, F32)
    acc_sc[...] = jnp.zeros(acc_sc.shape, F32)

    def tile(j, width, bias):
        off = pl.multiple_of(j * tk, tk)
        _softmax_step([qs], [k_ref[pl.ds(off, width), :]], [v_ref[pl.ds(off, width), :]], bias,
                      m_sc, l_sc, acc_sc)

    nfar = jnp.maximum(i - 1, 0)

    done = 0
    for width, reps in FAR_TILES:
        def far(j, c, width=width, reps=reps, done=done):
            for n in range(reps):
                tile(done + width * (reps * j + n), width * tk, None)
            return c

        trips = (nfar - done) // (width * reps)
        lax.fori_loop(0, trips, far, 0)
        done = done + trips * width * reps

    @pl.when(i >= 1)
    def _():
        b = jnp.concatenate([tab_ref[0, 1], tab_ref[0, 0]], axis=1)
        tile(i - 1, 2 * tk, jnp.concatenate([b, b], axis=0))

    @pl.when(i == 0)
    def _():
        b = tab_ref[0, 0]
        tile(0, tk, jnp.concatenate([b, b], axis=0))
    o_ref[...] = _diff_finish(lam_ref, za_ref[...], sg_ref[...], l_sc[...], acc_sc[...], tq).astype(BF16)


def _da_prompt(lamv, qn, knb, vb, za, tab, sg, tq):
    t = qn.shape[0]
    tk = tq
    kern = functools.partial(_da_prompt_kernel, tq=tq, tk=tk)
    return pl.pallas_call(
        kern,
        grid=(DA_HEADS, t // tq),
        in_specs=[
            pl.BlockSpec(lamv.shape, lambda h, i: (0, 0)),
            pl.BlockSpec((tq, LANES), lambda h, i: (i, h)),
            pl.BlockSpec((t, LANES), lambda h, i: (0, h)),
            pl.BlockSpec((t, LANES), lambda h, i: (0, h)),
            pl.BlockSpec((tq, LANES), lambda h, i: (i, h)),
            pl.BlockSpec((1, 2, tq, tk), lambda h, i: (h, 0, 0, 0)),
            pl.BlockSpec(sg.shape, lambda h, i: (0, 0)),
        ],
        out_specs=pl.BlockSpec((tq, LANES), lambda h, i: (i, h)),
        out_shape=jax.ShapeDtypeStruct((t, DA_WIDTH), BF16),
        scratch_shapes=[pltpu.VMEM((2 * tq, LANES), F32)] * 3,
        compiler_params=_params(("arbitrary", "arbitrary")),
        name="diff_attn_prompt",
    )(lamv, qn, knb, vb, za, tab, sg)


def _da_sample_kernel(lam_ref, q_ref, ck_ref, cv_ref, kn_ref, vn_ref, za_ref, tabc_ref, tabn_ref, sg_ref,
                      o_ref, m_sc, l_sc, acc_sc, *, ts):
    cols = lambda h: slice(h * LANES, (h + 1) * LANES)
    heads = range(DA_HEADS)
    qs = [_stack_halves(q_ref[:, cols(h)]) for h in heads]
    m_sc[...] = jnp.full(m_sc.shape, NEG, F32)
    l_sc[...] = jnp.zeros(l_sc.shape, F32)
    acc_sc[...] = jnp.zeros(acc_sc.shape, F32)
    kc = ck_ref[0].astype(BF16)
    vc = cv_ref[0].astype(BF16)
    _softmax_step(qs, [kc[:, cols(h)] for h in heads], [vc[:, cols(h)] for h in heads], tabc_ref[...],
                  m_sc, l_sc, acc_sc)
    pad = jnp.zeros((LANES - ts, LANES), BF16)
    kn = [jnp.concatenate([kn_ref[:, cols(h)].astype(BF16), pad], axis=0) for h in heads]
    vn = [jnp.concatenate([vn_ref[:, cols(h)].astype(BF16), pad], axis=0) for h in heads]
    _softmax_step(qs, kn, vn, tabn_ref[...], m_sc, l_sc, acc_sc)
    l = l_sc[...]
    acc = acc_sc[...]
    za = za_ref[...]
    rows = lambda h: slice(h * 2 * ts, (h + 1) * 2 * ts)
    o_ref[...] = jnp.concatenate(
        [_diff_finish(lam_ref, za[:, cols(h)], sg_ref[...], l[rows(h)], acc[rows(h)], ts) for h in heads],
        axis=1).astype(BF16)


def _da_sample(lamv, qn, ck, cv, kn, vn, za, tabc, tabn, sg, ts):
    nb, past, _ = ck.shape
    kern = functools.partial(_da_sample_kernel, ts=ts)
    stack = lambda tab: jnp.concatenate([tab, tab], axis=1).reshape(DA_HEADS * 2 * ts, tab.shape[-1])
    tabc, tabn = stack(tabc), stack(tabn)
    rows = pl.BlockSpec((ts, DA_WIDTH), lambda b: (b, 0))
    cache = pl.BlockSpec((1, past, DA_WIDTH), lambda b: (b, 0, 0))
    full = lambda a: pl.BlockSpec(a.shape, lambda b: (0,) * a.ndim)
    return pl.pallas_call(
        kern,
        grid=(nb,),
        in_specs=[full(lamv), rows, cache, cache, rows, rows, rows, full(tabc), full(tabn), full(sg)],
        out_specs=rows,
        out_shape=jax.ShapeDtypeStruct((nb * ts, DA_WIDTH), BF16),
        scratch_shapes=[pltpu.VMEM((DA_HEADS * 2 * ts, LANES), F32)] * 3,
        compiler_params=_params(("arbitrary",)),
        name="diff_attn_sample",
    )(lamv, qn, ck, cv, kn, vn, za, tabc, tabn, sg)


def _bias_kernel(idx_ref, bt_ref, o_ref):
    idx = idx_ref[...]
    bt = bt_ref[...]
    bt = bt - bt[:, FAR_BUCKET:FAR_BUCKET + 1]
    onehot = jnp.where(lax.broadcasted_iota(jnp.int32, (N_BUCKETS, idx.shape[1]), 0) == idx, 1.0, 0.0).astype(BF16)
    hi = bt.astype(BF16)
    rest = bt - hi.astype(F32)
    mid = rest.astype(BF16)
    lo = (rest - mid.astype(F32)).astype(BF16)
    val = _dot(hi, onehot) + _dot(mid, onehot) + _dot(lo, onehot)
    o_ref[...] = jnp.where(idx < 0, NEG, val * LOG2E)


def _bias_tables(rel_bias, tq, ts, past):
    def buckets(rel, mask):
        return np.where(mask, _t5_bucket_np(rel), -1).astype(np.int32).reshape(-1)

    r = np.arange(tq)[:, None]
    c = np.arange(tq)[None, :]
    qpos = past + np.arange(ts)[:, None]
    kc = np.arange(past)[None, :]
    kn = past + np.arange(LANES)[None, :]
    parts = [buckets(c - r, (c // CHUNK) <= (r // CHUNK)),
             buckets(c - r - tq, np.ones((tq, tq), bool)),
             buckets(kc - qpos, (kc // CHUNK) <= (qpos // CHUNK)),
             buckets(kn - qpos, ((kn // CHUNK) <= (qpos // CHUNK)) & (kn < past + ts))]
    idx = np.concatenate(parts)
    n = idx.size
    bt = jnp.zeros((SUBLANES, N_BUCKETS), F32).at[:DA_HEADS].set(rel_bias.T)
    blk = 4096
    assert n % blk == 0
    tab = pl.pallas_call(
        _bias_kernel,
        grid=(n // blk,),
        in_specs=[pl.BlockSpec((1, blk), lambda i: (0, i)), pl.BlockSpec(bt.shape, lambda i: (0, 0))],
        out_specs=pl.BlockSpec((SUBLANES, blk), lambda i: (0, i)),
        out_shape=jax.ShapeDtypeStruct((SUBLANES, n), F32),
        compiler_params=_params(("arbitrary",)),
        name="t5_bias_tables",
    )(jnp.asarray(idx).reshape(1, n), bt)[:DA_HEADS]
    o1 = 2 * tq * tq
    o2 = o1 + ts * past
    prompt_tab = tab[:, :o1].reshape(DA_HEADS, 2, tq, tq)
    tabc = tab[:, o1:o2].reshape(DA_HEADS, ts, past)
    tabn = tab[:, o2:].reshape(DA_HEADS, ts, LANES)
    return prompt_tab, tabc, tabn


def _rwkv_kernel(prw_ref, look_ref, first_ref, zb_ref, s0_ref, mu_ref, w0_ref, a0_ref, wwa_ref,
                 kkw_ref, kaw_ref, rk_ref, lng_ref, lnb_ref, g2_ref, hm_ref, dm_ref, pm_ref, ti_ref, to_ref,
                 ob_ref, sfin_ref,
                 s_sc, om_sc, rho_sc, at_sc, kt_sc, atb_sc, ktb_sc, wb_sc, vr_sc, o_sc, *, tm):
    j = pl.program_id(1)

    @pl.when(j == 0)
    def _():
        s_sc[...] = s0_ref[0]

    g2 = g2_ref[...]

    def head_sums(t):
        tb = t.astype(BF16)
        return jnp.concatenate([_dot(tb[:, c * RW_STATE_LANES:(c + 1) * RW_STATE_LANES], g2)
                                for c in range(RW_STATE_GROUPS)], axis=1)

    x = prw_ref[...]
    row0 = jnp.where(j == 0, first_ref[0], look_ref[7:8, :])
    rows = lax.broadcasted_iota(jnp.int32, x.shape, 0)
    prev = jnp.where(rows == 0, row0, pltpu.roll(x, 1, axis=0))
    mix = x + (prev - x) * mu_ref[...]
    r = mix[:, 0:512]
    kr = mix[:, 512:1024]
    vr = mix[:, 1024:1536]
    la = mix[:, 1536:1664]
    lane = lax.broadcasted_iota(jnp.int32, la.shape, 1)
    la = jnp.where(lane < HEAD, jnp.tanh(la), la)
    hi = la.astype(BF16)
    lo = (la - hi.astype(F32)).astype(BF16)
    wwa = wwa_ref[...]
    lora = _dot(hi, wwa[0]) + _dot(lo, wwa[0]) + _dot(hi, wwa[1])
    wpre = w0_ref[...] + lora[:, 0:512]
    nw = -wpre
    w = -(jnp.maximum(nw, 0.0) + jnp.log(1.0 + jnp.exp(-jnp.abs(nw)))) - 0.5
    ld = -jnp.exp(w)
    a = 1.0 / (1.0 + jnp.exp(-(a0_ref[...] + lora[:, 512:1024])))
    kk = kr * kkw_ref[...]
    nrm = jnp.sqrt(head_sums(kk * kk))
    kk = kk / jnp.maximum(nrm, 1e-12)
    kka = kk * a
    kh = kr * (1.0 + (a - 1.0) * kaw_ref[...])

    def dot3(m, t):
        hi = t.astype(BF16)
        r1 = t - hi.astype(F32)
        mid = r1.astype(BF16)
        lo = (r1 - mid.astype(F32)).astype(BF16)
        return _dot(m, hi) + _dot(m, mid) + _dot(m, lo)

    incl = dot3(ti_ref[...], ld)
    tot = dot3(to_ref[...], ld)
    e_rem = jnp.exp(tot - incl)
    e_inv = jnp.exp(-incl)
    om_sc[...] = jnp.exp(incl - ld) * kk
    rho_sc[...] = jnp.exp(incl) * r
    at_sc[...] = kka * e_inv
    kt_sc[...] = kh * e_inv
    atb_sc[...] = kka * e_rem
    ktb_sc[...] = kh * e_rem
    wb_sc[...] = jnp.exp(tot)
    hm = hm_ref[...]
    vx = (jnp.broadcast_to(vr[:, None, :], (tm, RW_HEADS, RW_WIDTH)) * hm[None]).reshape(tm * RW_HEADS, RW_WIDTH)
    vr_sc[...] = _dot(vx.astype(BF16), pm_ref[...])

    nb = RW_BLOCK
    heads_per_group = RW_HEADS // RW_STATE_GROUPS
    dm = dm_ref[...]

    def head_rows(blk):
        return jnp.concatenate([jnp.broadcast_to(blk[j:j + 1], (RW_HEADS, RW_WIDTH)) * hm for j in range(nb)],
                               axis=0)

    def fold(x):
        return x[:, :RW_STATE_LANES] + x[:, RW_STATE_LANES:]

    def query_rows(b):
        base = pl.multiple_of(b * nb, nb)
        return jnp.concatenate([head_rows(om_sc[pl.ds(base, nb), :]), head_rows(rho_sc[pl.ds(base, nb), :])], axis=0)

    nblk = tm // nb

    def coefficients(b, x):
        base = pl.multiple_of(b * nb, nb)
        rhs = jnp.concatenate([at_sc[pl.ds(base, nb), :], kt_sc[pl.ds(base, nb), :],
                               jnp.zeros((LANES - 2 * nb, RW_WIDTH), F32)], axis=0)
        return _dot_nt(x.astype(BF16), rhs.astype(BF16))

    def block(b, carry):
        s, g, coef = carry
        base = pl.multiple_of(b * nb, nb)
        atb_b, ktb_b, wb_b = (ref[pl.ds(base, nb), :] for ref in (atb_sc, ktb_sc, wb_sc))
        vr_b = vr_sc[pl.ds(pl.multiple_of(b * (nb * RW_HEADS), nb * RW_HEADS), nb * RW_HEADS), :]
        bn = jnp.minimum(b + 1, nblk - 1)
        xn = query_rows(bn)
        coef_next = coefficients(bn, xn)
        rhs2 = jnp.concatenate([fold(head_rows(atb_b)), fold(head_rows(ktb_b))], axis=0).astype(BF16)
        cross = _dot_nt(fold(xn).astype(BF16), rhs2).astype(BF16)
        g_decayed = _dot_nt(fold(xn * wb_b[0:1]).astype(BF16), s.astype(BF16))

        def rows(a, j):
            return a[j * RW_HEADS:(j + 1) * RW_HEADS]

        def cf(j, lane):
            return jnp.broadcast_to(rows(coef, j)[:, lane:lane + 1], (RW_HEADS, LANES))

        sa = []
        for j in range(nb):
            acc = rows(g, j) * rows(dm, j)
            for i in range(j):
                acc = acc + cf(j, nb + i) * rows(vr_b, i)
            for i in range(j):
                acc = acc - cf(j, i) * sa[i]
            sa.append(acc)
        outs = []
        for j in range(nb):
            acc = rows(g, nb + j) * rows(dm, j)
            for i in range(j + 1):
                acc = acc + cf(nb + j, nb + i) * rows(vr_b, i) - cf(nb + j, i) * sa[i]
            outs.append(acc)
        lhs = jnp.concatenate([-t for t in sa] + [vr_b], axis=0).astype(BF16)
        g_next = g_decayed + _dot(cross, lhs)
        ds = lax.dot_general(lhs, rhs2, (((0,), (0,)), ((), ())), preferred_element_type=F32)
        wb = jnp.concatenate(
            [jnp.broadcast_to(wb_b[0:1, c * RW_STATE_LANES:(c + 1) * RW_STATE_LANES], (HEAD, RW_STATE_LANES))
             for c in range(RW_STATE_GROUPS)], axis=0)
        for h in range(RW_HEADS):
            o_sc[h, pl.ds(base, nb), :] = jnp.concatenate([t[h:h + 1] for t in outs], axis=0)
        return s * wb + ds, g_next, coef_next

    s0 = s_sc[...]
    x0 = query_rows(0)
    g0 = _dot_nt(fold(x0).astype(BF16), s0.astype(BF16))
    s_sc[...] = lax.fori_loop(0, nblk, block, (s0, g0, coefficients(0, x0)), unroll=min(RW_UNROLL, nblk))[0]

    o = jnp.concatenate([o_sc[c * heads_per_group + hh][:, c * HEAD:(c + 1) * HEAD]
                         for c in range(RW_STATE_GROUPS) for hh in range(heads_per_group)], axis=1)
    mean = head_sums(o) * (1.0 / HEAD)
    d = o - mean
    var = head_sums(d * d) * (1.0 / HEAD)
    y = d * lax.rsqrt(var + RW_GN_EPS) * lng_ref[...] + lnb_ref[...]
    bonus = head_sums(r * kh * rk_ref[...]) * vr
    ob_ref[...] = ((y + bonus) * _silu(zb_ref[...])).astype(BF16)

    @pl.when(j == pl.num_programs(1) - 1)
    def _():
        sfin_ref[0] = s_sc[...]


def _block_sum_matrices(tm):
    t = np.arange(tm)
    same = (t[:, None] // RW_BLOCK) == (t[None, :] // RW_BLOCK)
    incl = same & (t[None, :] <= t[:, None])
    return [jnp.asarray(incl.astype(np.float32), dtype=BF16), jnp.asarray(same.astype(np.float32), dtype=BF16)]


def _rwkv(prw, first, zb, s0, p, seq_len, tm):
    m = prw.shape[0]
    nseq = m // seq_len
    ntile = seq_len // tm
    kern = functools.partial(_rwkv_kernel, tm=tm)
    full = lambda a: pl.BlockSpec(a.shape, lambda b, j: (0,) * a.ndim)
    rows = lambda c: pl.BlockSpec((tm, c), lambda b, j: (b * ntile + j, 0))
    look = pl.BlockSpec((8, RW_SHIFT_COLS), lambda b, j: (jnp.maximum((b * ntile + j) * (tm // 8) - 1, 0), 0))
    consts = [p["mu"], p["w0"], p["a0"], p["wwa"], p["kkw"], p["kaw"], p["rk"], p["lng"], p["lnb"],
              p["g2"], p["hm"], p["dm"], p["pm"]] + _block_sum_matrices(tm)
    return pl.pallas_call(
        kern,
        grid=(nseq, ntile),
        in_specs=[rows(RW_SHIFT_COLS), look,
                  pl.BlockSpec((1, 1, RW_SHIFT_COLS), lambda b, j: (b, 0, 0)),
                  rows(RW_WIDTH),
                  pl.BlockSpec((1,) + RW_STATE_SHAPE, lambda b, j: (b, 0, 0))] + [full(c) for c in consts],
        out_specs=[rows(RW_WIDTH), pl.BlockSpec((1,) + RW_STATE_SHAPE, lambda b, j: (b, 0, 0))],
        out_shape=[jax.ShapeDtypeStruct((m, RW_WIDTH), BF16), jax.ShapeDtypeStruct((nseq,) + RW_STATE_SHAPE, F32)],
        scratch_shapes=([pltpu.VMEM(RW_STATE_SHAPE, F32)] + [pltpu.VMEM((tm, RW_WIDTH), F32)] * 7
                        + [pltpu.VMEM((tm * RW_HEADS, RW_STATE_SHAPE[0]), F32),
                           pltpu.VMEM((RW_HEADS, tm, RW_STATE_SHAPE[0]), F32)]),
        compiler_params=_params(("arbitrary", "arbitrary")),
        name="rwkv7",
    )(prw, prw, first, zb, s0, *consts)


def _state_to_rows(s):
    b = s.shape[0]
    per = RW_HEADS // RW_STATE_GROUPS
    return s.reshape(b, RW_STATE_GROUPS, per, HEAD, HEAD).transpose(0, 1, 3, 2, 4).reshape((b,) + RW_STATE_SHAPE)


def _rows_to_state(s):
    b = s.shape[0]
    per = RW_HEADS // RW_STATE_GROUPS
    return s.reshape(b, RW_STATE_GROUPS, HEAD, per, HEAD).transpose(0, 1, 3, 2, 4).reshape(b, RW_HEADS, HEAD, HEAD)


SKIP_LOG = -104.0
SB_RECENT_KEYS = 256


def _sb_tile(qs, k, v, u, c, mask):
    tk = k[0].shape[0]
    rows = qs[0].shape[0]
    z = jnp.concatenate([_dot_nt(q_p, k_p) for q_p, k_p in zip(qs, k)], axis=0)
    lg = -(jnp.maximum(z, 0.0) + jnp.log(1.0 + jnp.exp(-jnp.abs(z))))
    if mask is not None:
        lg = jnp.where(mask, lg, 0.0)
    it = _dot(lg.astype(BF16), u)
    a = jnp.exp(z + it + jnp.concatenate([c] * (tk // LANES), axis=1))
    if mask is not None:
        a = jnp.where(mask, a, 0.0)
    a = a.astype(BF16)
    da = jnp.concatenate([_dot(a[p * rows:(p + 1) * rows], v_p) for p, v_p in enumerate(v)], axis=0)
    return da, c + jnp.broadcast_to(it[:, 0:1], c.shape)


def _sb_finish(acc, z):
    t = z.shape[0]
    lane = lax.broadcasted_iota(jnp.int32, z.shape, 1)
    return jnp.where(lane < HEAD, acc[:t], acc[t:]) * _silu(z)


def _sb_prompt_kernel(q_ref, k_ref, v_ref, z_ref, u_ref, o_ref, acc_sc, c_sc, *, tq):
    i = pl.program_id(1)
    tk = u_ref.shape[0]
    r = tq // tk
    qs = _stack_halves(q_ref[...])
    u = u_ref[...]
    acc_sc[...] = jnp.zeros(acc_sc.shape, F32)
    c_sc[...] = jnp.zeros(c_sc.shape, F32)
    def tile(j, lo, hi, masked):
        n = hi - lo
        halves = lambda x: jnp.concatenate([x[lo:hi], x[tq + lo:tq + hi]], axis=0)
        off = pl.multiple_of(j * tk, tk)
        mask = None
        if masked:
            qrow = lax.broadcasted_iota(jnp.int32, (2 * n, tk), 0) % n
            col = lax.broadcasted_iota(jnp.int32, (2 * n, tk), 1)
            mask = col < qrow
        da, c = _sb_tile([halves(qs)], [k_ref[pl.ds(off, tk), :]], [v_ref[pl.ds(off, tk), :]], u,
                         halves(c_sc[...]), mask)
        for h in range(2):
            acc_sc[h * tq + lo:h * tq + hi, :] += da[h * n:(h + 1) * n]
            c_sc[h * tq + lo:h * tq + hi, :] = c[h * n:(h + 1) * n]

    for jj in reversed(range(r)):
        tile(i * r + jj, jj * tk, tq, True)

    def body(st):
        tile(i * r - 1 - st[0], 0, tq, False)
        return st[0] + 1, jnp.max(c_sc[...])

    lax.while_loop(lambda st: jnp.logical_and(st[0] < i * r, st[1] > SKIP_LOG), body,
                   (jnp.int32(0), jnp.max(c_sc[...])))
    o_ref[...] = _sb_finish(acc_sc[...], z_ref[...]).astype(BF16)


def _sb_prompt(qb, kb, vb, z, u, tq):
    t = qb.shape[0]
    kern = functools.partial(_sb_prompt_kernel, tq=tq)
    rows = pl.BlockSpec((tq, LANES), lambda h, i: (i, h))
    whole = pl.BlockSpec((t, LANES), lambda h, i: (0, h))
    return pl.pallas_call(
        kern,
        grid=(SB_HEADS // 2, t // tq),
        in_specs=[rows, whole, whole, rows, pl.BlockSpec(u.shape, lambda h, i: (0, 0))],
        out_specs=rows,
        out_shape=jax.ShapeDtypeStruct((t, SB_WIDTH), BF16),
        scratch_shapes=[pltpu.VMEM((2 * tq, LANES), F32)] * 2,
        compiler_params=_params(("arbitrary", "arbitrary")),
        name="stick_breaking_prompt",
    )(qb, kb, vb, z, u)


def _sb_sample_kernel(q_ref, ck_ref, cv_ref, kn_ref, vn_ref, z_ref, u_ref, o_ref, left_ref, acc_sc, c_sc, *, ts):
    tk = u_ref.shape[0]
    ntile = ck_ref.shape[1] // tk
    npair = SB_HEADS // 2
    cols = lambda p: slice(p * LANES, (p + 1) * LANES)
    qs = [_stack_halves(q_ref[:, cols(p)]) for p in range(npair)]
    u = u_ref[...]
    pad = jnp.zeros((tk - ts, LANES), BF16)
    kn = [jnp.concatenate([kn_ref[:, cols(p)], pad], axis=0) for p in range(npair)]
    vn = [jnp.concatenate([vn_ref[:, cols(p)], pad], axis=0) for p in range(npair)]
    rows = npair * 2 * ts
    qrow = lax.broadcasted_iota(jnp.int32, (rows, tk), 0) % ts
    col = lax.broadcasted_iota(jnp.int32, (rows, tk), 1)
    acc, c = _sb_tile(qs, kn, vn, u, jnp.zeros((rows, LANES), F32), col < qrow)
    acc_sc[...] = acc
    c_sc[...] = c

    def body(st):
        off = pl.multiple_of((ntile - 1 - st[0]) * tk, tk)
        kc = ck_ref[0, pl.ds(off, tk), :].astype(BF16)
        vc = cv_ref[0, pl.ds(off, tk), :].astype(BF16)
        da, c = _sb_tile(qs, [kc[:, cols(p)] for p in range(npair)], [vc[:, cols(p)] for p in range(npair)],
                         u, c_sc[...], None)
        acc_sc[...] += da
        c_sc[...] = c
        return st[0] + 1, jnp.max(c)

    lax.while_loop(lambda st: jnp.logical_and(st[0] < ntile, st[1] > SKIP_LOG), body,
                   (jnp.int32(0), jnp.max(c)))
    left_ref[0] = jnp.broadcast_to(jnp.max(c_sc[...], axis=0, keepdims=True), (SUBLANES, LANES))
    acc = acc_sc[...]
    z = z_ref[...]
    o_ref[...] = jnp.concatenate(
        [_sb_finish(acc[p * 2 * ts:(p + 1) * 2 * ts], z[:, cols(p)]) for p in range(npair)], axis=1).astype(BF16)


def _sb_sample(qb, ck, cv, kb, vb, z, u, ts):
    nb, past, _ = ck.shape
    kern = functools.partial(_sb_sample_kernel, ts=ts)
    rows = pl.BlockSpec((ts, SB_WIDTH), lambda b: (b, 0))
    cache = pl.BlockSpec((1, past, SB_WIDTH), lambda b: (b, 0, 0))
    return pl.pallas_call(
        kern,
        grid=(nb,),
        in_specs=[rows, cache, cache, rows, rows, rows, pl.BlockSpec(u.shape, lambda b: (0, 0))],
        out_specs=[rows, pl.BlockSpec((1, SUBLANES, LANES), lambda b: (b, 0, 0))],
        out_shape=[jax.ShapeDtypeStruct((nb * ts, SB_WIDTH), BF16),
                   jax.ShapeDtypeStruct((nb, SUBLANES, LANES), F32)],
        scratch_shapes=[pltpu.VMEM((SB_HEADS * ts, LANES), F32)] * 2,
        compiler_params=_params(("arbitrary",)),
        name="stick_breaking_sample",
    )(qb, ck, cv, kb, vb, z, u)


def _sb_sample_two_phase(qb, cache_k, cache_v, kb, vb, z, u, ts):
    nb, past = cache_k.shape[:2]
    dense = lambda c: c.reshape(nb, c.shape[1], SB_WIDTH)
    recent = min(past, SB_RECENT_KEYS)
    out, left = _sb_sample(qb, dense(cache_k[:, past - recent:]), dense(cache_v[:, past - recent:]), kb, vb, z, u, ts)
    if recent == past:
        return out
    return lax.cond(jnp.max(left) > SKIP_LOG,
                    lambda: _sb_sample(qb, dense(cache_k), dense(cache_v), kb, vb, z, u, ts)[0],
                    lambda: out)


def _suffix_matrix(tk):
    j = np.arange(tk)[:, None]
    s = np.arange(tk)[None, :]
    return jnp.asarray((j >= s).astype(np.float32), dtype=BF16)


def _row_tile(m):
    for tm in PROJ_TILES:
        if m % tm == 0:
            return tm
    return m


def kernel(x_prompt, x_sample, cache_l0_k, cache_l0_v, state_l0_shift, state_l0_wkv, cache_l1_k, cache_l1_v, rel_bias, norm_l0, w_in_l0, w_out_l0, da_q_norm, da_k_norm, da_lambda_q1, da_lambda_k1, da_lambda_q2, da_lambda_k2, da_subln, rw_mu, rw_w0, rw_w_up, rw_a0, rw_a_up, rw_k_k, rw_k_a, rw_r_k, rw_lnx_g, rw_lnx_b, norm_l1, w_in_l1, w_out_l1):
    bp, t, d = x_prompt.shape
    nb, ts, _ = x_sample.shape
    past = cache_l0_k.shape[1]
    assert bp == 1 and ts <= LANES and ts % RW_BLOCK == 0
    assert t % max(SB_TILE, DA_TILE, RW_TILE) == 0 and past % LANES == 0 and DA_TILE >= MAX_DISTANCE

    row = lambda v: v.reshape(1, -1).astype(F32)
    w_in0 = w_in_l0.astype(BF16)
    w_in1 = w_in_l1.astype(BF16)
    w_out0a = w_out_l0[:DA_WIDTH].astype(BF16)
    w_out0b = w_out_l0[DA_WIDTH:].astype(BF16)
    w_out1 = w_out_l1.astype(BF16)
    qg = row(jnp.tile(da_q_norm, 512 // HEAD))
    kg = row(jnp.tile(da_k_norm, 512 // HEAD))
    lamv = jnp.stack([da_lambda_q1, da_lambda_k1, da_lambda_q2, da_lambda_k2]).astype(F32)
    sg = row(da_subln)
    zeros = jnp.zeros((HEAD, RW_WIDTH), F32)
    wwa_f = jnp.concatenate([jnp.concatenate([rw_w_up, zeros], axis=1),
                             jnp.concatenate([zeros, rw_a_up], axis=1)], axis=0)
    wwa_hi = wwa_f.astype(BF16)
    wwa = jnp.stack([wwa_hi, (wwa_f - wwa_hi.astype(F32)).astype(BF16)])
    heads = np.arange(RW_HEADS)
    per_group = RW_HEADS // RW_STATE_GROUPS
    chan = np.arange(RW_WIDTH)
    out_lane = np.arange(RW_STATE_SHAPE[0])
    hm = np.equal(heads[:, None], chan[None, :] // HEAD)
    dm = np.equal(heads[:, None] // per_group, out_lane[None, :] // HEAD)
    pm = (chan[:, None] // HEAD // per_group == out_lane[None, :] // HEAD) & (chan[:, None] % HEAD == out_lane[None, :] % HEAD)
    rw = dict(mu=row(rw_mu), w0=row(rw_w0), a0=row(rw_a0), wwa=wwa, kkw=row(rw_k_k), kaw=row(rw_k_a),
              rk=row(rw_r_k), lng=row(rw_lnx_g), lnb=row(rw_lnx_b), g2=_group_matrix(RW_STATE_LANES, 1.0),
              hm=jnp.asarray(hm.astype(np.float32)),
              dm=jnp.asarray(np.tile(dm.astype(np.float32), (RW_BLOCK, 1))),
              pm=jnp.asarray(pm.astype(np.float32), dtype=BF16))
    prompt_tab, tabc, tabn = _bias_tables(rel_bias.astype(F32), DA_TILE, ts, past)
    u_prompt = _suffix_matrix(2 * LANES)
    u_sample = _suffix_matrix(LANES)

    def layer0(x, first, s0, seq_len, attend):
        m = x.shape[0]
        tm = _row_tile(m)
        qn, kn, knb, v, vb, za, prw, zb = _inproj0(x, row(norm_l0), w_in0, qg, kg, tm)
        oa = attend(qn, kn, knb, v, vb, za)
        ob, sfin = _rwkv(prw, first, zb, _state_to_rows(s0), rw, seq_len, min(seq_len, RW_TILE))
        y = _outproj(x, [oa, ob], [w_out0a, w_out0b], tm)
        shift = prw.reshape(m // seq_len, seq_len, RW_SHIFT_COLS)[:, -1:]
        return y, kn, v, shift, _rows_to_state(sfin)

    def layer1(x, attend):
        m = x.shape[0]
        tm = _row_tile(m)
        qb, k, kb, v, vb, z = _inproj1(x, row(norm_l1), w_in1, tm)
        o = attend(qb, kb, vb, z)
        return _outproj(x, [o], [w_out1], tm), k, v

    xp = x_prompt.reshape(t, d)
    yp, k0p, v0p, shp, wkvp = layer0(
        xp, jnp.zeros((1, 1, RW_SHIFT_COLS), F32), jnp.zeros((1, RW_HEADS, HEAD, HEAD), F32), t,
        lambda qn, kn, knb, v, vb, za: _da_prompt(lamv, qn, knb, vb, za, prompt_tab, sg, DA_TILE))
    yp, k1p, v1p = layer1(yp, lambda qb, kb, vb, z: _sb_prompt(qb, kb, vb, z, u_prompt, SB_TILE))

    xs = x_sample.reshape(nb * ts, d)
    ck0 = cache_l0_k.reshape(nb, past, DA_WIDTH)
    cv0 = cache_l0_v.reshape(nb, past, DA_WIDTH)
    ys, k0s, v0s, shs, wkvs = layer0(
        xs, state_l0_shift, state_l0_wkv, ts,
        lambda qn, kn, knb, v, vb, za: _da_sample(lamv, qn, ck0, cv0, knb, vb, za, tabc, tabn, sg, ts))
    ys, k1s, v1s = layer1(ys, lambda qb, kb, vb, z: _sb_sample_two_phase(
        qb, cache_l1_k, cache_l1_v, kb, vb, z, u_sample, ts))

    return (yp.reshape(1, t, d), ys.reshape(nb, ts, d),
            k0p.reshape(1, t, DA_HEADS, LANES), v0p.reshape(1, t, DA_HEADS, LANES), shp, wkvp,
            k1p.reshape(1, t, SB_HEADS, HEAD), v1p.reshape(1, t, SB_HEADS, HEAD),
            k0s.reshape(nb, ts, DA_HEADS, LANES), v0s.reshape(nb, ts, DA_HEADS, LANES), shs, wkvs,
            k1s.reshape(nb, ts, SB_HEADS, HEAD), v1s.reshape(nb, ts, SB_HEADS, HEAD))
```

```python
import functools
import math

import numpy as np
import jax
import jax.numpy as jnp
from jax import lax
from jax.experimental import pallas as pl
from jax.experimental.pallas import tpu as pltpu

F32 = jnp.float32
BF16 = jnp.bfloat16

EPS = 1e-6
NEG = -1e30
CHUNK = 64
LANES = 128
SUBLANES = 8
HEAD = 64
DA_HEADS = 4
DA_WIDTH = 512
RW_WIDTH = 512
RW_HEADS = 8
RW_SHIFT_COLS = 3 * RW_WIDTH + 128
RW_GN_EPS = 64e-5
RW_STATE_LANES = 256
RW_STATE_GROUPS = RW_WIDTH // RW_STATE_LANES
RW_STATE_SHAPE = (RW_STATE_GROUPS * HEAD, RW_STATE_LANES)
RW_BLOCK = 8
RW_UNROLL = 4
SB_HEADS = 16
SB_WIDTH = 1024
N_BUCKETS = 32
MAX_DISTANCE = 128
LAMBDA_INIT = 0.8 - 0.6 * math.exp(-0.3 * 0)
LOG2E = math.log2(math.e)
VMEM_LIMIT = 56 * 1024 * 1024
PROJ_TILES = (512, 256)
DA_TILE = 512
SB_TILE = 512
RW_TILE = 256


def _params(sem):
    return pltpu.CompilerParams(dimension_semantics=sem, vmem_limit_bytes=VMEM_LIMIT)


def _dot(a, b):
    return jnp.dot(a, b, preferred_element_type=F32)


def _dot_nt(a, b):
    return lax.dot_general(a, b, (((1,), (1,)), ((), ())), preferred_element_type=F32)


def _silu(z):
    return z / (1.0 + jnp.exp(-z))


def _rms(x, g):
    return x * lax.rsqrt(jnp.mean(x * x, axis=-1, keepdims=True) + EPS) * g


def _group_matrix(n, scale):
    idx = np.arange(n) // HEAD
    return jnp.asarray((idx[:, None] == idx[None, :]).astype(np.float32) * scale, dtype=BF16)


def _inproj0_kernel(x_ref, g_ref, w_ref, qg_ref, kg_ref, avg_ref,
                    qn_ref, kn_ref, knb_ref, v_ref, vb_ref, za_ref, prw_ref, zb_ref):
    y = _dot(_rms(x_ref[...], g_ref[...]).astype(BF16), w_ref[...])

    def head_norm(t, g):
        sq = (t * t).astype(BF16)
        half = avg_ref.shape[0]
        ms = jnp.concatenate([_dot(sq[:, c:c + half], avg_ref[...]) for c in range(0, t.shape[1], half)], axis=1)
        return t * lax.rsqrt(ms + EPS) * g

    qn_ref[...] = (head_norm(y[:, 0:512], qg_ref[...]) * (HEAD ** -0.5 * LOG2E)).astype(BF16)
    kn = head_norm(y[:, 512:1024], kg_ref[...])
    knb_ref[...] = kn.astype(BF16)
    v = y[:, 1024:1536]
    vb_ref[...] = v.astype(BF16)
    for h in range(DA_HEADS):
        kn_ref[:, h, :] = kn[:, h * LANES:(h + 1) * LANES]
        v_ref[:, h, :] = v[:, h * LANES:(h + 1) * LANES]
    za_ref[...] = y[:, 1536:2048]
    prw_ref[...] = y[:, 2048:2048 + RW_SHIFT_COLS]
    zb_ref[...] = y[:, 2048 + RW_SHIFT_COLS:]


def _inproj0(x, g, w, qg, kg, tm):
    m, d = x.shape
    n = w.shape[1]
    row = lambda c: pl.BlockSpec((tm, c), lambda i: (i, 0))
    full = lambda a: pl.BlockSpec(a.shape, lambda i: (0,) * a.ndim)
    avg = _group_matrix(2 * LANES, 1.0 / HEAD)
    widths = [(512, BF16), (None, F32), (512, BF16), (None, F32), (512, BF16), (512, F32),
              (RW_SHIFT_COLS, F32), (512, F32)]
    heads = (DA_HEADS, LANES)
    return pl.pallas_call(
        _inproj0_kernel,
        grid=(m // tm,),
        in_specs=[row(d), full(g), full(w), full(qg), full(kg), full(avg)],
        out_specs=[row(c) if c else pl.BlockSpec((tm,) + heads, lambda i: (i, 0, 0)) for c, _ in widths],
        out_shape=[jax.ShapeDtypeStruct((m, c) if c else (m,) + heads, dt) for c, dt in widths],
        compiler_params=_params(("arbitrary",)),
        name="inproj0",
    )(x, g, w, qg, kg, avg)


def _inproj1_kernel(x_ref, g_ref, w_ref, qb_ref, k_ref, kb_ref, v_ref, vb_ref, z_ref):
    y = _dot(_rms(x_ref[...], g_ref[...]).astype(BF16), w_ref[...])
    qb_ref[...] = (y[:, 0:1024] * (HEAD ** -0.5)).astype(BF16)
    k = y[:, 1024:2048]
    k_ref[...] = k
    kb_ref[...] = k.astype(BF16)
    v = y[:, 2048:3072]
    v_ref[...] = v
    vb_ref[...] = v.astype(BF16)
    z_ref[...] = y[:, 3072:4096]


def _inproj1(x, g, w, tm):
    m, d = x.shape
    row = lambda c: pl.BlockSpec((tm, c), lambda i: (i, 0))
    full = lambda a: pl.BlockSpec(a.shape, lambda i: (0,) * a.ndim)
    dts = [BF16, F32, BF16, F32, BF16, F32]
    return pl.pallas_call(
        _inproj1_kernel,
        grid=(m // tm,),
        in_specs=[row(d), full(g), full(w)],
        out_specs=[row(1024) for _ in dts],
        out_shape=[jax.ShapeDtypeStruct((m, 1024), dt) for dt in dts],
        compiler_params=_params(("arbitrary",)),
        name="inproj1",
    )(x, g, w)


def _outproj_kernel(*refs):
    n = (len(refs) - 2) // 2
    x_ref, o_ref = refs[0], refs[-1]
    acc = x_ref[...]
    for a_ref, w_ref in zip(refs[1:1 + n], refs[1 + n:1 + 2 * n]):
        acc = acc + _dot(a_ref[...], w_ref[...])
    o_ref[...] = acc


def _outproj(x, acts, ws, tm):
    m, d = x.shape
    row = lambda c: pl.BlockSpec((tm, c), lambda i: (i, 0))
    full = lambda a: pl.BlockSpec(a.shape, lambda i: (0,) * a.ndim)
    return pl.pallas_call(
        _outproj_kernel,
        grid=(m // tm,),
        in_specs=[row(d)] + [row(a.shape[1]) for a in acts] + [full(w) for w in ws],
        out_specs=row(d),
        out_shape=jax.ShapeDtypeStruct((m, d), F32),
        compiler_params=_params(("arbitrary",)),
        name="outproj",
    )(x, *acts, *ws)


def _t5_bucket_np(rel):
    nb = N_BUCKETS // 2
    max_exact = nb // 2
    n = np.abs(rel)
    nf = np.maximum(n, 1).astype(np.float32)
    large = max_exact + (np.log(nf / np.float32(max_exact)) / np.float32(math.log(MAX_DISTANCE / max_exact))
                         * np.float32(nb - max_exact)).astype(np.int32)
    large = np.minimum(large, nb - 1)
    return np.where(rel > 0, nb, 0) + np.where(n < max_exact, n, large)


FAR_BUCKET = N_BUCKETS // 2 - 1
BIAS_LANES_PER_STEP = 32768
FAR_TILES = ((4, 2), (4, 1), (2, 1), (1, 1))


def _stack_halves(q):
    lane = lax.broadcasted_iota(jnp.int32, q.shape, 1)
    zero = jnp.zeros_like(q)
    return jnp.concatenate([jnp.where(lane < HEAD, q, zero), jnp.where(lane >= HEAD, q, zero)], axis=0)


def _softmax_step(qs, k, v, bias, m_sc, l_sc, acc_sc):
    s = jnp.concatenate([_dot_nt(q_h, k_h) for q_h, k_h in zip(qs, k)], axis=0)
    if bias is not None:
        s = s + bias
    _softmax_update(s, v, m_sc, l_sc, acc_sc)


def _softmax_update(s, v, m_sc, l_sc, acc_sc):
    reps = s.shape[1] // LANES
    m_old = m_sc[...]
    m_new = jnp.maximum(m_old, jnp.max(s, axis=-1, keepdims=True))
    alpha = jnp.exp2(m_old - m_new)
    p = jnp.exp2(s - jnp.concatenate([m_new] * reps, axis=1))
    psum = p[:, :LANES]
    for r in range(1, reps):
        psum = psum + p[:, r * LANES:(r + 1) * LANES]
    l_sc[...] = alpha * l_sc[...] + psum
    p = p.astype(BF16)
    rows = p.shape[0] // len(v)
    pv = jnp.concatenate([_dot(p[h * rows:(h + 1) * rows], v_h) for h, v_h in enumerate(v)], axis=0)
    acc_sc[...] = alpha * acc_sc[...] + pv
    m_sc[...] = m_new


def _diff_finish(lam_ref, za, sg, l, acc, tq):
    lv = lam_ref[...]
    lam = (jnp.exp(jnp.sum(lv[0:1] * lv[1:2], axis=-1, keepdims=True))
           - jnp.exp(jnp.sum(lv[2:3] * lv[3:4], axis=-1, keepdims=True)) + LAMBDA_INIT)
    o = acc / jnp.sum(l, axis=-1, keepdims=True)
    attn = o[:tq] - lam * o[tq:]
    return (_rms(attn, sg) * (1.0 - LAMBDA_INIT)) * _silu(za)


def _da_prompt_kernel(lam_ref, q_ref, k_ref, v_ref, za_ref, tab_ref, sg_ref, o_ref,
                      m_sc, l_sc, acc_sc, *, tq, tk):
    i = pl.program_id(1)
    qs = _stack_halves(q_ref[...])
    m_sc[...] = jnp.full(m_sc.shape, NEG, F32)
    l_sc[...] = jnp.zeros(l_sc.shape, F32)
    acc_sc[...] = jnp.zeros(acc_sc.shape, F32)

    def tile(j, width, bias):
        off = pl.multiple_of(j * tk, tk)
        _softmax_step([qs], [k_ref[pl.ds(off, width), :]], [v_ref[pl.ds(off, width), :]], bias,
                      m_sc, l_sc, acc_sc)

    nfar = jnp.maximum(i - 1, 0)

    done = 0
    for width, reps in FAR_TILES:
        def far(j, c, width=width, reps=reps, done=done):
            for n in range(reps):
                tile(done + width * (reps * j + n), width * tk, None)
            return c

        trips = (nfar - done) // (width * reps)
        lax.fori_loop(0, trips, far, 0)
        done = done + trips * width * reps

    @pl.when(i >= 1)
    def _():
        b = jnp.concatenate([tab_ref[0, 1], tab_ref[0, 0]], axis=1)
        tile(i - 1, 2 * tk, jnp.concatenate([b, b], axis=0))

    @pl.when(i == 0)
    def _():
        b = tab_ref[0, 0]
        tile(0, tk, jnp.concatenate([b, b], axis=0))
    o_ref[...] = _diff_finish(lam_ref, za_ref[...], sg_ref[...], l_sc[...], acc_sc[...], tq).astype(BF16)


def _da_prompt(lamv, qn, knb, vb, za, tab, sg, tq):
    t = qn.shape[0]
    tk = tq
    kern = functools.partial(_da_prompt_kernel, tq=tq, tk=tk)
    return pl.pallas_call(
        kern,
        grid=(DA_HEADS, t // tq),
        in_specs=[
            pl.BlockSpec(lamv.shape, lambda h, i: (0, 0)),
            pl.BlockSpec((tq, LANES), lambda h, i: (i, h)),
            pl.BlockSpec((t, LANES), lambda h, i: (0, h)),
            pl.BlockSpec((t, LANES), lambda h, i: (0, h)),
            pl.BlockSpec((tq, LANES), lambda h, i: (i, h)),
            pl.BlockSpec((1, 2, tq, tk), lambda h, i: (h, 0, 0, 0)),
            pl.BlockSpec(sg.shape, lambda h, i: (0, 0)),
        ],
        out_specs=pl.BlockSpec((tq, LANES), lambda h, i: (i, h)),
        out_shape=jax.ShapeDtypeStruct((t, DA_WIDTH), BF16),
        scratch_shapes=[pltpu.VMEM((2 * tq, LANES), F32)] * 3,
        compiler_params=_params(("arbitrary", "arbitrary")),
        name="diff_attn_prompt",
    )(lamv, qn, knb, vb, za, tab, sg)


def _da_sample_kernel(lam_ref, q_ref, ck_ref, cv_ref, kn_ref, vn_ref, za_ref, tabc_ref, tabn_ref, sg_ref,
                      o_ref, m_sc, l_sc, acc_sc, *, ts):
    cols = lambda h: slice(h * LANES, (h + 1) * LANES)
    heads = range(DA_HEADS)
    qs = [_stack_halves(q_ref[:, cols(h)]) for h in heads]
    m_sc[...] = jnp.full(m_sc.shape, NEG, F32)
    l_sc[...] = jnp.zeros(l_sc.shape, F32)
    acc_sc[...] = jnp.zeros(acc_sc.shape, F32)
    kc = ck_ref[0].astype(BF16)
    vc = cv_ref[0].astype(BF16)
    _softmax_step(qs, [kc[:, cols(h)] for h in heads], [vc[:, cols(h)] for h in heads], tabc_ref[...],
                  m_sc, l_sc, acc_sc)
    pad = jnp.zeros((LANES - ts, LANES), BF16)
    kn = [jnp.concatenate([kn_ref[:, cols(h)].astype(BF16), pad], axis=0) for h in heads]
    vn = [jnp.concatenate([vn_ref[:, cols(h)].astype(BF16), pad], axis=0) for h in heads]
    _softmax_step(qs, kn, vn, tabn_ref[...], m_sc, l_sc, acc_sc)
    l = l_sc[...]
    acc = acc_sc[...]
    za = za_ref[...]
    rows = lambda h: slice(h * 2 * ts, (h + 1) * 2 * ts)
    o_ref[...] = jnp.concatenate(
        [_diff_finish(lam_ref, za[:, cols(h)], sg_ref[...], l[rows(h)], acc[rows(h)], ts) for h in heads],
        axis=1).astype(BF16)


def _da_sample(lamv, qn, ck, cv, kn, vn, za, tabc, tabn, sg, ts):
    nb, past, _ = ck.shape
    kern = functools.partial(_da_sample_kernel, ts=ts)
    stack = lambda tab: jnp.concatenate([tab, tab], axis=1).reshape(DA_HEADS * 2 * ts, tab.shape[-1])
    tabc, tabn = stack(tabc), stack(tabn)
    rows = pl.BlockSpec((ts, DA_WIDTH), lambda b: (b, 0))
    cache = pl.BlockSpec((1, past, DA_WIDTH), lambda b: (b, 0, 0))
    full = lambda a: pl.BlockSpec(a.shape, lambda b: (0,) * a.ndim)
    return pl.pallas_call(
        kern,
        grid=(nb,),
        in_specs=[full(lamv), rows, cache, cache, rows, rows, rows, full(tabc), full(tabn), full(sg)],
        out_specs=rows,
        out_shape=jax.ShapeDtypeStruct((nb * ts, DA_WIDTH), BF16),
        scratch_shapes=[pltpu.VMEM((DA_HEADS * 2 * ts, LANES), F32)] * 3,
        compiler_params=_params(("arbitrary",)),
        name="diff_attn_sample",
    )(lamv, qn, ck, cv, kn, vn, za, tabc, tabn, sg)


def _bias_kernel(idx_ref, bt_ref, o_ref):
    idx = idx_ref[...]
    bt = bt_ref[...]
    bt = bt - bt[:, FAR_BUCKET:FAR_BUCKET + 1]
    onehot = jnp.where(lax.broadcasted_iota(jnp.int32, (N_BUCKETS, idx.shape[1]), 0) == idx, 1.0, 0.0).astype(BF16)
    hi = bt.astype(BF16)
    rest = bt - hi.astype(F32)
    mid = rest.astype(BF16)
    lo = (rest - mid.astype(F32)).astype(BF16)
    val = _dot(hi, onehot) + _dot(mid, onehot) + _dot(lo, onehot)
    o_ref[...] = jnp.where(idx < 0, NEG, val * LOG2E)


def _bias_tables(rel_bias, tq, ts, past):
    def buckets(rel, mask):
        return np.where(mask, _t5_bucket_np(rel), -1).astype(np.int32).reshape(-1)

    r = np.arange(tq)[:, None]
    c = np.arange(tq)[None, :]
    qpos = past + np.arange(ts)[:, None]
    kc = np.arange(past)[None, :]
    kn = past + np.arange(LANES)[None, :]
    parts = [buckets(c - r, (c // CHUNK) <= (r // CHUNK)),
             buckets(c - r - tq, np.ones((tq, tq), bool)),
             buckets(kc - qpos, (kc // CHUNK) <= (qpos // CHUNK)),
             buckets(kn - qpos, ((kn // CHUNK) <= (qpos // CHUNK)) & (kn < past + ts))]
    blk = BIAS_LANES_PER_STEP
    used = sum(p.size for p in parts)
    n = -(-used // blk) * blk
    idx = np.concatenate(parts + [np.full(n - used, -1, np.int32)])
    bt = jnp.zeros((SUBLANES, N_BUCKETS), F32).at[:DA_HEADS].set(rel_bias.T)
    tab = pl.pallas_call(
        _bias_kernel,
        grid=(n // blk,),
        in_specs=[pl.BlockSpec((1, blk), lambda i: (0, i)), pl.BlockSpec(bt.shape, lambda i: (0, 0))],
        out_specs=pl.BlockSpec((SUBLANES, blk), lambda i: (0, i)),
        out_shape=jax.ShapeDtypeStruct((SUBLANES, n), F32),
        compiler_params=_params(("arbitrary",)),
        name="t5_bias_tables",
    )(jnp.asarray(idx).reshape(1, n), bt)[:DA_HEADS]
    o1 = 2 * tq * tq
    o2 = o1 + ts * past
    prompt_tab = tab[:, :o1].reshape(DA_HEADS, 2, tq, tq)
    tabc = tab[:, o1:o2].reshape(DA_HEADS, ts, past)
    tabn = tab[:, o2:used].reshape(DA_HEADS, ts, LANES)
    return prompt_tab, tabc, tabn


def _rwkv_kernel(prw_ref, look_ref, first_ref, zb_ref, s0_ref, mu_ref, w0_ref, a0_ref, wwa_ref,
                 kkw_ref, kaw_ref, rk_ref, lng_ref, lnb_ref, g2_ref, hm_ref, dm_ref, pm_ref, ti_ref, to_ref,
                 ob_ref, sfin_ref,
                 s_sc, om_sc, rho_sc, at_sc, kt_sc, atb_sc, ktb_sc, wb_sc, vr_sc, o_sc, *, tm):
    j = pl.program_id(1)

    @pl.when(j == 0)
    def _():
        s_sc[...] = s0_ref[0]

    g2 = g2_ref[...]

    def head_sums(t):
        tb = t.astype(BF16)
        return jnp.concatenate([_dot(tb[:, c * RW_STATE_LANES:(c + 1) * RW_STATE_LANES], g2)
                                for c in range(RW_STATE_GROUPS)], axis=1)

    x = prw_ref[...]
    row0 = jnp.where(j == 0, first_ref[0], look_ref[7:8, :])
    rows = lax.broadcasted_iota(jnp.int32, x.shape, 0)
    prev = jnp.where(rows == 0, row0, pltpu.roll(x, 1, axis=0))
    mix = x + (prev - x) * mu_ref[...]
    r = mix[:, 0:512]
    kr = mix[:, 512:1024]
    vr = mix[:, 1024:1536]
    la = mix[:, 1536:1664]
    lane = lax.broadcasted_iota(jnp.int32, la.shape, 1)
    la = jnp.where(lane < HEAD, jnp.tanh(la), la)
    hi = la.astype(BF16)
    lo = (la - hi.astype(F32)).astype(BF16)
    wwa = wwa_ref[...]
    lora = _dot(hi, wwa[0]) + _dot(lo, wwa[0]) + _dot(hi, wwa[1])
    wpre = w0_ref[...] + lora[:, 0:512]
    nw = -wpre
    w = -(jnp.maximum(nw, 0.0) + jnp.log(1.0 + jnp.exp(-jnp.abs(nw)))) - 0.5
    ld = -jnp.exp(w)
    a = 1.0 / (1.0 + jnp.exp(-(a0_ref[...] + lora[:, 512:1024])))
    kk = kr * kkw_ref[...]
    nrm = jnp.sqrt(head_sums(kk * kk))
    kk = kk / jnp.maximum(nrm, 1e-12)
    kka = kk * a
    kh = kr * (1.0 + (a - 1.0) * kaw_ref[...])

    def dot3(m, t):
        hi = t.astype(BF16)
        r1 = t - hi.astype(F32)
        mid = r1.astype(BF16)
        lo = (r1 - mid.astype(F32)).astype(BF16)
        return _dot(m, hi) + _dot(m, mid) + _dot(m, lo)

    incl = dot3(ti_ref[...], ld)
    tot = dot3(to_ref[...], ld)
    e_rem = jnp.exp(tot - incl)
    e_inv = jnp.exp(-incl)
    om_sc[...] = jnp.exp(incl - ld) * kk
    rho_sc[...] = jnp.exp(incl) * r
    at_sc[...] = kka * e_inv
    kt_sc[...] = kh * e_inv
    atb_sc[...] = kka * e_rem
    ktb_sc[...] = kh * e_rem
    wb_sc[...] = jnp.exp(tot)
    hm = hm_ref[...]
    vx = (jnp.broadcast_to(vr[:, None, :], (tm, RW_HEADS, RW_WIDTH)) * hm[None]).reshape(tm * RW_HEADS, RW_WIDTH)
    vr_sc[...] = _dot(vx.astype(BF16), pm_ref[...])

    nb = RW_BLOCK
    heads_per_group = RW_HEADS // RW_STATE_GROUPS
    dm = dm_ref[...]

    def head_rows(blk):
        return jnp.concatenate([jnp.broadcast_to(blk[j:j + 1], (RW_HEADS, RW_WIDTH)) * hm for j in range(nb)],
                               axis=0)

    def fold(x):
        return x[:, :RW_STATE_LANES] + x[:, RW_STATE_LANES:]

    def query_rows(b):
        base = pl.multiple_of(b * nb, nb)
        return jnp.concatenate([head_rows(om_sc[pl.ds(base, nb), :]), head_rows(rho_sc[pl.ds(base, nb), :])], axis=0)

    nblk = tm // nb

    def coefficients(b, x):
        base = pl.multiple_of(b * nb, nb)
        rhs = jnp.concatenate([at_sc[pl.ds(base, nb), :], kt_sc[pl.ds(base, nb), :],
                               jnp.zeros((LANES - 2 * nb, RW_WIDTH), F32)], axis=0)
        return _dot_nt(x.astype(BF16), rhs.astype(BF16))

    def block(b, carry):
        s, g, coef = carry
        base = pl.multiple_of(b * nb, nb)
        atb_b, ktb_b, wb_b = (ref[pl.ds(base, nb), :] for ref in (atb_sc, ktb_sc, wb_sc))
        vr_b = vr_sc[pl.ds(pl.multiple_of(b * (nb * RW_HEADS), nb * RW_HEADS), nb * RW_HEADS), :]
        bn = jnp.minimum(b + 1, nblk - 1)
        xn = query_rows(bn)
        coef_next = coefficients(bn, xn)
        rhs2 = jnp.concatenate([fold(head_rows(atb_b)), fold(head_rows(ktb_b))], axis=0).astype(BF16)
        cross = _dot_nt(fold(xn).astype(BF16), rhs2).astype(BF16)
        g_decayed = _dot_nt(fold(xn * wb_b[0:1]).astype(BF16), s.astype(BF16))

        def rows(a, j):
            return a[j * RW_HEADS:(j + 1) * RW_HEADS]

        def cf(j, lane):
            return jnp.broadcast_to(rows(coef, j)[:, lane:lane + 1], (RW_HEADS, LANES))

        sa = []
        for j in range(nb):
            acc = rows(g, j) * rows(dm, j)
            for i in range(j):
                acc = acc + cf(j, nb + i) * rows(vr_b, i)
            for i in range(j):
                acc = acc - cf(j, i) * sa[i]
            sa.append(acc)
        outs = []
        for j in range(nb):
            acc = rows(g, nb + j) * rows(dm, j)
            for i in range(j + 1):
                acc = acc + cf(nb + j, nb + i) * rows(vr_b, i) - cf(nb + j, i) * sa[i]
            outs.append(acc)
        lhs = jnp.concatenate([-t for t in sa] + [vr_b], axis=0).astype(BF16)
        g_next = g_decayed + _dot(cross, lhs)
        ds = lax.dot_general(lhs, rhs2, (((0,), (0,)), ((), ())), preferred_element_type=F32)
        wb = jnp.concatenate(
            [jnp.broadcast_to(wb_b[0:1, c * RW_STATE_LANES:(c + 1) * RW_STATE_LANES], (HEAD, RW_STATE_LANES))
             for c in range(RW_STATE_GROUPS)], axis=0)
        for h in range(RW_HEADS):
            o_sc[h, pl.ds(base, nb), :] = jnp.concatenate([t[h:h + 1] for t in outs], axis=0)
        return s * wb + ds, g_next, coef_next

    s0 = s_sc[...]
    x0 = query_rows(0)
    g0 = _dot_nt(fold(x0).astype(BF16), s0.astype(BF16))
    s_sc[...] = lax.fori_loop(0, nblk, block, (s0, g0, coefficients(0, x0)), unroll=min(RW_UNROLL, nblk))[0]

    o = jnp.concatenate([o_sc[c * heads_per_group + hh][:, c * HEAD:(c + 1) * HEAD]
                         for c in range(RW_STATE_GROUPS) for hh in range(heads_per_group)], axis=1)
    mean = head_sums(o) * (1.0 / HEAD)
    d = o - mean
    var = head_sums(d * d) * (1.0 / HEAD)
    y = d * lax.rsqrt(var + RW_GN_EPS) * lng_ref[...] + lnb_ref[...]
    bonus = head_sums(r * kh * rk_ref[...]) * vr
    ob_ref[...] = ((y + bonus) * _silu(zb_ref[...])).astype(BF16)

    @pl.when(j == pl.num_programs(1) - 1)
    def _():
        sfin_ref[0] = s_sc[...]


def _block_sum_matrices(tm):
    t = np.arange(tm)
    same = (t[:, None] // RW_BLOCK) == (t[None, :] // RW_BLOCK)
    incl = same & (t[None, :] <= t[:, None])
    return [jnp.asarray(incl.astype(np.float32), dtype=BF16), jnp.asarray(same.astype(np.float32), dtype=BF16)]


def _rwkv(prw, first, zb, s0, p, seq_len, tm):
    m = prw.shape[0]
    nseq = m // seq_len
    ntile = seq_len // tm
    kern = functools.partial(_rwkv_kernel, tm=tm)
    full = lambda a: pl.BlockSpec(a.shape, lambda b, j: (0,) * a.ndim)
    rows = lambda c: pl.BlockSpec((tm, c), lambda b, j: (b * ntile + j, 0))
    look = pl.BlockSpec((8, RW_SHIFT_COLS), lambda b, j: (jnp.maximum((b * ntile + j) * (tm // 8) - 1, 0), 0))
    consts = [p["mu"], p["w0"], p["a0"], p["wwa"], p["kkw"], p["kaw"], p["rk"], p["lng"], p["lnb"],
              p["g2"], p["hm"], p["dm"], p["pm"]] + _block_sum_matrices(tm)
    return pl.pallas_call(
        kern,
        grid=(nseq, ntile),
        in_specs=[rows(RW_SHIFT_COLS), look,
                  pl.BlockSpec((1, 1, RW_SHIFT_COLS), lambda b, j: (b, 0, 0)),
                  rows(RW_WIDTH),
                  pl.BlockSpec((1,) + RW_STATE_SHAPE, lambda b, j: (b, 0, 0))] + [full(c) for c in consts],
        out_specs=[rows(RW_WIDTH), pl.BlockSpec((1,) + RW_STATE_SHAPE, lambda b, j: (b, 0, 0))],
        out_shape=[jax.ShapeDtypeStruct((m, RW_WIDTH), BF16), jax.ShapeDtypeStruct((nseq,) + RW_STATE_SHAPE, F32)],
        scratch_shapes=([pltpu.VMEM(RW_STATE_SHAPE, F32)] + [pltpu.VMEM((tm, RW_WIDTH), F32)] * 7
                        + [pltpu.VMEM((tm * RW_HEADS, RW_STATE_SHAPE[0]), F32),
                           pltpu.VMEM((RW_HEADS, tm, RW_STATE_SHAPE[0]), F32)]),
        compiler_params=_params(("arbitrary", "arbitrary")),
        name="rwkv7",
    )(prw, prw, first, zb, s0, *consts)


def _state_to_rows(s):
    b = s.shape[0]
    per = RW_HEADS // RW_STATE_GROUPS
    return s.reshape(b, RW_STATE_GROUPS, per, HEAD, HEAD).transpose(0, 1, 3, 2, 4).reshape((b,) + RW_STATE_SHAPE)


def _rows_to_state(s):
    b = s.shape[0]
    per = RW_HEADS // RW_STATE_GROUPS
    return s.reshape(b, RW_STATE_GROUPS, HEAD, per, HEAD).transpose(0, 1, 3, 2, 4).reshape(b, RW_HEADS, HEAD, HEAD)


SKIP_LOG = -104.0
SB_RECENT_KEYS = 256


def _sb_tile(qs, k, v, u, c, mask):
    tk = k[0].shape[0]
    rows = qs[0].shape[0]
    z = jnp.concatenate([_dot_nt(q_p, k_p) for q_p, k_p in zip(qs, k)], axis=0)
    nz = -z
    lg = jnp.minimum(nz, 0.0) - jnp.log(1.0 + jnp.exp(jnp.minimum(z, nz)))
    if mask is not None:
        lg = jnp.where(mask, lg, 0.0)
    it = _dot(lg.astype(BF16), u)
    a = jnp.exp(z + it + jnp.concatenate([c] * (tk // LANES), axis=1))
    if mask is not None:
        a = jnp.where(mask, a, 0.0)
    a = a.astype(BF16)
    da = jnp.concatenate([_dot(a[p * rows:(p + 1) * rows], v_p) for p, v_p in enumerate(v)], axis=0)
    return da, c + jnp.broadcast_to(it[:, 0:1], c.shape)


def _sb_finish(acc, z):
    t = z.shape[0]
    lane = lax.broadcasted_iota(jnp.int32, z.shape, 1)
    return jnp.where(lane < HEAD, acc[:t], acc[t:]) * _silu(z)


def _sb_prompt_kernel(q_ref, k_ref, v_ref, z_ref, u_ref, o_ref, acc_sc, c_sc, *, tq):
    i = pl.program_id(1)
    tk = u_ref.shape[0]
    r = tq // tk
    qs = _stack_halves(q_ref[...])
    u = u_ref[...]
    acc_sc[...] = jnp.zeros(acc_sc.shape, F32)
    c_sc[...] = jnp.zeros(c_sc.shape, F32)
    def tile(j, lo, hi, masked):
        n = hi - lo
        halves = lambda x: jnp.concatenate([x[lo:hi], x[tq + lo:tq + hi]], axis=0)
        off = pl.multiple_of(j * tk, tk)
        mask = None
        if masked:
            qrow = lax.broadcasted_iota(jnp.int32, (2 * n, tk), 0) % n
            col = lax.broadcasted_iota(jnp.int32, (2 * n, tk), 1)
            mask = col < qrow
        da, c = _sb_tile([halves(qs)], [k_ref[pl.ds(off, tk), :]], [v_ref[pl.ds(off, tk), :]], u,
                         halves(c_sc[...]), mask)
        for h in range(2):
            acc_sc[h * tq + lo:h * tq + hi, :] += da[h * n:(h + 1) * n]
            c_sc[h * tq + lo:h * tq + hi, :] = c[h * n:(h + 1) * n]

    for jj in reversed(range(r)):
        tile(i * r + jj, jj * tk, tq, True)

    def body(st):
        tile(i * r - 1 - st[0], 0, tq, False)
        return st[0] + 1, jnp.max(c_sc[...])

    lax.while_loop(lambda st: jnp.logical_and(st[0] < i * r, st[1] > SKIP_LOG), body,
                   (jnp.int32(0), jnp.max(c_sc[...])))
    o_ref[...] = _sb_finish(acc_sc[...], z_ref[...]).astype(BF16)


def _sb_prompt(qb, kb, vb, z, u, tq):
    t = qb.shape[0]
    kern = functools.partial(_sb_prompt_kernel, tq=tq)
    rows = pl.BlockSpec((tq, LANES), lambda h, i: (i, h))
    whole = pl.BlockSpec((t, LANES), lambda h, i: (0, h))
    return pl.pallas_call(
        kern,
        grid=(SB_HEADS // 2, t // tq),
        in_specs=[rows, whole, whole, rows, pl.BlockSpec(u.shape, lambda h, i: (0, 0))],
        out_specs=rows,
        out_shape=jax.ShapeDtypeStruct((t, SB_WIDTH), BF16),
        scratch_shapes=[pltpu.VMEM((2 * tq, LANES), F32)] * 2,
        compiler_params=_params(("arbitrary", "arbitrary")),
        name="stick_breaking_prompt",
    )(qb, kb, vb, z, u)


def _sb_sample_kernel(q_ref, ck_ref, cv_ref, kn_ref, vn_ref, z_ref, u_ref, o_ref, left_ref, acc_sc, c_sc, *, ts):
    tk = u_ref.shape[0]
    ntile = ck_ref.shape[1] // tk
    npair = SB_HEADS // 2
    cols = lambda p: slice(p * LANES, (p + 1) * LANES)
    qs = [_stack_halves(q_ref[:, cols(p)]) for p in range(npair)]
    u = u_ref[...]
    pad = jnp.zeros((tk - ts, LANES), BF16)
    kn = [jnp.concatenate([kn_ref[:, cols(p)], pad], axis=0) for p in range(npair)]
    vn = [jnp.concatenate([vn_ref[:, cols(p)], pad], axis=0) for p in range(npair)]
    rows = npair * 2 * ts
    qrow = lax.broadcasted_iota(jnp.int32, (rows, tk), 0) % ts
    col = lax.broadcasted_iota(jnp.int32, (rows, tk), 1)
    acc, c = _sb_tile(qs, kn, vn, u, jnp.zeros((rows, LANES), F32), col < qrow)
    acc_sc[...] = acc
    c_sc[...] = c

    def body(st):
        off = pl.multiple_of((ntile - 1 - st[0]) * tk, tk)
        kc = ck_ref[0, pl.ds(off, tk), :].astype(BF16)
        vc = cv_ref[0, pl.ds(off, tk), :].astype(BF16)
        da, c = _sb_tile(qs, [kc[:, cols(p)] for p in range(npair)], [vc[:, cols(p)] for p in range(npair)],
                         u, c_sc[...], None)
        acc_sc[...] += da
        c_sc[...] = c
        return st[0] + 1, jnp.max(c)

    lax.while_loop(lambda st: jnp.logical_and(st[0] < ntile, st[1] > SKIP_LOG), body,
                   (jnp.int32(0), jnp.max(c)))
    left_ref[0] = jnp.broadcast_to(jnp.max(c_sc[...], axis=0, keepdims=True), (SUBLANES, LANES))
    acc = acc_sc[...]
    z = z_ref[...]
    o_ref[...] = jnp.concatenate(
        [_sb_finish(acc[p * 2 * ts:(p + 1) * 2 * ts], z[:, cols(p)]) for p in range(npair)], axis=1).astype(BF16)


def _sb_sample(qb, ck, cv, kb, vb, z, u, ts):
    nb, past, _ = ck.shape
    kern = functools.partial(_sb_sample_kernel, ts=ts)
    rows = pl.BlockSpec((ts, SB_WIDTH), lambda b: (b, 0))
    cache = pl.BlockSpec((1, past, SB_WIDTH), lambda b: (b, 0, 0))
    return pl.pallas_call(
        kern,
        grid=(nb,),
        in_specs=[rows, cache, cache, rows, rows, rows, pl.BlockSpec(u.shape, lambda b: (0, 0))],
        out_specs=[rows, pl.BlockSpec((1, SUBLANES, LANES), lambda b: (b, 0, 0))],
        out_shape=[jax.ShapeDtypeStruct((nb * ts, SB_WIDTH), BF16),
                   jax.ShapeDtypeStruct((nb, SUBLANES, LANES), F32)],
        scratch_shapes=[pltpu.VMEM((SB_HEADS * ts, LANES), F32)] * 2,
        compiler_params=_params(("arbitrary",)),
        name="stick_breaking_sample",
    )(qb, ck, cv, kb, vb, z, u)


def _sb_sample_two_phase(qb, cache_k, cache_v, kb, vb, z, u, ts):
    nb, past = cache_k.shape[:2]
    dense = lambda c: c.reshape(nb, c.shape[1], SB_WIDTH)
    recent = min(past, SB_RECENT_KEYS)
    out, left = _sb_sample(qb, dense(cache_k[:, past - recent:]), dense(cache_v[:, past - recent:]), kb, vb, z, u, ts)
    if recent == past:
        return out
    return lax.cond(jnp.max(left) > SKIP_LOG,
                    lambda: _sb_sample(qb, dense(cache_k), dense(cache_v), kb, vb, z, u, ts)[0],
                    lambda: out)


def _suffix_matrix(tk):
    j = np.arange(tk)[:, None]
    s = np.arange(tk)[None, :]
    return jnp.asarray((j >= s).astype(np.float32), dtype=BF16)


def _row_tile(m):
    for tm in PROJ_TILES:
        if m % tm == 0:
            return tm
    return m


def kernel(x_prompt, x_sample, cache_l0_k, cache_l0_v, state_l0_shift, state_l0_wkv, cache_l1_k, cache_l1_v, rel_bias, norm_l0, w_in_l0, w_out_l0, da_q_norm, da_k_norm, da_lambda_q1, da_lambda_k1, da_lambda_q2, da_lambda_k2, da_subln, rw_mu, rw_w0, rw_w_up, rw_a0, rw_a_up, rw_k_k, rw_k_a, rw_r_k, rw_lnx_g, rw_lnx_b, norm_l1, w_in_l1, w_out_l1):
    bp, t, d = x_prompt.shape
    nb, ts, _ = x_sample.shape
    past = cache_l0_k.shape[1]
    assert bp == 1 and ts <= LANES and ts % RW_BLOCK == 0
    assert t % max(SB_TILE, DA_TILE, RW_TILE) == 0 and past % LANES == 0 and DA_TILE >= MAX_DISTANCE

    row = lambda v: v.reshape(1, -1).astype(F32)
    w_in0 = w_in_l0.astype(BF16)
    w_in1 = w_in_l1.astype(BF16)
    w_out0a = w_out_l0[:DA_WIDTH].astype(BF16)
    w_out0b = w_out_l0[DA_WIDTH:].astype(BF16)
    w_out1 = w_out_l1.astype(BF16)
    qg = row(jnp.tile(da_q_norm, 512 // HEAD))
    kg = row(jnp.tile(da_k_norm, 512 // HEAD))
    lamv = jnp.stack([da_lambda_q1, da_lambda_k1, da_lambda_q2, da_lambda_k2]).astype(F32)
    sg = row(da_subln)
    zeros = jnp.zeros((HEAD, RW_WIDTH), F32)
    wwa_f = jnp.concatenate([jnp.concatenate([rw_w_up, zeros], axis=1),
                             jnp.concatenate([zeros, rw_a_up], axis=1)], axis=0)
    wwa_hi = wwa_f.astype(BF16)
    wwa = jnp.stack([wwa_hi, (wwa_f - wwa_hi.astype(F32)).astype(BF16)])
    heads = np.arange(RW_HEADS)
    per_group = RW_HEADS // RW_STATE_GROUPS
    chan = np.arange(RW_WIDTH)
    out_lane = np.arange(RW_STATE_SHAPE[0])
    hm = np.equal(heads[:, None], chan[None, :] // HEAD)
    dm = np.equal(heads[:, None] // per_group, out_lane[None, :] // HEAD)
    pm = (chan[:, None] // HEAD // per_group == out_lane[None, :] // HEAD) & (chan[:, None] % HEAD == out_lane[None, :] % HEAD)
    rw = dict(mu=row(rw_mu), w0=row(rw_w0), a0=row(rw_a0), wwa=wwa, kkw=row(rw_k_k), kaw=row(rw_k_a),
              rk=row(rw_r_k), lng=row(rw_lnx_g), lnb=row(rw_lnx_b), g2=_group_matrix(RW_STATE_LANES, 1.0),
              hm=jnp.asarray(hm.astype(np.float32)),
              dm=jnp.asarray(np.tile(dm.astype(np.float32), (RW_BLOCK, 1))),
              pm=jnp.asarray(pm.astype(np.float32), dtype=BF16))
    prompt_tab, tabc, tabn = _bias_tables(rel_bias.astype(F32), DA_TILE, ts, past)
    u_prompt = _suffix_matrix(2 * LANES)
    u_sample = _suffix_matrix(LANES)

    def layer0(x, first, s0, seq_len, attend):
        m = x.shape[0]
        tm = _row_tile(m)
        qn, kn, knb, v, vb, za, prw, zb = _inproj0(x, row(norm_l0), w_in0, qg, kg, tm)
        oa = attend(qn, kn, knb, v, vb, za)
        ob, sfin = _rwkv(prw, first, zb, _state_to_rows(s0), rw, seq_len, min(seq_len, RW_TILE))
        y = _outproj(x, [oa, ob], [w_out0a, w_out0b], tm)
        shift = prw.reshape(m // seq_len, seq_len, RW_SHIFT_COLS)[:, -1:]
        return y, kn, v, shift, _rows_to_state(sfin)

    def layer1(x, attend):
        m = x.shape[0]
        tm = _row_tile(m)
        qb, k, kb, v, vb, z = _inproj1(x, row(norm_l1), w_in1, tm)
        o = attend(qb, kb, vb, z)
        return _outproj(x, [o], [w_out1], tm), k, v

    xp = x_prompt.reshape(t, d)
    yp, k0p, v0p, shp, wkvp = layer0(
        xp, jnp.zeros((1, 1, RW_SHIFT_COLS), F32), jnp.zeros((1, RW_HEADS, HEAD, HEAD), F32), t,
        lambda qn, kn, knb, v, vb, za: _da_prompt(lamv, qn, knb, vb, za, prompt_tab, sg, DA_TILE))
    yp, k1p, v1p = layer1(yp, lambda qb, kb, vb, z: _sb_prompt(qb, kb, vb, z, u_prompt, SB_TILE))

    xs = x_sample.reshape(nb * ts, d)
    ck0 = cache_l0_k.reshape(nb, past, DA_WIDTH)
    cv0 = cache_l0_v.reshape(nb, past, DA_WIDTH)
    ys, k0s, v0s, shs, wkvs = layer0(
        xs, state_l0_shift, state_l0_wkv, ts,
        lambda qn, kn, knb, v, vb, za: _da_sample(lamv, qn, ck0, cv0, knb, vb, za, tabc, tabn, sg, ts))
    ys, k1s, v1s = layer1(ys, lambda qb, kb, vb, z: _sb_sample_two_phase(
        qb, cache_l1_k, cache_l1_v, kb, vb, z, u_sample, ts))

    return (yp.reshape(1, t, d), ys.reshape(nb, ts, d),
            k0p.reshape(1, t, DA_HEADS, LANES), v0p.reshape(1, t, DA_HEADS, LANES), shp, wkvp,
            k1p.reshape(1, t, SB_HEADS, HEAD), v1p.reshape(1, t, SB_HEADS, HEAD),
            k0s.reshape(nb, ts, DA_HEADS, LANES), v0s.reshape(nb, ts, DA_HEADS, LANES), shs, wkvs,
            k1s.reshape(nb, ts, SB_HEADS, HEAD), v1s.reshape(nb, ts, SB_HEADS, HEAD))
```

```python
import functools
import math

import numpy as np
import jax
import jax.numpy as jnp
from jax import lax
from jax.experimental import pallas as pl
from jax.experimental.pallas import tpu as pltpu

F32 = jnp.float32
BF16 = jnp.bfloat16

EPS = 1e-6
NEG = -1e30
CHUNK = 64
LANES = 128
SUBLANES = 8
HEAD = 64
DA_HEADS = 4
DA_WIDTH = 512
RW_WIDTH = 512
RW_HEADS = 8
RW_SHIFT_COLS = 3 * RW_WIDTH + 128
RW_GN_EPS = 64e-5
RW_STATE_LANES = 256
RW_STATE_GROUPS = RW_WIDTH // RW_STATE_LANES
RW_STATE_SHAPE = (RW_STATE_GROUPS * HEAD, RW_STATE_LANES)
RW_BLOCK = 8
RW_UNROLL = 4
SB_HEADS = 16
SB_WIDTH = 1024
N_BUCKETS = 32
MAX_DISTANCE = 128
LAMBDA_INIT = 0.8 - 0.6 * math.exp(-0.3 * 0)
LOG2E = math.log2(math.e)
VMEM_LIMIT = 56 * 1024 * 1024
PROJ_TILES = (512, 256)
DA_TILE = 512
SB_TILE = 512
RW_TILE = 256


def _params(sem):
    return pltpu.CompilerParams(dimension_semantics=sem, vmem_limit_bytes=VMEM_LIMIT)


def _dot(a, b):
    return jnp.dot(a, b, preferred_element_type=F32)


def _dot_nt(a, b):
    return lax.dot_general(a, b, (((1,), (1,)), ((), ())), preferred_element_type=F32)


def _silu(z):
    return z / (1.0 + jnp.exp(-z))


def _rms(x, g):
    return x * lax.rsqrt(jnp.mean(x * x, axis=-1, keepdims=True) + EPS) * g


def _group_matrix(n, scale):
    idx = np.arange(n) // HEAD
    return jnp.asarray((idx[:, None] == idx[None, :]).astype(np.float32) * scale, dtype=BF16)


def _inproj0_kernel(x_ref, g_ref, w_ref, qg_ref, kg_ref, avg_ref,
                    qn_ref, kn_ref, knb_ref, v_ref, vb_ref, za_ref, prw_ref, zb_ref):
    y = _dot(_rms(x_ref[...], g_ref[...]).astype(BF16), w_ref[...])

    def head_norm(t, g):
        sq = (t * t).astype(BF16)
        half = avg_ref.shape[0]
        ms = jnp.concatenate([_dot(sq[:, c:c + half], avg_ref[...]) for c in range(0, t.shape[1], half)], axis=1)
        return t * lax.rsqrt(ms + EPS) * g

    qn_ref[...] = (head_norm(y[:, 0:512], qg_ref[...]) * (HEAD ** -0.5 * LOG2E)).astype(BF16)
    kn = head_norm(y[:, 512:1024], kg_ref[...])
    knb_ref[...] = kn.astype(BF16)
    v = y[:, 1024:1536]
    vb_ref[...] = v.astype(BF16)
    for h in range(DA_HEADS):
        kn_ref[:, h, :] = kn[:, h * LANES:(h + 1) * LANES]
        v_ref[:, h, :] = v[:, h * LANES:(h + 1) * LANES]
    za_ref[...] = y[:, 1536:2048]
    prw_ref[...] = y[:, 2048:2048 + RW_SHIFT_COLS]
    zb_ref[...] = y[:, 2048 + RW_SHIFT_COLS:]


def _inproj0(x, g, w, qg, kg, tm):
    m, d = x.shape
    n = w.shape[1]
    row = lambda c: pl.BlockSpec((tm, c), lambda i: (i, 0))
    full = lambda a: pl.BlockSpec(a.shape, lambda i: (0,) * a.ndim)
    avg = _group_matrix(2 * LANES, 1.0 / HEAD)
    widths = [(512, BF16), (None, F32), (512, BF16), (None, F32), (512, BF16), (512, F32),
              (RW_SHIFT_COLS, F32), (512, F32)]
    heads = (DA_HEADS, LANES)
    return pl.pallas_call(
        _inproj0_kernel,
        grid=(m // tm,),
        in_specs=[row(d), full(g), full(w), full(qg), full(kg), full(avg)],
        out_specs=[row(c) if c else pl.BlockSpec((tm,) + heads, lambda i: (i, 0, 0)) for c, _ in widths],
        out_shape=[jax.ShapeDtypeStruct((m, c) if c else (m,) + heads, dt) for c, dt in widths],
        compiler_params=_params(("arbitrary",)),
        name="inproj0",
    )(x, g, w, qg, kg, avg)


def _inproj1_kernel(x_ref, g_ref, w_ref, qb_ref, k_ref, kb_ref, v_ref, vb_ref, z_ref):
    y = _dot(_rms(x_ref[...], g_ref[...]).astype(BF16), w_ref[...])
    qb_ref[...] = (y[:, 0:1024] * (HEAD ** -0.5)).astype(BF16)
    k = y[:, 1024:2048]
    k_ref[...] = k
    kb_ref[...] = k.astype(BF16)
    v = y[:, 2048:3072]
    v_ref[...] = v
    vb_ref[...] = v.astype(BF16)
    z_ref[...] = y[:, 3072:4096]


def _inproj1(x, g, w, tm):
    m, d = x.shape
    row = lambda c: pl.BlockSpec((tm, c), lambda i: (i, 0))
    full = lambda a: pl.BlockSpec(a.shape, lambda i: (0,) * a.ndim)
    dts = [BF16, F32, BF16, F32, BF16, F32]
    return pl.pallas_call(
        _inproj1_kernel,
        grid=(m // tm,),
        in_specs=[row(d), full(g), full(w)],
        out_specs=[row(1024) for _ in dts],
        out_shape=[jax.ShapeDtypeStruct((m, 1024), dt) for dt in dts],
        compiler_params=_params(("arbitrary",)),
        name="inproj1",
    )(x, g, w)


def _outproj_kernel(*refs):
    n = (len(refs) - 2) // 2
    x_ref, o_ref = refs[0], refs[-1]
    acc = x_ref[...]
    for a_ref, w_ref in zip(refs[1:1 + n], refs[1 + n:1 + 2 * n]):
        acc = acc + _dot(a_ref[...], w_ref[...])
    o_ref[...] = acc


def _outproj(x, acts, ws, tm):
    m, d = x.shape
    row = lambda c: pl.BlockSpec((tm, c), lambda i: (i, 0))
    full = lambda a: pl.BlockSpec(a.shape, lambda i: (0,) * a.ndim)
    return pl.pallas_call(
        _outproj_kernel,
        grid=(m // tm,),
        in_specs=[row(d)] + [row(a.shape[1]) for a in acts] + [full(w) for w in ws],
        out_specs=row(d),
        out_shape=jax.ShapeDtypeStruct((m, d), F32),
        compiler_params=_params(("arbitrary",)),
        name="outproj",
    )(x, *acts, *ws)


def _t5_bucket_np(rel):
    nb = N_BUCKETS // 2
    max_exact = nb // 2
    n = np.abs(rel)
    nf = np.maximum(n, 1).astype(np.float32)
    large = max_exact + (np.log(nf / np.float32(max_exact)) / np.float32(math.log(MAX_DISTANCE / max_exact))
                         * np.float32(nb - max_exact)).astype(np.int32)
    large = np.minimum(large, nb - 1)
    return np.where(rel > 0, nb, 0) + np.where(n < max_exact, n, large)


FAR_BUCKET = N_BUCKETS // 2 - 1
BIAS_LANES_PER_STEP = 32768
FAR_TILES = ((4, 2), (4, 1), (2, 1), (1, 1))


def _stack_halves(q):
    lane = lax.broadcasted_iota(jnp.int32, q.shape, 1)
    zero = jnp.zeros_like(q)
    return jnp.concatenate([jnp.where(lane < HEAD, q, zero), jnp.where(lane >= HEAD, q, zero)], axis=0)


def _softmax_step(qs, k, v, bias, m_sc, l_sc, acc_sc):
    s = jnp.concatenate([_dot_nt(q_h, k_h) for q_h, k_h in zip(qs, k)], axis=0)
    if bias is not None:
        s = s + bias
    _softmax_update(s, v, m_sc, l_sc, acc_sc)


def _softmax_update(s, v, m_sc, l_sc, acc_sc):
    reps = s.shape[1] // LANES
    m_old = m_sc[...]
    m_new = jnp.maximum(m_old, jnp.max(s, axis=-1, keepdims=True))
    alpha = jnp.exp2(m_old - m_new)
    p = jnp.exp2(s - jnp.concatenate([m_new] * reps, axis=1))
    psum = p[:, :LANES]
    for r in range(1, reps):
        psum = psum + p[:, r * LANES:(r + 1) * LANES]
    l_sc[...] = alpha * l_sc[...] + psum
    p = p.astype(BF16)
    rows = p.shape[0] // len(v)
    pv = jnp.concatenate([_dot(p[h * rows:(h + 1) * rows], v_h) for h, v_h in enumerate(v)], axis=0)
    acc_sc[...] = alpha * acc_sc[...] + pv
    m_sc[...] = m_new


def _diff_finish(lam_ref, za, sg, l, acc, tq):
    lv = lam_ref[...]
    lam = (jnp.exp(jnp.sum(lv[0:1] * lv[1:2], axis=-1, keepdims=True))
           - jnp.exp(jnp.sum(lv[2:3] * lv[3:4], axis=-1, keepdims=True)) + LAMBDA_INIT)
    o = acc / jnp.sum(l, axis=-1, keepdims=True)
    attn = o[:tq] - lam * o[tq:]
    return (_rms(attn, sg) * (1.0 - LAMBDA_INIT)) * _silu(za)


def _da_prompt_kernel(lam_ref, q_ref, k_ref, v_ref, za_ref, tab_ref, sg_ref, o_ref,
                      m_sc, l_sc, acc_sc, *, tq, tk):
    i = pl.program_id(1)
    qs = _stack_halves(q_ref[...])
    m_sc[...] = jnp.full(m_sc.shape, NEG, F32)
    l_sc[...] = jnp.zeros(l_sc.shape, F32)
    acc_sc[...] = jnp.zeros(acc_sc.shape, F32)

    def tile(j, width, bias):
        off = pl.multiple_of(j * tk, tk)
        _softmax_step([qs], [k_ref[pl.ds(off, width), :]], [v_ref[pl.ds(off, width), :]], bias,
                      m_sc, l_sc, acc_sc)

    nfar = jnp.maximum(i - 1, 0)

    done = 0
    for width, reps in FAR_TILES:
        def far(j, c, width=width, reps=reps, done=done):
            for n in range(reps):
                tile(done + width * (reps * j + n), width * tk, None)
            return c

        trips = (nfar - done) // (width * reps)
        lax.fori_loop(0, trips, far, 0)
        done = done + trips * width * reps

    @pl.when(i >= 1)
    def _():
        b = jnp.concatenate([tab_ref[0, 1], tab_ref[0, 0]], axis=1)
        tile(i - 1, 2 * tk, jnp.concatenate([b, b], axis=0))

    @pl.when(i == 0)
    def _():
        b = tab_ref[0, 0]
        tile(0, tk, jnp.concatenate([b, b], axis=0))
    o_ref[...] = _diff_finish(lam_ref, za_ref[...], sg_ref[...], l_sc[...], acc_sc[...], tq).astype(BF16)


def _da_prompt(lamv, qn, knb, vb, za, tab, sg, tq):
    t = qn.shape[0]
    tk = tq
    kern = functools.partial(_da_prompt_kernel, tq=tq, tk=tk)
    return pl.pallas_call(
        kern,
        grid=(DA_HEADS, t // tq),
        in_specs=[
            pl.BlockSpec(lamv.shape, lambda h, i: (0, 0)),
            pl.BlockSpec((tq, LANES), lambda h, i: (i, h)),
            pl.BlockSpec((t, LANES), lambda h, i: (0, h)),
            pl.BlockSpec((t, LANES), lambda h, i: (0, h)),
            pl.BlockSpec((tq, LANES), lambda h, i: (i, h)),
            pl.BlockSpec((1, 2, tq, tk), lambda h, i: (h, 0, 0, 0)),
            pl.BlockSpec(sg.shape, lambda h, i: (0, 0)),
        ],
        out_specs=pl.BlockSpec((tq, LANES), lambda h, i: (i, h)),
        out_shape=jax.ShapeDtypeStruct((t, DA_WIDTH), BF16),
        scratch_shapes=[pltpu.VMEM((2 * tq, LANES), F32)] * 3,
        compiler_params=_params(("arbitrary", "arbitrary")),
        name="diff_attn_prompt",
    )(lamv, qn, knb, vb, za, tab, sg)


def _da_sample_kernel(lam_ref, q_ref, ck_ref, cv_ref, kn_ref, vn_ref, za_ref, tabc_ref, tabn_ref, sg_ref,
                      o_ref, m_sc, l_sc, acc_sc, *, ts):
    cols = lambda h: slice(h * LANES, (h + 1) * LANES)
    heads = range(DA_HEADS)
    qs = [_stack_halves(q_ref[:, cols(h)]) for h in heads]
    m_sc[...] = jnp.full(m_sc.shape, NEG, F32)
    l_sc[...] = jnp.zeros(l_sc.shape, F32)
    acc_sc[...] = jnp.zeros(acc_sc.shape, F32)
    kc = ck_ref[0].astype(BF16)
    vc = cv_ref[0].astype(BF16)
    _softmax_step(qs, [kc[:, cols(h)] for h in heads], [vc[:, cols(h)] for h in heads], tabc_ref[...],
                  m_sc, l_sc, acc_sc)
    pad = jnp.zeros((LANES - ts, LANES), BF16)
    kn = [jnp.concatenate([kn_ref[:, cols(h)].astype(BF16), pad], axis=0) for h in heads]
    vn = [jnp.concatenate([vn_ref[:, cols(h)].astype(BF16), pad], axis=0) for h in heads]
    _softmax_step(qs, kn, vn, tabn_ref[...], m_sc, l_sc, acc_sc)
    l = l_sc[...]
    acc = acc_sc[...]
    za = za_ref[...]
    rows = lambda h: slice(h * 2 * ts, (h + 1) * 2 * ts)
    o_ref[...] = jnp.concatenate(
        [_diff_finish(lam_ref, za[:, cols(h)], sg_ref[...], l[rows(h)], acc[rows(h)], ts) for h in heads],
        axis=1).astype(BF16)


def _da_sample(lamv, qn, ck, cv, kn, vn, za, tabc, tabn, sg, ts):
    nb, past, _ = ck.shape
    kern = functools.partial(_da_sample_kernel, ts=ts)
    stack = lambda tab: jnp.concatenate([tab, tab], axis=1).reshape(DA_HEADS * 2 * ts, tab.shape[-1])
    tabc, tabn = stack(tabc), stack(tabn)
    rows = pl.BlockSpec((ts, DA_WIDTH), lambda b: (b, 0))
    cache = pl.BlockSpec((1, past, DA_WIDTH), lambda b: (b, 0, 0))
    full = lambda a: pl.BlockSpec(a.shape, lambda b: (0,) * a.ndim)
    return pl.pallas_call(
        kern,
        grid=(nb,),
        in_specs=[full(lamv), rows, cache, cache, rows, rows, rows, full(tabc), full(tabn), full(sg)],
        out_specs=rows,
        out_shape=jax.ShapeDtypeStruct((nb * ts, DA_WIDTH), BF16),
        scratch_shapes=[pltpu.VMEM((DA_HEADS * 2 * ts, LANES), F32)] * 3,
        compiler_params=_params(("arbitrary",)),
        name="diff_attn_sample",
    )(lamv, qn, ck, cv, kn, vn, za, tabc, tabn, sg)


def _bias_kernel(idx_ref, bt_ref, o_ref):
    idx = idx_ref[...]
    bt = bt_ref[...]
    bt = bt - bt[:, FAR_BUCKET:FAR_BUCKET + 1]
    onehot = jnp.where(lax.broadcasted_iota(jnp.int32, (N_BUCKETS, idx.shape[1]), 0) == idx, 1.0, 0.0).astype(BF16)
    hi = bt.astype(BF16)
    rest = bt - hi.astype(F32)
    mid = rest.astype(BF16)
    lo = (rest - mid.astype(F32)).astype(BF16)
    val = _dot(hi, onehot) + _dot(mid, onehot) + _dot(lo, onehot)
    o_ref[...] = jnp.where(idx < 0, NEG, val * LOG2E)


def _bias_tables(rel_bias, tq, ts, past):
    def buckets(rel, mask):
        return np.where(mask, _t5_bucket_np(rel), -1).astype(np.int32).reshape(-1)

    r = np.arange(tq)[:, None]
    c = np.arange(tq)[None, :]
    qpos = past + np.arange(ts)[:, None]
    kc = np.arange(past)[None, :]
    kn = past + np.arange(LANES)[None, :]
    parts = [buckets(c - r, (c // CHUNK) <= (r // CHUNK)),
             buckets(c - r - tq, np.ones((tq, tq), bool)),
             buckets(kc - qpos, (kc // CHUNK) <= (qpos // CHUNK)),
             buckets(kn - qpos, ((kn // CHUNK) <= (qpos // CHUNK)) & (kn < past + ts))]
    blk = BIAS_LANES_PER_STEP
    used = sum(p.size for p in parts)
    n = -(-used // blk) * blk
    idx = np.concatenate(parts + [np.full(n - used, -1, np.int32)])
    bt = jnp.zeros((SUBLANES, N_BUCKETS), F32).at[:DA_HEADS].set(rel_bias.T)
    tab = pl.pallas_call(
        _bias_kernel,
        grid=(n // blk,),
        in_specs=[pl.BlockSpec((1, blk), lambda i: (0, i)), pl.BlockSpec(bt.shape, lambda i: (0, 0))],
        out_specs=pl.BlockSpec((SUBLANES, blk), lambda i: (0, i)),
        out_shape=jax.ShapeDtypeStruct((SUBLANES, n), F32),
        compiler_params=_params(("arbitrary",)),
        name="t5_bias_tables",
    )(jnp.asarray(idx).reshape(1, n), bt)[:DA_HEADS]
    o1 = 2 * tq * tq
    o2 = o1 + ts * past
    prompt_tab = tab[:, :o1].reshape(DA_HEADS, 2, tq, tq)
    tabc = tab[:, o1:o2].reshape(DA_HEADS, ts, past)
    tabn = tab[:, o2:used].reshape(DA_HEADS, ts, LANES)
    return prompt_tab, tabc, tabn


def _rwkv_kernel(prw_ref, look_ref, first_ref, zb_ref, s0_ref, mu_ref, w0_ref, a0_ref, wwa_ref,
                 kkw_ref, kaw_ref, rk_ref, lng_ref, lnb_ref, g2_ref, hm_ref, dm_ref, pm_ref, ti_ref, to_ref,
                 ob_ref, sfin_ref,
                 s_sc, om_sc, rho_sc, at_sc, kt_sc, atb_sc, ktb_sc, wb_sc, vr_sc, o_sc, *, tm):
    j = pl.program_id(1)

    @pl.when(j == 0)
    def _():
        s_sc[...] = s0_ref[0]

    g2 = g2_ref[...]

    def head_sums(t):
        tb = t.astype(BF16)
        return jnp.concatenate([_dot(tb[:, c * RW_STATE_LANES:(c + 1) * RW_STATE_LANES], g2)
                                for c in range(RW_STATE_GROUPS)], axis=1)

    x = prw_ref[...]
    row0 = jnp.where(j == 0, first_ref[0], look_ref[7:8, :])
    rows = lax.broadcasted_iota(jnp.int32, x.shape, 0)
    prev = jnp.where(rows == 0, row0, pltpu.roll(x, 1, axis=0))
    mix = x + (prev - x) * mu_ref[...]
    r = mix[:, 0:512]
    kr = mix[:, 512:1024]
    vr = mix[:, 1024:1536]
    la = mix[:, 1536:1664]
    lane = lax.broadcasted_iota(jnp.int32, la.shape, 1)
    la = jnp.where(lane < HEAD, jnp.tanh(la), la)
    hi = la.astype(BF16)
    lo = (la - hi.astype(F32)).astype(BF16)
    wwa = wwa_ref[...]
    lora = _dot(hi, wwa[0]) + _dot(lo, wwa[0]) + _dot(hi, wwa[1])
    wpre = w0_ref[...] + lora[:, 0:512]
    nw = -wpre
    w = -(jnp.maximum(nw, 0.0) + jnp.log(1.0 + jnp.exp(-jnp.abs(nw)))) - 0.5
    ld = -jnp.exp(w)
    a = 1.0 / (1.0 + jnp.exp(-(a0_ref[...] + lora[:, 512:1024])))
    kk = kr * kkw_ref[...]
    nrm = jnp.sqrt(head_sums(kk * kk))
    kk = kk / jnp.maximum(nrm, 1e-12)
    kka = kk * a
    kh = kr * (1.0 + (a - 1.0) * kaw_ref[...])

    def dot3(m, t):
        hi = t.astype(BF16)
        r1 = t - hi.astype(F32)
        mid = r1.astype(BF16)
        lo = (r1 - mid.astype(F32)).astype(BF16)
        return _dot(m, hi) + _dot(m, mid) + _dot(m, lo)

    incl = dot3(ti_ref[...], ld)
    tot = dot3(to_ref[...], ld)
    e_rem = jnp.exp(tot - incl)
    e_inv = jnp.exp(-incl)
    om_sc[...] = jnp.exp(incl - ld) * kk
    rho_sc[...] = jnp.exp(incl) * r
    at_sc[...] = kka * e_inv
    kt_sc[...] = kh * e_inv
    atb_sc[...] = kka * e_rem
    ktb_sc[...] = kh * e_rem
    wb_sc[...] = jnp.exp(tot)
    hm = hm_ref[...]
    vx = (jnp.broadcast_to(vr[:, None, :], (tm, RW_HEADS, RW_WIDTH)) * hm[None]).reshape(tm * RW_HEADS, RW_WIDTH)
    vr_sc[...] = _dot(vx.astype(BF16), pm_ref[...])

    nb = RW_BLOCK
    heads_per_group = RW_HEADS // RW_STATE_GROUPS
    dm = dm_ref[...]

    def head_rows(blk):
        return jnp.concatenate([jnp.broadcast_to(blk[j:j + 1], (RW_HEADS, RW_WIDTH)) * hm for j in range(nb)],
                               axis=0)

    def fold(x):
        return x[:, :RW_STATE_LANES] + x[:, RW_STATE_LANES:]

    def query_rows(b):
        base = pl.multiple_of(b * nb, nb)
        return jnp.concatenate([head_rows(om_sc[pl.ds(base, nb), :]), head_rows(rho_sc[pl.ds(base, nb), :])], axis=0)

    nblk = tm // nb

    def coefficients(b, x):
        base = pl.multiple_of(b * nb, nb)
        rhs = jnp.concatenate([at_sc[pl.ds(base, nb), :], kt_sc[pl.ds(base, nb), :],
                               jnp.zeros((LANES - 2 * nb, RW_WIDTH), F32)], axis=0)
        return _dot_nt(x.astype(BF16), rhs.astype(BF16))

    def block(b, carry):
        s, g, coef = carry
        base = pl.multiple_of(b * nb, nb)
        atb_b, ktb_b, wb_b = (ref[pl.ds(base, nb), :] for ref in (atb_sc, ktb_sc, wb_sc))
        vr_b = vr_sc[pl.ds(pl.multiple_of(b * (nb * RW_HEADS), nb * RW_HEADS), nb * RW_HEADS), :]
        bn = jnp.minimum(b + 1, nblk - 1)
        xn = query_rows(bn)
        coef_next = coefficients(bn, xn)
        rhs2 = jnp.concatenate([fold(head_rows(atb_b)), fold(head_rows(ktb_b))], axis=0).astype(BF16)
        cross = _dot_nt(fold(xn).astype(BF16), rhs2).astype(BF16)
        g_decayed = _dot_nt(fold(xn * wb_b[0:1]).astype(BF16), s.astype(BF16))

        def rows(a, j):
            return a[j * RW_HEADS:(j + 1) * RW_HEADS]

        def cf(j, lane):
            return jnp.broadcast_to(rows(coef, j)[:, lane:lane + 1], (RW_HEADS, LANES))

        sa = []
        for j in range(nb):
            acc = rows(g, j) * rows(dm, j)
            for i in range(j):
                acc = acc + cf(j, nb + i) * rows(vr_b, i)
            for i in range(j):
                acc = acc - cf(j, i) * sa[i]
            sa.append(acc)
        outs = []
        for j in range(nb):
            acc = rows(g, nb + j) * rows(dm, j)
            for i in range(j + 1):
                acc = acc + cf(nb + j, nb + i) * rows(vr_b, i) - cf(nb + j, i) * sa[i]
            outs.append(acc)
        lhs = jnp.concatenate([-t for t in sa] + [vr_b], axis=0).astype(BF16)
        g_next = g_decayed + _dot(cross, lhs)
        ds = lax.dot_general(lhs, rhs2, (((0,), (0,)), ((), ())), preferred_element_type=F32)
        wb = jnp.concatenate(
            [jnp.broadcast_to(wb_b[0:1, c * RW_STATE_LANES:(c + 1) * RW_STATE_LANES], (HEAD, RW_STATE_LANES))
             for c in range(RW_STATE_GROUPS)], axis=0)
        for h in range(RW_HEADS):
            o_sc[h, pl.ds(base, nb), :] = jnp.concatenate([t[h:h + 1] for t in outs], axis=0)
        return s * wb + ds, g_next, coef_next

    s0 = s_sc[...]
    x0 = query_rows(0)
    g0 = _dot_nt(fold(x0).astype(BF16), s0.astype(BF16))
    s_sc[...] = lax.fori_loop(0, nblk, block, (s0, g0, coefficients(0, x0)), unroll=min(RW_UNROLL, nblk))[0]

    o = jnp.concatenate([o_sc[c * heads_per_group + hh][:, c * HEAD:(c + 1) * HEAD]
                         for c in range(RW_STATE_GROUPS) for hh in range(heads_per_group)], axis=1)
    mean = head_sums(o) * (1.0 / HEAD)
    d = o - mean
    var = head_sums(d * d) * (1.0 / HEAD)
    y = d * lax.rsqrt(var + RW_GN_EPS) * lng_ref[...] + lnb_ref[...]
    bonus = head_sums(r * kh * rk_ref[...]) * vr
    ob_ref[...] = ((y + bonus) * _silu(zb_ref[...])).astype(BF16)

    @pl.when(j == pl.num_programs(1) - 1)
    def _():
        sfin_ref[0] = s_sc[...]


def _block_sum_matrices(tm):
    t = np.arange(tm)
    same = (t[:, None] // RW_BLOCK) == (t[None, :] // RW_BLOCK)
    incl = same & (t[None, :] <= t[:, None])
    return [jnp.asarray(incl.astype(np.float32), dtype=BF16), jnp.asarray(same.astype(np.float32), dtype=BF16)]


def _rwkv(prw, first, zb, s0, p, seq_len, tm):
    m = prw.shape[0]
    nseq = m // seq_len
    ntile = seq_len // tm
    kern = functools.partial(_rwkv_kernel, tm=tm)
    full = lambda a: pl.BlockSpec(a.shape, lambda b, j: (0,) * a.ndim)
    rows = lambda c: pl.BlockSpec((tm, c), lambda b, j: (b * ntile + j, 0))
    look = pl.BlockSpec((8, RW_SHIFT_COLS), lambda b, j: (jnp.maximum((b * ntile + j) * (tm // 8) - 1, 0), 0))
    consts = [p["mu"], p["w0"], p["a0"], p["wwa"], p["kkw"], p["kaw"], p["rk"], p["lng"], p["lnb"],
              p["g2"], p["hm"], p["dm"], p["pm"]] + _block_sum_matrices(tm)
    return pl.pallas_call(
        kern,
        grid=(nseq, ntile),
        in_specs=[rows(RW_SHIFT_COLS), look,
                  pl.BlockSpec((1, 1, RW_SHIFT_COLS), lambda b, j: (b, 0, 0)),
                  rows(RW_WIDTH),
                  pl.BlockSpec((1,) + RW_STATE_SHAPE, lambda b, j: (b, 0, 0))] + [full(c) for c in consts],
        out_specs=[rows(RW_WIDTH), pl.BlockSpec((1,) + RW_STATE_SHAPE, lambda b, j: (b, 0, 0))],
        out_shape=[jax.ShapeDtypeStruct((m, RW_WIDTH), BF16), jax.ShapeDtypeStruct((nseq,) + RW_STATE_SHAPE, F32)],
        scratch_shapes=([pltpu.VMEM(RW_STATE_SHAPE, F32)] + [pltpu.VMEM((tm, RW_WIDTH), F32)] * 7
                        + [pltpu.VMEM((tm * RW_HEADS, RW_STATE_SHAPE[0]), F32),
                           pltpu.VMEM((RW_HEADS, tm, RW_STATE_SHAPE[0]), F32)]),
        compiler_params=_params(("arbitrary", "arbitrary")),
        name="rwkv7",
    )(prw, prw, first, zb, s0, *consts)


def _state_to_rows(s):
    b = s.shape[0]
    per = RW_HEADS // RW_STATE_GROUPS
    return s.reshape(b, RW_STATE_GROUPS, per, HEAD, HEAD).transpose(0, 1, 3, 2, 4).reshape((b,) + RW_STATE_SHAPE)


def _rows_to_state(s):
    b = s.shape[0]
    per = RW_HEADS // RW_STATE_GROUPS
    return s.reshape(b, RW_STATE_GROUPS, HEAD, per, HEAD).transpose(0, 1, 3, 2, 4).reshape(b, RW_HEADS, HEAD, HEAD)


SKIP_LOG = -104.0
SB_RECENT_KEYS = 256


def _sb_tile(qs, k, v, u, c, mask):
    tk = k[0].shape[0]
    rows = qs[0].shape[0]
    z = jnp.concatenate([_dot_nt(q_p, k_p) for q_p, k_p in zip(qs, k)], axis=0)
    nz = -z
    lg = jnp.minimum(nz, 0.0) - jnp.log(1.0 + jnp.exp(jnp.minimum(z, nz)))
    if mask is not None:
        lg = jnp.where(mask, lg, 0.0)
    it = _dot(lg.astype(BF16), u)
    a = jnp.exp(z + it + jnp.concatenate([c] * (tk // LANES), axis=1))
    if mask is not None:
        a = jnp.where(mask, a, 0.0)
    a = a.astype(BF16)
    da = jnp.concatenate([_dot(a[p * rows:(p + 1) * rows], v_p) for p, v_p in enumerate(v)], axis=0)
    return da, c + jnp.broadcast_to(it[:, 0:1], c.shape)


def _sb_finish(acc, z):
    t = z.shape[0]
    lane = lax.broadcasted_iota(jnp.int32, z.shape, 1)
    return jnp.where(lane < HEAD, acc[:t], acc[t:]) * _silu(z)


def _sb_prompt_kernel(q_ref, k_ref, v_ref, z_ref, u_ref, o_ref, acc_sc, c_sc, *, tq):
    i = pl.program_id(1)
    tk = u_ref.shape[0]
    r = tq // tk
    qs = _stack_halves(q_ref[...])
    u = u_ref[...]
    acc_sc[...] = jnp.zeros(acc_sc.shape, F32)
    c_sc[...] = jnp.zeros(c_sc.shape, F32)
    def tile(j, lo, hi, masked, valid=None):
        n = hi - lo
        halves = lambda x: jnp.concatenate([x[lo:hi], x[tq + lo:tq + hi]], axis=0)
        off = pl.multiple_of(j * tk, tk)
        mask = valid
        if masked:
            qrow = lax.broadcasted_iota(jnp.int32, (2 * n, tk), 0) % n
            col = lax.broadcasted_iota(jnp.int32, (2 * n, tk), 1)
            mask = col < qrow
        da, c = _sb_tile([halves(qs)], [k_ref[pl.ds(off, tk), :]], [v_ref[pl.ds(off, tk), :]], u,
                         halves(c_sc[...]), mask)
        for h in range(2):
            acc_sc[h * tq + lo:h * tq + hi, :] += da[h * n:(h + 1) * n]
            c_sc[h * tq + lo:h * tq + hi, :] = c[h * n:(h + 1) * n]

    for jj in reversed(range(r)):
        tile(i * r + jj, jj * tk, tq, True)

    tile(jnp.maximum(i * r - 1, 0), 0, tq, False, valid=i > 0)

    def body(st):
        tile(i * r - 1 - st[0], 0, tq, False)
        return st[0] + 1, jnp.max(c_sc[...])

    lax.while_loop(lambda st: jnp.logical_and(st[0] < i * r, st[1] > SKIP_LOG), body,
                   (jnp.int32(1), jnp.max(c_sc[...])))
    o_ref[...] = _sb_finish(acc_sc[...], z_ref[...]).astype(BF16)


def _sb_prompt(qb, kb, vb, z, u, tq):
    t = qb.shape[0]
    kern = functools.partial(_sb_prompt_kernel, tq=tq)
    rows = pl.BlockSpec((tq, LANES), lambda h, i: (i, h))
    whole = pl.BlockSpec((t, LANES), lambda h, i: (0, h))
    return pl.pallas_call(
        kern,
        grid=(SB_HEADS // 2, t // tq),
        in_specs=[rows, whole, whole, rows, pl.BlockSpec(u.shape, lambda h, i: (0, 0))],
        out_specs=rows,
        out_shape=jax.ShapeDtypeStruct((t, SB_WIDTH), BF16),
        scratch_shapes=[pltpu.VMEM((2 * tq, LANES), F32)] * 2,
        compiler_params=_params(("arbitrary", "arbitrary")),
        name="stick_breaking_prompt",
    )(qb, kb, vb, z, u)


def _sb_sample_kernel(q_ref, ck_ref, cv_ref, kn_ref, vn_ref, z_ref, u_ref, o_ref, left_ref, acc_sc, c_sc, *, ts):
    tk = u_ref.shape[0]
    ntile = ck_ref.shape[1] // tk
    npair = SB_HEADS // 2
    cols = lambda p: slice(p * LANES, (p + 1) * LANES)
    qs = [_stack_halves(q_ref[:, cols(p)]) for p in range(npair)]
    u = u_ref[...]
    pad = jnp.zeros((tk - ts, LANES), BF16)
    kn = [jnp.concatenate([kn_ref[:, cols(p)], pad], axis=0) for p in range(npair)]
    vn = [jnp.concatenate([vn_ref[:, cols(p)], pad], axis=0) for p in range(npair)]
    rows = npair * 2 * ts
    qrow = lax.broadcasted_iota(jnp.int32, (rows, tk), 0) % ts
    col = lax.broadcasted_iota(jnp.int32, (rows, tk), 1)
    acc, c = _sb_tile(qs, kn, vn, u, jnp.zeros((rows, LANES), F32), col < qrow)
    acc_sc[...] = acc
    c_sc[...] = c

    def body(st):
        off = pl.multiple_of((ntile - 1 - st[0]) * tk, tk)
        kc = ck_ref[0, pl.ds(off, tk), :].astype(BF16)
        vc = cv_ref[0, pl.ds(off, tk), :].astype(BF16)
        da, c = _sb_tile(qs, [kc[:, cols(p)] for p in range(npair)], [vc[:, cols(p)] for p in range(npair)],
                         u, c_sc[...], None)
        acc_sc[...] += da
        c_sc[...] = c
        return st[0] + 1, jnp.max(c)

    lax.while_loop(lambda st: jnp.logical_and(st[0] < ntile, st[1] > SKIP_LOG), body,
                   (jnp.int32(0), jnp.max(c)))
    left_ref[0] = jnp.broadcast_to(jnp.max(c_sc[...], axis=0, keepdims=True), (SUBLANES, LANES))
    acc = acc_sc[...]
    z = z_ref[...]
    o_ref[...] = jnp.concatenate(
        [_sb_finish(acc[p * 2 * ts:(p + 1) * 2 * ts], z[:, cols(p)]) for p in range(npair)], axis=1).astype(BF16)


def _sb_sample(qb, ck, cv, kb, vb, z, u, ts):
    nb, past, _ = ck.shape
    kern = functools.partial(_sb_sample_kernel, ts=ts)
    rows = pl.BlockSpec((ts, SB_WIDTH), lambda b: (b, 0))
    cache = pl.BlockSpec((1, past, SB_WIDTH), lambda b: (b, 0, 0))
    return pl.pallas_call(
        kern,
        grid=(nb,),
        in_specs=[rows, cache, cache, rows, rows, rows, pl.BlockSpec(u.shape, lambda b: (0, 0))],
        out_specs=[rows, pl.BlockSpec((1, SUBLANES, LANES), lambda b: (b, 0, 0))],
        out_shape=[jax.ShapeDtypeStruct((nb * ts, SB_WIDTH), BF16),
                   jax.ShapeDtypeStruct((nb, SUBLANES, LANES), F32)],
        scratch_shapes=[pltpu.VMEM((SB_HEADS * ts, LANES), F32)] * 2,
        compiler_params=_params(("arbitrary",)),
        name="stick_breaking_sample",
    )(qb, ck, cv, kb, vb, z, u)


def _sb_sample_two_phase(qb, cache_k, cache_v, kb, vb, z, u, ts):
    nb, past = cache_k.shape[:2]
    dense = lambda c: c.reshape(nb, c.shape[1], SB_WIDTH)
    recent = min(past, SB_RECENT_KEYS)
    out, left = _sb_sample(qb, dense(cache_k[:, past - recent:]), dense(cache_v[:, past - recent:]), kb, vb, z, u, ts)
    if recent == past:
        return out
    return lax.cond(jnp.max(left) > SKIP_LOG,
                    lambda: _sb_sample(qb, dense(cache_k), dense(cache_v), kb, vb, z, u, ts)[0],
                    lambda: out)


def _suffix_matrix(tk):
    j = np.arange(tk)[:, None]
    s = np.arange(tk)[None, :]
    return jnp.asarray((j >= s).astype(np.float32), dtype=BF16)


def _row_tile(m):
    for tm in PROJ_TILES:
        if m % tm == 0:
            return tm
    return m


def kernel(x_prompt, x_sample, cache_l0_k, cache_l0_v, state_l0_shift, state_l0_wkv, cache_l1_k, cache_l1_v, rel_bias, norm_l0, w_in_l0, w_out_l0, da_q_norm, da_k_norm, da_lambda_q1, da_lambda_k1, da_lambda_q2, da_lambda_k2, da_subln, rw_mu, rw_w0, rw_w_up, rw_a0, rw_a_up, rw_k_k, rw_k_a, rw_r_k, rw_lnx_g, rw_lnx_b, norm_l1, w_in_l1, w_out_l1):
    bp, t, d = x_prompt.shape
    nb, ts, _ = x_sample.shape
    past = cache_l0_k.shape[1]
    assert bp == 1 and ts <= LANES and ts % RW_BLOCK == 0
    assert t % max(SB_TILE, DA_TILE, RW_TILE) == 0 and past % LANES == 0 and DA_TILE >= MAX_DISTANCE

    row = lambda v: v.reshape(1, -1).astype(F32)
    w_in0 = w_in_l0.astype(BF16)
    w_in1 = w_in_l1.astype(BF16)
    w_out0a = w_out_l0[:DA_WIDTH].astype(BF16)
    w_out0b = w_out_l0[DA_WIDTH:].astype(BF16)
    w_out1 = w_out_l1.astype(BF16)
    qg = row(jnp.tile(da_q_norm, 512 // HEAD))
    kg = row(jnp.tile(da_k_norm, 512 // HEAD))
    lamv = jnp.stack([da_lambda_q1, da_lambda_k1, da_lambda_q2, da_lambda_k2]).astype(F32)
    sg = row(da_subln)
    zeros = jnp.zeros((HEAD, RW_WIDTH), F32)
    wwa_f = jnp.concatenate([jnp.concatenate([rw_w_up, zeros], axis=1),
                             jnp.concatenate([zeros, rw_a_up], axis=1)], axis=0)
    wwa_hi = wwa_f.astype(BF16)
    wwa = jnp.stack([wwa_hi, (wwa_f - wwa_hi.astype(F32)).astype(BF16)])
    heads = np.arange(RW_HEADS)
    per_group = RW_HEADS // RW_STATE_GROUPS
    chan = np.arange(RW_WIDTH)
    out_lane = np.arange(RW_STATE_SHAPE[0])
    hm = np.equal(heads[:, None], chan[None, :] // HEAD)
    dm = np.equal(heads[:, None] // per_group, out_lane[None, :] // HEAD)
    pm = (chan[:, None] // HEAD // per_group == out_lane[None, :] // HEAD) & (chan[:, None] % HEAD == out_lane[None, :] % HEAD)
    rw = dict(mu=row(rw_mu), w0=row(rw_w0), a0=row(rw_a0), wwa=wwa, kkw=row(rw_k_k), kaw=row(rw_k_a),
              rk=row(rw_r_k), lng=row(rw_lnx_g), lnb=row(rw_lnx_b), g2=_group_matrix(RW_STATE_LANES, 1.0),
              hm=jnp.asarray(hm.astype(np.float32)),
              dm=jnp.asarray(np.tile(dm.astype(np.float32), (RW_BLOCK, 1))),
              pm=jnp.asarray(pm.astype(np.float32), dtype=BF16))
    prompt_tab, tabc, tabn = _bias_tables(rel_bias.astype(F32), DA_TILE, ts, past)
    u_prompt = _suffix_matrix(2 * LANES)
    u_sample = _suffix_matrix(LANES)

    def layer0(x, first, s0, seq_len, attend):
        m = x.shape[0]
        tm = _row_tile(m)
        qn, kn, knb, v, vb, za, prw, zb = _inproj0(x, row(norm_l0), w_in0, qg, kg, tm)
        oa = attend(qn, kn, knb, v, vb, za)
        ob, sfin = _rwkv(prw, first, zb, _state_to_rows(s0), rw, seq_len, min(seq_len, RW_TILE))
        y = _outproj(x, [oa, ob], [w_out0a, w_out0b], tm)
        shift = prw.reshape(m // seq_len, seq_len, RW_SHIFT_COLS)[:, -1:]
        return y, kn, v, shift, _rows_to_state(sfin)

    def layer1(x, attend):
        m = x.shape[0]
        tm = _row_tile(m)
        qb, k, kb, v, vb, z = _inproj1(x, row(norm_l1), w_in1, tm)
        o = attend(qb, kb, vb, z)
        return _outproj(x, [o], [w_out1], tm), k, v

    xp = x_prompt.reshape(t, d)
    yp, k0p, v0p, shp, wkvp = layer0(
        xp, jnp.zeros((1, 1, RW_SHIFT_COLS), F32), jnp.zeros((1, RW_HEADS, HEAD, HEAD), F32), t,
        lambda qn, kn, knb, v, vb, za: _da_prompt(lamv, qn, knb, vb, za, prompt_tab, sg, DA_TILE))
    yp, k1p, v1p = layer1(yp, lambda qb, kb, vb, z: _sb_prompt(qb, kb, vb, z, u_prompt, SB_TILE))

    xs = x_sample.reshape(nb * ts, d)
    ck0 = cache_l0_k.reshape(nb, past, DA_WIDTH)
    cv0 = cache_l0_v.reshape(nb, past, DA_WIDTH)
    ys, k0s, v0s, shs, wkvs = layer0(
        xs, state_l0_shift, state_l0_wkv, ts,
        lambda qn, kn, knb, v, vb, za: _da_sample(lamv, qn, ck0, cv0, knb, vb, za, tabc, tabn, sg, ts))
    ys, k1s, v1s = layer1(ys, lambda qb, kb, vb, z: _sb_sample_two_phase(
        qb, cache_l1_k, cache_l1_v, kb, vb, z, u_sample, ts))

    return (yp.reshape(1, t, d), ys.reshape(nb, ts, d),
            k0p.reshape(1, t, DA_HEADS, LANES), v0p.reshape(1, t, DA_HEADS, LANES), shp, wkvp,
            k1p.reshape(1, t, SB_HEADS, HEAD), v1p.reshape(1, t, SB_HEADS, HEAD),
            k0s.reshape(nb, ts, DA_HEADS, LANES), v0s.reshape(nb, ts, DA_HEADS, LANES), shs, wkvs,
            k1s.reshape(nb, ts, SB_HEADS, HEAD), v1s.reshape(nb, ts, SB_HEADS, HEAD))
```

```python
import functools
import math

import numpy as np
import jax
import jax.numpy as jnp
from jax import lax
from jax.experimental import pallas as pl
from jax.experimental.pallas import tpu as pltpu

F32 = jnp.float32
BF16 = jnp.bfloat16

EPS = 1e-6
NEG = -1e30
CHUNK = 64
LANES = 128
SUBLANES = 8
HEAD = 64
DA_HEADS = 4
DA_WIDTH = 512
RW_WIDTH = 512
RW_HEADS = 8
RW_SHIFT_COLS = 3 * RW_WIDTH + 128
RW_GN_EPS = 64e-5
RW_STATE_LANES = 256
RW_STATE_GROUPS = RW_WIDTH // RW_STATE_LANES
RW_STATE_SHAPE = (RW_STATE_GROUPS * HEAD, RW_STATE_LANES)
RW_BLOCK = 8
RW_UNROLL = 4
SB_HEADS = 16
SB_WIDTH = 1024
N_BUCKETS = 32
MAX_DISTANCE = 128
LAMBDA_INIT = 0.8 - 0.6 * math.exp(-0.3 * 0)
LOG2E = math.log2(math.e)
VMEM_LIMIT = 56 * 1024 * 1024
PROJ_TILES = (512, 256)
DA_TILE = 512
SB_TILE = 512
RW_TILE = 256


def _params(sem):
    return pltpu.CompilerParams(dimension_semantics=sem, vmem_limit_bytes=VMEM_LIMIT)


def _dot(a, b):
    return jnp.dot(a, b, preferred_element_type=F32)


def _dot_nt(a, b):
    return lax.dot_general(a, b, (((1,), (1,)), ((), ())), preferred_element_type=F32)


def _silu(z):
    return z / (1.0 + jnp.exp(-z))


def _rms(x, g):
    return x * lax.rsqrt(jnp.mean(x * x, axis=-1, keepdims=True) + EPS) * g


def _group_matrix(n, scale):
    idx = np.arange(n) // HEAD
    return jnp.asarray((idx[:, None] == idx[None, :]).astype(np.float32) * scale, dtype=BF16)


def _inproj0_kernel(x_ref, g_ref, w_ref, qg_ref, kg_ref, avg_ref,
                    qn_ref, kn_ref, knb_ref, v_ref, vb_ref, za_ref, prw_ref, zb_ref):
    y = _dot(_rms(x_ref[...], g_ref[...]).astype(BF16), w_ref[...])

    def head_norm(t, g):
        sq = (t * t).astype(BF16)
        half = avg_ref.shape[0]
        ms = jnp.concatenate([_dot(sq[:, c:c + half], avg_ref[...]) for c in range(0, t.shape[1], half)], axis=1)
        return t * lax.rsqrt(ms + EPS) * g

    qn_ref[...] = (head_norm(y[:, 0:512], qg_ref[...]) * (HEAD ** -0.5 * LOG2E)).astype(BF16)
    kn = head_norm(y[:, 512:1024], kg_ref[...])
    knb_ref[...] = kn.astype(BF16)
    v = y[:, 1024:1536]
    vb_ref[...] = v.astype(BF16)
    for h in range(DA_HEADS):
        kn_ref[:, h, :] = kn[:, h * LANES:(h + 1) * LANES]
        v_ref[:, h, :] = v[:, h * LANES:(h + 1) * LANES]
    za_ref[...] = y[:, 1536:2048]
    prw_ref[...] = y[:, 2048:2048 + RW_SHIFT_COLS]
    zb_ref[...] = y[:, 2048 + RW_SHIFT_COLS:]


def _inproj0(x, g, w, qg, kg, tm):
    m, d = x.shape
    n = w.shape[1]
    row = lambda c: pl.BlockSpec((tm, c), lambda i: (i, 0))
    full = lambda a: pl.BlockSpec(a.shape, lambda i: (0,) * a.ndim)
    avg = _group_matrix(2 * LANES, 1.0 / HEAD)
    widths = [(512, BF16), (None, F32), (512, BF16), (None, F32), (512, BF16), (512, F32),
              (RW_SHIFT_COLS, F32), (512, F32)]
    heads = (DA_HEADS, LANES)
    return pl.pallas_call(
        _inproj0_kernel,
        grid=(m // tm,),
        in_specs=[row(d), full(g), full(w), full(qg), full(kg), full(avg)],
        out_specs=[row(c) if c else pl.BlockSpec((tm,) + heads, lambda i: (i, 0, 0)) for c, _ in widths],
        out_shape=[jax.ShapeDtypeStruct((m, c) if c else (m,) + heads, dt) for c, dt in widths],
        compiler_params=_params(("arbitrary",)),
        name="inproj0",
    )(x, g, w, qg, kg, avg)


def _inproj1_kernel(x_ref, g_ref, w_ref, qb_ref, k_ref, kb_ref, v_ref, vb_ref, z_ref):
    y = _dot(_rms(x_ref[...], g_ref[...]).astype(BF16), w_ref[...])
    qb_ref[...] = (y[:, 0:1024] * (HEAD ** -0.5)).astype(BF16)
    k = y[:, 1024:2048]
    k_ref[...] = k
    kb_ref[...] = k.astype(BF16)
    v = y[:, 2048:3072]
    v_ref[...] = v
    vb_ref[...] = v.astype(BF16)
    z_ref[...] = y[:, 3072:4096]


def _inproj1(x, g, w, tm):
    m, d = x.shape
    row = lambda c: pl.BlockSpec((tm, c), lambda i: (i, 0))
    full = lambda a: pl.BlockSpec(a.shape, lambda i: (0,) * a.ndim)
    dts = [BF16, F32, BF16, F32, BF16, F32]
    return pl.pallas_call(
        _inproj1_kernel,
        grid=(m // tm,),
        in_specs=[row(d), full(g), full(w)],
        out_specs=[row(1024) for _ in dts],
        out_shape=[jax.ShapeDtypeStruct((m, 1024), dt) for dt in dts],
        compiler_params=_params(("arbitrary",)),
        name="inproj1",
    )(x, g, w)


def _outproj_kernel(*refs):
    n = (len(refs) - 2) // 2
    x_ref, o_ref = refs[0], refs[-1]
    acc = x_ref[...]
    for a_ref, w_ref in zip(refs[1:1 + n], refs[1 + n:1 + 2 * n]):
        acc = acc + _dot(a_ref[...], w_ref[...])
    o_ref[...] = acc


def _outproj(x, acts, ws, tm):
    m, d = x.shape
    row = lambda c: pl.BlockSpec((tm, c), lambda i: (i, 0))
    full = lambda a: pl.BlockSpec(a.shape, lambda i: (0,) * a.ndim)
    return pl.pallas_call(
        _outproj_kernel,
        grid=(m // tm,),
        in_specs=[row(d)] + [row(a.shape[1]) for a in acts] + [full(w) for w in ws],
        out_specs=row(d),
        out_shape=jax.ShapeDtypeStruct((m, d), F32),
        compiler_params=_params(("arbitrary",)),
        name="outproj",
    )(x, *acts, *ws)


def _t5_bucket_np(rel):
    nb = N_BUCKETS // 2
    max_exact = nb // 2
    n = np.abs(rel)
    nf = np.maximum(n, 1).astype(np.float32)
    large = max_exact + (np.log(nf / np.float32(max_exact)) / np.float32(math.log(MAX_DISTANCE / max_exact))
                         * np.float32(nb - max_exact)).astype(np.int32)
    large = np.minimum(large, nb - 1)
    return np.where(rel > 0, nb, 0) + np.where(n < max_exact, n, large)


FAR_BUCKET = N_BUCKETS // 2 - 1
BIAS_LANES_PER_STEP = 32768
FAR_TILES = ((4, 2), (4, 1), (2, 1), (1, 1))


def _stack_halves(q):
    lane = lax.broadcasted_iota(jnp.int32, q.shape, 1)
    zero = jnp.zeros_like(q)
    return jnp.concatenate([jnp.where(lane < HEAD, q, zero), jnp.where(lane >= HEAD, q, zero)], axis=0)


def _softmax_step(qs, k, v, bias, m_sc, l_sc, acc_sc):
    s = jnp.concatenate([_dot_nt(q_h, k_h) for q_h, k_h in zip(qs, k)], axis=0)
    if bias is not None:
        s = s + bias
    _softmax_update(s, v, m_sc, l_sc, acc_sc)


def _softmax_update(s, v, m_sc, l_sc, acc_sc):
    reps = s.shape[1] // LANES
    m_old = m_sc[...]
    m_new = jnp.maximum(m_old, jnp.max(s, axis=-1, keepdims=True))
    alpha = jnp.exp2(m_old - m_new)
    p = jnp.exp2(s - jnp.concatenate([m_new] * reps, axis=1))
    psum = p[:, :LANES]
    for r in range(1, reps):
        psum = psum + p[:, r * LANES:(r + 1) * LANES]
    l_sc[...] = alpha * l_sc[...] + psum
    p = p.astype(BF16)
    rows = p.shape[0] // len(v)
    pv = jnp.concatenate([_dot(p[h * rows:(h + 1) * rows], v_h) for h, v_h in enumerate(v)], axis=0)
    acc_sc[...] = alpha * acc_sc[...] + pv
    m_sc[...] = m_new


def _diff_finish(lam_ref, za, sg, l, acc, tq):
    lv = lam_ref[...]
    lam = (jnp.exp(jnp.sum(lv[0:1] * lv[1:2], axis=-1, keepdims=True))
           - jnp.exp(jnp.sum(lv[2:3] * lv[3:4], axis=-1, keepdims=True)) + LAMBDA_INIT)
    o = acc / jnp.sum(l, axis=-1, keepdims=True)
    attn = o[:tq] - lam * o[tq:]
    return (_rms(attn, sg) * (1.0 - LAMBDA_INIT)) * _silu(za)


def _da_prompt_kernel(lam_ref, q_ref, k_ref, v_ref, za_ref, tab_ref, sg_ref, o_ref,
                      m_sc, l_sc, acc_sc, *, tq, tk):
    i = pl.program_id(1)
    qs = _stack_halves(q_ref[...])
    m_sc[...] = jnp.full(m_sc.shape, NEG, F32)
    l_sc[...] = jnp.zeros(l_sc.shape, F32)
    acc_sc[...] = jnp.zeros(acc_sc.shape, F32)

    def tile(j, width, bias):
        off = pl.multiple_of(j * tk, tk)
        _softmax_step([qs], [k_ref[pl.ds(off, width), :]], [v_ref[pl.ds(off, width), :]], bias,
                      m_sc, l_sc, acc_sc)

    nfar = jnp.maximum(i - 1, 0)

    done = 0
    for width, reps in FAR_TILES:
        def far(j, c, width=width, reps=reps, done=done):
            for n in range(reps):
                tile(done + width * (reps * j + n), width * tk, None)
            return c

        trips = (nfar - done) // (width * reps)
        lax.fori_loop(0, trips, far, 0)
        done = done + trips * width * reps

    @pl.when(i >= 1)
    def _():
        b = jnp.concatenate([tab_ref[0, 1], tab_ref[0, 0]], axis=1)
        tile(i - 1, 2 * tk, jnp.concatenate([b, b], axis=0))

    @pl.when(i == 0)
    def _():
        b = tab_ref[0, 0]
        tile(0, tk, jnp.concatenate([b, b], axis=0))
    o_ref[...] = _diff_finish(lam_ref, za_ref[...], sg_ref[...], l_sc[...], acc_sc[...], tq).astype(BF16)


def _da_prompt(lamv, qn, knb, vb, za, tab, sg, tq):
    t = qn.shape[0]
    tk = tq
    kern = functools.partial(_da_prompt_kernel, tq=tq, tk=tk)
    return pl.pallas_call(
        kern,
        grid=(DA_HEADS, t // tq),
        in_specs=[
            pl.BlockSpec(lamv.shape, lambda h, i: (0, 0)),
            pl.BlockSpec((tq, LANES), lambda h, i: (i, h)),
            pl.BlockSpec((t, LANES), lambda h, i: (0, h)),
            pl.BlockSpec((t, LANES), lambda h, i: (0, h)),
            pl.BlockSpec((tq, LANES), lambda h, i: (i, h)),
            pl.BlockSpec((1, 2, tq, tk), lambda h, i: (h, 0, 0, 0)),
            pl.BlockSpec(sg.shape, lambda h, i: (0, 0)),
        ],
        out_specs=pl.BlockSpec((tq, LANES), lambda h, i: (i, h)),
        out_shape=jax.ShapeDtypeStruct((t, DA_WIDTH), BF16),
        scratch_shapes=[pltpu.VMEM((2 * tq, LANES), F32)] * 3,
        compiler_params=_params(("arbitrary", "arbitrary")),
        name="diff_attn_prompt",
    )(lamv, qn, knb, vb, za, tab, sg)


def _da_sample_kernel(lam_ref, q_ref, ck_ref, cv_ref, kn_ref, vn_ref, za_ref, tabc_ref, tabn_ref, sg_ref,
                      o_ref, m_sc, l_sc, acc_sc, *, ts):
    cols = lambda h: slice(h * LANES, (h + 1) * LANES)
    heads = range(DA_HEADS)
    qs = [_stack_halves(q_ref[:, cols(h)]) for h in heads]
    m_sc[...] = jnp.full(m_sc.shape, NEG, F32)
    l_sc[...] = jnp.zeros(l_sc.shape, F32)
    acc_sc[...] = jnp.zeros(acc_sc.shape, F32)
    kc = ck_ref[0].astype(BF16)
    vc = cv_ref[0].astype(BF16)
    _softmax_step(qs, [kc[:, cols(h)] for h in heads], [vc[:, cols(h)] for h in heads], tabc_ref[...],
                  m_sc, l_sc, acc_sc)
    pad = jnp.zeros((LANES - ts, LANES), BF16)
    kn = [jnp.concatenate([kn_ref[:, cols(h)].astype(BF16), pad], axis=0) for h in heads]
    vn = [jnp.concatenate([vn_ref[:, cols(h)].astype(BF16), pad], axis=0) for h in heads]
    _softmax_step(qs, kn, vn, tabn_ref[...], m_sc, l_sc, acc_sc)
    l = l_sc[...]
    acc = acc_sc[...]
    za = za_ref[...]
    rows = lambda h: slice(h * 2 * ts, (h + 1) * 2 * ts)
    o_ref[...] = jnp.concatenate(
        [_diff_finish(lam_ref, za[:, cols(h)], sg_ref[...], l[rows(h)], acc[rows(h)], ts) for h in heads],
        axis=1).astype(BF16)


def _da_sample(lamv, qn, ck, cv, kn, vn, za, tabc, tabn, sg, ts):
    nb, past, _ = ck.shape
    kern = functools.partial(_da_sample_kernel, ts=ts)
    stack = lambda tab: jnp.concatenate([tab, tab], axis=1).reshape(DA_HEADS * 2 * ts, tab.shape[-1])
    tabc, tabn = stack(tabc), stack(tabn)
    rows = pl.BlockSpec((ts, DA_WIDTH), lambda b: (b, 0))
    cache = pl.BlockSpec((1, past, DA_WIDTH), lambda b: (b, 0, 0))
    full = lambda a: pl.BlockSpec(a.shape, lambda b: (0,) * a.ndim)
    return pl.pallas_call(
        kern,
        grid=(nb,),
        in_specs=[full(lamv), rows, cache, cache, rows, rows, rows, full(tabc), full(tabn), full(sg)],
        out_specs=rows,
        out_shape=jax.ShapeDtypeStruct((nb * ts, DA_WIDTH), BF16),
        scratch_shapes=[pltpu.VMEM((DA_HEADS * 2 * ts, LANES), F32)] * 3,
        compiler_params=_params(("arbitrary",)),
        name="diff_attn_sample",
    )(lamv, qn, ck, cv, kn, vn, za, tabc, tabn, sg)


def _bias_kernel(idx_ref, bt_ref, o_ref):
    idx = idx_ref[...]
    bt = bt_ref[...]
    bt = bt - bt[:, FAR_BUCKET:FAR_BUCKET + 1]
    onehot = jnp.where(lax.broadcasted_iota(jnp.int32, (N_BUCKETS, idx.shape[1]), 0) == idx, 1.0, 0.0).astype(BF16)
    hi = bt.astype(BF16)
    rest = bt - hi.astype(F32)
    mid = rest.astype(BF16)
    lo = (rest - mid.astype(F32)).astype(BF16)
    val = _dot(hi, onehot) + _dot(mid, onehot) + _dot(lo, onehot)
    o_ref[...] = jnp.where(idx < 0, NEG, val * LOG2E)


def _bias_tables(rel_bias, tq, ts, past):
    def buckets(rel, mask):
        return np.where(mask, _t5_bucket_np(rel), -1).astype(np.int32).reshape(-1)

    r = np.arange(tq)[:, None]
    c = np.arange(tq)[None, :]
    qpos = past + np.arange(ts)[:, None]
    kc = np.arange(past)[None, :]
    kn = past + np.arange(LANES)[None, :]
    parts = [buckets(c - r, (c // CHUNK) <= (r // CHUNK)),
             buckets(c - r - tq, np.ones((tq, tq), bool)),
             buckets(kc - qpos, (kc // CHUNK) <= (qpos // CHUNK)),
             buckets(kn - qpos, ((kn // CHUNK) <= (qpos // CHUNK)) & (kn < past + ts))]
    blk = BIAS_LANES_PER_STEP
    used = sum(p.size for p in parts)
    n = -(-used // blk) * blk
    idx = np.concatenate(parts + [np.full(n - used, -1, np.int32)])
    bt = jnp.zeros((SUBLANES, N_BUCKETS), F32).at[:DA_HEADS].set(rel_bias.T)
    tab = pl.pallas_call(
        _bias_kernel,
        grid=(n // blk,),
        in_specs=[pl.BlockSpec((1, blk), lambda i: (0, i)), pl.BlockSpec(bt.shape, lambda i: (0, 0))],
        out_specs=pl.BlockSpec((SUBLANES, blk), lambda i: (0, i)),
        out_shape=jax.ShapeDtypeStruct((SUBLANES, n), F32),
        compiler_params=_params(("arbitrary",)),
        name="t5_bias_tables",
    )(jnp.asarray(idx).reshape(1, n), bt)[:DA_HEADS]
    o1 = 2 * tq * tq
    o2 = o1 + ts * past
    prompt_tab = tab[:, :o1].reshape(DA_HEADS, 2, tq, tq)
    tabc = tab[:, o1:o2].reshape(DA_HEADS, ts, past)
    tabn = tab[:, o2:used].reshape(DA_HEADS, ts, LANES)
    return prompt_tab, tabc, tabn


def _rwkv_kernel(prw_ref, look_ref, first_ref, zb_ref, s0_ref, mu_ref, w0_ref, a0_ref, wwa_ref,
                 kkw_ref, kaw_ref, rk_ref, lng_ref, lnb_ref, g2_ref, hm_ref, dm_ref, pm_ref, ti_ref, to_ref,
                 ob_ref, sfin_ref,
                 s_sc, om_sc, rho_sc, at_sc, kt_sc, atb_sc, ktb_sc, wb_sc, vr_sc, o_sc, *, tm):
    j = pl.program_id(1)

    @pl.when(j == 0)
    def _():
        s_sc[...] = s0_ref[0]

    g2 = g2_ref[...]

    def head_sums(t):
        tb = t.astype(BF16)
        return jnp.concatenate([_dot(tb[:, c * RW_STATE_LANES:(c + 1) * RW_STATE_LANES], g2)
                                for c in range(RW_STATE_GROUPS)], axis=1)

    x = prw_ref[...]
    row0 = jnp.where(j == 0, first_ref[0], look_ref[7:8, :])
    rows = lax.broadcasted_iota(jnp.int32, x.shape, 0)
    prev = jnp.where(rows == 0, row0, pltpu.roll(x, 1, axis=0))
    mix = x + (prev - x) * mu_ref[...]
    r = mix[:, 0:512]
    kr = mix[:, 512:1024]
    vr = mix[:, 1024:1536]
    la = mix[:, 1536:1664]
    lane = lax.broadcasted_iota(jnp.int32, la.shape, 1)
    la = jnp.where(lane < HEAD, jnp.tanh(la), la)
    hi = la.astype(BF16)
    lo = (la - hi.astype(F32)).astype(BF16)
    wwa = wwa_ref[...]
    lora = _dot(hi, wwa[0]) + _dot(lo, wwa[0]) + _dot(hi, wwa[1])
    wpre = w0_ref[...] + lora[:, 0:512]
    nw = -wpre
    w = -(jnp.maximum(nw, 0.0) + jnp.log(1.0 + jnp.exp(-jnp.abs(nw)))) - 0.5
    ld = -jnp.exp(w)
    a = 1.0 / (1.0 + jnp.exp(-(a0_ref[...] + lora[:, 512:1024])))
    kk = kr * kkw_ref[...]
    nrm = jnp.sqrt(head_sums(kk * kk))
    kk = kk / jnp.maximum(nrm, 1e-12)
    kka = kk * a
    kh = kr * (1.0 + (a - 1.0) * kaw_ref[...])

    def dot3(m, t):
        hi = t.astype(BF16)
        r1 = t - hi.astype(F32)
        mid = r1.astype(BF16)
        lo = (r1 - mid.astype(F32)).astype(BF16)
        return _dot(m, hi) + _dot(m, mid) + _dot(m, lo)

    incl = dot3(ti_ref[...], ld)
    tot = dot3(to_ref[...], ld)
    e_rem = jnp.exp(tot - incl)
    e_inv = jnp.exp(-incl)
    om_sc[...] = jnp.exp(incl - ld) * kk
    rho_sc[...] = jnp.exp(incl) * r
    at_sc[...] = kka * e_inv
    kt_sc[...] = kh * e_inv
    atb_sc[...] = kka * e_rem
    ktb_sc[...] = kh * e_rem
    wb_sc[...] = jnp.exp(tot)
    hm = hm_ref[...]
    vx = (jnp.broadcast_to(vr[:, None, :], (tm, RW_HEADS, RW_WIDTH)) * hm[None]).reshape(tm * RW_HEADS, RW_WIDTH)
    vr_sc[...] = _dot(vx.astype(BF16), pm_ref[...])

    nb = RW_BLOCK
    heads_per_group = RW_HEADS // RW_STATE_GROUPS
    dm = dm_ref[...]

    def head_rows(blk):
        return jnp.concatenate([jnp.broadcast_to(blk[j:j + 1], (RW_HEADS, RW_WIDTH)) * hm for j in range(nb)],
                               axis=0)

    def fold(x):
        return x[:, :RW_STATE_LANES] + x[:, RW_STATE_LANES:]

    def query_rows(b):
        base = pl.multiple_of(b * nb, nb)
        return jnp.concatenate([head_rows(om_sc[pl.ds(base, nb), :]), head_rows(rho_sc[pl.ds(base, nb), :])], axis=0)

    nblk = tm // nb

    def coefficients(b, x):
        base = pl.multiple_of(b * nb, nb)
        rhs = jnp.concatenate([at_sc[pl.ds(base, nb), :], kt_sc[pl.ds(base, nb), :],
                               jnp.zeros((LANES - 2 * nb, RW_WIDTH), F32)], axis=0)
        return _dot_nt(x.astype(BF16), rhs.astype(BF16))

    def block(b, carry):
        s, g, coef = carry
        base = pl.multiple_of(b * nb, nb)
        atb_b, ktb_b, wb_b = (ref[pl.ds(base, nb), :] for ref in (atb_sc, ktb_sc, wb_sc))
        vr_b = vr_sc[pl.ds(pl.multiple_of(b * (nb * RW_HEADS), nb * RW_HEADS), nb * RW_HEADS), :]
        bn = jnp.minimum(b + 1, nblk - 1)
        xn = query_rows(bn)
        coef_next = coefficients(bn, xn)
        rhs2 = jnp.concatenate([fold(head_rows(atb_b)), fold(head_rows(ktb_b))], axis=0).astype(BF16)
        cross = _dot_nt(fold(xn).astype(BF16), rhs2).astype(BF16)
        g_decayed = _dot_nt(fold(xn * wb_b[0:1]).astype(BF16), s.astype(BF16))

        def rows(a, j):
            return a[j * RW_HEADS:(j + 1) * RW_HEADS]

        def cf(j, lane):
            return jnp.broadcast_to(rows(coef, j)[:, lane:lane + 1], (RW_HEADS, LANES))

        sa = []
        for j in range(nb):
            acc = rows(g, j) * rows(dm, j)
            for i in range(j):
                acc = acc + cf(j, nb + i) * rows(vr_b, i)
            for i in range(j):
                acc = acc - cf(j, i) * sa[i]
            sa.append(acc)
        outs = []
        for j in range(nb):
            acc = rows(g, nb + j) * rows(dm, j)
            for i in range(j + 1):
                acc = acc + cf(nb + j, nb + i) * rows(vr_b, i) - cf(nb + j, i) * sa[i]
            outs.append(acc)
        lhs = jnp.concatenate([-t for t in sa] + [vr_b], axis=0).astype(BF16)
        g_next = g_decayed + _dot(cross, lhs)
        ds = lax.dot_general(lhs, rhs2, (((0,), (0,)), ((), ())), preferred_element_type=F32)
        wb = jnp.concatenate(
            [jnp.broadcast_to(wb_b[0:1, c * RW_STATE_LANES:(c + 1) * RW_STATE_LANES], (HEAD, RW_STATE_LANES))
             for c in range(RW_STATE_GROUPS)], axis=0)
        for h in range(RW_HEADS):
            o_sc[h, pl.ds(base, nb), :] = jnp.concatenate([t[h:h + 1] for t in outs], axis=0)
        return s * wb + ds, g_next, coef_next

    s0 = s_sc[...]
    x0 = query_rows(0)
    g0 = _dot_nt(fold(x0).astype(BF16), s0.astype(BF16))
    s_sc[...] = lax.fori_loop(0, nblk, block, (s0, g0, coefficients(0, x0)), unroll=min(RW_UNROLL, nblk))[0]

    o = jnp.concatenate([o_sc[c * heads_per_group + hh][:, c * HEAD:(c + 1) * HEAD]
                         for c in range(RW_STATE_GROUPS) for hh in range(heads_per_group)], axis=1)
    mean = head_sums(o) * (1.0 / HEAD)
    d = o - mean
    var = head_sums(d * d) * (1.0 / HEAD)
    y = d * lax.rsqrt(var + RW_GN_EPS) * lng_ref[...] + lnb_ref[...]
    bonus = head_sums(r * kh * rk_ref[...]) * vr
    ob_ref[...] = ((y + bonus) * _silu(zb_ref[...])).astype(BF16)

    @pl.when(j == pl.num_programs(1) - 1)
    def _():
        sfin_ref[0] = s_sc[...]


def _block_sum_matrices(tm):
    t = np.arange(tm)
    same = (t[:, None] // RW_BLOCK) == (t[None, :] // RW_BLOCK)
    incl = same & (t[None, :] <= t[:, None])
    return [jnp.asarray(incl.astype(np.float32), dtype=BF16), jnp.asarray(same.astype(np.float32), dtype=BF16)]


def _rwkv(prw, first, zb, s0, p, seq_len, tm):
    m = prw.shape[0]
    nseq = m // seq_len
    ntile = seq_len // tm
    kern = functools.partial(_rwkv_kernel, tm=tm)
    full = lambda a: pl.BlockSpec(a.shape, lambda b, j: (0,) * a.ndim)
    rows = lambda c: pl.BlockSpec((tm, c), lambda b, j: (b * ntile + j, 0))
    look = pl.BlockSpec((8, RW_SHIFT_COLS), lambda b, j: (jnp.maximum((b * ntile + j) * (tm // 8) - 1, 0), 0))
    consts = [p["mu"], p["w0"], p["a0"], p["wwa"], p["kkw"], p["kaw"], p["rk"], p["lng"], p["lnb"],
              p["g2"], p["hm"], p["dm"], p["pm"]] + _block_sum_matrices(tm)
    return pl.pallas_call(
        kern,
        grid=(nseq, ntile),
        in_specs=[rows(RW_SHIFT_COLS), look,
                  pl.BlockSpec((1, 1, RW_SHIFT_COLS), lambda b, j: (b, 0, 0)),
                  rows(RW_WIDTH),
                  pl.BlockSpec((1,) + RW_STATE_SHAPE, lambda b, j: (b, 0, 0))] + [full(c) for c in consts],
        out_specs=[rows(RW_WIDTH), pl.BlockSpec((1,) + RW_STATE_SHAPE, lambda b, j: (b, 0, 0))],
        out_shape=[jax.ShapeDtypeStruct((m, RW_WIDTH), BF16), jax.ShapeDtypeStruct((nseq,) + RW_STATE_SHAPE, F32)],
        scratch_shapes=([pltpu.VMEM(RW_STATE_SHAPE, F32)] + [pltpu.VMEM((tm, RW_WIDTH), F32)] * 7
                        + [pltpu.VMEM((tm * RW_HEADS, RW_STATE_SHAPE[0]), F32),
                           pltpu.VMEM((RW_HEADS, tm, RW_STATE_SHAPE[0]), F32)]),
        compiler_params=_params(("arbitrary", "arbitrary")),
        name="rwkv7",
    )(prw, prw, first, zb, s0, *consts)


def _state_to_rows(s):
    b = s.shape[0]
    per = RW_HEADS // RW_STATE_GROUPS
    return s.reshape(b, RW_STATE_GROUPS, per, HEAD, HEAD).transpose(0, 1, 3, 2, 4).reshape((b,) + RW_STATE_SHAPE)


def _rows_to_state(s):
    b = s.shape[0]
    per = RW_HEADS // RW_STATE_GROUPS
    return s.reshape(b, RW_STATE_GROUPS, HEAD, per, HEAD).transpose(0, 1, 3, 2, 4).reshape(b, RW_HEADS, HEAD, HEAD)


SKIP_LOG = -104.0
SB_RECENT_KEYS = 256


def _sb_tile(qs, k, v, u, c, mask):
    tk = k[0].shape[0]
    rows = qs[0].shape[0]
    z = jnp.concatenate([_dot_nt(q_p, k_p) for q_p, k_p in zip(qs, k)], axis=0)
    nz = -z
    lg = jnp.minimum(nz, 0.0) - jnp.log(1.0 + jnp.exp(jnp.minimum(z, nz)))
    if mask is not None:
        lg = jnp.where(mask, lg, 0.0)
    it = _dot(lg.astype(BF16), u)
    a = jnp.exp(z + it + jnp.concatenate([c] * (tk // LANES), axis=1))
    if mask is not None:
        a = jnp.where(mask, a, 0.0)
    a = a.astype(BF16)
    da = jnp.concatenate([_dot(a[p * rows:(p + 1) * rows], v_p) for p, v_p in enumerate(v)], axis=0)
    return da, c + jnp.broadcast_to(it[:, 0:1], c.shape)


def _sb_finish(acc, z):
    t = z.shape[0]
    lane = lax.broadcasted_iota(jnp.int32, z.shape, 1)
    return jnp.where(lane < HEAD, acc[:t], acc[t:]) * _silu(z)


def _sb_prompt_kernel(q_ref, k_ref, v_ref, z_ref, u_ref, o_ref, acc_sc, c_sc, *, tq):
    i = pl.program_id(1)
    tk = u_ref.shape[0]
    r = tq // tk
    qs = _stack_halves(q_ref[...])
    u = u_ref[...]
    acc_sc[...] = jnp.zeros(acc_sc.shape, F32)
    c_sc[...] = jnp.zeros(c_sc.shape, F32)
    def tile(j, lo, hi, masked, valid=None):
        n = hi - lo
        halves = lambda x: jnp.concatenate([x[lo:hi], x[tq + lo:tq + hi]], axis=0)
        off = pl.multiple_of(j * tk, tk)
        mask = valid
        if masked:
            qrow = lax.broadcasted_iota(jnp.int32, (2 * n, tk), 0) % n
            col = lax.broadcasted_iota(jnp.int32, (2 * n, tk), 1)
            mask = col < qrow
        da, c = _sb_tile([halves(qs)], [k_ref[pl.ds(off, tk), :]], [v_ref[pl.ds(off, tk), :]], u,
                         halves(c_sc[...]), mask)
        for h in range(2):
            acc_sc[h * tq + lo:h * tq + hi, :] += da[h * n:(h + 1) * n]
            c_sc[h * tq + lo:h * tq + hi, :] = c[h * n:(h + 1) * n]

    for jj in reversed(range(r)):
        tile(i * r + jj, jj * tk, tq, True)

    tile(jnp.maximum(i * r - 1, 0), 0, tq, False, valid=i > 0)

    def body(st):
        tile(i * r - 1 - st[0], 0, tq, False)
        return st[0] + 1, jnp.max(c_sc[...])

    lax.while_loop(lambda st: jnp.logical_and(st[0] < i * r, st[1] > SKIP_LOG), body,
                   (jnp.int32(1), jnp.max(c_sc[...])))
    o_ref[...] = _sb_finish(acc_sc[...], z_ref[...]).astype(BF16)


def _sb_prompt(qb, kb, vb, z, u, tq):
    t = qb.shape[0]
    kern = functools.partial(_sb_prompt_kernel, tq=tq)
    rows = pl.BlockSpec((tq, LANES), lambda h, i: (i, h))
    whole = pl.BlockSpec((t, LANES), lambda h, i: (0, h))
    return pl.pallas_call(
        kern,
        grid=(SB_HEADS // 2, t // tq),
        in_specs=[rows, whole, whole, rows, pl.BlockSpec(u.shape, lambda h, i: (0, 0))],
        out_specs=rows,
        out_shape=jax.ShapeDtypeStruct((t, SB_WIDTH), BF16),
        scratch_shapes=[pltpu.VMEM((2 * tq, LANES), F32)] * 2,
        compiler_params=_params(("arbitrary", "arbitrary")),
        name="stick_breaking_prompt",
    )(qb, kb, vb, z, u)


def _sb_sample_kernel(q_ref, ck_ref, cv_ref, kn_ref, vn_ref, z_ref, u_ref, o_ref, left_ref, acc_sc, c_sc, *, ts):
    tk = u_ref.shape[0]
    ntile = ck_ref.shape[1] // tk
    npair = SB_HEADS // 2
    cols = lambda p: slice(p * LANES, (p + 1) * LANES)
    qs = [_stack_halves(q_ref[:, cols(p)]) for p in range(npair)]
    u = u_ref[...]
    pad = jnp.zeros((tk - ts, LANES), BF16)
    kn = [jnp.concatenate([kn_ref[:, cols(p)], pad], axis=0) for p in range(npair)]
    vn = [jnp.concatenate([vn_ref[:, cols(p)], pad], axis=0) for p in range(npair)]
    rows = npair * 2 * ts
    qrow = lax.broadcasted_iota(jnp.int32, (rows, tk), 0) % ts
    col = lax.broadcasted_iota(jnp.int32, (rows, tk), 1)
    acc, c = _sb_tile(qs, kn, vn, u, jnp.zeros((rows, LANES), F32), col < qrow)

    def cache_tile(n, c):
        off = pl.multiple_of((ntile - 1 - n) * tk, tk)
        kc = ck_ref[0, pl.ds(off, tk), :].astype(BF16)
        vc = cv_ref[0, pl.ds(off, tk), :].astype(BF16)
        return _sb_tile(qs, [kc[:, cols(p)] for p in range(npair)], [vc[:, cols(p)] for p in range(npair)], u, c, None)

    da, c = cache_tile(0, c)
    acc_sc[...] = acc + da
    c_sc[...] = c

    def body(st):
        da, c = cache_tile(st[0], c_sc[...])
        acc_sc[...] += da
        c_sc[...] = c
        return st[0] + 1, jnp.max(c)

    lax.while_loop(lambda st: jnp.logical_and(st[0] < ntile, st[1] > SKIP_LOG), body,
                   (jnp.int32(1), jnp.max(c)))
    left_ref[0] = jnp.broadcast_to(jnp.max(c_sc[...], axis=0, keepdims=True), (SUBLANES, LANES))
    acc = acc_sc[...]
    z = z_ref[...]
    o_ref[...] = jnp.concatenate(
        [_sb_finish(acc[p * 2 * ts:(p + 1) * 2 * ts], z[:, cols(p)]) for p in range(npair)], axis=1).astype(BF16)


def _sb_sample(qb, ck, cv, kb, vb, z, u, ts):
    nb, past, _ = ck.shape
    kern = functools.partial(_sb_sample_kernel, ts=ts)
    rows = pl.BlockSpec((ts, SB_WIDTH), lambda b: (b, 0))
    cache = pl.BlockSpec((1, past, SB_WIDTH), lambda b: (b, 0, 0))
    return pl.pallas_call(
        kern,
        grid=(nb,),
        in_specs=[rows, cache, cache, rows, rows, rows, pl.BlockSpec(u.shape, lambda b: (0, 0))],
        out_specs=[rows, pl.BlockSpec((1, SUBLANES, LANES), lambda b: (b, 0, 0))],
        out_shape=[jax.ShapeDtypeStruct((nb * ts, SB_WIDTH), BF16),
                   jax.ShapeDtypeStruct((nb, SUBLANES, LANES), F32)],
        scratch_shapes=[pltpu.VMEM((SB_HEADS * ts, LANES), F32)] * 2,
        compiler_params=_params(("arbitrary",)),
        name="stick_breaking_sample",
    )(qb, ck, cv, kb, vb, z, u)


def _sb_sample_two_phase(qb, cache_k, cache_v, kb, vb, z, u, ts):
    nb, past = cache_k.shape[:2]
    dense = lambda c: c.reshape(nb, c.shape[1], SB_WIDTH)
    recent = min(past, SB_RECENT_KEYS)
    out, left = _sb_sample(qb, dense(cache_k[:, past - recent:]), dense(cache_v[:, past - recent:]), kb, vb, z, u, ts)
    if recent == past:
        return out
    return lax.cond(jnp.max(left) > SKIP_LOG,
                    lambda: _sb_sample(qb, dense(cache_k), dense(cache_v), kb, vb, z, u, ts)[0],
                    lambda: out)


def _suffix_matrix(tk):
    j = np.arange(tk)[:, None]
    s = np.arange(tk)[None, :]
    return jnp.asarray((j >= s).astype(np.float32), dtype=BF16)


def _row_tile(m):
    for tm in PROJ_TILES:
        if m % tm == 0:
            return tm
    return m


def kernel(x_prompt, x_sample, cache_l0_k, cache_l0_v, state_l0_shift, state_l0_wkv, cache_l1_k, cache_l1_v, rel_bias, norm_l0, w_in_l0, w_out_l0, da_q_norm, da_k_norm, da_lambda_q1, da_lambda_k1, da_lambda_q2, da_lambda_k2, da_subln, rw_mu, rw_w0, rw_w_up, rw_a0, rw_a_up, rw_k_k, rw_k_a, rw_r_k, rw_lnx_g, rw_lnx_b, norm_l1, w_in_l1, w_out_l1):
    bp, t, d = x_prompt.shape
    nb, ts, _ = x_sample.shape
    past = cache_l0_k.shape[1]
    assert bp == 1 and ts <= LANES and ts % RW_BLOCK == 0
    assert t % max(SB_TILE, DA_TILE, RW_TILE) == 0 and past % LANES == 0 and DA_TILE >= MAX_DISTANCE

    row = lambda v: v.reshape(1, -1).astype(F32)
    w_in0 = w_in_l0.astype(BF16)
    w_in1 = w_in_l1.astype(BF16)
    w_out0a = w_out_l0[:DA_WIDTH].astype(BF16)
    w_out0b = w_out_l0[DA_WIDTH:].astype(BF16)
    w_out1 = w_out_l1.astype(BF16)
    qg = row(jnp.tile(da_q_norm, 512 // HEAD))
    kg = row(jnp.tile(da_k_norm, 512 // HEAD))
    lamv = jnp.stack([da_lambda_q1, da_lambda_k1, da_lambda_q2, da_lambda_k2]).astype(F32)
    sg = row(da_subln)
    zeros = jnp.zeros((HEAD, RW_WIDTH), F32)
    wwa_f = jnp.concatenate([jnp.concatenate([rw_w_up, zeros], axis=1),
                             jnp.concatenate([zeros, rw_a_up], axis=1)], axis=0)
    wwa_hi = wwa_f.astype(BF16)
    wwa = jnp.stack([wwa_hi, (wwa_f - wwa_hi.astype(F32)).astype(BF16)])
    heads = np.arange(RW_HEADS)
    per_group = RW_HEADS // RW_STATE_GROUPS
    chan = np.arange(RW_WIDTH)
    out_lane = np.arange(RW_STATE_SHAPE[0])
    hm = np.equal(heads[:, None], chan[None, :] // HEAD)
    dm = np.equal(heads[:, None] // per_group, out_lane[None, :] // HEAD)
    pm = (chan[:, None] // HEAD // per_group == out_lane[None, :] // HEAD) & (chan[:, None] % HEAD == out_lane[None, :] % HEAD)
    rw = dict(mu=row(rw_mu), w0=row(rw_w0), a0=row(rw_a0), wwa=wwa, kkw=row(rw_k_k), kaw=row(rw_k_a),
              rk=row(rw_r_k), lng=row(rw_lnx_g), lnb=row(rw_lnx_b), g2=_group_matrix(RW_STATE_LANES, 1.0),
              hm=jnp.asarray(hm.astype(np.float32)),
              dm=jnp.asarray(np.tile(dm.astype(np.float32), (RW_BLOCK, 1))),
              pm=jnp.asarray(pm.astype(np.float32), dtype=BF16))
    prompt_tab, tabc, tabn = _bias_tables(rel_bias.astype(F32), DA_TILE, ts, past)
    u_prompt = _suffix_matrix(2 * LANES)
    u_sample = _suffix_matrix(LANES)

    def layer0(x, first, s0, seq_len, attend):
        m = x.shape[0]
        tm = _row_tile(m)
        qn, kn, knb, v, vb, za, prw, zb = _inproj0(x, row(norm_l0), w_in0, qg, kg, tm)
        oa = attend(qn, kn, knb, v, vb, za)
        ob, sfin = _rwkv(prw, first, zb, _state_to_rows(s0), rw, seq_len, min(seq_len, RW_TILE))
        y = _outproj(x, [oa, ob], [w_out0a, w_out0b], tm)
        shift = prw.reshape(m // seq_len, seq_len, RW_SHIFT_COLS)[:, -1:]
        return y, kn, v, shift, _rows_to_state(sfin)

    def layer1(x, attend):
        m = x.shape[0]
        tm = _row_tile(m)
        qb, k, kb, v, vb, z = _inproj1(x, row(norm_l1), w_in1, tm)
        o = attend(qb, kb, vb, z)
        return _outproj(x, [o], [w_out1], tm), k, v

    xp = x_prompt.reshape(t, d)
    yp, k0p, v0p, shp, wkvp = layer0(
        xp, jnp.zeros((1, 1, RW_SHIFT_COLS), F32), jnp.zeros((1, RW_HEADS, HEAD, HEAD), F32), t,
        lambda qn, kn, knb, v, vb, za: _da_prompt(lamv, qn, knb, vb, za, prompt_tab, sg, DA_TILE))
    yp, k1p, v1p = layer1(yp, lambda qb, kb, vb, z: _sb_prompt(qb, kb, vb, z, u_prompt, SB_TILE))

    xs = x_sample.reshape(nb * ts, d)
    ck0 = cache_l0_k.reshape(nb, past, DA_WIDTH)
    cv0 = cache_l0_v.reshape(nb, past, DA_WIDTH)
    ys, k0s, v0s, shs, wkvs = layer0(
        xs, state_l0_shift, state_l0_wkv, ts,
        lambda qn, kn, knb, v, vb, za: _da_sample(lamv, qn, ck0, cv0, knb, vb, za, tabc, tabn, sg, ts))
    ys, k1s, v1s = layer1(ys, lambda qb, kb, vb, z: _sb_sample_two_phase(
        qb, cache_l1_k, cache_l1_v, kb, vb, z, u_sample, ts))

    return (yp.reshape(1, t, d), ys.reshape(nb, ts, d),
            k0p.reshape(1, t, DA_HEADS, LANES), v0p.reshape(1, t, DA_HEADS, LANES), shp, wkvp,
            k1p.reshape(1, t, SB_HEADS, HEAD), v1p.reshape(1, t, SB_HEADS, HEAD),
            k0s.reshape(nb, ts, DA_HEADS, LANES), v0s.reshape(nb, ts, DA_HEADS, LANES), shs, wkvs,
            k1s.reshape(nb, ts, SB_HEADS, HEAD), v1s.reshape(nb, ts, SB_HEADS, HEAD))
```

```python
import functools
import math

import numpy as np
import jax
import jax.numpy as jnp
from jax import lax
from jax.experimental import pallas as pl
from jax.experimental.pallas import tpu as pltpu

F32 = jnp.float32
BF16 = jnp.bfloat16

EPS = 1e-6
NEG = -1e30
CHUNK = 64
LANES = 128
SUBLANES = 8
HEAD = 64
DA_HEADS = 4
DA_WIDTH = 512
RW_WIDTH = 512
RW_HEADS = 8
RW_SHIFT_COLS = 3 * RW_WIDTH + 128
RW_GN_EPS = 64e-5
RW_STATE_LANES = 256
RW_STATE_GROUPS = RW_WIDTH // RW_STATE_LANES
RW_STATE_SHAPE = (RW_STATE_GROUPS * HEAD, RW_STATE_LANES)
RW_BLOCK = 8
RW_UNROLL = 4
SB_HEADS = 16
SB_WIDTH = 1024
N_BUCKETS = 32
MAX_DISTANCE = 128
LAMBDA_INIT = 0.8 - 0.6 * math.exp(-0.3 * 0)
LOG2E = math.log2(math.e)
VMEM_LIMIT = 56 * 1024 * 1024
PROJ_TILES = (512, 256)
DA_TILE = 512
SB_TILE = 512
RW_TILE = 256


def _params(sem):
    return pltpu.CompilerParams(dimension_semantics=sem, vmem_limit_bytes=VMEM_LIMIT)


def _dot(a, b):
    return jnp.dot(a, b, preferred_element_type=F32)


def _dot_nt(a, b):
    return lax.dot_general(a, b, (((1,), (1,)), ((), ())), preferred_element_type=F32)


def _silu(z):
    return z / (1.0 + jnp.exp(-z))


def _rms(x, g):
    return x * lax.rsqrt(jnp.mean(x * x, axis=-1, keepdims=True) + EPS) * g


def _group_matrix(n, scale):
    idx = np.arange(n) // HEAD
    return jnp.asarray((idx[:, None] == idx[None, :]).astype(np.float32) * scale, dtype=BF16)


def _inproj0_kernel(x_ref, g_ref, w_ref, qg_ref, kg_ref, avg_ref,
                    qn_ref, kn_ref, knb_ref, v_ref, vb_ref, za_ref, prw_ref, zb_ref):
    y = _dot(_rms(x_ref[...], g_ref[...]).astype(BF16), w_ref[...])

    def head_norm(t, g):
        sq = (t * t).astype(BF16)
        half = avg_ref.shape[0]
        ms = jnp.concatenate([_dot(sq[:, c:c + half], avg_ref[...]) for c in range(0, t.shape[1], half)], axis=1)
        return t * lax.rsqrt(ms + EPS) * g

    qn_ref[...] = (head_norm(y[:, 0:512], qg_ref[...]) * (HEAD ** -0.5 * LOG2E)).astype(BF16)
    kn = head_norm(y[:, 512:1024], kg_ref[...])
    knb_ref[...] = kn.astype(BF16)
    v = y[:, 1024:1536]
    vb_ref[...] = v.astype(BF16)
    for h in range(DA_HEADS):
        kn_ref[:, h, :] = kn[:, h * LANES:(h + 1) * LANES]
        v_ref[:, h, :] = v[:, h * LANES:(h + 1) * LANES]
    za_ref[...] = y[:, 1536:2048]
    prw_ref[...] = y[:, 2048:2048 + RW_SHIFT_COLS]
    zb_ref[...] = y[:, 2048 + RW_SHIFT_COLS:]


def _inproj0(x, g, w, qg, kg, tm):
    m, d = x.shape
    n = w.shape[1]
    row = lambda c: pl.BlockSpec((tm, c), lambda i: (i, 0))
    full = lambda a: pl.BlockSpec(a.shape, lambda i: (0,) * a.ndim)
    avg = _group_matrix(2 * LANES, 1.0 / HEAD)
    widths = [(512, BF16), (None, F32), (512, BF16), (None, F32), (512, BF16), (512, F32),
              (RW_SHIFT_COLS, F32), (512, F32)]
    heads = (DA_HEADS, LANES)
    return pl.pallas_call(
        _inproj0_kernel,
        grid=(m // tm,),
        in_specs=[row(d), full(g), full(w), full(qg), full(kg), full(avg)],
        out_specs=[row(c) if c else pl.BlockSpec((tm,) + heads, lambda i: (i, 0, 0)) for c, _ in widths],
        out_shape=[jax.ShapeDtypeStruct((m, c) if c else (m,) + heads, dt) for c, dt in widths],
        compiler_params=_params(("arbitrary",)),
        name="inproj0",
    )(x, g, w, qg, kg, avg)


def _inproj1_kernel(x_ref, g_ref, w_ref, qb_ref, k_ref, kb_ref, v_ref, vb_ref, z_ref):
    y = _dot(_rms(x_ref[...], g_ref[...]).astype(BF16), w_ref[...])
    qb_ref[...] = (y[:, 0:1024] * (HEAD ** -0.5)).astype(BF16)
    k = y[:, 1024:2048]
    k_ref[...] = k
    kb_ref[...] = k.astype(BF16)
    v = y[:, 2048:3072]
    v_ref[...] = v
    vb_ref[...] = v.astype(BF16)
    z_ref[...] = y[:, 3072:4096]


def _inproj1(x, g, w, tm):
    m, d = x.shape
    row = lambda c: pl.BlockSpec((tm, c), lambda i: (i, 0))
    full = lambda a: pl.BlockSpec(a.shape, lambda i: (0,) * a.ndim)
    dts = [BF16, F32, BF16, F32, BF16, F32]
    return pl.pallas_call(
        _inproj1_kernel,
        grid=(m // tm,),
        in_specs=[row(d), full(g), full(w)],
        out_specs=[row(1024) for _ in dts],
        out_shape=[jax.ShapeDtypeStruct((m, 1024), dt) for dt in dts],
        compiler_params=_params(("arbitrary",)),
        name="inproj1",
    )(x, g, w)


def _outproj_kernel(*refs):
    n = (len(refs) - 2) // 2
    x_ref, o_ref = refs[0], refs[-1]
    acc = x_ref[...]
    for a_ref, w_ref in zip(refs[1:1 + n], refs[1 + n:1 + 2 * n]):
        acc = acc + _dot(a_ref[...], w_ref[...])
    o_ref[...] = acc


def _outproj(x, acts, ws, tm):
    m, d = x.shape
    row = lambda c: pl.BlockSpec((tm, c), lambda i: (i, 0))
    full = lambda a: pl.BlockSpec(a.shape, lambda i: (0,) * a.ndim)
    return pl.pallas_call(
        _outproj_kernel,
        grid=(m // tm,),
        in_specs=[row(d)] + [row(a.shape[1]) for a in acts] + [full(w) for w in ws],
        out_specs=row(d),
        out_shape=jax.ShapeDtypeStruct((m, d), F32),
        compiler_params=_params(("arbitrary",)),
        name="outproj",
    )(x, *acts, *ws)


def _t5_bucket_np(rel):
    nb = N_BUCKETS // 2
    max_exact = nb // 2
    n = np.abs(rel)
    nf = np.maximum(n, 1).astype(np.float32)
    large = max_exact + (np.log(nf / np.float32(max_exact)) / np.float32(math.log(MAX_DISTANCE / max_exact))
                         * np.float32(nb - max_exact)).astype(np.int32)
    large = np.minimum(large, nb - 1)
    return np.where(rel > 0, nb, 0) + np.where(n < max_exact, n, large)


FAR_BUCKET = N_BUCKETS // 2 - 1
BIAS_LANES_PER_STEP = 32768
FAR_TILES = ((4, 2), (4, 1), (2, 1), (1, 1))


def _stack_halves(q):
    lane = lax.broadcasted_iota(jnp.int32, q.shape, 1)
    zero = jnp.zeros_like(q)
    return jnp.concatenate([jnp.where(lane < HEAD, q, zero), jnp.where(lane >= HEAD, q, zero)], axis=0)


def _softmax_step(qs, k, v, bias, m_sc, l_sc, acc_sc):
    s = jnp.concatenate([_dot_nt(q_h, k_h) for q_h, k_h in zip(qs, k)], axis=0)
    if bias is not None:
        s = s + bias
    _softmax_update(s, v, m_sc, l_sc, acc_sc)


def _softmax_update(s, v, m_sc, l_sc, acc_sc):
    reps = s.shape[1] // LANES
    m_old = m_sc[...]
    m_new = jnp.maximum(m_old, jnp.max(s, axis=-1, keepdims=True))
    alpha = jnp.exp2(m_old - m_new)
    p = jnp.exp2(s - jnp.concatenate([m_new] * reps, axis=1))
    psum = p[:, :LANES]
    for r in range(1, reps):
        psum = psum + p[:, r * LANES:(r + 1) * LANES]
    l_sc[...] = alpha * l_sc[...] + psum
    p = p.astype(BF16)
    rows = p.shape[0] // len(v)
    pv = jnp.concatenate([_dot(p[h * rows:(h + 1) * rows], v_h) for h, v_h in enumerate(v)], axis=0)
    acc_sc[...] = alpha * acc_sc[...] + pv
    m_sc[...] = m_new


def _diff_finish(lam_ref, za, sg, l, acc, tq):
    lv = lam_ref[...]
    lam = (jnp.exp(jnp.sum(lv[0:1] * lv[1:2], axis=-1, keepdims=True))
           - jnp.exp(jnp.sum(lv[2:3] * lv[3:4], axis=-1, keepdims=True)) + LAMBDA_INIT)
    o = acc / jnp.sum(l, axis=-1, keepdims=True)
    attn = o[:tq] - lam * o[tq:]
    return (_rms(attn, sg) * (1.0 - LAMBDA_INIT)) * _silu(za)


def _da_prompt_kernel(lam_ref, q_ref, k_ref, v_ref, za_ref, tab_ref, sg_ref, o_ref,
                      m_sc, l_sc, acc_sc, *, tq, tk):
    i = pl.program_id(1)
    qs = _stack_halves(q_ref[...])
    m_sc[...] = jnp.full(m_sc.shape, NEG, F32)
    l_sc[...] = jnp.zeros(l_sc.shape, F32)
    acc_sc[...] = jnp.zeros(acc_sc.shape, F32)

    def tile(j, width, bias):
        off = pl.multiple_of(j * tk, tk)
        _softmax_step([qs], [k_ref[pl.ds(off, width), :]], [v_ref[pl.ds(off, width), :]], bias,
                      m_sc, l_sc, acc_sc)

    nfar = jnp.maximum(i - 1, 0)

    done = 0
    for width, reps in FAR_TILES:
        def far(j, c, width=width, reps=reps, done=done):
            for n in range(reps):
                tile(done + width * (reps * j + n), width * tk, None)
            return c

        trips = (nfar - done) // (width * reps)
        lax.fori_loop(0, trips, far, 0)
        done = done + trips * width * reps

    @pl.when(i >= 1)
    def _():
        b = jnp.concatenate([tab_ref[0, 1], tab_ref[0, 0]], axis=1)
        tile(i - 1, 2 * tk, jnp.concatenate([b, b], axis=0))

    @pl.when(i == 0)
    def _():
        b = tab_ref[0, 0]
        tile(0, tk, jnp.concatenate([b, b], axis=0))
    o_ref[...] = _diff_finish(lam_ref, za_ref[...], sg_ref[...], l_sc[...], acc_sc[...], tq).astype(BF16)


def _da_prompt(lamv, qn, knb, vb, za, tab, sg, tq):
    t = qn.shape[0]
    tk = tq
    kern = functools.partial(_da_prompt_kernel, tq=tq, tk=tk)
    return pl.pallas_call(
        kern,
        grid=(DA_HEADS, t // tq),
        in_specs=[
            pl.BlockSpec(lamv.shape, lambda h, i: (0, 0)),
            pl.BlockSpec((tq, LANES), lambda h, i: (i, h)),
            pl.BlockSpec((t, LANES), lambda h, i: (0, h)),
            pl.BlockSpec((t, LANES), lambda h, i: (0, h)),
            pl.BlockSpec((tq, LANES), lambda h, i: (i, h)),
            pl.BlockSpec((1, 2, tq, tk), lambda h, i: (h, 0, 0, 0)),
            pl.BlockSpec(sg.shape, lambda h, i: (0, 0)),
        ],
        out_specs=pl.BlockSpec((tq, LANES), lambda h, i: (i, h)),
        out_shape=jax.ShapeDtypeStruct((t, DA_WIDTH), BF16),
        scratch_shapes=[pltpu.VMEM((2 * tq, LANES), F32)] * 3,
        compiler_params=_params(("arbitrary", "arbitrary")),
        name="diff_attn_prompt",
    )(lamv, qn, knb, vb, za, tab, sg)


def _da_sample_kernel(lam_ref, q_ref, ck_ref, cv_ref, kn_ref, vn_ref, za_ref, tabc_ref, tabn_ref, sg_ref,
                      o_ref, m_sc, l_sc, acc_sc, *, ts):
    cols = lambda h: slice(h * LANES, (h + 1) * LANES)
    heads = range(DA_HEADS)
    qs = [_stack_halves(q_ref[:, cols(h)]) for h in heads]
    m_sc[...] = jnp.full(m_sc.shape, NEG, F32)
    l_sc[...] = jnp.zeros(l_sc.shape, F32)
    acc_sc[...] = jnp.zeros(acc_sc.shape, F32)
    kc = ck_ref[0].astype(BF16)
    vc = cv_ref[0].astype(BF16)
    _softmax_step(qs, [kc[:, cols(h)] for h in heads], [vc[:, cols(h)] for h in heads], tabc_ref[...],
                  m_sc, l_sc, acc_sc)
    pad = jnp.zeros((LANES - ts, LANES), BF16)
    kn = [jnp.concatenate([kn_ref[:, cols(h)].astype(BF16), pad], axis=0) for h in heads]
    vn = [jnp.concatenate([vn_ref[:, cols(h)].astype(BF16), pad], axis=0) for h in heads]
    _softmax_step(qs, kn, vn, tabn_ref[...], m_sc, l_sc, acc_sc)
    l = l_sc[...]
    acc = acc_sc[...]
    za = za_ref[...]
    rows = lambda h: slice(h * 2 * ts, (h + 1) * 2 * ts)
    o_ref[...] = jnp.concatenate(
        [_diff_finish(lam_ref, za[:, cols(h)], sg_ref[...], l[rows(h)], acc[rows(h)], ts) for h in heads],
        axis=1).astype(BF16)


def _da_sample(lamv, qn, ck, cv, kn, vn, za, tabc, tabn, sg, ts):
    nb, past, _ = ck.shape
    kern = functools.partial(_da_sample_kernel, ts=ts)
    stack = lambda tab: jnp.concatenate([tab, tab], axis=1).reshape(DA_HEADS * 2 * ts, tab.shape[-1])
    tabc, tabn = stack(tabc), stack(tabn)
    rows = pl.BlockSpec((ts, DA_WIDTH), lambda b: (b, 0))
    cache = pl.BlockSpec((1, past, DA_WIDTH), lambda b: (b, 0, 0))
    full = lambda a: pl.BlockSpec(a.shape, lambda b: (0,) * a.ndim)
    return pl.pallas_call(
        kern,
        grid=(nb,),
        in_specs=[full(lamv), rows, cache, cache, rows, rows, rows, full(tabc), full(tabn), full(sg)],
        out_specs=rows,
        out_shape=jax.ShapeDtypeStruct((nb * ts, DA_WIDTH), BF16),
        scratch_shapes=[pltpu.VMEM((DA_HEADS * 2 * ts, LANES), F32)] * 3,
        compiler_params=_params(("arbitrary",)),
        name="diff_attn_sample",
    )(lamv, qn, ck, cv, kn, vn, za, tabc, tabn, sg)


def _bias_kernel(idx_ref, bt_ref, o_ref):
    idx = idx_ref[...]
    bt = bt_ref[...]
    bt = bt - bt[:, FAR_BUCKET:FAR_BUCKET + 1]
    onehot = jnp.where(lax.broadcasted_iota(jnp.int32, (N_BUCKETS, idx.shape[1]), 0) == idx, 1.0, 0.0).astype(BF16)
    hi = bt.astype(BF16)
    rest = bt - hi.astype(F32)
    mid = rest.astype(BF16)
    lo = (rest - mid.astype(F32)).astype(BF16)
    val = _dot(hi, onehot) + _dot(mid, onehot) + _dot(lo, onehot)
    o_ref[...] = jnp.where(idx < 0, NEG, val * LOG2E)


def _bias_tables(rel_bias, tq, ts, past):
    def buckets(rel, mask):
        return np.where(mask, _t5_bucket_np(rel), -1).astype(np.int32).reshape(-1)

    r = np.arange(tq)[:, None]
    c = np.arange(tq)[None, :]
    qpos = past + np.arange(ts)[:, None]
    kc = np.arange(past)[None, :]
    kn = past + np.arange(LANES)[None, :]
    parts = [buckets(c - r, (c // CHUNK) <= (r // CHUNK)),
             buckets(c - r - tq, np.ones((tq, tq), bool)),
             buckets(kc - qpos, (kc // CHUNK) <= (qpos // CHUNK)),
             buckets(kn - qpos, ((kn // CHUNK) <= (qpos // CHUNK)) & (kn < past + ts))]
    blk = BIAS_LANES_PER_STEP
    used = sum(p.size for p in parts)
    n = -(-used // blk) * blk
    idx = np.concatenate(parts + [np.full(n - used, -1, np.int32)])
    bt = jnp.zeros((SUBLANES, N_BUCKETS), F32).at[:DA_HEADS].set(rel_bias.T)
    tab = pl.pallas_call(
        _bias_kernel,
        grid=(n // blk,),
        in_specs=[pl.BlockSpec((1, blk), lambda i: (0, i)), pl.BlockSpec(bt.shape, lambda i: (0, 0))],
        out_specs=pl.BlockSpec((SUBLANES, blk), lambda i: (0, i)),
        out_shape=jax.ShapeDtypeStruct((SUBLANES, n), F32),
        compiler_params=_params(("arbitrary",)),
        name="t5_bias_tables",
    )(jnp.asarray(idx).reshape(1, n), bt)[:DA_HEADS]
    o1 = 2 * tq * tq
    o2 = o1 + ts * past
    prompt_tab = tab[:, :o1].reshape(DA_HEADS, 2, tq, tq)
    tabc = tab[:, o1:o2].reshape(DA_HEADS, ts, past)
    tabn = tab[:, o2:used].reshape(DA_HEADS, ts, LANES)
    return prompt_tab, tabc, tabn


def _rwkv_kernel(prw_ref, look_ref, first_ref, zb_ref, s0_ref, mu_ref, w0_ref, a0_ref, wwa_ref,
                 kkw_ref, kaw_ref, rk_ref, lng_ref, lnb_ref, g2_ref, hm_ref, dm_ref, pm_ref, mo_ref,
                 ti_ref, to_ref,
                 ob_ref, sfin_ref,
                 s_sc, om_sc, rho_sc, at_sc, kt_sc, atb_sc, ktb_sc, wb_sc, vr_sc, o_sc, *, tm):
    j = pl.program_id(1)

    @pl.when(j == 0)
    def _():
        s_sc[...] = s0_ref[0]

    g2 = g2_ref[...]

    def head_sums(t):
        tb = t.astype(BF16)
        return jnp.concatenate([_dot(tb[:, c * RW_STATE_LANES:(c + 1) * RW_STATE_LANES], g2)
                                for c in range(RW_STATE_GROUPS)], axis=1)

    x = prw_ref[...]
    row0 = jnp.where(j == 0, first_ref[0], look_ref[7:8, :])
    rows = lax.broadcasted_iota(jnp.int32, x.shape, 0)
    prev = jnp.where(rows == 0, row0, pltpu.roll(x, 1, axis=0))
    mix = x + (prev - x) * mu_ref[...]
    r = mix[:, 0:512]
    kr = mix[:, 512:1024]
    vr = mix[:, 1024:1536]
    la = mix[:, 1536:1664]
    lane = lax.broadcasted_iota(jnp.int32, la.shape, 1)
    la = jnp.where(lane < HEAD, jnp.tanh(la), la)
    hi = la.astype(BF16)
    lo = (la - hi.astype(F32)).astype(BF16)
    wwa = wwa_ref[...]
    lora = _dot(hi, wwa[0]) + _dot(lo, wwa[0]) + _dot(hi, wwa[1])
    wpre = w0_ref[...] + lora[:, 0:512]
    nw = -wpre
    w = -(jnp.maximum(nw, 0.0) + jnp.log(1.0 + jnp.exp(-jnp.abs(nw)))) - 0.5
    ld = -jnp.exp(w)
    a = 1.0 / (1.0 + jnp.exp(-(a0_ref[...] + lora[:, 512:1024])))
    kk = kr * kkw_ref[...]
    nrm = jnp.sqrt(head_sums(kk * kk))
    kk = kk / jnp.maximum(nrm, 1e-12)
    kka = kk * a
    kh = kr * (1.0 + (a - 1.0) * kaw_ref[...])

    def dot3(m, t):
        hi = t.astype(BF16)
        r1 = t - hi.astype(F32)
        mid = r1.astype(BF16)
        lo = (r1 - mid.astype(F32)).astype(BF16)
        return _dot(m, hi) + _dot(m, mid) + _dot(m, lo)

    incl = dot3(ti_ref[...], ld)
    tot = dot3(to_ref[...], ld)
    e_rem = jnp.exp(tot - incl)
    e_inv = jnp.exp(-incl)
    om_sc[...] = jnp.exp(incl - ld) * kk
    rho_sc[...] = jnp.exp(incl) * r
    at_sc[...] = kka * e_inv
    kt_sc[...] = kh * e_inv
    atb_sc[...] = kka * e_rem
    ktb_sc[...] = kh * e_rem
    wb_sc[...] = jnp.exp(tot)
    hm = hm_ref[...]
    vx = (jnp.broadcast_to(vr[:, None, :], (tm, RW_HEADS, RW_WIDTH)) * hm[None]).reshape(tm * RW_HEADS, RW_WIDTH)
    vr_sc[...] = _dot(vx.astype(BF16), pm_ref[...])

    nb = RW_BLOCK
    heads_per_group = RW_HEADS // RW_STATE_GROUPS
    dm = dm_ref[...]

    def head_rows(blk):
        return jnp.concatenate([jnp.broadcast_to(blk[j:j + 1], (RW_HEADS, RW_WIDTH)) * hm for j in range(nb)],
                               axis=0)

    def fold(x):
        return x[:, :RW_STATE_LANES] + x[:, RW_STATE_LANES:]

    def query_rows(b):
        base = pl.multiple_of(b * nb, nb)
        return jnp.concatenate([head_rows(om_sc[pl.ds(base, nb), :]), head_rows(rho_sc[pl.ds(base, nb), :])], axis=0)

    nblk = tm // nb

    def coefficients(b, x):
        base = pl.multiple_of(b * nb, nb)
        rhs = jnp.concatenate([at_sc[pl.ds(base, nb), :], kt_sc[pl.ds(base, nb), :],
                               jnp.zeros((LANES - 2 * nb, RW_WIDTH), F32)], axis=0)
        xb = x.astype(BF16)
        rhs_o = jnp.concatenate([head_rows(at_sc[pl.ds(base, nb), :]), head_rows(kt_sc[pl.ds(base, nb), :])], axis=0)
        return _dot_nt(xb, rhs.astype(BF16)), _dot_nt(xb[nb * RW_HEADS:], rhs_o.astype(BF16))

    def block(b, carry):
        s, g, coef, coef_o = carry
        base = pl.multiple_of(b * nb, nb)
        atb_b, ktb_b, wb_b = (ref[pl.ds(base, nb), :] for ref in (atb_sc, ktb_sc, wb_sc))
        vr_b = vr_sc[pl.ds(pl.multiple_of(b * (nb * RW_HEADS), nb * RW_HEADS), nb * RW_HEADS), :]
        bn = jnp.minimum(b + 1, nblk - 1)
        xn = query_rows(bn)
        coef_next, coef_o_next = coefficients(bn, xn)
        rhs2 =jnp.concatenate([fold(head_rows(atb_b)), fold(head_rows(ktb_b))], axis=0).astype(BF16)
        cross = _dot_nt(fold(xn).astype(BF16), rhs2).astype(BF16)
        g_decayed = _dot_nt(fold(xn * wb_b[0:1]).astype(BF16), s.astype(BF16))

        def rows(a, j):
            return a[j * RW_HEADS:(j + 1) * RW_HEADS]

        def cf(j, lane):
            return jnp.broadcast_to(rows(coef, j)[:, lane:lane + 1], (RW_HEADS, LANES))

        sa = []
        for j in range(nb):
            acc = rows(g, j) * rows(dm, j)
            for i in range(j):
                acc = acc + cf(j, nb + i) * rows(vr_b, i)
            for i in range(j):
                acc = acc - cf(j, i) * sa[i]
            sa.append(acc)
        lhs = jnp.concatenate([-t for t in sa] + [vr_b], axis=0).astype(BF16)
        o_all = g[nb * RW_HEADS:] * dm + _dot((coef_o * mo_ref[...]).astype(BF16), lhs)
        outs = [rows(o_all, j) for j in range(nb)]
        g_next = g_decayed + _dot(cross, lhs)
        ds = lax.dot_general(lhs, rhs2, (((0,), (0,)), ((), ())), preferred_element_type=F32)
        wb = jnp.concatenate(
            [jnp.broadcast_to(wb_b[0:1, c * RW_STATE_LANES:(c + 1) * RW_STATE_LANES], (HEAD, RW_STATE_LANES))
             for c in range(RW_STATE_GROUPS)], axis=0)
        for h in range(RW_HEADS):
            o_sc[h, pl.ds(base, nb), :] = jnp.concatenate([t[h:h + 1] for t in outs], axis=0)
        return s * wb + ds, g_next, coef_next, coef_o_next

    s0 = s_sc[...]
    x0 = query_rows(0)
    g0 = _dot_nt(fold(x0).astype(BF16), s0.astype(BF16))
    s_sc[...] = lax.fori_loop(0, nblk, block, (s0, g0) + coefficients(0, x0), unroll=min(RW_UNROLL, nblk))[0]

    o = jnp.concatenate([o_sc[c * heads_per_group + hh][:, c * HEAD:(c + 1) * HEAD]
                         for c in range(RW_STATE_GROUPS) for hh in range(heads_per_group)], axis=1)
    mean = head_sums(o) * (1.0 / HEAD)
    d = o - mean
    var = head_sums(d * d) * (1.0 / HEAD)
    y = d * lax.rsqrt(var + RW_GN_EPS) * lng_ref[...] + lnb_ref[...]
    bonus = head_sums(r * kh * rk_ref[...]) * vr
    ob_ref[...] = ((y + bonus) * _silu(zb_ref[...])).astype(BF16)

    @pl.when(j == pl.num_programs(1) - 1)
    def _():
        sfin_ref[0] = s_sc[...]


def _block_sum_matrices(tm):
    t = np.arange(tm)
    same = (t[:, None] // RW_BLOCK) == (t[None, :] // RW_BLOCK)
    incl = same & (t[None, :] <= t[:, None])
    return [jnp.asarray(incl.astype(np.float32), dtype=BF16), jnp.asarray(same.astype(np.float32), dtype=BF16)]


def _rwkv(prw, first, zb, s0, p, seq_len, tm):
    m = prw.shape[0]
    nseq = m // seq_len
    ntile = seq_len // tm
    kern = functools.partial(_rwkv_kernel, tm=tm)
    full = lambda a: pl.BlockSpec(a.shape, lambda b, j: (0,) * a.ndim)
    rows = lambda c: pl.BlockSpec((tm, c), lambda b, j: (b * ntile + j, 0))
    look = pl.BlockSpec((8, RW_SHIFT_COLS), lambda b, j: (jnp.maximum((b * ntile + j) * (tm // 8) - 1, 0), 0))
    consts = [p["mu"], p["w0"], p["a0"], p["wwa"], p["kkw"], p["kaw"], p["rk"], p["lng"], p["lnb"],
              p["g2"], p["hm"], p["dm"], p["pm"], p["mo"]] + _block_sum_matrices(tm)
    return pl.pallas_call(
        kern,
        grid=(nseq, ntile),
        in_specs=[rows(RW_SHIFT_COLS), look,
                  pl.BlockSpec((1, 1, RW_SHIFT_COLS), lambda b, j: (b, 0, 0)),
                  rows(RW_WIDTH),
                  pl.BlockSpec((1,) + RW_STATE_SHAPE, lambda b, j: (b, 0, 0))] + [full(c) for c in consts],
        out_specs=[rows(RW_WIDTH), pl.BlockSpec((1,) + RW_STATE_SHAPE, lambda b, j: (b, 0, 0))],
        out_shape=[jax.ShapeDtypeStruct((m, RW_WIDTH), BF16), jax.ShapeDtypeStruct((nseq,) + RW_STATE_SHAPE, F32)],
        scratch_shapes=([pltpu.VMEM(RW_STATE_SHAPE, F32)] + [pltpu.VMEM((tm, RW_WIDTH), F32)] * 7
                        + [pltpu.VMEM((tm * RW_HEADS, RW_STATE_SHAPE[0]), F32),
                           pltpu.VMEM((RW_HEADS, tm, RW_STATE_SHAPE[0]), F32)]),
        compiler_params=_params(("arbitrary", "arbitrary")),
        name="rwkv7",
    )(prw, prw, first, zb, s0, *consts)


def _state_to_rows(s):
    b = s.shape[0]
    per = RW_HEADS // RW_STATE_GROUPS
    return s.reshape(b, RW_STATE_GROUPS, per, HEAD, HEAD).transpose(0, 1, 3, 2, 4).reshape((b,) + RW_STATE_SHAPE)


def _rows_to_state(s):
    b = s.shape[0]
    per = RW_HEADS // RW_STATE_GROUPS
    return s.reshape(b, RW_STATE_GROUPS, HEAD, per, HEAD).transpose(0, 1, 3, 2, 4).reshape(b, RW_HEADS, HEAD, HEAD)


SKIP_LOG = -104.0
SB_RECENT_KEYS = 256


def _sb_tile(qs, k, v, u, c, mask):
    tk = k[0].shape[0]
    rows = qs[0].shape[0]
    z = jnp.concatenate([_dot_nt(q_p, k_p) for q_p, k_p in zip(qs, k)], axis=0)
    nz = -z
    lg = jnp.minimum(nz, 0.0) - jnp.log(1.0 + jnp.exp(jnp.minimum(z, nz)))
    if mask is not None:
        lg = jnp.where(mask, lg, 0.0)
    it = _dot(lg.astype(BF16), u)
    a = jnp.exp(z + it + jnp.concatenate([c] * (tk // LANES), axis=1))
    if mask is not None:
        a = jnp.where(mask, a, 0.0)
    a = a.astype(BF16)
    da = jnp.concatenate([_dot(a[p * rows:(p + 1) * rows], v_p) for p, v_p in enumerate(v)], axis=0)
    return da, c + jnp.broadcast_to(it[:, 0:1], c.shape)


def _sb_finish(acc, z):
    t = z.shape[0]
    lane = lax.broadcasted_iota(jnp.int32, z.shape, 1)
    return jnp.where(lane < HEAD, acc[:t], acc[t:]) * _silu(z)


def _sb_prompt_kernel(q_ref, k_ref, v_ref, z_ref, u_ref, o_ref, acc_sc, c_sc, *, tq):
    i = pl.program_id(1)
    tk = u_ref.shape[0]
    r = tq // tk
    qs = _stack_halves(q_ref[...])
    u = u_ref[...]
    acc_sc[...] = jnp.zeros(acc_sc.shape, F32)
    c_sc[...] = jnp.zeros(c_sc.shape, F32)
    def tile(j, lo, hi, masked, valid=None):
        n = hi - lo
        halves = lambda x: jnp.concatenate([x[lo:hi], x[tq + lo:tq + hi]], axis=0)
        off = pl.multiple_of(j * tk, tk)
        mask = valid
        if masked:
            qrow = lax.broadcasted_iota(jnp.int32, (2 * n, tk), 0) % n
            col = lax.broadcasted_iota(jnp.int32, (2 * n, tk), 1)
            mask = col < qrow
        da, c = _sb_tile([halves(qs)], [k_ref[pl.ds(off, tk), :]], [v_ref[pl.ds(off, tk), :]], u,
                         halves(c_sc[...]), mask)
        for h in range(2):
            acc_sc[h * tq + lo:h * tq + hi, :] += da[h * n:(h + 1) * n]
            c_sc[h * tq + lo:h * tq + hi, :] = c[h * n:(h + 1) * n]

    for jj in reversed(range(r)):
        tile(i * r + jj, jj * tk, tq, True)

    tile(jnp.maximum(i * r - 1, 0), 0, tq, False, valid=i > 0)

    def body(st):
        tile(i * r - 1 - st[0], 0, tq, False)
        return st[0] + 1, jnp.max(c_sc[...])

    lax.while_loop(lambda st: jnp.logical_and(st[0] < i * r, st[1] > SKIP_LOG), body,
                   (jnp.int32(1), jnp.max(c_sc[...])))
    o_ref[...] = _sb_finish(acc_sc[...], z_ref[...]).astype(BF16)


def _sb_prompt(qb, kb, vb, z, u, tq):
    t = qb.shape[0]
    kern = functools.partial(_sb_prompt_kernel, tq=tq)
    rows = pl.BlockSpec((tq, LANES), lambda h, i: (i, h))
    whole = pl.BlockSpec((t, LANES), lambda h, i: (0, h))
    return pl.pallas_call(
        kern,
        grid=(SB_HEADS // 2, t // tq),
        in_specs=[rows, whole, whole, rows, pl.BlockSpec(u.shape, lambda h, i: (0, 0))],
        out_specs=rows,
        out_shape=jax.ShapeDtypeStruct((t, SB_WIDTH), BF16),
        scratch_shapes=[pltpu.VMEM((2 * tq, LANES), F32)] * 2,
        compiler_params=_params(("arbitrary", "arbitrary")),
        name="stick_breaking_prompt",
    )(qb, kb, vb, z, u)


def _sb_sample_kernel(q_ref, ck_ref, cv_ref, kn_ref, vn_ref, z_ref, u_ref, o_ref, left_ref, acc_sc, c_sc, *, ts):
    tk = u_ref.shape[0]
    ntile = ck_ref.shape[1] // tk
    npair = SB_HEADS // 2
    cols = lambda p: slice(p * LANES, (p + 1) * LANES)
    qs = [_stack_halves(q_ref[:, cols(p)]) for p in range(npair)]
    u = u_ref[...]
    pad = jnp.zeros((tk - ts, LANES), BF16)
    kn = [jnp.concatenate([kn_ref[:, cols(p)], pad], axis=0) for p in range(npair)]
    vn = [jnp.concatenate([vn_ref[:, cols(p)], pad], axis=0) for p in range(npair)]
    rows = npair * 2 * ts
    qrow = lax.broadcasted_iota(jnp.int32, (rows, tk), 0) % ts
    col = lax.broadcasted_iota(jnp.int32, (rows, tk), 1)
    acc, c = _sb_tile(qs, kn, vn, u, jnp.zeros((rows, LANES), F32), col < qrow)

    def cache_tile(n, c):
        off = pl.multiple_of((ntile - 1 - n) * tk, tk)
        kc = ck_ref[0, pl.ds(off, tk), :].astype(BF16)
        vc = cv_ref[0, pl.ds(off, tk), :].astype(BF16)
        return _sb_tile(qs, [kc[:, cols(p)] for p in range(npair)], [vc[:, cols(p)] for p in range(npair)], u, c, None)

    da, c = cache_tile(0, c)
    acc_sc[...] = acc + da
    c_sc[...] = c

    def body(st):
        da, c = cache_tile(st[0], c_sc[...])
        acc_sc[...] += da
        c_sc[...] = c
        return st[0] + 1, jnp.max(c)

    lax.while_loop(lambda st: jnp.logical_and(st[0] < ntile, st[1] > SKIP_LOG), body,
                   (jnp.int32(1), jnp.max(c)))
    left_ref[0] = jnp.broadcast_to(jnp.max(c_sc[...], axis=0, keepdims=True), (SUBLANES, LANES))
    acc = acc_sc[...]
    z = z_ref[...]
    o_ref[...] = jnp.concatenate(
        [_sb_finish(acc[p * 2 * ts:(p + 1) * 2 * ts], z[:, cols(p)]) for p in range(npair)], axis=1).astype(BF16)


def _sb_sample(qb, ck, cv, kb, vb, z, u, ts):
    nb, past, _ = ck.shape
    kern = functools.partial(_sb_sample_kernel, ts=ts)
    rows = pl.BlockSpec((ts, SB_WIDTH), lambda b: (b, 0))
    cache = pl.BlockSpec((1, past, SB_WIDTH), lambda b: (b, 0, 0))
    return pl.pallas_call(
        kern,
        grid=(nb,),
        in_specs=[rows, cache, cache, rows, rows, rows, pl.BlockSpec(u.shape, lambda b: (0, 0))],
        out_specs=[rows, pl.BlockSpec((1, SUBLANES, LANES), lambda b: (b, 0, 0))],
        out_shape=[jax.ShapeDtypeStruct((nb * ts, SB_WIDTH), BF16),
                   jax.ShapeDtypeStruct((nb, SUBLANES, LANES), F32)],
        scratch_shapes=[pltpu.VMEM((SB_HEADS * ts, LANES), F32)] * 2,
        compiler_params=_params(("arbitrary",)),
        name="stick_breaking_sample",
    )(qb, ck, cv, kb, vb, z, u)


def _sb_sample_two_phase(qb, cache_k, cache_v, kb, vb, z, u, ts):
    nb, past = cache_k.shape[:2]
    dense = lambda c: c.reshape(nb, c.shape[1], SB_WIDTH)
    recent = min(past, SB_RECENT_KEYS)
    out, left = _sb_sample(qb, dense(cache_k[:, past - recent:]), dense(cache_v[:, past - recent:]), kb, vb, z, u, ts)
    if recent == past:
        return out
    return lax.cond(jnp.max(left) > SKIP_LOG,
                    lambda: _sb_sample(qb, dense(cache_k), dense(cache_v), kb, vb, z, u, ts)[0],
                    lambda: out)


def _suffix_matrix(tk):
    j = np.arange(tk)[:, None]
    s = np.arange(tk)[None, :]
    return jnp.asarray((j >= s).astype(np.float32), dtype=BF16)


def _row_tile(m):
    for tm in PROJ_TILES:
        if m % tm == 0:
            return tm
    return m


def kernel(x_prompt, x_sample, cache_l0_k, cache_l0_v, state_l0_shift, state_l0_wkv, cache_l1_k, cache_l1_v, rel_bias, norm_l0, w_in_l0, w_out_l0, da_q_norm, da_k_norm, da_lambda_q1, da_lambda_k1, da_lambda_q2, da_lambda_k2, da_subln, rw_mu, rw_w0, rw_w_up, rw_a0, rw_a_up, rw_k_k, rw_k_a, rw_r_k, rw_lnx_g, rw_lnx_b, norm_l1, w_in_l1, w_out_l1):
    bp, t, d = x_prompt.shape
    nb, ts, _ = x_sample.shape
    past = cache_l0_k.shape[1]
    assert bp == 1 and ts <= LANES and ts % RW_BLOCK == 0
    assert t % max(SB_TILE, DA_TILE, RW_TILE) == 0 and past % LANES == 0 and DA_TILE >= MAX_DISTANCE

    row = lambda v: v.reshape(1, -1).astype(F32)
    w_in0 = w_in_l0.astype(BF16)
    w_in1 = w_in_l1.astype(BF16)
    w_out0a = w_out_l0[:DA_WIDTH].astype(BF16)
    w_out0b = w_out_l0[DA_WIDTH:].astype(BF16)
    w_out1 = w_out_l1.astype(BF16)
    qg = row(jnp.tile(da_q_norm, 512 // HEAD))
    kg = row(jnp.tile(da_k_norm, 512 // HEAD))
    lamv = jnp.stack([da_lambda_q1, da_lambda_k1, da_lambda_q2, da_lambda_k2]).astype(F32)
    sg = row(da_subln)
    zeros = jnp.zeros((HEAD, RW_WIDTH), F32)
    wwa_f = jnp.concatenate([jnp.concatenate([rw_w_up, zeros], axis=1),
                             jnp.concatenate([zeros, rw_a_up], axis=1)], axis=0)
    wwa_hi = wwa_f.astype(BF16)
    wwa = jnp.stack([wwa_hi, (wwa_f - wwa_hi.astype(F32)).astype(BF16)])
    heads = np.arange(RW_HEADS)
    per_group = RW_HEADS // RW_STATE_GROUPS
    chan = np.arange(RW_WIDTH)
    out_lane = np.arange(RW_STATE_SHAPE[0])
    hm = np.equal(heads[:, None], chan[None, :] // HEAD)
    dm = np.equal(heads[:, None] // per_group, out_lane[None, :] // HEAD)
    pm = (chan[:, None] // HEAD // per_group == out_lane[None, :] // HEAD) & (chan[:, None] % HEAD == out_lane[None, :] % HEAD)
    tok_row = np.arange(RW_BLOCK * RW_HEADS)[:, None] // RW_HEADS
    tok_lane = (np.arange(LANES)[None, :] % (RW_BLOCK * RW_HEADS)) // RW_HEADS
    rw = dict(mu=row(rw_mu), w0=row(rw_w0), a0=row(rw_a0), wwa=wwa, kkw=row(rw_k_k), kaw=row(rw_k_a),
              rk=row(rw_r_k), lng=row(rw_lnx_g), lnb=row(rw_lnx_b), g2=_group_matrix(RW_STATE_LANES, 1.0),
              hm=jnp.asarray(hm.astype(np.float32)),
              dm=jnp.asarray(np.tile(dm.astype(np.float32), (RW_BLOCK, 1))),
              pm=jnp.asarray(pm.astype(np.float32), dtype=BF16),
              mo=jnp.asarray((tok_lane <= tok_row).astype(np.float32)))
    prompt_tab, tabc, tabn = _bias_tables(rel_bias.astype(F32), DA_TILE, ts, past)
    u_prompt = _suffix_matrix(2 * LANES)
    u_sample = _suffix_matrix(LANES)

    def layer0(x, first, s0, seq_len, attend):
        m = x.shape[0]
        tm = _row_tile(m)
        qn, kn, knb, v, vb, za, prw, zb = _inproj0(x, row(norm_l0), w_in0, qg, kg, tm)
        oa = attend(qn, kn, knb, v, vb, za)
        ob, sfin = _rwkv(prw, first, zb, _state_to_rows(s0), rw, seq_len, min(seq_len, RW_TILE))
        y = _outproj(x, [oa, ob], [w_out0a, w_out0b], tm)
        shift = prw.reshape(m // seq_len, seq_len, RW_SHIFT_COLS)[:, -1:]
        return y, kn, v, shift, _rows_to_state(sfin)

    def layer1(x, attend):
        m = x.shape[0]
        tm = _row_tile(m)
        qb, k, kb, v, vb, z = _inproj1(x, row(norm_l1), w_in1, tm)
        o = attend(qb, kb, vb, z)
        return _outproj(x, [o], [w_out1], tm), k, v

    xp = x_prompt.reshape(t, d)
    yp, k0p, v0p, shp, wkvp = layer0(
        xp, jnp.zeros((1, 1, RW_SHIFT_COLS), F32), jnp.zeros((1, RW_HEADS, HEAD, HEAD), F32), t,
        lambda qn, kn, knb, v, vb, za: _da_prompt(lamv, qn, knb, vb, za, prompt_tab, sg, DA_TILE))
    yp, k1p, v1p = layer1(yp, lambda qb, kb, vb, z: _sb_prompt(qb, kb, vb, z, u_prompt, SB_TILE))

    xs = x_sample.reshape(nb * ts, d)
    ck0 = cache_l0_k.reshape(nb, past, DA_WIDTH)
    cv0 = cache_l0_v.reshape(nb, past, DA_WIDTH)
    ys, k0s, v0s, shs, wkvs = layer0(
        xs, state_l0_shift, state_l0_wkv, ts,
        lambda qn, kn, knb, v, vb, za: _da_sample(lamv, qn, ck0, cv0, knb, vb, za, tabc, tabn, sg, ts))
    ys, k1s, v1s = layer1(ys, lambda qb, kb, vb, z: _sb_sample_two_phase(
        qb, cache_l1_k, cache_l1_v, kb, vb, z, u_sample, ts))

    return (yp.reshape(1, t, d), ys.reshape(nb, ts, d),
            k0p.reshape(1, t, DA_HEADS, LANES), v0p.reshape(1, t, DA_HEADS, LANES), shp, wkvp,
            k1p.reshape(1, t, SB_HEADS, HEAD), v1p.reshape(1, t, SB_HEADS, HEAD),
            k0s.reshape(nb, ts, DA_HEADS, LANES), v0s.reshape(nb, ts, DA_HEADS, LANES), shs, wkvs,
            k1s.reshape(nb, ts, SB_HEADS, HEAD), v1s.reshape(nb, ts, SB_HEADS, HEAD))
```

```python
import functools
import math

import numpy as np
import jax
import jax.numpy as jnp
from jax import lax
from jax.experimental import pallas as pl
from jax.experimental.pallas import tpu as pltpu

F32 = jnp.float32
BF16 = jnp.bfloat16

EPS = 1e-6
NEG = -1e30
CHUNK = 64
LANES = 128
SUBLANES = 8
HEAD = 64
DA_HEADS = 4
DA_WIDTH = 512
RW_WIDTH = 512
RW_HEADS = 8
RW_SHIFT_COLS = 3 * RW_WIDTH + 128
RW_GN_EPS = 64e-5
RW_STATE_LANES = 256
RW_STATE_GROUPS = RW_WIDTH // RW_STATE_LANES
RW_STATE_SHAPE = (RW_STATE_GROUPS * HEAD, RW_STATE_LANES)
RW_BLOCK = 8
RW_UNROLL = 4
SB_HEADS = 16
SB_WIDTH = 1024
N_BUCKETS = 32
MAX_DISTANCE = 128
LAMBDA_INIT = 0.8 - 0.6 * math.exp(-0.3 * 0)
LOG2E = math.log2(math.e)
VMEM_LIMIT = 56 * 1024 * 1024
PROJ_TILES = (512, 256)
BOUNDARY_TILE = 256
DA_TILE = 512
SB_TILE = 512
RW_TILE = 256


def _params(sem):
    return pltpu.CompilerParams(dimension_semantics=sem, vmem_limit_bytes=VMEM_LIMIT)


def _dot(a, b):
    return jnp.dot(a, b, preferred_element_type=F32)


def _dot_nt(a, b):
    return lax.dot_general(a, b, (((1,), (1,)), ((), ())), preferred_element_type=F32)


def _silu(z):
    return z / (1.0 + jnp.exp(-z))


def _rms(x, g):
    return x * lax.rsqrt(jnp.mean(x * x, axis=-1, keepdims=True) + EPS) * g


def _group_matrix(n, scale):
    idx = np.arange(n) // HEAD
    return jnp.asarray((idx[:, None] == idx[None, :]).astype(np.float32) * scale, dtype=BF16)


def _inproj0_kernel(x_ref, g_ref, w_ref, qg_ref, kg_ref, avg_ref,
                    qn_ref, kn_ref, knb_ref, v_ref, vb_ref, za_ref, prw_ref, zb_ref):
    y = _dot(_rms(x_ref[...], g_ref[...]).astype(BF16), w_ref[...])

    def head_norm(t, g):
        sq = (t * t).astype(BF16)
        half = avg_ref.shape[0]
        ms = jnp.concatenate([_dot(sq[:, c:c + half], avg_ref[...]) for c in range(0, t.shape[1], half)], axis=1)
        return t * lax.rsqrt(ms + EPS) * g

    qn_ref[...] = (head_norm(y[:, 0:512], qg_ref[...]) * (HEAD ** -0.5 * LOG2E)).astype(BF16)
    kn = head_norm(y[:, 512:1024], kg_ref[...])
    knb_ref[...] = kn.astype(BF16)
    v = y[:, 1024:1536]
    vb_ref[...] = v.astype(BF16)
    for h in range(DA_HEADS):
        kn_ref[:, h, :] = kn[:, h * LANES:(h + 1) * LANES]
        v_ref[:, h, :] = v[:, h * LANES:(h + 1) * LANES]
    za_ref[...] = y[:, 1536:2048]
    prw_ref[...] = y[:, 2048:2048 + RW_SHIFT_COLS]
    zb_ref[...] = y[:, 2048 + RW_SHIFT_COLS:]


def _inproj0(x, g, w, qg, kg, tm):
    m, d = x.shape
    n = w.shape[1]
    row = lambda c: pl.BlockSpec((tm, c), lambda i: (i, 0))
    full = lambda a: pl.BlockSpec(a.shape, lambda i: (0,) * a.ndim)
    avg = _group_matrix(2 * LANES, 1.0 / HEAD)
    widths = [(512, BF16), (None, F32), (512, BF16), (None, F32), (512, BF16), (512, F32),
              (RW_SHIFT_COLS, F32), (512, F32)]
    heads = (DA_HEADS, LANES)
    return pl.pallas_call(
        _inproj0_kernel,
        grid=(m // tm,),
        in_specs=[row(d), full(g), full(w), full(qg), full(kg), full(avg)],
        out_specs=[row(c) if c else pl.BlockSpec((tm,) + heads, lambda i: (i, 0, 0)) for c, _ in widths],
        out_shape=[jax.ShapeDtypeStruct((m, c) if c else (m,) + heads, dt) for c, dt in widths],
        compiler_params=_params(("arbitrary",)),
        name="inproj0",
    )(x, g, w, qg, kg, avg)


def _inproj1_kernel(x_ref, g_ref, w_ref, qb_ref, k_ref, kb_ref, v_ref, vb_ref, z_ref):
    _inproj1_body(x_ref[...], g_ref, w_ref, qb_ref, k_ref, kb_ref, v_ref, vb_ref, z_ref)


def _boundary_kernel(x_ref, a1_ref, a2_ref, w1_ref, w2_ref, g_ref, w_ref,
                     y_ref, qb_ref, k_ref, kb_ref, v_ref, vb_ref, z_ref):
    y = x_ref[...] + _dot(a1_ref[...], w1_ref[...]) + _dot(a2_ref[...], w2_ref[...])
    y_ref[...] = y
    _inproj1_body(y, g_ref, w_ref, qb_ref, k_ref, kb_ref, v_ref, vb_ref, z_ref)


def _boundary(x, a1, a2, w1, w2, g, w, tm):
    m, d = x.shape
    row = lambda c: pl.BlockSpec((tm, c), lambda i: (i, 0))
    full = lambda a: pl.BlockSpec(a.shape, lambda i: (0,) * a.ndim)
    dts = [F32, BF16, F32, BF16, F32, BF16, F32]
    return pl.pallas_call(
        _boundary_kernel,
        grid=(m // tm,),
        in_specs=[row(d), row(a1.shape[1]), row(a2.shape[1]), full(w1), full(w2), full(g), full(w)],
        out_specs=[row(1024) for _ in dts],
        out_shape=[jax.ShapeDtypeStruct((m, 1024), dt) for dt in dts],
        compiler_params=_params(("arbitrary",)),
        name="outproj0_inproj1",
    )(x, a1, a2, w1, w2, g, w)


def _inproj1_body(x, g_ref, w_ref, qb_ref, k_ref, kb_ref, v_ref, vb_ref, z_ref):
    y = _dot(_rms(x, g_ref[...]).astype(BF16), w_ref[...])
    qb_ref[...] = (y[:, 0:1024] * (HEAD ** -0.5)).astype(BF16)
    k = y[:, 1024:2048]
    k_ref[...] = k
    kb_ref[...] = k.astype(BF16)
    v = y[:, 2048:3072]
    v_ref[...] = v
    vb_ref[...] = v.astype(BF16)
    z_ref[...] = y[:, 3072:4096]


def _inproj1(x, g, w, tm):
    m, d = x.shape
    row = lambda c: pl.BlockSpec((tm, c), lambda i: (i, 0))
    full = lambda a: pl.BlockSpec(a.shape, lambda i: (0,) * a.ndim)
    dts = [BF16, F32, BF16, F32, BF16, F32]
    return pl.pallas_call(
        _inproj1_kernel,
        grid=(m // tm,),
        in_specs=[row(d), full(g), full(w)],
        out_specs=[row(1024) for _ in dts],
        out_shape=[jax.ShapeDtypeStruct((m, 1024), dt) for dt in dts],
        compiler_params=_params(("arbitrary",)),
        name="inproj1",
    )(x, g, w)


def _outproj_kernel(*refs):
    n = (len(refs) - 2) // 2
    x_ref, o_ref = refs[0], refs[-1]
    acc = x_ref[...]
    for a_ref, w_ref in zip(refs[1:1 + n], refs[1 + n:1 + 2 * n]):
        acc = acc + _dot(a_ref[...], w_ref[...])
    o_ref[...] = acc


def _outproj(x, acts, ws, tm):
    m, d = x.shape
    row = lambda c: pl.BlockSpec((tm, c), lambda i: (i, 0))
    full = lambda a: pl.BlockSpec(a.shape, lambda i: (0,) * a.ndim)
    return pl.pallas_call(
        _outproj_kernel,
        grid=(m // tm,),
        in_specs=[row(d)] + [row(a.shape[1]) for a in acts] + [full(w) for w in ws],
        out_specs=row(d),
        out_shape=jax.ShapeDtypeStruct((m, d), F32),
        compiler_params=_params(("arbitrary",)),
        name="outproj",
    )(x, *acts, *ws)


def _t5_bucket_np(rel):
    nb = N_BUCKETS // 2
    max_exact = nb // 2
    n = np.abs(rel)
    nf = np.maximum(n, 1).astype(np.float32)
    large = max_exact + (np.log(nf / np.float32(max_exact)) / np.float32(math.log(MAX_DISTANCE / max_exact))
                         * np.float32(nb - max_exact)).astype(np.int32)
    large = np.minimum(large, nb - 1)
    return np.where(rel > 0, nb, 0) + np.where(n < max_exact, n, large)


FAR_BUCKET = N_BUCKETS // 2 - 1
BIAS_LANES_PER_STEP = 32768
FAR_TILES = ((4, 2), (4, 1), (2, 1), (1, 1))


def _stack_halves(q):
    lane = lax.broadcasted_iota(jnp.int32, q.shape, 1)
    zero = jnp.zeros_like(q)
    return jnp.concatenate([jnp.where(lane < HEAD, q, zero), jnp.where(lane >= HEAD, q, zero)], axis=0)


def _softmax_step(qs, k, v, bias, m_sc, l_sc, acc_sc):
    s = jnp.concatenate([_dot_nt(q_h, k_h) for q_h, k_h in zip(qs, k)], axis=0)
    if bias is not None:
        s = s + bias
    _softmax_update(s, v, m_sc, l_sc, acc_sc)


def _softmax_update(s, v, m_sc, l_sc, acc_sc):
    reps = s.shape[1] // LANES
    m_old = m_sc[...]
    m_new = jnp.maximum(m_old, jnp.max(s, axis=-1, keepdims=True))
    alpha = jnp.exp2(m_old - m_new)
    p = jnp.exp2(s - jnp.concatenate([m_new] * reps, axis=1))
    psum = p[:, :LANES]
    for r in range(1, reps):
        psum = psum + p[:, r * LANES:(r + 1) * LANES]
    l_sc[...] = alpha * l_sc[...] + psum
    p = p.astype(BF16)
    rows = p.shape[0] // len(v)
    pv = jnp.concatenate([_dot(p[h * rows:(h + 1) * rows], v_h) for h, v_h in enumerate(v)], axis=0)
    acc_sc[...] = alpha * acc_sc[...] + pv
    m_sc[...] = m_new


def _diff_finish(lam_ref, za, sg, l, acc, tq):
    lv = lam_ref[...]
    lam = (jnp.exp(jnp.sum(lv[0:1] * lv[1:2], axis=-1, keepdims=True))
           - jnp.exp(jnp.sum(lv[2:3] * lv[3:4], axis=-1, keepdims=True)) + LAMBDA_INIT)
    o = acc / jnp.sum(l, axis=-1, keepdims=True)
    attn = o[:tq] - lam * o[tq:]
    return (_rms(attn, sg) * (1.0 - LAMBDA_INIT)) * _silu(za)


def _da_prompt_kernel(lam_ref, q_ref, k_ref, v_ref, za_ref, tab_ref, sg_ref, o_ref,
                      m_sc, l_sc, acc_sc, *, tq, tk):
    i = pl.program_id(1)
    qs = _stack_halves(q_ref[...])
    m_sc[...] = jnp.full(m_sc.shape, NEG, F32)
    l_sc[...] = jnp.zeros(l_sc.shape, F32)
    acc_sc[...] = jnp.zeros(acc_sc.shape, F32)

    def tile(j, width, bias):
        off = pl.multiple_of(j * tk, tk)
        _softmax_step([qs], [k_ref[pl.ds(off, width), :]], [v_ref[pl.ds(off, width), :]], bias,
                      m_sc, l_sc, acc_sc)

    nfar = jnp.maximum(i - 1, 0)

    done = 0
    for width, reps in FAR_TILES:
        def far(j, c, width=width, reps=reps, done=done):
            for n in range(reps):
                tile(done + width * (reps * j + n), width * tk, None)
            return c

        trips = (nfar - done) // (width * reps)
        lax.fori_loop(0, trips, far, 0)
        done = done + trips * width * reps

    @pl.when(i >= 1)
    def _():
        b = jnp.concatenate([tab_ref[0, 1], tab_ref[0, 0]], axis=1)
        tile(i - 1, 2 * tk, jnp.concatenate([b, b], axis=0))

    @pl.when(i == 0)
    def _():
        b = tab_ref[0, 0]
        tile(0, tk, jnp.concatenate([b, b], axis=0))
    o_ref[...] = _diff_finish(lam_ref, za_ref[...], sg_ref[...], l_sc[...], acc_sc[...], tq).astype(BF16)


def _da_prompt(lamv, qn, knb, vb, za, tab, sg, tq):
    t = qn.shape[0]
    tk = tq
    kern = functools.partial(_da_prompt_kernel, tq=tq, tk=tk)
    return pl.pallas_call(
        kern,
        grid=(DA_HEADS, t // tq),
        in_specs=[
            pl.BlockSpec(lamv.shape, lambda h, i: (0, 0)),
            pl.BlockSpec((tq, LANES), lambda h, i: (i, h)),
            pl.BlockSpec((t, LANES), lambda h, i: (0, h)),
            pl.BlockSpec((t, LANES), lambda h, i: (0, h)),
            pl.BlockSpec((tq, LANES), lambda h, i: (i, h)),
            pl.BlockSpec((1, 2, tq, tk), lambda h, i: (h, 0, 0, 0)),
            pl.BlockSpec(sg.shape, lambda h, i: (0, 0)),
        ],
        out_specs=pl.BlockSpec((tq, LANES), lambda h, i: (i, h)),
        out_shape=jax.ShapeDtypeStruct((t, DA_WIDTH), BF16),
        scratch_shapes=[pltpu.VMEM((2 * tq, LANES), F32)] * 3,
        compiler_params=_params(("arbitrary", "arbitrary")),
        name="diff_attn_prompt",
    )(lamv, qn, knb, vb, za, tab, sg)


def _da_sample_kernel(lam_ref, q_ref, ck_ref, cv_ref, kn_ref, vn_ref, za_ref, tabc_ref, tabn_ref, sg_ref,
                      o_ref, m_sc, l_sc, acc_sc, *, ts):
    cols = lambda h: slice(h * LANES, (h + 1) * LANES)
    heads = range(DA_HEADS)
    qs = [_stack_halves(q_ref[:, cols(h)]) for h in heads]
    m_sc[...] = jnp.full(m_sc.shape, NEG, F32)
    l_sc[...] = jnp.zeros(l_sc.shape, F32)
    acc_sc[...] = jnp.zeros(acc_sc.shape, F32)
    kc = ck_ref[0].astype(BF16)
    vc = cv_ref[0].astype(BF16)
    _softmax_step(qs, [kc[:, cols(h)] for h in heads], [vc[:, cols(h)] for h in heads], tabc_ref[...],
                  m_sc, l_sc, acc_sc)
    pad = jnp.zeros((LANES - ts, LANES), BF16)
    kn = [jnp.concatenate([kn_ref[:, cols(h)].astype(BF16), pad], axis=0) for h in heads]
    vn = [jnp.concatenate([vn_ref[:, cols(h)].astype(BF16), pad], axis=0) for h in heads]
    _softmax_step(qs, kn, vn, tabn_ref[...], m_sc, l_sc, acc_sc)
    l = l_sc[...]
    acc = acc_sc[...]
    za = za_ref[...]
    rows = lambda h: slice(h * 2 * ts, (h + 1) * 2 * ts)
    o_ref[...] = jnp.concatenate(
        [_diff_finish(lam_ref, za[:, cols(h)], sg_ref[...], l[rows(h)], acc[rows(h)], ts) for h in heads],
        axis=1).astype(BF16)


def _da_sample(lamv, qn, ck, cv, kn, vn, za, tabc, tabn, sg, ts):
    nb, past, _ = ck.shape
    kern = functools.partial(_da_sample_kernel, ts=ts)
    stack = lambda tab: jnp.concatenate([tab, tab], axis=1).reshape(DA_HEADS * 2 * ts, tab.shape[-1])
    tabc, tabn = stack(tabc), stack(tabn)
    rows = pl.BlockSpec((ts, DA_WIDTH), lambda b: (b, 0))
    cache = pl.BlockSpec((1, past, DA_WIDTH), lambda b: (b, 0, 0))
    full = lambda a: pl.BlockSpec(a.shape, lambda b: (0,) * a.ndim)
    return pl.pallas_call(
        kern,
        grid=(nb,),
        in_specs=[full(lamv), rows, cache, cache, rows, rows, rows, full(tabc), full(tabn), full(sg)],
        out_specs=rows,
        out_shape=jax.ShapeDtypeStruct((nb * ts, DA_WIDTH), BF16),
        scratch_shapes=[pltpu.VMEM((DA_HEADS * 2 * ts, LANES), F32)] * 3,
        compiler_params=_params(("arbitrary",)),
        name="diff_attn_sample",
    )(lamv, qn, ck, cv, kn, vn, za, tabc, tabn, sg)


def _bias_kernel(idx_ref, bt_ref, o_ref):
    idx = idx_ref[...]
    bt = bt_ref[...]
    bt = bt - bt[:, FAR_BUCKET:FAR_BUCKET + 1]
    onehot = jnp.where(lax.broadcasted_iota(jnp.int32, (N_BUCKETS, idx.shape[1]), 0) == idx, 1.0, 0.0).astype(BF16)
    hi = bt.astype(BF16)
    rest = bt - hi.astype(F32)
    mid = rest.astype(BF16)
    lo = (rest - mid.astype(F32)).astype(BF16)
    val = _dot(hi, onehot) + _dot(mid, onehot) + _dot(lo, onehot)
    o_ref[...] = jnp.where(idx < 0, NEG, val * LOG2E)


def _bias_tables(rel_bias, tq, ts, past):
    def buckets(rel, mask):
        return np.where(mask, _t5_bucket_np(rel), -1).astype(np.int32).reshape(-1)

    r = np.arange(tq)[:, None]
    c = np.arange(tq)[None, :]
    qpos = past + np.arange(ts)[:, None]
    kc = np.arange(past)[None, :]
    kn = past + np.arange(LANES)[None, :]
    parts = [buckets(c - r, (c // CHUNK) <= (r // CHUNK)),
             buckets(c - r - tq, np.ones((tq, tq), bool)),
             buckets(kc - qpos, (kc // CHUNK) <= (qpos // CHUNK)),
             buckets(kn - qpos, ((kn // CHUNK) <= (qpos // CHUNK)) & (kn < past + ts))]
    blk = BIAS_LANES_PER_STEP
    used = sum(p.size for p in parts)
    n = -(-used // blk) * blk
    idx = np.concatenate(parts + [np.full(n - used, -1, np.int32)])
    bt = jnp.zeros((SUBLANES, N_BUCKETS), F32).at[:DA_HEADS].set(rel_bias.T)
    tab = pl.pallas_call(
        _bias_kernel,
        grid=(n // blk,),
        in_specs=[pl.BlockSpec((1, blk), lambda i: (0, i)), pl.BlockSpec(bt.shape, lambda i: (0, 0))],
        out_specs=pl.BlockSpec((SUBLANES, blk), lambda i: (0, i)),
        out_shape=jax.ShapeDtypeStruct((SUBLANES, n), F32),
        compiler_params=_params(("arbitrary",)),
        name="t5_bias_tables",
    )(jnp.asarray(idx).reshape(1, n), bt)[:DA_HEADS]
    o1 = 2 * tq * tq
    o2 = o1 + ts * past
    prompt_tab = tab[:, :o1].reshape(DA_HEADS, 2, tq, tq)
    tabc = tab[:, o1:o2].reshape(DA_HEADS, ts, past)
    tabn = tab[:, o2:used].reshape(DA_HEADS, ts, LANES)
    return prompt_tab, tabc, tabn


def _rwkv_kernel(prw_ref, look_ref, first_ref, zb_ref, s0_ref, mu_ref, w0_ref, a0_ref, wwa_ref,
                 kkw_ref, kaw_ref, rk_ref, lng_ref, lnb_ref, g2_ref, hm_ref, dm_ref, pm_ref, mo_ref,
                 ti_ref, to_ref,
                 ob_ref, sfin_ref,
                 s_sc, om_sc, rho_sc, at_sc, kt_sc, atb_sc, ktb_sc, wb_sc, vr_sc, o_sc, *, tm):
    j = pl.program_id(1)

    @pl.when(j == 0)
    def _():
        s_sc[...] = s0_ref[0]

    g2 = g2_ref[...]

    def head_sums(t):
        tb = t.astype(BF16)
        return jnp.concatenate([_dot(tb[:, c * RW_STATE_LANES:(c + 1) * RW_STATE_LANES], g2)
                                for c in range(RW_STATE_GROUPS)], axis=1)

    x = prw_ref[...]
    row0 = jnp.where(j == 0, first_ref[0], look_ref[7:8, :])
    rows = lax.broadcasted_iota(jnp.int32, x.shape, 0)
    prev = jnp.where(rows == 0, row0, pltpu.roll(x, 1, axis=0))
    mix = x + (prev - x) * mu_ref[...]
    r = mix[:, 0:512]
    kr = mix[:, 512:1024]
    vr = mix[:, 1024:1536]
    la = mix[:, 1536:1664]
    lane = lax.broadcasted_iota(jnp.int32, la.shape, 1)
    la = jnp.where(lane < HEAD, jnp.tanh(la), la)
    hi = la.astype(BF16)
    lo = (la - hi.astype(F32)).astype(BF16)
    wwa = wwa_ref[...]
    lora = _dot(hi, wwa[0]) + _dot(lo, wwa[0]) + _dot(hi, wwa[1])
    wpre = w0_ref[...] + lora[:, 0:512]
    nw = -wpre
    w = -(jnp.maximum(nw, 0.0) + jnp.log(1.0 + jnp.exp(-jnp.abs(nw)))) - 0.5
    ld = -jnp.exp(w)
    a = 1.0 / (1.0 + jnp.exp(-(a0_ref[...] + lora[:, 512:1024])))
    kk = kr * kkw_ref[...]
    nrm = jnp.sqrt(head_sums(kk * kk))
    kk = kk / jnp.maximum(nrm, 1e-12)
    kka = kk * a
    kh = kr * (1.0 + (a - 1.0) * kaw_ref[...])

    def dot3(m, t):
        hi = t.astype(BF16)
        r1 = t - hi.astype(F32)
        mid = r1.astype(BF16)
        lo = (r1 - mid.astype(F32)).astype(BF16)
        return _dot(m, hi) + _dot(m, mid) + _dot(m, lo)

    incl = dot3(ti_ref[...], ld)
    tot = dot3(to_ref[...], ld)
    e_rem = jnp.exp(tot - incl)
    e_inv = jnp.exp(-incl)
    om_sc[...] = jnp.exp(incl - ld) * kk
    rho_sc[...] = jnp.exp(incl) * r
    at_sc[...] = kka * e_inv
    kt_sc[...] = kh * e_inv
    atb_sc[...] = kka * e_rem
    ktb_sc[...] = kh * e_rem
    wb_sc[...] = jnp.exp(tot)
    hm = hm_ref[...]
    vx = (jnp.broadcast_to(vr[:, None, :], (tm, RW_HEADS, RW_WIDTH)) * hm[None]).reshape(tm * RW_HEADS, RW_WIDTH)
    vr_sc[...] = _dot(vx.astype(BF16), pm_ref[...])

    nb = RW_BLOCK
    heads_per_group = RW_HEADS // RW_STATE_GROUPS
    dm = dm_ref[...]

    def head_rows(blk):
        return jnp.concatenate([jnp.broadcast_to(blk[j:j + 1], (RW_HEADS, RW_WIDTH)) * hm for j in range(nb)],
                               axis=0)

    def fold(x):
        return x[:, :RW_STATE_LANES] + x[:, RW_STATE_LANES:]

    def query_rows(b):
        base = pl.multiple_of(b * nb, nb)
        return jnp.concatenate([head_rows(om_sc[pl.ds(base, nb), :]), head_rows(rho_sc[pl.ds(base, nb), :])], axis=0)

    nblk = tm // nb

    def coefficients(b, x):
        base = pl.multiple_of(b * nb, nb)
        rhs = jnp.concatenate([at_sc[pl.ds(base, nb), :], kt_sc[pl.ds(base, nb), :],
                               jnp.zeros((LANES - 2 * nb, RW_WIDTH), F32)], axis=0)
        xb = x.astype(BF16)
        rhs_o = jnp.concatenate([head_rows(at_sc[pl.ds(base, nb), :]), head_rows(kt_sc[pl.ds(base, nb), :])], axis=0)
        return _dot_nt(xb, rhs.astype(BF16)), _dot_nt(xb[nb * RW_HEADS:], rhs_o.astype(BF16))

    def block(b, carry):
        s, g, coef, coef_o = carry
        base = pl.multiple_of(b * nb, nb)
        atb_b, ktb_b, wb_b = (ref[pl.ds(base, nb), :] for ref in (atb_sc, ktb_sc, wb_sc))
        vr_b = vr_sc[pl.ds(pl.multiple_of(b * (nb * RW_HEADS), nb * RW_HEADS), nb * RW_HEADS), :]
        bn = jnp.minimum(b + 1, nblk - 1)
        xn = query_rows(bn)
        coef_next, coef_o_next = coefficients(bn, xn)
        rhs2 =jnp.concatenate([fold(head_rows(atb_b)), fold(head_rows(ktb_b))], axis=0).astype(BF16)
        cross = _dot_nt(fold(xn).astype(BF16), rhs2).astype(BF16)
        g_decayed = _dot_nt(fold(xn * wb_b[0:1]).astype(BF16), s.astype(BF16))

        def rows(a, j):
            return a[j * RW_HEADS:(j + 1) * RW_HEADS]

        def cf(j, lane):
            return jnp.broadcast_to(rows(coef, j)[:, lane:lane + 1], (RW_HEADS, LANES))

        sa = []
        for j in range(nb):
            acc = rows(g, j) * rows(dm, j)
            for i in range(j):
                acc = acc + cf(j, nb + i) * rows(vr_b, i)
            for i in range(j):
                acc = acc - cf(j, i) * sa[i]
            sa.append(acc)
        lhs = jnp.concatenate([-t for t in sa] + [vr_b], axis=0).astype(BF16)
        o_all = g[nb * RW_HEADS:] * dm + _dot((coef_o * mo_ref[...]).astype(BF16), lhs)
        outs = [rows(o_all, j) for j in range(nb)]
        g_next = g_decayed + _dot(cross, lhs)
        ds = lax.dot_general(lhs, rhs2, (((0,), (0,)), ((), ())), preferred_element_type=F32)
        wb = jnp.concatenate(
            [jnp.broadcast_to(wb_b[0:1, c * RW_STATE_LANES:(c + 1) * RW_STATE_LANES], (HEAD, RW_STATE_LANES))
             for c in range(RW_STATE_GROUPS)], axis=0)
        for h in range(RW_HEADS):
            o_sc[h, pl.ds(base, nb), :] = jnp.concatenate([t[h:h + 1] for t in outs], axis=0)
        return s * wb + ds, g_next, coef_next, coef_o_next

    s0 = s_sc[...]
    x0 = query_rows(0)
    g0 = _dot_nt(fold(x0).astype(BF16), s0.astype(BF16))
    s_sc[...] = lax.fori_loop(0, nblk, block, (s0, g0) + coefficients(0, x0), unroll=min(RW_UNROLL, nblk))[0]

    o = jnp.concatenate([o_sc[c * heads_per_group + hh][:, c * HEAD:(c + 1) * HEAD]
                         for c in range(RW_STATE_GROUPS) for hh in range(heads_per_group)], axis=1)
    mean = head_sums(o) * (1.0 / HEAD)
    d = o - mean
    var = head_sums(d * d) * (1.0 / HEAD)
    y = d * lax.rsqrt(var + RW_GN_EPS) * lng_ref[...] + lnb_ref[...]
    bonus = head_sums(r * kh * rk_ref[...]) * vr
    ob_ref[...] = ((y + bonus) * _silu(zb_ref[...])).astype(BF16)

    @pl.when(j == pl.num_programs(1) - 1)
    def _():
        sfin_ref[0] = s_sc[...]


def _block_sum_matrices(tm):
    t = np.arange(tm)
    same = (t[:, None] // RW_BLOCK) == (t[None, :] // RW_BLOCK)
    incl = same & (t[None, :] <= t[:, None])
    return [jnp.asarray(incl.astype(np.float32), dtype=BF16), jnp.asarray(same.astype(np.float32), dtype=BF16)]


def _rwkv(prw, first, zb, s0, p, seq_len, tm):
    m = prw.shape[0]
    nseq = m // seq_len
    ntile = seq_len // tm
    kern = functools.partial(_rwkv_kernel, tm=tm)
    full = lambda a: pl.BlockSpec(a.shape, lambda b, j: (0,) * a.ndim)
    rows = lambda c: pl.BlockSpec((tm, c), lambda b, j: (b * ntile + j, 0))
    look = pl.BlockSpec((8, RW_SHIFT_COLS), lambda b, j: (jnp.maximum((b * ntile + j) * (tm // 8) - 1, 0), 0))
    consts = [p["mu"], p["w0"], p["a0"], p["wwa"], p["kkw"], p["kaw"], p["rk"], p["lng"], p["lnb"],
              p["g2"], p["hm"], p["dm"], p["pm"], p["mo"]] + _block_sum_matrices(tm)
    return pl.pallas_call(
        kern,
        grid=(nseq, ntile),
        in_specs=[rows(RW_SHIFT_COLS), look,
                  pl.BlockSpec((1, 1, RW_SHIFT_COLS), lambda b, j: (b, 0, 0)),
                  rows(RW_WIDTH),
                  pl.BlockSpec((1,) + RW_STATE_SHAPE, lambda b, j: (b, 0, 0))] + [full(c) for c in consts],
        out_specs=[rows(RW_WIDTH), pl.BlockSpec((1,) + RW_STATE_SHAPE, lambda b, j: (b, 0, 0))],
        out_shape=[jax.ShapeDtypeStruct((m, RW_WIDTH), BF16), jax.ShapeDtypeStruct((nseq,) + RW_STATE_SHAPE, F32)],
        scratch_shapes=([pltpu.VMEM(RW_STATE_SHAPE, F32)] + [pltpu.VMEM((tm, RW_WIDTH), F32)] * 7
                        + [pltpu.VMEM((tm * RW_HEADS, RW_STATE_SHAPE[0]), F32),
                           pltpu.VMEM((RW_HEADS, tm, RW_STATE_SHAPE[0]), F32)]),
        compiler_params=_params(("arbitrary", "arbitrary")),
        name="rwkv7",
    )(prw, prw, first, zb, s0, *consts)


def _state_to_rows(s):
    b = s.shape[0]
    per = RW_HEADS // RW_STATE_GROUPS
    return s.reshape(b, RW_STATE_GROUPS, per, HEAD, HEAD).transpose(0, 1, 3, 2, 4).reshape((b,) + RW_STATE_SHAPE)


def _rows_to_state(s):
    b = s.shape[0]
    per = RW_HEADS // RW_STATE_GROUPS
    return s.reshape(b, RW_STATE_GROUPS, HEAD, per, HEAD).transpose(0, 1, 3, 2, 4).reshape(b, RW_HEADS, HEAD, HEAD)


SKIP_LOG = -104.0
SB_RECENT_KEYS = 256


def _sb_tile(qs, k, v, u, c, mask):
    tk = k[0].shape[0]
    rows = qs[0].shape[0]
    z = jnp.concatenate([_dot_nt(q_p, k_p) for q_p, k_p in zip(qs, k)], axis=0)
    nz = -z
    lg = jnp.minimum(nz, 0.0) - jnp.log(1.0 + jnp.exp(jnp.minimum(z, nz)))
    if mask is not None:
        lg = jnp.where(mask, lg, 0.0)
    it = _dot(lg.astype(BF16), u)
    a = jnp.exp(z + it + jnp.concatenate([c] * (tk // LANES), axis=1))
    if mask is not None:
        a = jnp.where(mask, a, 0.0)
    a = a.astype(BF16)
    da = jnp.concatenate([_dot(a[p * rows:(p + 1) * rows], v_p) for p, v_p in enumerate(v)], axis=0)
    return da, c + jnp.broadcast_to(it[:, 0:1], c.shape)


def _sb_finish(acc, z):
    t = z.shape[0]
    lane = lax.broadcasted_iota(jnp.int32, z.shape, 1)
    return jnp.where(lane < HEAD, acc[:t], acc[t:]) * _silu(z)


def _sb_prompt_kernel(q_ref, k_ref, v_ref, z_ref, u_ref, o_ref, acc_sc, c_sc, *, tq):
    i = pl.program_id(1)
    tk = u_ref.shape[0]
    r = tq // tk
    qs = _stack_halves(q_ref[...])
    u = u_ref[...]
    acc_sc[...] = jnp.zeros(acc_sc.shape, F32)
    c_sc[...] = jnp.zeros(c_sc.shape, F32)
    def tile(j, lo, hi, masked, valid=None):
        n = hi - lo
        halves = lambda x: jnp.concatenate([x[lo:hi], x[tq + lo:tq + hi]], axis=0)
        off = pl.multiple_of(j * tk, tk)
        mask = valid
        if masked:
            qrow = lax.broadcasted_iota(jnp.int32, (2 * n, tk), 0) % n
            col = lax.broadcasted_iota(jnp.int32, (2 * n, tk), 1)
            mask = col < qrow
        da, c = _sb_tile([halves(qs)], [k_ref[pl.ds(off, tk), :]], [v_ref[pl.ds(off, tk), :]], u,
                         halves(c_sc[...]), mask)
        for h in range(2):
            acc_sc[h * tq + lo:h * tq + hi, :] += da[h * n:(h + 1) * n]
            c_sc[h * tq + lo:h * tq + hi, :] = c[h * n:(h + 1) * n]

    for jj in reversed(range(r)):
        tile(i * r + jj, jj * tk, tq, True)

    tile(jnp.maximum(i * r - 1, 0), 0, tq, False, valid=i > 0)

    def body(st):
        tile(i * r - 1 - st[0], 0, tq, False)
        return st[0] + 1, jnp.max(c_sc[...])

    lax.while_loop(lambda st: jnp.logical_and(st[0] < i * r, st[1] > SKIP_LOG), body,
                   (jnp.int32(1), jnp.max(c_sc[...])))
    o_ref[...] = _sb_finish(acc_sc[...], z_ref[...]).astype(BF16)


def _sb_prompt(qb, kb, vb, z, u, tq):
    t = qb.shape[0]
    kern = functools.partial(_sb_prompt_kernel, tq=tq)
    rows = pl.BlockSpec((tq, LANES), lambda h, i: (i, h))
    whole = pl.BlockSpec((t, LANES), lambda h, i: (0, h))
    return pl.pallas_call(
        kern,
        grid=(SB_HEADS // 2, t // tq),
        in_specs=[rows, whole, whole, rows, pl.BlockSpec(u.shape, lambda h, i: (0, 0))],
        out_specs=rows,
        out_shape=jax.ShapeDtypeStruct((t, SB_WIDTH), BF16),
        scratch_shapes=[pltpu.VMEM((2 * tq, LANES), F32)] * 2,
        compiler_params=_params(("arbitrary", "arbitrary")),
        name="stick_breaking_prompt",
    )(qb, kb, vb, z, u)


def _sb_sample_kernel(q_ref, ck_ref, cv_ref, kn_ref, vn_ref, z_ref, u_ref, o_ref, left_ref, acc_sc, c_sc, *, ts):
    tk = u_ref.shape[0]
    ntile = ck_ref.shape[1] // tk
    npair = SB_HEADS // 2
    cols = lambda p: slice(p * LANES, (p + 1) * LANES)
    qs = [_stack_halves(q_ref[:, cols(p)]) for p in range(npair)]
    u = u_ref[...]
    pad = jnp.zeros((tk - ts, LANES), BF16)
    kn = [jnp.concatenate([kn_ref[:, cols(p)], pad], axis=0) for p in range(npair)]
    vn = [jnp.concatenate([vn_ref[:, cols(p)], pad], axis=0) for p in range(npair)]
    rows = npair * 2 * ts
    qrow = lax.broadcasted_iota(jnp.int32, (rows, tk), 0) % ts
    col = lax.broadcasted_iota(jnp.int32, (rows, tk), 1)
    acc, c = _sb_tile(qs, kn, vn, u, jnp.zeros((rows, LANES), F32), col < qrow)

    def cache_tile(n, c):
        off = pl.multiple_of((ntile - 1 - n) * tk, tk)
        kc = ck_ref[0, pl.ds(off, tk), :].astype(BF16)
        vc = cv_ref[0, pl.ds(off, tk), :].astype(BF16)
        return _sb_tile(qs, [kc[:, cols(p)] for p in range(npair)], [vc[:, cols(p)] for p in range(npair)], u, c, None)

    da, c = cache_tile(0, c)
    acc_sc[...] = acc + da
    c_sc[...] = c

    def body(st):
        da, c = cache_tile(st[0], c_sc[...])
        acc_sc[...] += da
        c_sc[...] = c
        return st[0] + 1, jnp.max(c)

    lax.while_loop(lambda st: jnp.logical_and(st[0] < ntile, st[1] > SKIP_LOG), body,
                   (jnp.int32(1), jnp.max(c)))
    left_ref[0] = jnp.broadcast_to(jnp.max(c_sc[...], axis=0, keepdims=True), (SUBLANES, LANES))
    acc = acc_sc[...]
    z = z_ref[...]
    o_ref[...] = jnp.concatenate(
        [_sb_finish(acc[p * 2 * ts:(p + 1) * 2 * ts], z[:, cols(p)]) for p in range(npair)], axis=1).astype(BF16)


def _sb_sample(qb, ck, cv, kb, vb, z, u, ts):
    nb, past, _ = ck.shape
    kern = functools.partial(_sb_sample_kernel, ts=ts)
    rows = pl.BlockSpec((ts, SB_WIDTH), lambda b: (b, 0))
    cache = pl.BlockSpec((1, past, SB_WIDTH), lambda b: (b, 0, 0))
    return pl.pallas_call(
        kern,
        grid=(nb,),
        in_specs=[rows, cache, cache, rows, rows, rows, pl.BlockSpec(u.shape, lambda b: (0, 0))],
        out_specs=[rows, pl.BlockSpec((1, SUBLANES, LANES), lambda b: (b, 0, 0))],
        out_shape=[jax.ShapeDtypeStruct((nb * ts, SB_WIDTH), BF16),
                   jax.ShapeDtypeStruct((nb, SUBLANES, LANES), F32)],
        scratch_shapes=[pltpu.VMEM((SB_HEADS * ts, LANES), F32)] * 2,
        compiler_params=_params(("arbitrary",)),
        name="stick_breaking_sample",
    )(qb, ck, cv, kb, vb, z, u)


def _sb_sample_two_phase(qb, cache_k, cache_v, kb, vb, z, u, ts):
    nb, past = cache_k.shape[:2]
    dense = lambda c: c.reshape(nb, c.shape[1], SB_WIDTH)
    recent = min(past, SB_RECENT_KEYS)
    out, left = _sb_sample(qb, dense(cache_k[:, past - recent:]), dense(cache_v[:, past - recent:]), kb, vb, z, u, ts)
    if recent == past:
        return out
    return lax.cond(jnp.max(left) > SKIP_LOG,
                    lambda: _sb_sample(qb, dense(cache_k), dense(cache_v), kb, vb, z, u, ts)[0],
                    lambda: out)


def _suffix_matrix(tk):
    j = np.arange(tk)[:, None]
    s = np.arange(tk)[None, :]
    return jnp.asarray((j >= s).astype(np.float32), dtype=BF16)


def _row_tile(m):
    for tm in PROJ_TILES:
        if m % tm == 0:
            return tm
    return m


def kernel(x_prompt, x_sample, cache_l0_k, cache_l0_v, state_l0_shift, state_l0_wkv, cache_l1_k, cache_l1_v, rel_bias, norm_l0, w_in_l0, w_out_l0, da_q_norm, da_k_norm, da_lambda_q1, da_lambda_k1, da_lambda_q2, da_lambda_k2, da_subln, rw_mu, rw_w0, rw_w_up, rw_a0, rw_a_up, rw_k_k, rw_k_a, rw_r_k, rw_lnx_g, rw_lnx_b, norm_l1, w_in_l1, w_out_l1):
    bp, t, d = x_prompt.shape
    nb, ts, _ = x_sample.shape
    past = cache_l0_k.shape[1]
    assert bp == 1 and ts <= LANES and ts % RW_BLOCK == 0
    assert t % max(SB_TILE, DA_TILE, RW_TILE) == 0 and past % LANES == 0 and DA_TILE >= MAX_DISTANCE

    row = lambda v: v.reshape(1, -1).astype(F32)
    w_in0 = w_in_l0.astype(BF16)
    w_in1 = w_in_l1.astype(BF16)
    w_out0a = w_out_l0[:DA_WIDTH].astype(BF16)
    w_out0b = w_out_l0[DA_WIDTH:].astype(BF16)
    w_out1 = w_out_l1.astype(BF16)
    qg = row(jnp.tile(da_q_norm, 512 // HEAD))
    kg = row(jnp.tile(da_k_norm, 512 // HEAD))
    lamv = jnp.stack([da_lambda_q1, da_lambda_k1, da_lambda_q2, da_lambda_k2]).astype(F32)
    sg = row(da_subln)
    zeros = jnp.zeros((HEAD, RW_WIDTH), F32)
    wwa_f = jnp.concatenate([jnp.concatenate([rw_w_up, zeros], axis=1),
                             jnp.concatenate([zeros, rw_a_up], axis=1)], axis=0)
    wwa_hi = wwa_f.astype(BF16)
    wwa = jnp.stack([wwa_hi, (wwa_f - wwa_hi.astype(F32)).astype(BF16)])
    heads = np.arange(RW_HEADS)
    per_group = RW_HEADS // RW_STATE_GROUPS
    chan = np.arange(RW_WIDTH)
    out_lane = np.arange(RW_STATE_SHAPE[0])
    hm = np.equal(heads[:, None], chan[None, :] // HEAD)
    dm = np.equal(heads[:, None] // per_group, out_lane[None, :] // HEAD)
    pm = (chan[:, None] // HEAD // per_group == out_lane[None, :] // HEAD) & (chan[:, None] % HEAD == out_lane[None, :] % HEAD)
    tok_row = np.arange(RW_BLOCK * RW_HEADS)[:, None] // RW_HEADS
    tok_lane = (np.arange(LANES)[None, :] % (RW_BLOCK * RW_HEADS)) // RW_HEADS
    rw = dict(mu=row(rw_mu), w0=row(rw_w0), a0=row(rw_a0), wwa=wwa, kkw=row(rw_k_k), kaw=row(rw_k_a),
              rk=row(rw_r_k), lng=row(rw_lnx_g), lnb=row(rw_lnx_b), g2=_group_matrix(RW_STATE_LANES, 1.0),
              hm=jnp.asarray(hm.astype(np.float32)),
              dm=jnp.asarray(np.tile(dm.astype(np.float32), (RW_BLOCK, 1))),
              pm=jnp.asarray(pm.astype(np.float32), dtype=BF16),
              mo=jnp.asarray((tok_lane <= tok_row).astype(np.float32)))
    prompt_tab, tabc, tabn = _bias_tables(rel_bias.astype(F32), DA_TILE, ts, past)
    u_prompt = _suffix_matrix(2 * LANES)
    u_sample = _suffix_matrix(LANES)

    def layer0(x, first, s0, seq_len, attend):
        m = x.shape[0]
        tm = _row_tile(m)
        qn, kn, knb, v, vb, za, prw, zb = _inproj0(x, row(norm_l0), w_in0, qg, kg, tm)
        oa = attend(qn, kn, knb, v, vb, za)
        ob, sfin = _rwkv(prw, first, zb, _state_to_rows(s0), rw, seq_len, min(seq_len, RW_TILE))
        y, *proj1 = _boundary(x, oa, ob, w_out0a, w_out0b, row(norm_l1), w_in1, min(tm, BOUNDARY_TILE))
        shift = prw.reshape(m // seq_len, seq_len, RW_SHIFT_COLS)[:, -1:]
        return (y, proj1), kn, v, shift, _rows_to_state(sfin)

    def layer1(x_and_proj, attend):
        x, (qb, k, kb, v, vb, z) = x_and_proj
        tm = _row_tile(x.shape[0])
        o = attend(qb, kb, vb, z)
        return _outproj(x, [o], [w_out1], tm), k, v

    xp = x_prompt.reshape(t, d)
    yp, k0p, v0p, shp, wkvp = layer0(
        xp, jnp.zeros((1, 1, RW_SHIFT_COLS), F32), jnp.zeros((1, RW_HEADS, HEAD, HEAD), F32), t,
        lambda qn, kn, knb, v, vb, za: _da_prompt(lamv, qn, knb, vb, za, prompt_tab, sg, DA_TILE))
    yp, k1p, v1p = layer1(yp, lambda qb, kb, vb, z: _sb_prompt(qb, kb, vb, z, u_prompt, SB_TILE))

    xs = x_sample.reshape(nb * ts, d)
    ck0 = cache_l0_k.reshape(nb, past, DA_WIDTH)
    cv0 = cache_l0_v.reshape(nb, past, DA_WIDTH)
    ys, k0s, v0s, shs, wkvs = layer0(
        xs, state_l0_shift, state_l0_wkv, ts,
        lambda qn, kn, knb, v, vb, za: _da_sample(lamv, qn, ck0, cv0, knb, vb, za, tabc, tabn, sg, ts))
    ys, k1s, v1s = layer1(ys, lambda qb, kb, vb, z: _sb_sample_two_phase(
        qb, cache_l1_k, cache_l1_v, kb, vb, z, u_sample, ts))

    return (yp.reshape(1, t, d), ys.reshape(nb, ts, d),
            k0p.reshape(1, t, DA_HEADS, LANES), v0p.reshape(1, t, DA_HEADS, LANES), shp, wkvp,
            k1p.reshape(1, t, SB_HEADS, HEAD), v1p.reshape(1, t, SB_HEADS, HEAD),
            k0s.reshape(nb, ts, DA_HEADS, LANES), v0s.reshape(nb, ts, DA_HEADS, LANES), shs, wkvs,
            k1s.reshape(nb, ts, SB_HEADS, HEAD), v1s.reshape(nb, ts, SB_HEADS, HEAD))
```
